```python
import jax, jax.numpy as jnp
from jax import lax
import numpy as np

D_MODEL = 1024
BATCH = 8
SEQ = 8192
DEPTH = 2

D_MIX = D_MODEL
CONV_WIDTH = D_MIX // 4
CONV_GROUPS = 4
CONV_K = 31
MLA_HEADS = 8
MLA_NOPE = 64
MLA_ROPE = 32
MLA_QK = MLA_NOPE + MLA_ROPE
MLA_V = 64
MLA_WIDTH = MLA_HEADS * MLA_V
Q_LORA = 768
KV_LORA = 256
ROPE_THETA = 10000.0
Q_BLOCK = 128
SG_WIDTH = D_MIX - CONV_WIDTH - MLA_WIDTH
SG_HEADS = 4
SG_HEAD_DIM = SG_WIDTH // SG_HEADS
SG_CHUNK = 128
IN_SIZES = (CONV_WIDTH, CONV_WIDTH, CONV_WIDTH,
            Q_LORA, KV_LORA, MLA_ROPE, MLA_WIDTH,
            SG_WIDTH, SG_WIDTH, SG_WIDTH)
IN_COLS = 3 * CONV_WIDTH + Q_LORA + KV_LORA + MLA_ROPE + MLA_WIDTH + 3 * SG_WIDTH
EPS = 1e-6

kernel_name = 'hybrid_conv_mla_sgu_parallel_heads'


def _rms_norm(x, g):
    xf = x.astype(jnp.float32)
    y = xf * lax.rsqrt(jnp.mean(xf * xf, axis=-1, keepdims=True) + EPS)
    return (y * g.astype(jnp.float32)).astype(x.dtype)


def _layer_norm(x, g, b):
    xf = x.astype(jnp.float32)
    mu = jnp.mean(xf, axis=-1, keepdims=True)
    var = jnp.mean(jnp.square(xf - mu), axis=-1, keepdims=True)
    y = (xf - mu) * lax.rsqrt(var + EPS) * g.astype(jnp.float32) + b.astype(jnp.float32)
    return y.astype(x.dtype)


def _rope_tables(seq):
    half = MLA_ROPE // 2
    inv_freq = ROPE_THETA ** (-jnp.arange(half, dtype=jnp.float32) / half)
    ang = jnp.arange(seq, dtype=jnp.float32)[:, None] * inv_freq[None, :]
    return jnp.cos(ang), jnp.sin(ang)


def _apply_rope(x, cos, sin):
    half = MLA_ROPE // 2
    c = cos[None, :, None, :].astype(x.dtype)
    s = sin[None, :, None, :].astype(x.dtype)
    x1, x2 = x[..., :half], x[..., half:]
    return jnp.concatenate([x1 * c - x2 * s, x1 * s + x2 * c], axis=-1)


def _conv_branch(a, a_glu, conv_w, conv_b, ln_g, ln_b, pw_w, pw_b):
    y = a * jax.nn.sigmoid(a_glu)
    y = lax.conv_general_dilated(
        y, conv_w[:, None, :], window_strides=(1,),
        padding=[(CONV_K - 1, 0)],
        dimension_numbers=('NWC', 'WIO', 'NWC'),
        feature_group_count=CONV_WIDTH) + conv_b
    y = jax.nn.silu(_layer_norm(y, ln_g, ln_b))
    return y @ pw_w + pw_b


def _causal_attention(q, k, v):
    B, S, H, Dq = q.shape
    nb = S // Q_BLOCK
    scale = Dq ** -0.5
    q_blocks = jnp.moveaxis(q.reshape(B, nb, Q_BLOCK, H, Dq), 1, 0)
    k_pos = jnp.arange(S)

    def block(args):
        qb, i = args
        q_pos = i * Q_BLOCK + jnp.arange(Q_BLOCK)
        s = jnp.einsum('bqhd,bkhd->bhqk', qb, k).astype(jnp.float32) * scale
        s = jnp.where(k_pos[None, :] <= q_pos[:, None], s, -jnp.inf)
        p = jax.nn.softmax(s, axis=-1).astype(v.dtype)
        return jnp.einsum('bhqk,bkhd->bqhd', p, v)

    out = lax.map(block, (q_blocks, jnp.arange(nb)))
    return jnp.moveaxis(out, 0, 1).reshape(B, S, H * v.shape[-1])


def _mla_branch(c_q, c_kv, k_rope, q_norm_g, w_uq, kv_norm_g, w_ukv,
                qk_q_g, qk_k_g, cos, sin):
    B, S, _ = c_q.shape
    q = (_rms_norm(c_q, q_norm_g) @ w_uq).reshape(B, S, MLA_HEADS, MLA_QK)
    kv = (_rms_norm(c_kv, kv_norm_g) @ w_ukv).reshape(B, S, MLA_HEADS, MLA_NOPE + MLA_V)
    k_nope, v = kv[..., :MLA_NOPE], kv[..., MLA_NOPE:]
    k_r = jnp.broadcast_to(k_rope[:, :, None, :], (B, S, MLA_HEADS, MLA_ROPE))
    k = jnp.concatenate([k_nope, k_r], axis=-1)
    q = _rms_norm(q, qk_q_g)
    k = _rms_norm(k, qk_k_g)
    q = jnp.concatenate([q[..., :MLA_NOPE], _apply_rope(q[..., MLA_NOPE:], cos, sin)], axis=-1)
    k = jnp.concatenate([k[..., :MLA_NOPE], _apply_rope(k[..., MLA_NOPE:], cos, sin)], axis=-1)
    return _causal_attention(q, k, v)


def _sgu_branch(u, v, ln_g, ln_b, sg_w, sg_b):
    B, S, _ = u.shape
    nc = S // SG_CHUNK
    u = jax.nn.gelu(u)
    v = _layer_norm(jax.nn.gelu(v), ln_g, ln_b)
    v = v.reshape(B, nc, SG_CHUNK, SG_HEADS, SG_HEAD_DIM)
    mask = jnp.tril(jnp.ones((SG_CHUNK, SG_CHUNK), dtype=bool))
    w = jnp.where(mask[None], sg_w, jnp.zeros_like(sg_w))
    mixed = jnp.einsum('gts,bcsgd->bctgd', w, v) + sg_b.T[None, None, :, :, None]
    return u * mixed.reshape(B, S, SG_WIDTH)


def _layer(x, cos, sin, norm_g, w_in, conv_w, conv_b, conv_ln_g, conv_ln_b,
           conv_pw_w, conv_pw_b, q_norm_g, w_uq, kv_norm_g, w_ukv, qk_q_g, qk_k_g,
           sg_ln_g, sg_ln_b, sg_w, sg_b, branch_norm_g, w_out):
    h = _rms_norm(x, norm_g)
    proj = h @ w_in
    idx = np.cumsum(IN_SIZES)[:-1].tolist()
    (a, a_glu, z_conv, c_q, c_kv, k_rope, z_mla,
     u_sg, v_sg, z_sg) = jnp.split(proj, idx, axis=-1)

    y_conv = _conv_branch(a, a_glu, conv_w, conv_b, conv_ln_g, conv_ln_b,
                          conv_pw_w, conv_pw_b) * jax.nn.silu(z_conv)
    y_mla = _mla_branch(c_q, c_kv, k_rope, q_norm_g, w_uq, kv_norm_g, w_ukv,
                        qk_q_g, qk_k_g, cos, sin) * jax.nn.silu(z_mla)
    y_sg = _sgu_branch(u_sg, v_sg, sg_ln_g, sg_ln_b, sg_w, sg_b) * jax.nn.silu(z_sg)

    g_conv = branch_norm_g[:CONV_WIDTH]
    g_mla = branch_norm_g[CONV_WIDTH:CONV_WIDTH + MLA_WIDTH]
    g_sg = branch_norm_g[CONV_WIDTH + MLA_WIDTH:]
    y = jnp.concatenate([_rms_norm(y_conv, g_conv),
                         _rms_norm(y_mla, g_mla),
                         _rms_norm(y_sg, g_sg)], axis=-1)
    return x + y @ w_out


def _fwd_setup_inputs(seed: int = 0) -> dict:
    key = jax.random.key(seed)
    ks = jax.random.split(key, 24)
    f32 = jnp.float32

    def nrm(k, shape, scale):
        return jax.random.normal(k, shape, f32) * scale

    def gain(k, shape):
        return 1.0 + 0.02 * jax.random.normal(k, shape, f32)

    L = DEPTH
    return {
        'x': jax.random.normal(ks[0], (BATCH, SEQ, D_MODEL), f32),
        'norm_g': gain(ks[1], (L, D_MODEL)),
        'w_in': nrm(ks[2], (L, D_MODEL, IN_COLS), D_MODEL ** -0.5),
        'conv_w': nrm(ks[3], (L, CONV_K, CONV_WIDTH), CONV_K ** -0.5),
        'conv_b': nrm(ks[4], (L, CONV_WIDTH), 0.01),
        'conv_ln_g': gain(ks[5], (L, CONV_WIDTH)),
        'conv_ln_b': nrm(ks[6], (L, CONV_WIDTH), 0.01),
        'conv_pw_w': nrm(ks[7], (L, CONV_WIDTH, CONV_WIDTH), CONV_WIDTH ** -0.5),
        'conv_pw_b': nrm(ks[8], (L, CONV_WIDTH), 0.01),
        'q_norm_g': gain(ks[9], (L, Q_LORA)),
        'w_uq': nrm(ks[10], (L, Q_LORA, MLA_HEADS * MLA_QK), Q_LORA ** -0.5),
        'kv_norm_g': gain(ks[11], (L, KV_LORA)),
        'w_ukv': nrm(ks[12], (L, KV_LORA, MLA_HEADS * (MLA_NOPE + MLA_V)), KV_LORA ** -0.5),
        'qk_q_g': gain(ks[13], (L, MLA_QK)),
        'qk_k_g': gain(ks[14], (L, MLA_QK)),
        'sg_ln_g': gain(ks[15], (L, SG_WIDTH)),
        'sg_ln_b': nrm(ks[16], (L, SG_WIDTH), 0.01),
        'sg_w': nrm(ks[17], (L, SG_HEADS, SG_CHUNK, SG_CHUNK), SG_CHUNK ** -0.5),
        'sg_b': 1.0 + nrm(ks[18], (L, SG_HEADS, SG_CHUNK), 0.1),
        'branch_norm_g': gain(ks[19], (L, D_MIX)),
        'w_out': nrm(ks[20], (L, D_MIX, D_MODEL), D_MIX ** -0.5),
    }


def _fwd_reference(x, norm_g, w_in, conv_w, conv_b, conv_ln_g, conv_ln_b, conv_pw_w,
              conv_pw_b, q_norm_g, w_uq, kv_norm_g, w_ukv, qk_q_g, qk_k_g,
              sg_ln_g, sg_ln_b, sg_w, sg_b, branch_norm_g, w_out):
    cos, sin = _rope_tables(x.shape[1])
    for l in range(DEPTH):
        x = _layer(x, cos, sin, norm_g[l], w_in[l], conv_w[l], conv_b[l],
                   conv_ln_g[l], conv_ln_b[l], conv_pw_w[l], conv_pw_b[l],
                   q_norm_g[l], w_uq[l], kv_norm_g[l], w_ukv[l], qk_q_g[l], qk_k_g[l],
                   sg_ln_g[l], sg_ln_b[l], sg_w[l], sg_b[l], branch_norm_g[l], w_out[l])
    return x


import jax as _jax
import jax.numpy as _jnp

TWIN_FORMAT = 'train_step'
FWD_PARAMS = ['x', 'norm_g', 'w_in', 'conv_w', 'conv_b', 'conv_ln_g', 'conv_ln_b', 'conv_pw_w', 'conv_pw_b', 'q_norm_g', 'w_uq', 'kv_norm_g', 'w_ukv', 'qk_q_g', 'qk_k_g', 'sg_ln_g', 'sg_ln_b', 'sg_w', 'sg_b', 'branch_norm_g', 'w_out']
TWIN_WEIGHTS = ['norm_g', 'w_in', 'conv_w', 'conv_b', 'conv_ln_g', 'conv_ln_b', 'conv_pw_w', 'conv_pw_b', 'q_norm_g', 'w_uq', 'kv_norm_g', 'w_ukv', 'qk_q_g', 'qk_k_g', 'sg_ln_g', 'sg_ln_b', 'sg_w', 'sg_b', 'branch_norm_g', 'w_out']
TWIN_DIFF_INPUT = 'x'
TWIN_INPUTS = ['x', 'norm_g', 'w_in', 'conv_w', 'conv_b', 'conv_ln_g', 'conv_ln_b', 'conv_pw_w', 'conv_pw_b', 'q_norm_g', 'w_uq', 'kv_norm_g', 'w_ukv', 'qk_q_g', 'qk_k_g', 'sg_ln_g', 'sg_ln_b', 'sg_w', 'sg_b', 'branch_norm_g', 'w_out', 'loss_target', 'm_norm_g', 'm_w_in', 'm_conv_w', 'm_conv_b', 'm_conv_ln_g', 'm_conv_ln_b', 'm_conv_pw_w', 'm_conv_pw_b', 'm_q_norm_g', 'm_w_uq', 'm_kv_norm_g', 'm_w_ukv', 'm_qk_q_g', 'm_qk_k_g', 'm_sg_ln_g', 'm_sg_ln_b', 'm_sg_w', 'm_sg_b', 'm_branch_norm_g', 'm_w_out', 'v_norm_g', 'v_w_in', 'v_conv_w', 'v_conv_b', 'v_conv_ln_g', 'v_conv_ln_b', 'v_conv_pw_w', 'v_conv_pw_b', 'v_q_norm_g', 'v_w_uq', 'v_kv_norm_g', 'v_w_ukv', 'v_qk_q_g', 'v_qk_k_g', 'v_sg_ln_g', 'v_sg_ln_b', 'v_sg_w', 'v_sg_b', 'v_branch_norm_g', 'v_w_out']
TWIN_OUTPUTS = ['loss', 'grad_x', 'grad_norm_g', 'grad_w_in', 'grad_conv_w', 'grad_conv_b', 'grad_conv_ln_g', 'grad_conv_ln_b', 'grad_conv_pw_w', 'grad_conv_pw_b', 'grad_q_norm_g', 'grad_w_uq', 'grad_kv_norm_g', 'grad_w_ukv', 'grad_qk_q_g', 'grad_qk_k_g', 'grad_sg_ln_g', 'grad_sg_ln_b', 'grad_sg_w', 'grad_sg_b', 'grad_branch_norm_g', 'grad_w_out', 'delta_norm_g', 'delta_w_in', 'delta_conv_w', 'delta_conv_b', 'delta_conv_ln_g', 'delta_conv_ln_b', 'delta_conv_pw_w', 'delta_conv_pw_b', 'delta_q_norm_g', 'delta_w_uq', 'delta_kv_norm_g', 'delta_w_ukv', 'delta_qk_q_g', 'delta_qk_k_g', 'delta_sg_ln_g', 'delta_sg_ln_b', 'delta_sg_w', 'delta_sg_b', 'delta_branch_norm_g', 'delta_w_out', 'new_m_norm_g', 'new_m_w_in', 'new_m_conv_w', 'new_m_conv_b', 'new_m_conv_ln_g', 'new_m_conv_ln_b', 'new_m_conv_pw_w', 'new_m_conv_pw_b', 'new_m_q_norm_g', 'new_m_w_uq', 'new_m_kv_norm_g', 'new_m_w_ukv', 'new_m_qk_q_g', 'new_m_qk_k_g', 'new_m_sg_ln_g', 'new_m_sg_ln_b', 'new_m_sg_w', 'new_m_sg_b', 'new_m_branch_norm_g', 'new_m_w_out', 'new_v_norm_g', 'new_v_w_in', 'new_v_conv_w', 'new_v_conv_b', 'new_v_conv_ln_g', 'new_v_conv_ln_b', 'new_v_conv_pw_w', 'new_v_conv_pw_b', 'new_v_q_norm_g', 'new_v_w_uq', 'new_v_kv_norm_g', 'new_v_w_ukv', 'new_v_qk_q_g', 'new_v_qk_k_g', 'new_v_sg_ln_g', 'new_v_sg_ln_b', 'new_v_sg_w', 'new_v_sg_b', 'new_v_branch_norm_g', 'new_v_w_out']
TWIN_LEAF_KINDS = {'loss': 'loss', 'grad_x': 'grad_x', 'grad_norm_g': 'grad_w', 'grad_w_in': 'grad_w', 'grad_conv_w': 'grad_w', 'grad_conv_b': 'grad_w', 'grad_conv_ln_g': 'grad_w', 'grad_conv_ln_b': 'grad_w', 'grad_conv_pw_w': 'grad_w', 'grad_conv_pw_b': 'grad_w', 'grad_q_norm_g': 'grad_w', 'grad_w_uq': 'grad_w', 'grad_kv_norm_g': 'grad_w', 'grad_w_ukv': 'grad_w', 'grad_qk_q_g': 'grad_w', 'grad_qk_k_g': 'grad_w', 'grad_sg_ln_g': 'grad_w', 'grad_sg_ln_b': 'grad_w', 'grad_sg_w': 'grad_w', 'grad_sg_b': 'grad_w', 'grad_branch_norm_g': 'grad_w', 'grad_w_out': 'grad_w', 'delta_norm_g': 'delta_w', 'delta_w_in': 'delta_w', 'delta_conv_w': 'delta_w', 'delta_conv_b': 'delta_w', 'delta_conv_ln_g': 'delta_w', 'delta_conv_ln_b': 'delta_w', 'delta_conv_pw_w': 'delta_w', 'delta_conv_pw_b': 'delta_w', 'delta_q_norm_g': 'delta_w', 'delta_w_uq': 'delta_w', 'delta_kv_norm_g': 'delta_w', 'delta_w_ukv': 'delta_w', 'delta_qk_q_g': 'delta_w', 'delta_qk_k_g': 'delta_w', 'delta_sg_ln_g': 'delta_w', 'delta_sg_ln_b': 'delta_w', 'delta_sg_w': 'delta_w', 'delta_sg_b': 'delta_w', 'delta_branch_norm_g': 'delta_w', 'delta_w_out': 'delta_w', 'new_m_norm_g': 'new_m', 'new_m_w_in': 'new_m', 'new_m_conv_w': 'new_m', 'new_m_conv_b': 'new_m', 'new_m_conv_ln_g': 'new_m', 'new_m_conv_ln_b': 'new_m', 'new_m_conv_pw_w': 'new_m', 'new_m_conv_pw_b': 'new_m', 'new_m_q_norm_g': 'new_m', 'new_m_w_uq': 'new_m', 'new_m_kv_norm_g': 'new_m', 'new_m_w_ukv': 'new_m', 'new_m_qk_q_g': 'new_m', 'new_m_qk_k_g': 'new_m', 'new_m_sg_ln_g': 'new_m', 'new_m_sg_ln_b': 'new_m', 'new_m_sg_w': 'new_m', 'new_m_sg_b': 'new_m', 'new_m_branch_norm_g': 'new_m', 'new_m_w_out': 'new_m', 'new_v_norm_g': 'new_v', 'new_v_w_in': 'new_v', 'new_v_conv_w': 'new_v', 'new_v_conv_b': 'new_v', 'new_v_conv_ln_g': 'new_v', 'new_v_conv_ln_b': 'new_v', 'new_v_conv_pw_w': 'new_v', 'new_v_conv_pw_b': 'new_v', 'new_v_q_norm_g': 'new_v', 'new_v_w_uq': 'new_v', 'new_v_kv_norm_g': 'new_v', 'new_v_w_ukv': 'new_v', 'new_v_qk_q_g': 'new_v', 'new_v_qk_k_g': 'new_v', 'new_v_sg_ln_g': 'new_v', 'new_v_sg_ln_b': 'new_v', 'new_v_sg_w': 'new_v', 'new_v_sg_b': 'new_v', 'new_v_branch_norm_g': 'new_v', 'new_v_w_out': 'new_v'}


def _forward(args):
    return _fwd_reference(*[args[k] for k in FWD_PARAMS])


def _output_shape():
    def fwd():
        inp = _fwd_setup_inputs(0)
        return _fwd_reference(*[inp[k] for k in FWD_PARAMS])
    out = _jax.eval_shape(fwd)
    return out.shape, out.dtype

N_MICROBATCH = 1
ADAM_LR = 0.001
ADAM_B1 = 0.9
ADAM_B2 = 0.999
ADAM_EPS = 1e-08
ADAM_WD = 0.01
ADAM_STEP = 10
PER_EXAMPLE_BATCH_AXIS = {'x': 0, 'loss_target': 0}
SHARED_INPUTS = []
_WEIGHT_DTYPES = {'norm_g': _jnp.float32, 'w_in': _jnp.float32, 'conv_w': _jnp.float32, 'conv_b': _jnp.float32, 'conv_ln_g': _jnp.float32, 'conv_ln_b': _jnp.float32, 'conv_pw_w': _jnp.float32, 'conv_pw_b': _jnp.float32, 'q_norm_g': _jnp.float32, 'w_uq': _jnp.float32, 'kv_norm_g': _jnp.float32, 'w_ukv': _jnp.float32, 'qk_q_g': _jnp.float32, 'qk_k_g': _jnp.float32, 'sg_ln_g': _jnp.float32, 'sg_ln_b': _jnp.float32, 'sg_w': _jnp.float32, 'sg_b': _jnp.float32, 'branch_norm_g': _jnp.float32, 'w_out': _jnp.float32}
MOMENT_SCALE = {'norm_g': 1.524336e+00, 'w_in': 8.610377e-01, 'conv_w': 7.826627e-01, 'conv_b': 1.003037e+01, 'conv_ln_g': 3.463237e+00, 'conv_ln_b': 6.065275e+00, 'conv_pw_w': 1.849813e+00, 'conv_pw_b': 1.184538e+01, 'q_norm_g': 5.064986e-01, 'w_uq': 5.302392e-01, 'kv_norm_g': 3.077268e+00, 'w_ukv': 1.304614e+00, 'qk_q_g': 1.412316e+00, 'qk_k_g': 1.387820e+00, 'sg_ln_g': 5.347480e-01, 'sg_ln_b': 3.730734e-01, 'sg_w': 2.602580e-01, 'sg_b': 3.731286e-01, 'branch_norm_g': 6.585814e+01, 'w_out': 2.460789e+00}


def _to_microbatches(a, axis):
    t = _jnp.moveaxis(a, axis, 0)
    t = t.reshape((N_MICROBATCH, t.shape[0] // N_MICROBATCH) + t.shape[1:])
    return _jnp.moveaxis(t, 1, axis + 1)


def setup_inputs(seed: int = 0) -> dict:
    inp = _fwd_setup_inputs(seed)
    key = _jax.random.fold_in(_jax.random.key(seed), 7919)
    shape, _ = _output_shape()
    out = dict(inp)
    out["loss_target"] = _jax.random.normal(_jax.random.fold_in(key, 0), shape, _jnp.float32)
    for i, name in enumerate(TWIN_WEIGHTS):
        w = inp[name].astype(_jnp.float32)
        if MOMENT_SCALE is None:
            s = _jnp.sqrt(_jnp.mean(_jnp.square(w)) + 1e-30)
        else:
            s = MOMENT_SCALE[name]
        km, kv = _jax.random.split(_jax.random.fold_in(key, i + 1))
        out[name] = w
        out["m_" + name] = s * _jax.random.normal(km, w.shape, _jnp.float32)
        out["v_" + name] = (s * s) * _jax.random.uniform(kv, w.shape, _jnp.float32, 0.5, 1.5)
    if N_MICROBATCH > 1:
        for name, axis in PER_EXAMPLE_BATCH_AXIS.items():
            out[name] = _to_microbatches(out[name], axis)
    return {'x': out['x'], 'norm_g': out['norm_g'], 'w_in': out['w_in'], 'conv_w': out['conv_w'], 'conv_b': out['conv_b'], 'conv_ln_g': out['conv_ln_g'], 'conv_ln_b': out['conv_ln_b'], 'conv_pw_w': out['conv_pw_w'], 'conv_pw_b': out['conv_pw_b'], 'q_norm_g': out['q_norm_g'], 'w_uq': out['w_uq'], 'kv_norm_g': out['kv_norm_g'], 'w_ukv': out['w_ukv'], 'qk_q_g': out['qk_q_g'], 'qk_k_g': out['qk_k_g'], 'sg_ln_g': out['sg_ln_g'], 'sg_ln_b': out['sg_ln_b'], 'sg_w': out['sg_w'], 'sg_b': out['sg_b'], 'branch_norm_g': out['branch_norm_g'], 'w_out': out['w_out'], 'loss_target': out['loss_target'], 'm_norm_g': out['m_norm_g'], 'm_w_in': out['m_w_in'], 'm_conv_w': out['m_conv_w'], 'm_conv_b': out['m_conv_b'], 'm_conv_ln_g': out['m_conv_ln_g'], 'm_conv_ln_b': out['m_conv_ln_b'], 'm_conv_pw_w': out['m_conv_pw_w'], 'm_conv_pw_b': out['m_conv_pw_b'], 'm_q_norm_g': out['m_q_norm_g'], 'm_w_uq': out['m_w_uq'], 'm_kv_norm_g': out['m_kv_norm_g'], 'm_w_ukv': out['m_w_ukv'], 'm_qk_q_g': out['m_qk_q_g'], 'm_qk_k_g': out['m_qk_k_g'], 'm_sg_ln_g': out['m_sg_ln_g'], 'm_sg_ln_b': out['m_sg_ln_b'], 'm_sg_w': out['m_sg_w'], 'm_sg_b': out['m_sg_b'], 'm_branch_norm_g': out['m_branch_norm_g'], 'm_w_out': out['m_w_out'], 'v_norm_g': out['v_norm_g'], 'v_w_in': out['v_w_in'], 'v_conv_w': out['v_conv_w'], 'v_conv_b': out['v_conv_b'], 'v_conv_ln_g': out['v_conv_ln_g'], 'v_conv_ln_b': out['v_conv_ln_b'], 'v_conv_pw_w': out['v_conv_pw_w'], 'v_conv_pw_b': out['v_conv_pw_b'], 'v_q_norm_g': out['v_q_norm_g'], 'v_w_uq': out['v_w_uq'], 'v_kv_norm_g': out['v_kv_norm_g'], 'v_w_ukv': out['v_w_ukv'], 'v_qk_q_g': out['v_qk_q_g'], 'v_qk_k_g': out['v_qk_k_g'], 'v_sg_ln_g': out['v_sg_ln_g'], 'v_sg_ln_b': out['v_sg_ln_b'], 'v_sg_w': out['v_sg_w'], 'v_sg_b': out['v_sg_b'], 'v_branch_norm_g': out['v_branch_norm_g'], 'v_w_out': out['v_w_out']}


def _loss(weights, diff, rest, loss_target):
    with _jax.named_scope("forward"):
        args = {**rest, TWIN_DIFF_INPUT: diff, **{k: w.astype(_WEIGHT_DTYPES[k]) for k, w in weights.items()}}
        y = _forward(args)
    with _jax.named_scope("loss_head"):
        err = _jnp.square(y.astype(_jnp.float32) - loss_target)
        return 0.5 * _jnp.sum(_jnp.mean(err, axis=-1)) if err.ndim else 0.5 * err


def _adamw(w, g, m, v):
    m = ADAM_B1 * m + (1.0 - ADAM_B1) * g
    v = ADAM_B2 * v + (1.0 - ADAM_B2) * _jnp.square(g)
    m_hat = m / (1.0 - ADAM_B1 ** ADAM_STEP)
    v_hat = v / (1.0 - ADAM_B2 ** ADAM_STEP)
    delta = -ADAM_LR * (m_hat / (_jnp.sqrt(v_hat) + ADAM_EPS) + ADAM_WD * w)
    return delta, m, v


def reference(x, norm_g, w_in, conv_w, conv_b, conv_ln_g, conv_ln_b, conv_pw_w, conv_pw_b, q_norm_g, w_uq, kv_norm_g, w_ukv, qk_q_g, qk_k_g, sg_ln_g, sg_ln_b, sg_w, sg_b, branch_norm_g, w_out, loss_target, m_norm_g, m_w_in, m_conv_w, m_conv_b, m_conv_ln_g, m_conv_ln_b, m_conv_pw_w, m_conv_pw_b, m_q_norm_g, m_w_uq, m_kv_norm_g, m_w_ukv, m_qk_q_g, m_qk_k_g, m_sg_ln_g, m_sg_ln_b, m_sg_w, m_sg_b, m_branch_norm_g, m_w_out, v_norm_g, v_w_in, v_conv_w, v_conv_b, v_conv_ln_g, v_conv_ln_b, v_conv_pw_w, v_conv_pw_b, v_q_norm_g, v_w_uq, v_kv_norm_g, v_w_ukv, v_qk_q_g, v_qk_k_g, v_sg_ln_g, v_sg_ln_b, v_sg_w, v_sg_b, v_branch_norm_g, v_w_out):
    given = dict(x=x, norm_g=norm_g, w_in=w_in, conv_w=conv_w, conv_b=conv_b, conv_ln_g=conv_ln_g, conv_ln_b=conv_ln_b, conv_pw_w=conv_pw_w, conv_pw_b=conv_pw_b, q_norm_g=q_norm_g, w_uq=w_uq, kv_norm_g=kv_norm_g, w_ukv=w_ukv, qk_q_g=qk_q_g, qk_k_g=qk_k_g, sg_ln_g=sg_ln_g, sg_ln_b=sg_ln_b, sg_w=sg_w, sg_b=sg_b, branch_norm_g=branch_norm_g, w_out=w_out, loss_target=loss_target, m_norm_g=m_norm_g, m_w_in=m_w_in, m_conv_w=m_conv_w, m_conv_b=m_conv_b, m_conv_ln_g=m_conv_ln_g, m_conv_ln_b=m_conv_ln_b, m_conv_pw_w=m_conv_pw_w, m_conv_pw_b=m_conv_pw_b, m_q_norm_g=m_q_norm_g, m_w_uq=m_w_uq, m_kv_norm_g=m_kv_norm_g, m_w_ukv=m_w_ukv, m_qk_q_g=m_qk_q_g, m_qk_k_g=m_qk_k_g, m_sg_ln_g=m_sg_ln_g, m_sg_ln_b=m_sg_ln_b, m_sg_w=m_sg_w, m_sg_b=m_sg_b, m_branch_norm_g=m_branch_norm_g, m_w_out=m_w_out, v_norm_g=v_norm_g, v_w_in=v_w_in, v_conv_w=v_conv_w, v_conv_b=v_conv_b, v_conv_ln_g=v_conv_ln_g, v_conv_ln_b=v_conv_ln_b, v_conv_pw_w=v_conv_pw_w, v_conv_pw_b=v_conv_pw_b, v_q_norm_g=v_q_norm_g, v_w_uq=v_w_uq, v_kv_norm_g=v_kv_norm_g, v_w_ukv=v_w_ukv, v_qk_q_g=v_qk_q_g, v_qk_k_g=v_qk_k_g, v_sg_ln_g=v_sg_ln_g, v_sg_ln_b=v_sg_ln_b, v_sg_w=v_sg_w, v_sg_b=v_sg_b, v_branch_norm_g=v_branch_norm_g, v_w_out=v_w_out)
    weights = {n: given[n] for n in TWIN_WEIGHTS}
    shared = {n: given[n] for n in SHARED_INPUTS}
    per_example = {n: given[n] for n in ['x']}
    grad_fn = _jax.value_and_grad(_loss, argnums=(0, 1))

    def one_microbatch(ex, loss_target):
        ex = dict(ex)
        diff = ex.pop(TWIN_DIFF_INPUT)
        return grad_fn(weights, diff, {**shared, **ex}, loss_target)

    if N_MICROBATCH == 1:
        loss, (grad_w, grad_x) = one_microbatch(per_example, given["loss_target"])
    else:
        def body(carry, xs):
            loss_sum, grad_sum = carry
            l_k, (gw_k, gx_k) = one_microbatch(xs[0], xs[1])
            with _jax.named_scope("update"):
                return (loss_sum + l_k, _jax.tree.map(_jnp.add, grad_sum, gw_k)), gx_k

        init = (_jnp.zeros((), _jnp.float32), _jax.tree.map(_jnp.zeros_like, weights))
        (loss, grad_w), grad_x = _jax.lax.scan(body, init, (per_example, given["loss_target"]))
    with _jax.named_scope("update"):
        delta_w, new_m, new_v = {}, {}, {}
        for n in TWIN_WEIGHTS:
            delta_w[n], new_m[n], new_v[n] = _adamw(weights[n], grad_w[n], given["m_" + n], given["v_" + n])
    return (loss, grad_x, *[grad_w[n] for n in TWIN_WEIGHTS], *[delta_w[n] for n in TWIN_WEIGHTS],
            *[new_m[n] for n in TWIN_WEIGHTS], *[new_v[n] for n in TWIN_WEIGHTS])
```

```python
import functools
import math

import jax
import jax.numpy as jnp
from jax import lax
from jax.experimental import pallas as pl
from jax.experimental.pallas import tpu as pltpu

F32 = jnp.float32
BF16 = jnp.bfloat16

N_DEV = 8
DEPTH = 2
D_MODEL = 1024
CONV_W = 256
CONV_K = 31
HEADS = 8
NOPE = 64
ROPE = 32
QK = NOPE + ROPE
HEAD_PAD = 128
V_DIM = 64
MLA_W = HEADS * V_DIM
Q_LORA = 768
KV_LORA = 256
SG_W = 256
SG_HEADS = 4
SG_CHUNK = 128
ROPE_THETA = 10000.0
EPS = 1e-6
IN_COLS = 3104
NP = 3200
C_A, C_CQ, C_ZM, C_CKV, C_SG, C_KR = 0, 768, 1536, 2048, 2304, 3072
HALO = 32
SUB = 64
NEG = -1e30
LANES = 128
VMEM_LIMIT_V7X = 52 * 1024 * 1024

ADAM_LR = 0.001
ADAM_B1 = 0.9
ADAM_B2 = 0.999
ADAM_EPS = 1e-08
ADAM_WD = 0.01
ADAM_STEP = 10

W_NAMES = ['norm_g', 'w_in', 'conv_w', 'conv_b', 'conv_ln_g', 'conv_ln_b', 'conv_pw_w', 'conv_pw_b',
           'q_norm_g', 'w_uq', 'kv_norm_g', 'w_ukv', 'qk_q_g', 'qk_k_g', 'sg_ln_g', 'sg_ln_b', 'sg_w',
           'sg_b', 'branch_norm_g', 'w_out']
SHARD_AXIS = {'w_in': 2, 'conv_w': 2, 'conv_pw_w': 1, 'w_uq': 1, 'w_ukv': 2, 'w_out': 1}
SHARDED = [n for n in W_NAMES if n in SHARD_AXIS]
REPL = [n for n in W_NAMES if n not in SHARD_AXIS]


def _tile(s):
    for t in (512, 256, 128):
        if s % t == 0 and s // t >= 2:
            return t
    return s


def _cp(sem):
    return pltpu.CompilerParams(dimension_semantics=sem, vmem_limit_bytes=VMEM_LIMIT_V7X)


def _mm(a, b):
    return jnp.dot(a.astype(BF16), b.astype(BF16), preferred_element_type=F32)


def _mm_nt(a, b):
    return lax.dot_general(a.astype(BF16), b.astype(BF16), (((1,), (1,)), ((), ())),
                           preferred_element_type=F32)


def _mm_tn(a, b):
    return lax.dot_general(a.astype(BF16), b.astype(BF16), (((0,), (0,)), ((), ())),
                           preferred_element_type=F32)


_GC = math.sqrt(2.0 / math.pi)
_GA = 0.044715


def _sig(x):
    return 1.0 / (1.0 + jnp.exp(-x))


def _silu(x):
    return x * _sig(x)


def _dsilu(x):
    s = _sig(x)
    return s * (1.0 + x * (1.0 - s))


def _gelu(x):
    return 0.5 * x * (1.0 + jnp.tanh(_GC * (x + _GA * x * x * x)))


def _dgelu(x):
    t = jnp.tanh(_GC * (x + _GA * x * x * x))
    return 0.5 * (1.0 + t) + 0.5 * x * (1.0 - t * t) * _GC * (1.0 + 3.0 * _GA * x * x)


def _rsum(x):
    return jnp.sum(x, axis=-1, keepdims=True)


def _csum(x):
    return jnp.sum(x, axis=0, keepdims=True)


def _rms_fwd(x, n):
    r = lax.rsqrt(_rsum(x * x) * (1.0 / n) + EPS)
    return x * r, r


def _rms_bwd(dxh, xn, r, n):
    return r * (dxh - xn * (_rsum(dxh * xn) * (1.0 / n)))


def _ln_fwd(x, n):
    mu = _rsum(x) * (1.0 / n)
    xc = x - mu
    r = lax.rsqrt(_rsum(xc * xc) * (1.0 / n) + EPS)
    return xc * r, r


def _ln_bwd(dxh, xh, r, n):
    return r * (dxh - _rsum(dxh) * (1.0 / n) - xh * (_rsum(dxh * xh) * (1.0 / n)))


def _partner(x, lane):
    return jnp.where(lane < NOPE + ROPE // 2, pltpu.roll(x, LANES - ROPE // 2, 1), pltpu.roll(x, ROPE // 2, 1))


def _row_spec(ts, w, col=0):
    return pl.BlockSpec((ts, w), lambda i, col=col: (i, col))


def _full_spec(shape):
    nd = len(shape)
    return pl.BlockSpec(shape, lambda i, nd=nd: (0,) * nd)


PROJ_CHUNK = 640


def _proj_fwd(x, ng, win_p, name):
    s = x.shape[0]
    ts = _tile(s)

    def body(x_ref, g_ref, w_ref, o_ref):
        xv = x_ref[...]
        xn, _ = _rms_fwd(xv, D_MODEL)
        h = (xn * g_ref[...]).astype(BF16)
        for c in range(0, NP, PROJ_CHUNK):
            o_ref[:, c:c + PROJ_CHUNK] = jnp.dot(h, w_ref[:, c:c + PROJ_CHUNK], preferred_element_type=F32)

    return pl.pallas_call(
        body, name=name, grid=(s // ts,),
        in_specs=[_row_spec(ts, D_MODEL), _full_spec((1, D_MODEL)), _full_spec((D_MODEL, NP))],
        out_specs=_row_spec(ts, NP),
        out_shape=jax.ShapeDtypeStruct((s, NP), F32),
        compiler_params=_cp(("parallel",)),
    )(x, ng, win_p)


def _proj_bwd(x, ng, win_p, d_out, pieces, name):
    s = x.shape[0]
    ts = _tile(s)
    offs = [o for _, o in pieces]
    widths = [p.shape[1] for p, _ in pieces]

    def body(x_ref, g_ref, w_ref, do_ref, *rest):
        p_refs = rest[:len(pieces)]
        dx_ref, h_ref, gg_ref = rest[len(pieces):]
        i = pl.program_id(0)
        xv = x_ref[...]
        xn, r = _rms_fwd(xv, D_MODEL)
        g = g_ref[...]
        h_ref[...] = (xn * g).astype(BF16)
        dh = jnp.zeros((ts, D_MODEL), F32)
        for p_ref, off, w in zip(p_refs, offs, widths):
            dh = dh + _mm_nt(p_ref[...], w_ref[:, off:off + w])

        @pl.when(i == 0)
        def _():
            gg_ref[...] = jnp.zeros_like(gg_ref)

        gg_ref[...] += _csum(dh * xn)
        dx_ref[...] = _rms_bwd(dh * g, xn, r, D_MODEL) + do_ref[...]

    in_specs = [_row_spec(ts, D_MODEL), _full_spec((1, D_MODEL)), _full_spec((D_MODEL, NP)), _row_spec(ts, D_MODEL)]
    in_specs += [_row_spec(ts, w) for w in widths]
    return pl.pallas_call(
        body, name=name, grid=(s // ts,),
        in_specs=in_specs,
        out_specs=[_row_spec(ts, D_MODEL), _row_spec(ts, D_MODEL), _full_spec((1, D_MODEL))],
        out_shape=[jax.ShapeDtypeStruct((s, D_MODEL), F32), jax.ShapeDtypeStruct((s, D_MODEL), BF16),
                   jax.ShapeDtypeStruct((1, D_MODEL), F32)],
        compiler_params=_cp(("arbitrary",)),
    )(x, ng, win_p, d_out, *[p for p, _ in pieces])


def _tn_acc(a, b, name):
    s, m = a.shape
    n = b.shape[1]
    ts = _tile(s)

    def body(a_ref, b_ref, o_ref):
        @pl.when(pl.program_id(0) == 0)
        def _():
            o_ref[...] = jnp.zeros_like(o_ref)

        o_ref[...] += _mm_tn(a_ref[...], b_ref[...])

    return pl.pallas_call(
        body, name=name, grid=(s // ts,),
        in_specs=[_row_spec(ts, m), _row_spec(ts, n)],
        out_specs=_full_spec((m, n)),
        out_shape=jax.ShapeDtypeStruct((m, n), F32),
        compiler_params=_cp(("arbitrary",)),
    )(a, b)


def _halo_spec(ts):
    per = ts // HALO
    return pl.BlockSpec((HALO, 2 * CONV_W), lambda i: (jnp.maximum(i * per - 1, 0), 0))


def _conv_taps(ext_ref, cw_ref, cv_ref, cb, ts):
    base = HALO - (CONV_K - 1)
    for r0 in range(0, ts, SUB):
        acc = jnp.zeros((SUB, CONV_W), F32)
        for k in range(CONV_K):
            acc = acc + cw_ref[k:k + 1, :] * ext_ref[r0 + base + k:r0 + base + k + SUB, :]
        cv_ref[r0:r0 + SUB, :] = acc + cb


def _conv_fwd(proj, cw, cb, lg, lb, pww, pwb, name):
    s = proj.shape[0]
    ts = _tile(s)

    def body(pa_ref, ph_ref, cw_ref, cb_ref, lg_ref, lb_ref, pww_ref, pwb_ref, y_ref, ext_ref, cv_ref):
        i = pl.program_id(0)
        pa = pa_ref[...]
        a, ag, zc = pa[:, :CONV_W], pa[:, CONV_W:2 * CONV_W], pa[:, 2 * CONV_W:]
        ph = ph_ref[...]
        hglu = ph[:, :CONV_W] * _sig(ph[:, CONV_W:])
        ext_ref[0:HALO, :] = jnp.where(i > 0, hglu, 0.0)
        ext_ref[HALO:HALO + ts, :] = a * _sig(ag)
        _conv_taps(ext_ref, cw_ref, cv_ref, cb_ref[...], ts)
        xh, _ = _ln_fwd(cv_ref[...], CONV_W)
        ln = xh * lg_ref[...] + lb_ref[...]
        pw = _mm(_silu(ln), pww_ref[...]) + pwb_ref[...]
        y_ref[...] = pw * _silu(zc)

    vec = _full_spec((1, CONV_W))
    return pl.pallas_call(
        body, name=name, grid=(s // ts,),
        in_specs=[_row_spec(ts, 3 * CONV_W, 0), _halo_spec(ts), _full_spec((HALO, CONV_W)), vec, vec, vec,
                  _full_spec((CONV_W, CONV_W)), vec],
        out_specs=_row_spec(ts, CONV_W),
        out_shape=jax.ShapeDtypeStruct((s, CONV_W), F32),
        scratch_shapes=[pltpu.VMEM((HALO + ts, CONV_W), F32), pltpu.VMEM((ts, CONV_W), F32)],
        compiler_params=_cp(("parallel",)),
    )(proj, proj, cw, cb, lg, lb, pww, pwb)


def _conv_bwd(proj, dy, cw, cb, lg, lb, pww, pwb, name):
    s = proj.shape[0]
    ts = _tile(s)
    nt = s // ts
    per = ts // HALO

    def body(pa_ref, ph_ref, dy_ref, cw_ref, cb_ref, lg_ref, lb_ref, pww_ref, pwb_ref,
             dp_ref, gcw_ref, gpw_ref, gv_ref, ext_ref, cv_ref, dext_ref, carry_ref, gacc_ref):
        i = pl.program_id(0)
        ti = nt - 1 - i

        @pl.when(i == 0)
        def _():
            carry_ref[...] = jnp.zeros_like(carry_ref)
            gacc_ref[...] = jnp.zeros_like(gacc_ref)
            gpw_ref[...] = jnp.zeros_like(gpw_ref)
            gv_ref[...] = jnp.zeros_like(gv_ref)

        pa = pa_ref[...]
        a, ag, zc = pa[:, :CONV_W], pa[:, CONV_W:2 * CONV_W], pa[:, 2 * CONV_W:]
        sag = _sig(ag)
        ph = ph_ref[...]
        hglu = ph[:, :CONV_W] * _sig(ph[:, CONV_W:])
        ext_ref[0:HALO, :] = jnp.where(ti > 0, hglu, 0.0)
        ext_ref[HALO:HALO + ts, :] = a * sag
        _conv_taps(ext_ref, cw_ref, cv_ref, cb_ref[...], ts)
        xh, rl = _ln_fwd(cv_ref[...], CONV_W)
        lg = lg_ref[...]
        ln = xh * lg + lb_ref[...]
        sw = _silu(ln)
        pww = pww_ref[...]
        pw = _mm(sw, pww) + pwb_ref[...]
        d_y = dy_ref[...]
        d_pw = d_y * _silu(zc)
        d_zc = d_y * pw * _dsilu(zc)
        gpw_ref[...] += _mm_tn(sw, d_pw)
        d_ln = _mm_nt(d_pw, pww) * _dsilu(ln)
        d_cv = _ln_bwd(d_ln * lg, xh, rl, CONV_W)
        gv_ref[0:1, :] += _csum(d_cv)
        gv_ref[1:2, :] += _csum(d_ln * xh)
        gv_ref[2:3, :] += _csum(d_ln)
        gv_ref[3:4, :] += _csum(d_pw)
        dext_ref[0:ts, :] = d_cv
        dext_ref[ts:ts + HALO, :] = carry_ref[...]
        carry_ref[...] = d_cv[0:HALO, :]
        base = HALO - (CONV_K - 1)
        for r0 in range(0, ts, SUB):
            dcv_r = dext_ref[r0:r0 + SUB, :]
            dg = jnp.zeros((SUB, CONV_W), F32)
            for k in range(CONV_K):
                prod = dcv_r * ext_ref[r0 + base + k:r0 + base + k + SUB, :]
                gacc_ref[8 * k:8 * k + 8, :] += jnp.sum(prod.reshape(SUB // 8, 8, CONV_W), axis=0)
                dg = dg + cw_ref[k:k + 1, :] * dext_ref[r0 + CONV_K - 1 - k:r0 + CONV_K - 1 - k + SUB, :]
            dp_ref[r0:r0 + SUB, 0:CONV_W] = dg * sag[r0:r0 + SUB, :]
            dp_ref[r0:r0 + SUB, CONV_W:2 * CONV_W] = dg * a[r0:r0 + SUB, :] * sag[r0:r0 + SUB, :] * (1.0 - sag[r0:r0 + SUB, :])
        dp_ref[:, 2 * CONV_W:] = d_zc

        @pl.when(i == nt - 1)
        def _():
            gcw_ref[...] = jnp.zeros_like(gcw_ref)
            for k in range(CONV_K):
                gcw_ref[k:k + 1, :] = _csum(gacc_ref[8 * k:8 * k + 8, :])

    vec = _full_spec((1, CONV_W))
    rev = lambda w, col=0: pl.BlockSpec((ts, w), lambda i, col=col: (nt - 1 - i, col))
    halo = pl.BlockSpec((HALO, 2 * CONV_W), lambda i: (jnp.maximum((nt - 1 - i) * per - 1, 0), 0))
    return pl.pallas_call(
        body, name=name, grid=(nt,),
        in_specs=[rev(3 * CONV_W), halo, rev(CONV_W), _full_spec((HALO, CONV_W)), vec, vec, vec,
                  _full_spec((CONV_W, CONV_W)), vec],
        out_specs=[rev(3 * CONV_W), _full_spec((HALO, CONV_W)), _full_spec((CONV_W, CONV_W)), _full_spec((8, CONV_W))],
        out_shape=[jax.ShapeDtypeStruct((s, 3 * CONV_W), F32), jax.ShapeDtypeStruct((HALO, CONV_W), F32),
                   jax.ShapeDtypeStruct((CONV_W, CONV_W), F32), jax.ShapeDtypeStruct((8, CONV_W), F32)],
        scratch_shapes=[pltpu.VMEM((HALO + ts, CONV_W), F32), pltpu.VMEM((ts, CONV_W), F32),
                        pltpu.VMEM((ts + HALO, CONV_W), F32), pltpu.VMEM((HALO, CONV_W), F32),
                        pltpu.VMEM((8 * HALO, CONV_W), F32)],
        compiler_params=_cp(("arbitrary",)),
    )(proj, proj, dy, cw, cb, lg, lb, pww, pwb)


def _sg_mix(wm_ref, vc, head):
    out = jnp.zeros((SG_CHUNK, SG_W), F32)
    vb = vc.astype(BF16)
    for g in range(SG_HEADS):
        out = jnp.where(head == g, jnp.dot(wm_ref[g], vb, preferred_element_type=F32), out)
    return out


def _sgu_fwd(proj, lg, lb, wm, sbx, name):
    s = proj.shape[0]
    ts = _tile(s)

    def body(ps_ref, lg_ref, lb_ref, wm_ref, sbx_ref, y_ref, mix_ref):
        ps = ps_ref[...]
        us, vs, zs = ps[:, :SG_W], ps[:, SG_W:2 * SG_W], ps[:, 2 * SG_W:]
        xh, _ = _ln_fwd(_gelu(vs), SG_W)
        vn = xh * lg_ref[...] + lb_ref[...]
        head = lax.broadcasted_iota(jnp.int32, (SG_CHUNK, SG_W), 1) // (SG_W // SG_HEADS)
        for c0 in range(0, ts, SG_CHUNK):
            mix_ref[c0:c0 + SG_CHUNK, :] = _sg_mix(wm_ref, vn[c0:c0 + SG_CHUNK, :], head) + sbx_ref[...]
        y_ref[...] = _gelu(us) * mix_ref[...] * _silu(zs)

    vec = _full_spec((1, SG_W))
    return pl.pallas_call(
        body, name=name, grid=(s // ts,),
        in_specs=[_row_spec(ts, 3 * SG_W, C_SG // (3 * SG_W)), vec, vec,
                  _full_spec((SG_HEADS, SG_CHUNK, SG_CHUNK)), _full_spec((SG_CHUNK, SG_W))],
        out_specs=_row_spec(ts, SG_W),
        out_shape=jax.ShapeDtypeStruct((s, SG_W), F32),
        scratch_shapes=[pltpu.VMEM((ts, SG_W), F32)],
        compiler_params=_cp(("parallel",)),
    )(proj, lg, lb, wm, sbx)


def _sgu_bwd(proj, dy, lg, lb, wm, wmt, sbx, name):
    s = proj.shape[0]
    ts = _tile(s)

    def body(ps_ref, dy_ref, lg_ref, lb_ref, wm_ref, wmt_ref, sbx_ref,
             dp_ref, gwm_ref, dms_ref, gv_ref, mix_ref, dvn_ref):
        i = pl.program_id(0)

        @pl.when(i == 0)
        def _():
            gwm_ref[...] = jnp.zeros_like(gwm_ref)
            dms_ref[...] = jnp.zeros_like(dms_ref)
            gv_ref[...] = jnp.zeros_like(gv_ref)

        ps = ps_ref[...]
        us, vs, zs = ps[:, :SG_W], ps[:, SG_W:2 * SG_W], ps[:, 2 * SG_W:]
        xh, rl = _ln_fwd(_gelu(vs), SG_W)
        lg = lg_ref[...]
        vn = xh * lg + lb_ref[...]
        head = lax.broadcasted_iota(jnp.int32, (SG_CHUNK, SG_W), 1) // (SG_W // SG_HEADS)
        for c0 in range(0, ts, SG_CHUNK):
            mix_ref[c0:c0 + SG_CHUNK, :] = _sg_mix(wm_ref, vn[c0:c0 + SG_CHUNK, :], head) + sbx_ref[...]
        mixed = mix_ref[...]
        u = _gelu(us)
        sz = _silu(zs)
        d_y = dy_ref[...]
        d_mixed = d_y * u * sz
        dp_ref[:, 0:SG_W] = d_y * mixed * sz * _dgelu(us)
        dp_ref[:, 2 * SG_W:] = d_y * u * mixed * _dsilu(zs)
        dms = jnp.zeros((SG_CHUNK, SG_W), F32)
        for c0 in range(0, ts, SG_CHUNK):
            dm = d_mixed[c0:c0 + SG_CHUNK, :]
            vc = vn[c0:c0 + SG_CHUNK, :]
            dms = dms + dm
            for g in range(SG_HEADS):
                gwm_ref[g] += _mm_nt(jnp.where(head == g, dm, 0.0), vc)
            dvn_ref[c0:c0 + SG_CHUNK, :] = _sg_mix(wmt_ref, dm, head)
        dms_ref[...] += dms
        d_vn = dvn_ref[...]
        gv_ref[0:1, :] += _csum(d_vn * xh)
        gv_ref[1:2, :] += _csum(d_vn)
        dp_ref[:, SG_W:2 * SG_W] = _ln_bwd(d_vn * lg, xh, rl, SG_W) * _dgelu(vs)

    vec = _full_spec((1, SG_W))
    wspec = _full_spec((SG_HEADS, SG_CHUNK, SG_CHUNK))
    return pl.pallas_call(
        body, name=name, grid=(s // ts,),
        in_specs=[_row_spec(ts, 3 * SG_W, C_SG // (3 * SG_W)), _row_spec(ts, SG_W), vec, vec, wspec, wspec,
                  _full_spec((SG_CHUNK, SG_W))],
        out_specs=[_row_spec(ts, 3 * SG_W), wspec, _full_spec((SG_CHUNK, SG_W)), _full_spec((8, SG_W))],
        out_shape=[jax.ShapeDtypeStruct((s, 3 * SG_W), F32), jax.ShapeDtypeStruct((SG_HEADS, SG_CHUNK, SG_CHUNK), F32),
                   jax.ShapeDtypeStruct((SG_CHUNK, SG_W), F32), jax.ShapeDtypeStruct((8, SG_W), F32)],
        scratch_shapes=[pltpu.VMEM((ts, SG_W), F32), pltpu.VMEM((ts, SG_W), F32)],
        compiler_params=_cp(("arbitrary",)),
    )(proj, dy, lg, lb, wm, wmt, sbx)


QW = HEADS * HEAD_PAD
KVW = QW + MLA_W
ATT_SCALE = QK ** -0.5


def _mla_specs(ts):
    return [_row_spec(ts, Q_LORA, C_CQ // Q_LORA), _row_spec(ts, KV_LORA, C_CKV // KV_LORA),
            _row_spec(ts, LANES, C_KR // LANES), _row_spec(ts, LANES), _row_spec(ts, LANES),
            _full_spec((1, Q_LORA)), _full_spec((Q_LORA, QW)), _full_spec((1, KV_LORA)), _full_spec((KV_LORA, KVW)),
            _full_spec((1, LANES)), _full_spec((1, LANES))]


def _mla_fwd(proj, rc, rs, qg, wuq, kvg, wukv, gq, gk, name):
    s = proj.shape[0]
    ts = _tile(s)

    def body(cq_ref, ckv_ref, kr_ref, rc_ref, rs_ref, qg_ref, wuq_ref, kvg_ref, wukv_ref, gq_ref, gk_ref,
             q_ref, k_ref, v_ref):
        lane = lax.broadcasted_iota(jnp.int32, (ts, LANES), 1)
        c, sn = rc_ref[...], rs_ref[...]
        cqn, _ = _rms_fwd(cq_ref[...], Q_LORA)
        q0 = _mm(cqn * qg_ref[...], wuq_ref[...])
        gq = gq_ref[...]
        for h in range(HEADS):
            xn, _ = _rms_fwd(q0[:, h * LANES:(h + 1) * LANES], QK)
            qn = xn * gq
            q_ref[:, h * LANES:(h + 1) * LANES] = ((qn * c + _partner(qn, lane) * sn) * ATT_SCALE).astype(BF16)
        ckvn, _ = _rms_fwd(ckv_ref[...], KV_LORA)
        kv = _mm(ckvn * kvg_ref[...], wukv_ref[...])
        kr = pltpu.roll(kr_ref[...], NOPE, 1)
        gk = gk_ref[...]
        for h in range(HEADS):
            xn, _ = _rms_fwd(kv[:, h * LANES:(h + 1) * LANES] + kr, QK)
            kn = xn * gk
            k_ref[:, h * LANES:(h + 1) * LANES] = (kn * c + _partner(kn, lane) * sn).astype(BF16)
        v_ref[...] = kv[:, QW:].astype(BF16)

    return pl.pallas_call(
        body, name=name, grid=(s // ts,),
        in_specs=_mla_specs(ts),
        out_specs=[_row_spec(ts, QW), _row_spec(ts, QW), _row_spec(ts, MLA_W)],
        out_shape=[jax.ShapeDtypeStruct((s, QW), BF16), jax.ShapeDtypeStruct((s, QW), BF16),
                   jax.ShapeDtypeStruct((s, MLA_W), BF16)],
        compiler_params=_cp(("parallel",)),
    )(proj, proj, proj, rc, rs, qg, wuq, kvg, wukv, gq, gk)


def _mla_bwd(proj, rc, rs, qg, wuq, kvg, wukv, gq, gk, dq, dk, dv, name):
    s = proj.shape[0]
    ts = _tile(s)

    def body(cq_ref, ckv_ref, kr_ref, rc_ref, rs_ref, qg_ref, wuq_ref, kvg_ref, wukv_ref, gq_ref, gk_ref,
             dq_ref, dk_ref, dv_ref, dcq_ref, dckv_ref, dkr_ref, gwuq_ref, gwukv_ref, gv_ref, d0_ref):
        i = pl.program_id(0)

        @pl.when(i == 0)
        def _():
            gwuq_ref[...] = jnp.zeros_like(gwuq_ref)
            gwukv_ref[...] = jnp.zeros_like(gwukv_ref)
            gv_ref[...] = jnp.zeros_like(gv_ref)

        lane = lax.broadcasted_iota(jnp.int32, (ts, LANES), 1)
        c, sn = rc_ref[...], rs_ref[...]
        cq = cq_ref[...]
        cqx, rq0 = _rms_fwd(cq, Q_LORA)
        qg = qg_ref[...]
        cqn = cqx * qg
        wuq = wuq_ref[...]
        q0 = _mm(cqn, wuq)
        gq = gq_ref[...]
        ggq = jnp.zeros((1, LANES), F32)
        for h in range(HEADS):
            xn, r = _rms_fwd(q0[:, h * LANES:(h + 1) * LANES], QK)
            d = dq_ref[:, h * LANES:(h + 1) * LANES] * ATT_SCALE
            d_qn = d * c - _partner(d, lane) * sn
            ggq = ggq + _csum(d_qn * xn)
            d0_ref[:, h * LANES:(h + 1) * LANES] = _rms_bwd(d_qn * gq, xn, r, QK)
        dq0 = d0_ref[:, 0:QW]
        gwuq_ref[...] += _mm_tn(cqn, dq0)
        d_cqn = _mm_nt(dq0, wuq)
        gv_ref[0:1, 0:Q_LORA] += _csum(d_cqn * cqx)
        gv_ref[2:3, 0:LANES] += ggq
        dcq_ref[...] = _rms_bwd(d_cqn * qg, cqx, rq0, Q_LORA)
        ckv = ckv_ref[...]
        ckx, rk0 = _rms_fwd(ckv, KV_LORA)
        kvg = kvg_ref[...]
        ckvn = ckx * kvg
        wukv = wukv_ref[...]
        kv = _mm(ckvn, wukv)
        kr = pltpu.roll(kr_ref[...], NOPE, 1)
        gk = gk_ref[...]
        ggk = jnp.zeros((1, LANES), F32)
        dkr = jnp.zeros((ts, LANES), F32)
        for h in range(HEADS):
            xn, r = _rms_fwd(kv[:, h * LANES:(h + 1) * LANES] + kr, QK)
            d = dk_ref[:, h * LANES:(h + 1) * LANES]
            d_kn = d * c - _partner(d, lane) * sn
            ggk = ggk + _csum(d_kn * xn)
            d_k0 = _rms_bwd(d_kn * gk, xn, r, QK)
            dkr = dkr + d_k0
            d0_ref[:, h * LANES:(h + 1) * LANES] = d_k0
        d0_ref[:, QW:KVW] = dv_ref[...]
        dkv = d0_ref[...]
        dkr_ref[...] = jnp.where(lane < ROPE, pltpu.roll(dkr, NOPE, 1), 0.0)
        gwukv_ref[...] += _mm_tn(ckvn, dkv)
        d_ckvn = _mm_nt(dkv, wukv)
        gv_ref[1:2, 0:KV_LORA] += _csum(d_ckvn * ckx)
        gv_ref[3:4, 0:LANES] += ggk
        dckv_ref[...] = _rms_bwd(d_ckvn * kvg, ckx, rk0, KV_LORA)

    return pl.pallas_call(
        body, name=name, grid=(s // ts,),
        in_specs=_mla_specs(ts) + [_row_spec(ts, QW), _row_spec(ts, QW), _row_spec(ts, MLA_W)],
        out_specs=[_row_spec(ts, Q_LORA), _row_spec(ts, KV_LORA), _row_spec(ts, LANES),
                   _full_spec((Q_LORA, QW)), _full_spec((KV_LORA, KVW)), _full_spec((8, QW))],
        out_shape=[jax.ShapeDtypeStruct((s, Q_LORA), F32), jax.ShapeDtypeStruct((s, KV_LORA), F32),
                   jax.ShapeDtypeStruct((s, LANES), F32), jax.ShapeDtypeStruct((Q_LORA, QW), F32),
                   jax.ShapeDtypeStruct((KV_LORA, KVW), F32), jax.ShapeDtypeStruct((8, QW), F32)],
        scratch_shapes=[pltpu.VMEM((ts, KVW), F32)],
        compiler_params=_cp(("arbitrary",)),
    )(proj, proj, proj, rc, rs, qg, wuq, kvg, wukv, gq, gk, dq, dk, dv)


PAIRS = HEADS // 2


def _attn_fwd(q, k, v, name):
    s = q.shape[0]
    tq = _tile(s)
    tk = tq

    def body(q_ref, k_ref, v_ref, o_ref, lse_ref):
        i = pl.program_id(1)
        row = lax.broadcasted_iota(jnp.int32, (tq, tk), 0)
        col = lax.broadcasted_iota(jnp.int32, (tq, tk), 1)
        first = lax.broadcasted_iota(jnp.int32, (tq, LANES), 1) < V_DIM
        outs, lses = [], []
        for a in range(2):
            qa = q_ref[:, a * LANES:(a + 1) * LANES]

            def blk(j, carry, masked, a=a, qa=qa):
                m, l, acc = carry
                st = pl.multiple_of(j * tk, tk)
                sc = _mm_nt(qa, k_ref[pl.ds(st, tk), a * LANES:(a + 1) * LANES])
                if masked:
                    sc = jnp.where(col <= row, sc, NEG)
                m_new = jnp.maximum(m, jnp.max(sc, axis=-1, keepdims=True))
                alpha = jnp.exp(m - m_new)
                p = jnp.exp(sc - m_new)
                l = alpha * l + _rsum(p)
                acc = alpha * acc + jnp.dot(p.astype(BF16), v_ref[pl.ds(st, tk), :], preferred_element_type=F32)
                return m_new, l, acc

            init = (jnp.full((tq, 1), NEG, F32), jnp.zeros((tq, 1), F32), jnp.zeros((tq, LANES), F32))
            carry = lax.fori_loop(0, i, functools.partial(blk, masked=False), init)
            m, l, acc = blk(i, carry, True)
            outs.append(acc / l)
            lses.append(m + jnp.log(l))
        o_ref[...] = jnp.where(first, outs[0], outs[1])
        lse_ref[...] = jnp.where(first, lses[0], lses[1])

    return pl.pallas_call(
        body, name=name, grid=(PAIRS, s // tq),
        in_specs=[pl.BlockSpec((tq, 2 * LANES), lambda p, i: (i, p)),
                  pl.BlockSpec((s, 2 * LANES), lambda p, i: (0, p)),
                  pl.BlockSpec((s, LANES), lambda p, i: (0, p))],
        out_specs=[pl.BlockSpec((tq, LANES), lambda p, i: (i, p)), pl.BlockSpec((tq, LANES), lambda p, i: (i, p))],
        out_shape=[jax.ShapeDtypeStruct((s, MLA_W), F32), jax.ShapeDtypeStruct((s, MLA_W), F32)],
        compiler_params=_cp(("parallel", "parallel")),
    )(q, k, v)


def _attn_bwd(q, k, v, do, stats, name):
    s = q.shape[0]
    tq = _tile(s)
    tk = tq
    nq = s // tq

    def body(q_ref, k_ref, v_ref, do_ref, st_ref, dq_ref, dk_ref, dv_ref):
        j = pl.program_id(1)

        @pl.when(j == 0)
        def _():
            dq_ref[...] = jnp.zeros_like(dq_ref)

        dk_ref[...] = jnp.zeros_like(dk_ref)
        dv_ref[...] = jnp.zeros_like(dv_ref)
        row = lax.broadcasted_iota(jnp.int32, (tq, tk), 0)
        col = lax.broadcasted_iota(jnp.int32, (tq, tk), 1)
        lane = lax.broadcasted_iota(jnp.int32, (tq, LANES), 1)
        vj = v_ref[...]
        for a in range(2):
            ka = k_ref[:, a * LANES:(a + 1) * LANES]
            mine = (lane < V_DIM) if a == 0 else (lane >= V_DIM)

            def blk(i, masked, a=a, ka=ka, mine=mine):
                st = pl.multiple_of(i * tq, tq)
                qa = q_ref[pl.ds(st, tq), a * LANES:(a + 1) * LANES]
                doa = jnp.where(mine, do_ref[pl.ds(st, tq), :], jnp.zeros((), BF16))
                stt = st_ref[pl.ds(st, tq), :]
                lse = stt[:, a * V_DIM:a * V_DIM + 1]
                dl = stt[:, a * V_DIM + V_DIM // 2:a * V_DIM + V_DIM // 2 + 1]
                p = jnp.exp(_mm_nt(qa, ka) - lse)
                if masked:
                    p = jnp.where(col <= row, p, 0.0)
                ds = (p * (_mm_nt(doa, vj) - dl)).astype(BF16)
                dv_ref[...] += _mm_tn(p, doa)
                dk_ref[:, a * LANES:(a + 1) * LANES] += _mm_tn(ds, qa)
                dq_ref[pl.ds(st, tq), a * LANES:(a + 1) * LANES] += jnp.dot(ds, ka, preferred_element_type=F32)

            blk(j, True)

            def loop_body(i, carry, blk=blk):
                blk(i, False)
                return carry

            lax.fori_loop(j + 1, nq, loop_body, 0)

    return pl.pallas_call(
        body, name=name, grid=(PAIRS, s // tk),
        in_specs=[pl.BlockSpec((s, 2 * LANES), lambda p, j: (0, p)),
                  pl.BlockSpec((tk, 2 * LANES), lambda p, j: (j, p)),
                  pl.BlockSpec((tk, LANES), lambda p, j: (j, p)),
                  pl.BlockSpec((s, LANES), lambda p, j: (0, p)),
                  pl.BlockSpec((s, LANES), lambda p, j: (0, p))],
        out_specs=[pl.BlockSpec((s, 2 * LANES), lambda p, j: (0, p)),
                   pl.BlockSpec((tk, 2 * LANES), lambda p, j: (j, p)),
                   pl.BlockSpec((tk, LANES), lambda p, j: (j, p))],
        out_shape=[jax.ShapeDtypeStruct((s, QW), F32), jax.ShapeDtypeStruct((s, QW), F32),
                   jax.ShapeDtypeStruct((s, MLA_W), F32)],
        compiler_params=_cp(("parallel", "arbitrary")),
    )(q, k, v, do, stats)


BR = ((0, CONV_W), (CONV_W, CONV_W + MLA_W), (CONV_W + MLA_W, D_MODEL))


def _post_fwd(x, yc, o, proj, ys, bng, wout, name):
    s = x.shape[0]
    ts = _tile(s)

    def body(x_ref, yc_ref, o_ref, zm_ref, ys_ref, g_ref, w_ref, out_ref):
        ys3 = (yc_ref[...], o_ref[...] * _silu(zm_ref[...]), ys_ref[...])
        acc = x_ref[...]
        for (lo, hi), yb in zip(BR, ys3):
            yn, _ = _rms_fwd(yb, hi - lo)
            acc = acc + _mm(yn * g_ref[:, lo:hi], w_ref[lo:hi, :])
        out_ref[...] = acc

    return pl.pallas_call(
        body, name=name, grid=(s // ts,),
        in_specs=[_row_spec(ts, D_MODEL), _row_spec(ts, CONV_W), _row_spec(ts, MLA_W),
                  _row_spec(ts, MLA_W, C_ZM // MLA_W), _row_spec(ts, SG_W), _full_spec((1, D_MODEL)),
                  _full_spec((D_MODEL, D_MODEL))],
        out_specs=_row_spec(ts, D_MODEL),
        out_shape=jax.ShapeDtypeStruct((s, D_MODEL), F32),
        compiler_params=_cp(("parallel",)),
    )(x, yc, o, proj, ys, bng, wout)


def _post_bwd(d_out, yc, o, lse, proj, ys, bng, wout, name):
    s = d_out.shape[0]
    ts = _tile(s)

    def body(do_ref, yc_ref, o_ref, lse_ref, zm_ref, ys_ref, g_ref, w_ref,
             dyc_ref, dys_ref, dob_ref, dzm_ref, st_ref, gw_ref, gg_ref, yn_ref):
        i = pl.program_id(0)

        @pl.when(i == 0)
        def _():
            gw_ref[...] = jnp.zeros_like(gw_ref)
            gg_ref[...] = jnp.zeros_like(gg_ref)

        d_out_b = do_ref[...].astype(BF16)
        o = o_ref[...]
        zm = zm_ref[...]
        szm = _silu(zm)
        ys3 = (yc_ref[...], o * szm, ys_ref[...])
        d_ys = []
        for (lo, hi), yb in zip(BR, ys3):
            n = hi - lo
            yn, r = _rms_fwd(yb, n)
            g = g_ref[:, lo:hi]
            yn_ref[:, lo:hi] = (yn * g).astype(BF16)
            d_yn = _mm_nt(d_out_b, w_ref[lo:hi, :])
            gg_ref[:, lo:hi] += _csum(d_yn * yn)
            d_ys.append(_rms_bwd(d_yn * g, yn, r, n))
        gw_ref[...] += _mm_tn(yn_ref[...], d_out_b)
        dyc_ref[...] = d_ys[0]
        dys_ref[...] = d_ys[2]
        d_ym = d_ys[1]
        d_o = d_ym * szm
        dob_ref[...] = d_o.astype(BF16)
        dzm_ref[...] = d_ym * o * _dsilu(zm)
        prod = d_o * o
        head = lax.broadcasted_iota(jnp.int32, (ts, MLA_W), 1) // V_DIM
        delta = jnp.zeros((ts, MLA_W), F32)
        for h in range(HEADS):
            delta = jnp.where(head == h, _rsum(jnp.where(head == h, prod, 0.0)), delta)
        lane = lax.broadcasted_iota(jnp.int32, (ts, MLA_W), 1)
        st_ref[...] = jnp.where(lane % V_DIM < V_DIM // 2, lse_ref[...], delta)

    return pl.pallas_call(
        body, name=name, grid=(s // ts,),
        in_specs=[_row_spec(ts, D_MODEL), _row_spec(ts, CONV_W), _row_spec(ts, MLA_W), _row_spec(ts, MLA_W),
                  _row_spec(ts, MLA_W, C_ZM // MLA_W), _row_spec(ts, SG_W), _full_spec((1, D_MODEL)),
                  _full_spec((D_MODEL, D_MODEL))],
        out_specs=[_row_spec(ts, CONV_W), _row_spec(ts, SG_W), _row_spec(ts, MLA_W), _row_spec(ts, MLA_W),
                   _row_spec(ts, MLA_W), _full_spec((D_MODEL, D_MODEL)), _full_spec((1, D_MODEL))],
        out_shape=[jax.ShapeDtypeStruct((s, CONV_W), F32), jax.ShapeDtypeStruct((s, SG_W), F32),
                   jax.ShapeDtypeStruct((s, MLA_W), BF16), jax.ShapeDtypeStruct((s, MLA_W), F32),
                   jax.ShapeDtypeStruct((s, MLA_W), F32), jax.ShapeDtypeStruct((D_MODEL, D_MODEL), F32),
                   jax.ShapeDtypeStruct((1, D_MODEL), F32)],
        scratch_shapes=[pltpu.VMEM((ts, D_MODEL), BF16)],
        compiler_params=_cp(("arbitrary",)),
    )(d_out, yc, o, lse, proj, ys, bng, wout)


def _loss_head(y, target, name):
    s = y.shape[0]
    ts = _tile(s)
    nt = s // ts

    def body(y_ref, t_ref, dy_ref, l_ref, acc_ref):
        i = pl.program_id(0)

        @pl.when(i == 0)
        def _():
            acc_ref[...] = jnp.zeros_like(acc_ref)

        e = y_ref[...] - t_ref[...]
        dy_ref[...] = e * (1.0 / D_MODEL)
        sq = jnp.sum((e * e).reshape(ts // 8, 8, D_MODEL), axis=0)
        part = sq[:, 0:LANES]
        for c in range(LANES, D_MODEL, LANES):
            part = part + sq[:, c:c + LANES]
        acc_ref[...] += part

        @pl.when(i == nt - 1)
        def _():
            tot = jnp.sum(_rsum(acc_ref[...]), axis=0, keepdims=True) * (0.5 / D_MODEL)
            l_ref[...] = jnp.broadcast_to(tot, (8, LANES))

    return pl.pallas_call(
        body, name=name, grid=(nt,),
        in_specs=[_row_spec(ts, D_MODEL), _row_spec(ts, D_MODEL)],
        out_specs=[_row_spec(ts, D_MODEL), _full_spec((8, LANES))],
        out_shape=[jax.ShapeDtypeStruct((s, D_MODEL), F32), jax.ShapeDtypeStruct((8, LANES), F32)],
        scratch_shapes=[pltpu.VMEM((8, LANES), F32)],
        compiler_params=_cp(("arbitrary",)),
    )(y, target)


MESH = pl.DeviceIdType.MESH
ANY = pl.BlockSpec(memory_space=pl.ANY)


def _all_gather(xs, name):
    r = xs.shape[0]

    def body(x_ref, out_ref, send_sems, recv_sems, local_sem):
        x, y, c = lax.axis_index("x"), lax.axis_index("y"), lax.axis_index("c")
        me, sibling = (x, y, c), (x, y, 1 - c)
        chips = [(1 - x, y), (x, 1 - y), (1 - x, 1 - y)]

        def slot(px, py, pc):
            return out_ref.at[4 * px + 2 * py + pc]

        def copy(k, block, to, src=None):
            return pltpu.make_async_remote_copy(
                src_ref=slot(*block) if src is None else src, dst_ref=slot(*block),
                send_sem=send_sems.at[k], recv_sem=recv_sems.at[k], device_id=to, device_id_type=MESH)

        mine = pltpu.make_async_copy(x_ref, slot(*me), local_sem)
        mine.start()
        first = [copy(0, me, sibling, src=x_ref)]
        first += [copy(1 + j, me, (*chip, c), src=x_ref) for j, chip in enumerate(chips)]
        for cp in first:
            cp.start()
        passed = [copy(4 + j, (*chip, c), sibling) for j, chip in enumerate(chips)]
        for j, chip in enumerate(chips):
            copy(1 + j, (*chip, c), me).wait_recv()
            passed[j].start()
        copy(0, sibling, me).wait_recv()
        for j, chip in enumerate(chips):
            copy(4 + j, (*chip, 1 - c), me).wait_recv()
        for cp in first + passed:
            cp.wait_send()
        mine.wait()

    return pl.pallas_call(
        body, name=name,
        out_shape=jax.ShapeDtypeStruct((N_DEV, r, xs.shape[1]), xs.dtype),
        in_specs=[ANY], out_specs=ANY,
        scratch_shapes=[pltpu.SemaphoreType.DMA((7,)), pltpu.SemaphoreType.DMA((7,)), pltpu.SemaphoreType.DMA(())],
    )(xs)


def _grad_exchange(gs, gr, name):
    r, rr = gs.shape[1], gr.shape[0]

    def body(gs_ref, gr_ref, os_ref, or_ref, send_s, recv_s, send_r, recv_r, local_sems):
        x, y, c = lax.axis_index("x"), lax.axis_index("y"), lax.axis_index("c")
        me = 4 * x + 2 * y + c
        l_s = pltpu.make_async_copy(gs_ref.at[me], os_ref.at[me], local_sems.at[0])
        l_r = pltpu.make_async_copy(gr_ref, or_ref.at[me], local_sems.at[1])
        l_s.start()
        l_r.start()
        sends, recvs = [], []
        for k in range(1, N_DEV):
            px = 1 - x if k & 4 else x
            py = 1 - y if k & 2 else y
            pc = 1 - c if k & 1 else c
            peer = 4 * px + 2 * py + pc
            to = (px, py, pc)
            sends.append(pltpu.make_async_remote_copy(
                src_ref=gs_ref.at[peer], dst_ref=os_ref.at[me], send_sem=send_s.at[k - 1], recv_sem=recv_s.at[k - 1],
                device_id=to, device_id_type=MESH))
            sends.append(pltpu.make_async_remote_copy(
                src_ref=gr_ref, dst_ref=or_ref.at[me], send_sem=send_r.at[k - 1], recv_sem=recv_r.at[k - 1],
                device_id=to, device_id_type=MESH))
            recvs.append(pltpu.make_async_remote_copy(
                src_ref=gs_ref.at[peer], dst_ref=os_ref.at[peer], send_sem=send_s.at[k - 1], recv_sem=recv_s.at[k - 1],
                device_id=to, device_id_type=MESH))
            recvs.append(pltpu.make_async_remote_copy(
                src_ref=gr_ref, dst_ref=or_ref.at[peer], send_sem=send_r.at[k - 1], recv_sem=recv_r.at[k - 1],
                device_id=to, device_id_type=MESH))
        for cp in sends:
            cp.start()
        for cp in recvs:
            cp.wait_recv()
        for cp in sends:
            cp.wait_send()
        l_s.wait()
        l_r.wait()

    sem7 = pltpu.SemaphoreType.DMA((N_DEV - 1,))
    return pl.pallas_call(
        body, name=name,
        out_shape=[jax.ShapeDtypeStruct((N_DEV, r, gs.shape[2]), F32), jax.ShapeDtypeStruct((N_DEV, rr, gr.shape[1]), F32)],
        in_specs=[ANY, ANY], out_specs=[ANY, ANY],
        scratch_shapes=[sem7, sem7, sem7, sem7, pltpu.SemaphoreType.DMA((2,))],
    )(gs, gr)


ADAM_ROWS = 128


def _adamw(parts, w, m, v, name):
    r = w.shape[0]
    tr = ADAM_ROWS if r % ADAM_ROWS == 0 else r

    def body(p_ref, w_ref, m_ref, v_ref, g_ref, d_ref, nm_ref, nv_ref):
        g = p_ref[0]
        for sidx in range(1, N_DEV):
            g = g + p_ref[sidx]
        mm = ADAM_B1 * m_ref[...] + (1.0 - ADAM_B1) * g
        vv = ADAM_B2 * v_ref[...] + (1.0 - ADAM_B2) * (g * g)
        m_hat = mm / (1.0 - ADAM_B1 ** ADAM_STEP)
        v_hat = vv / (1.0 - ADAM_B2 ** ADAM_STEP)
        g_ref[...] = g
        d_ref[...] = -ADAM_LR * (m_hat / (jnp.sqrt(v_hat) + ADAM_EPS) + ADAM_WD * w_ref[...])
        nm_ref[...] = mm
        nv_ref[...] = vv

    row = pl.BlockSpec((tr, LANES * 8), lambda i: (i, 0))
    return pl.pallas_call(
        body, name=name, grid=(r // tr,),
        in_specs=[pl.BlockSpec((N_DEV, tr, LANES * 8), lambda i: (0, i, 0)), row, row, row],
        out_specs=[row, row, row, row],
        out_shape=[jax.ShapeDtypeStruct((r, LANES * 8), F32)] * 4,
        compiler_params=_cp(("parallel",)),
    )(parts, w, m, v)


PACK_W = 8 * LANES


def _pack(flat_parts, rows):
    flat = jnp.concatenate([p.reshape(-1) for p in flat_parts])
    return jnp.pad(flat, (0, rows * PACK_W - flat.shape[0])).reshape(rows, PACK_W)


def _pack8(parts8, rows):
    flat = jnp.concatenate([p.reshape(N_DEV, -1) for p in parts8], axis=1)
    return jnp.pad(flat, ((0, 0), (0, rows * PACK_W - flat.shape[1]))).reshape(N_DEV, rows, PACK_W)


def _rows_for(n, mult):
    rows = -(-n // PACK_W)
    return -(-rows // mult) * mult


def _unshard(arr8, axis):
    full = jnp.moveaxis(arr8, 0, axis)
    shp = list(full.shape)
    shp[axis:axis + 2] = [shp[axis] * shp[axis + 1]]
    return full.reshape(shp)


def _split8(full, axis):
    shp = list(full.shape)
    shp[axis:axis + 1] = [N_DEV, shp[axis] // N_DEV]
    return jnp.moveaxis(full.reshape(shp), axis, 0)


def _unpack(flat2d, shapes, lead=()):
    flat = flat2d.reshape(lead + (-1,))
    out, off = [], 0
    for shp in shapes:
        n = math.prod(shp)
        out.append(flat[..., off:off + n].reshape(lead + tuple(shp)))
        off += n
    return out


def _to_layout(w):
    return jnp.concatenate([w[:, :1536], w[:, 1824:2336], w[:, 1536:1792], w[:, 2336:3104], w[:, 1792:1824],
                            jnp.zeros((w.shape[0], NP - IN_COLS), w.dtype)], axis=1)


def _from_layout(g):
    return jnp.concatenate([g[:, :1536], g[:, C_CKV:C_CKV + KV_LORA], g[:, C_KR:C_KR + ROPE],
                            g[:, C_ZM:C_ZM + MLA_W], g[:, C_SG:C_SG + 3 * SG_W]], axis=1)


def _pad_heads(w, real):
    lead = w.shape[:-1]
    w = w.reshape(lead + (HEADS, real))
    return jnp.pad(w, [(0, 0)] * len(lead) + [(0, 0), (0, HEAD_PAD - real)]).reshape(lead + (QW,))


def _rope_tables(s):
    half = ROPE // 2
    inv_freq = ROPE_THETA ** (-jnp.arange(half, dtype=F32) / half)
    ang = jnp.arange(s, dtype=F32)[:, None] * inv_freq[None, :]
    cos, sin = jnp.cos(ang), jnp.sin(ang)
    ones = jnp.ones((s, NOPE), F32)
    zeros = jnp.zeros((s, NOPE), F32)
    pad = jnp.zeros((s, HEAD_PAD - QK), F32)
    rc = jnp.concatenate([ones, cos, cos, pad + 1.0], axis=1)
    rs = jnp.concatenate([zeros, -sin, sin, pad], axis=1)
    return rc, rs


def kernel(x, norm_g, w_in, conv_w, conv_b, conv_ln_g, conv_ln_b, conv_pw_w, conv_pw_b, q_norm_g, w_uq, kv_norm_g, w_ukv, qk_q_g, qk_k_g, sg_ln_g, sg_ln_b, sg_w, sg_b, branch_norm_g, w_out, loss_target, m_norm_g, m_w_in, m_conv_w, m_conv_b, m_conv_ln_g, m_conv_ln_b, m_conv_pw_w, m_conv_pw_b, m_q_norm_g, m_w_uq, m_kv_norm_g, m_w_ukv, m_qk_q_g, m_qk_k_g, m_sg_ln_g, m_sg_ln_b, m_sg_w, m_sg_b, m_branch_norm_g, m_w_out, v_norm_g, v_w_in, v_conv_w, v_conv_b, v_conv_ln_g, v_conv_ln_b, v_conv_pw_w, v_conv_pw_b, v_q_norm_g, v_w_uq, v_kv_norm_g, v_w_ukv, v_qk_q_g, v_qk_k_g, v_sg_ln_g, v_sg_ln_b, v_sg_w, v_sg_b, v_branch_norm_g, v_w_out):
    given = dict(locals())
    wts = {n: given[n] for n in W_NAMES}
    mom_m = {n: given['m_' + n] for n in W_NAMES}
    mom_v = {n: given['v_' + n] for n in W_NAMES}
    s = x.shape[1]
    xs = x.reshape(s, D_MODEL)
    target = loss_target.reshape(s, D_MODEL)

    sh_shapes = [wts[n].shape for n in SHARDED]
    rp_shapes = [wts[n].shape for n in REPL]
    rows_sh = _rows_for(sum(math.prod(p) for p in sh_shapes), ADAM_ROWS)
    rows_rp = _rows_for(sum(math.prod(p) for p in rp_shapes), 8)

    gathered = _all_gather(_pack([wts[n] for n in SHARDED], rows_sh), "weights_all_gather")
    full = {n: _unshard(a, SHARD_AXIS[n]) for n, a in zip(SHARDED, _unpack(gathered, sh_shapes, (N_DEV,)))}
    full.update({n: wts[n] for n in REPL})

    rc, rs = _rope_tables(s)
    tril = jnp.tril(jnp.ones((SG_CHUNK, SG_CHUNK), dtype=bool))

    def vec(a, width=None):
        a = a.reshape(1, -1)
        return a if width is None else jnp.pad(a, ((0, 0), (0, width - a.shape[1])))

    layers = []
    for l in range(DEPTH):
        p = {n: full[n][l] for n in W_NAMES}
        wukv = p['w_ukv'].reshape(KV_LORA, HEADS, NOPE + V_DIM)
        wm = jnp.where(tril[None], p['sg_w'], 0.0)
        layers.append(dict(
            ng=vec(p['norm_g']), win=_to_layout(p['w_in']).astype(BF16),
            cw=jnp.pad(p['conv_w'], ((0, HALO - CONV_K), (0, 0))), cb=vec(p['conv_b']), clg=vec(p['conv_ln_g']),
            clb=vec(p['conv_ln_b']), pww=p['conv_pw_w'].astype(BF16), pwb=vec(p['conv_pw_b']),
            qg=vec(p['q_norm_g']), wuq=_pad_heads(p['w_uq'], QK).astype(BF16), kvg=vec(p['kv_norm_g']),
            wukv=jnp.concatenate([_pad_heads(wukv[:, :, :NOPE].reshape(KV_LORA, HEADS * NOPE), NOPE),
                                  wukv[:, :, NOPE:].reshape(KV_LORA, MLA_W)], axis=1).astype(BF16),
            gq=vec(p['qk_q_g'], LANES), gk=vec(p['qk_k_g'], LANES),
            slg=vec(p['sg_ln_g']), slb=vec(p['sg_ln_b']), wm=wm.astype(BF16),
            wmt=jnp.swapaxes(wm, 1, 2).astype(BF16),
            sbx=jnp.repeat(p['sg_b'].T, SG_W // SG_HEADS, axis=1),
            bng=vec(p['branch_norm_g']), wout=p['w_out'].astype(BF16)))

    acts = []
    h_in = xs
    for l, p in enumerate(layers):
        proj = _proj_fwd(h_in, p['ng'], p['win'], f"proj_fwd_{l}")
        yc = _conv_fwd(proj, p['cw'], p['cb'], p['clg'], p['clb'], p['pww'], p['pwb'], f"conv_fwd_{l}")
        ys = _sgu_fwd(proj, p['slg'], p['slb'], p['wm'], p['sbx'], f"sgu_fwd_{l}")
        q, k, v = _mla_fwd(proj, rc, rs, p['qg'], p['wuq'], p['kvg'], p['wukv'], p['gq'], p['gk'], f"mla_fwd_{l}")
        o, lse = _attn_fwd(q, k, v, f"attn_fwd_{l}")
        h_out = _post_fwd(h_in, yc, o, proj, ys, p['bng'], p['wout'], f"post_fwd_{l}")
        acts.append(dict(x=h_in, proj=proj, yc=yc, ys=ys, q=q, k=k, v=v, o=o, lse=lse))
        h_in = h_out

    d_out, loss_blk = _loss_head(h_in, target, "loss_head")
    loss = lax.psum(loss_blk[0, 0], ("x", "y", "c"))

    grads = {n: [None] * DEPTH for n in W_NAMES}
    for l in reversed(range(DEPTH)):
        p, a = layers[l], acts[l]
        d_yc, d_ys, d_o, d_zm, stats, g_wout, g_bng = _post_bwd(
            d_out, a['yc'], a['o'], a['lse'], a['proj'], a['ys'], p['bng'], p['wout'], f"post_bwd_{l}")
        dq, dk, dv = _attn_bwd(a['q'], a['k'], a['v'], d_o, stats, f"attn_bwd_{l}")
        d_a, g_cw, g_pww, gv_c = _conv_bwd(a['proj'], d_yc, p['cw'], p['cb'], p['clg'], p['clb'], p['pww'], p['pwb'],
                                           f"conv_bwd_{l}")
        d_sg, g_wm, dms, gv_s = _sgu_bwd(a['proj'], d_ys, p['slg'], p['slb'], p['wm'], p['wmt'], p['sbx'],
                                         f"sgu_bwd_{l}")
        d_cq, d_ckv, d_kr, g_wuq, g_wukv, gv_m = _mla_bwd(
            a['proj'], rc, rs, p['qg'], p['wuq'], p['kvg'], p['wukv'], p['gq'], p['gk'], dq, dk, dv, f"mla_bwd_{l}")
        pieces = [(d_a, C_A), (d_cq, C_CQ), (d_zm, C_ZM), (d_ckv, C_CKV), (d_sg, C_SG), (d_kr, C_KR)]
        d_x, h_b, g_ng = _proj_bwd(a['x'], p['ng'], p['win'], d_out, pieces, f"proj_bwd_{l}")
        g_win = jnp.concatenate([_tn_acc(h_b, dp, f"win_grad_{l}_{off}") for dp, off in pieces], axis=1)
        d_out = d_x

        grads['norm_g'][l] = g_ng[0]
        grads['w_in'][l] = _from_layout(g_win)
        grads['conv_w'][l] = g_cw[:CONV_K]
        grads['conv_b'][l] = gv_c[0]
        grads['conv_ln_g'][l] = gv_c[1]
        grads['conv_ln_b'][l] = gv_c[2]
        grads['conv_pw_w'][l] = g_pww
        grads['conv_pw_b'][l] = gv_c[3]
        grads['q_norm_g'][l] = gv_m[0, :Q_LORA]
        grads['w_uq'][l] = g_wuq.reshape(Q_LORA, HEADS, HEAD_PAD)[:, :, :QK].reshape(Q_LORA, HEADS * QK)
        grads['kv_norm_g'][l] = gv_m[1, :KV_LORA]
        grads['w_ukv'][l] = jnp.concatenate(
            [g_wukv[:, :QW].reshape(KV_LORA, HEADS, HEAD_PAD)[:, :, :NOPE],
             g_wukv[:, QW:].reshape(KV_LORA, HEADS, V_DIM)], axis=2).reshape(KV_LORA, HEADS * (NOPE + V_DIM))
        grads['qk_q_g'][l] = gv_m[2, :QK]
        grads['qk_k_g'][l] = gv_m[3, :QK]
        grads['sg_ln_g'][l] = gv_s[0]
        grads['sg_ln_b'][l] = gv_s[1]
        grads['sg_w'][l] = jnp.where(tril[None], g_wm, 0.0)
        grads['sg_b'][l] = dms.reshape(SG_CHUNK, SG_HEADS, SG_W // SG_HEADS).sum(axis=2).T
        grads['branch_norm_g'][l] = g_bng[0]
        grads['w_out'][l] = g_wout
    grad_x = d_out.reshape(x.shape)
    g_full = {n: jnp.stack(grads[n]) for n in W_NAMES}

    gs = _pack8([_split8(g_full[n], SHARD_AXIS[n]) for n in SHARDED], rows_sh)
    gr = _pack([g_full[n] for n in REPL], rows_rp)
    parts_sh, parts_rp = _grad_exchange(gs, gr, "grad_exchange")
    res_sh = _adamw(parts_sh, _pack([wts[n] for n in SHARDED], rows_sh), _pack([mom_m[n] for n in SHARDED], rows_sh),
                    _pack([mom_v[n] for n in SHARDED], rows_sh), "adamw_sharded")
    res_rp = _adamw(parts_rp, _pack([wts[n] for n in REPL], rows_rp), _pack([mom_m[n] for n in REPL], rows_rp),
                    _pack([mom_v[n] for n in REPL], rows_rp), "adamw_replicated")
    outs = []
    for kind in range(4):
        vals = dict(zip(SHARDED, _unpack(res_sh[kind], sh_shapes)))
        vals.update(zip(REPL, _unpack(res_rp[kind], rp_shapes)))
        outs.extend(vals[n] for n in W_NAMES)
    return (loss, grad_x, *outs)
```

```python
import functools
import math

import jax
import jax.numpy as jnp
from jax import lax
from jax.experimental import pallas as pl
from jax.experimental.pallas import tpu as pltpu

F32 = jnp.float32
BF16 = jnp.bfloat16

N_DEV = 8
DEPTH = 2
D_MODEL = 1024
CONV_W = 256
CONV_K = 31
HEADS = 8
NOPE = 64
ROPE = 32
QK = NOPE + ROPE
HEAD_PAD = 128
V_DIM = 64
MLA_W = HEADS * V_DIM
Q_LORA = 768
KV_LORA = 256
SG_W = 256
SG_HEADS = 4
SG_CHUNK = 128
ROPE_THETA = 10000.0
EPS = 1e-6
IN_COLS = 3104
NP = 3200
C_A, C_CQ, C_ZM, C_CKV, C_SG, C_KR = 0, 768, 1536, 2048, 2304, 3072
HALO = 32
SUB = 64
NEG = -1e30
LANES = 128
VMEM_LIMIT_V7X = 52 * 1024 * 1024

ADAM_LR = 0.001
ADAM_B1 = 0.9
ADAM_B2 = 0.999
ADAM_EPS = 1e-08
ADAM_WD = 0.01
ADAM_STEP = 10

W_NAMES = ['norm_g', 'w_in', 'conv_w', 'conv_b', 'conv_ln_g', 'conv_ln_b', 'conv_pw_w', 'conv_pw_b',
           'q_norm_g', 'w_uq', 'kv_norm_g', 'w_ukv', 'qk_q_g', 'qk_k_g', 'sg_ln_g', 'sg_ln_b', 'sg_w',
           'sg_b', 'branch_norm_g', 'w_out']
SHARD_AXIS = {'w_in': 2, 'conv_w': 2, 'conv_pw_w': 1, 'w_uq': 1, 'w_ukv': 2, 'w_out': 1}
SHARDED = [n for n in W_NAMES if n in SHARD_AXIS]
REPL = [n for n in W_NAMES if n not in SHARD_AXIS]


def _tile(s):
    for t in (512, 256, 128):
        if s % t == 0 and s // t >= 2:
            return t
    return s


def _cp(sem):
    return pltpu.CompilerParams(dimension_semantics=sem, vmem_limit_bytes=VMEM_LIMIT_V7X)


def _mm(a, b):
    return jnp.dot(a.astype(BF16), b.astype(BF16), preferred_element_type=F32)


def _mm_nt(a, b):
    return lax.dot_general(a.astype(BF16), b.astype(BF16), (((1,), (1,)), ((), ())),
                           preferred_element_type=F32)


def _mm_tn(a, b):
    return lax.dot_general(a.astype(BF16), b.astype(BF16), (((0,), (0,)), ((), ())),
                           preferred_element_type=F32)


_GC = math.sqrt(2.0 / math.pi)
_GA = 0.044715


def _sig(x):
    return 1.0 / (1.0 + jnp.exp(-x))


def _silu(x):
    return x * _sig(x)


def _dsilu(x):
    s = _sig(x)
    return s * (1.0 + x * (1.0 - s))


def _gelu(x):
    return 0.5 * x * (1.0 + jnp.tanh(_GC * (x + _GA * x * x * x)))


def _dgelu(x):
    t = jnp.tanh(_GC * (x + _GA * x * x * x))
    return 0.5 * (1.0 + t) + 0.5 * x * (1.0 - t * t) * _GC * (1.0 + 3.0 * _GA * x * x)


def _rsum(x):
    return jnp.sum(x, axis=-1, keepdims=True)


def _csum(x):
    return jnp.sum(x, axis=0, keepdims=True)


def _rms_fwd(x, n):
    r = lax.rsqrt(_rsum(x * x) * (1.0 / n) + EPS)
    return x * r, r


def _rms_bwd(dxh, xn, r, n):
    return r * (dxh - xn * (_rsum(dxh * xn) * (1.0 / n)))


def _ln_fwd(x, n):
    mu = _rsum(x) * (1.0 / n)
    xc = x - mu
    r = lax.rsqrt(_rsum(xc * xc) * (1.0 / n) + EPS)
    return xc * r, r


def _ln_bwd(dxh, xh, r, n):
    return r * (dxh - _rsum(dxh) * (1.0 / n) - xh * (_rsum(dxh * xh) * (1.0 / n)))


def _partner(x, lane):
    return jnp.where(lane < NOPE + ROPE // 2, pltpu.roll(x, LANES - ROPE // 2, 1), pltpu.roll(x, ROPE // 2, 1))


def _row_spec(ts, w, col=0):
    return pl.BlockSpec((ts, w), lambda i, col=col: (i, col))


def _full_spec(shape):
    nd = len(shape)
    return pl.BlockSpec(shape, lambda i, nd=nd: (0,) * nd)


PROJ_CHUNK = 640


def _proj_fwd(x, ng, win_p, name):
    s = x.shape[0]
    ts = _tile(s)

    def body(x_ref, g_ref, w_ref, o_ref):
        xv = x_ref[...]
        xn, _ = _rms_fwd(xv, D_MODEL)
        h = (xn * g_ref[...]).astype(BF16)
        for c in range(0, NP, PROJ_CHUNK):
            o_ref[:, c:c + PROJ_CHUNK] = jnp.dot(h, w_ref[:, c:c + PROJ_CHUNK], preferred_element_type=F32)

    return pl.pallas_call(
        body, name=name, grid=(s // ts,),
        in_specs=[_row_spec(ts, D_MODEL), _full_spec((1, D_MODEL)), _full_spec((D_MODEL, NP))],
        out_specs=_row_spec(ts, NP),
        out_shape=jax.ShapeDtypeStruct((s, NP), F32),
        compiler_params=_cp(("parallel",)),
    )(x, ng, win_p)


def _proj_bwd(x, ng, win_p, d_out, pieces, name):
    s = x.shape[0]
    ts = _tile(s)
    offs = [o for _, o in pieces]
    widths = [p.shape[1] for p, _ in pieces]

    def body(x_ref, g_ref, w_ref, do_ref, *rest):
        p_refs = rest[:len(pieces)]
        dx_ref, h_ref, gg_ref = rest[len(pieces):]
        i = pl.program_id(0)
        xv = x_ref[...]
        xn, r = _rms_fwd(xv, D_MODEL)
        g = g_ref[...]
        h_ref[...] = (xn * g).astype(BF16)
        dh = jnp.zeros((ts, D_MODEL), F32)
        for p_ref, off, w in zip(p_refs, offs, widths):
            dh = dh + _mm_nt(p_ref[...], w_ref[:, off:off + w])

        @pl.when(i == 0)
        def _():
            gg_ref[...] = jnp.zeros_like(gg_ref)

        gg_ref[...] += _csum(dh * xn)
        dx_ref[...] = _rms_bwd(dh * g, xn, r, D_MODEL) + do_ref[...]

    in_specs = [_row_spec(ts, D_MODEL), _full_spec((1, D_MODEL)), _full_spec((D_MODEL, NP)), _row_spec(ts, D_MODEL)]
    in_specs += [_row_spec(ts, w) for w in widths]
    return pl.pallas_call(
        body, name=name, grid=(s // ts,),
        in_specs=in_specs,
        out_specs=[_row_spec(ts, D_MODEL), _row_spec(ts, D_MODEL), _full_spec((1, D_MODEL))],
        out_shape=[jax.ShapeDtypeStruct((s, D_MODEL), F32), jax.ShapeDtypeStruct((s, D_MODEL), BF16),
                   jax.ShapeDtypeStruct((1, D_MODEL), F32)],
        compiler_params=_cp(("arbitrary",)),
    )(x, ng, win_p, d_out, *[p for p, _ in pieces])


def _tn_acc(a, b, name):
    s, m = a.shape
    n = b.shape[1]
    ts = _tile(s)

    def body(a_ref, b_ref, o_ref):
        @pl.when(pl.program_id(0) == 0)
        def _():
            o_ref[...] = jnp.zeros_like(o_ref)

        o_ref[...] += _mm_tn(a_ref[...], b_ref[...])

    return pl.pallas_call(
        body, name=name, grid=(s // ts,),
        in_specs=[_row_spec(ts, m), _row_spec(ts, n)],
        out_specs=_full_spec((m, n)),
        out_shape=jax.ShapeDtypeStruct((m, n), F32),
        compiler_params=_cp(("arbitrary",)),
    )(a, b)


def _halo_spec(ts):
    per = ts // HALO
    return pl.BlockSpec((HALO, 2 * CONV_W), lambda i: (jnp.maximum(i * per - 1, 0), 0))


def _conv_taps(ext_ref, cw_ref, cv_ref, cb, ts):
    base = HALO - (CONV_K - 1)
    for r0 in range(0, ts, SUB):
        acc = jnp.zeros((SUB, CONV_W), F32)
        for k in range(CONV_K):
            acc = acc + cw_ref[k:k + 1, :] * ext_ref[r0 + base + k:r0 + base + k + SUB, :]
        cv_ref[r0:r0 + SUB, :] = acc + cb


def _conv_fwd(proj, cw, cb, lg, lb, pww, pwb, name):
    s = proj.shape[0]
    ts = _tile(s)

    def body(pa_ref, ph_ref, cw_ref, cb_ref, lg_ref, lb_ref, pww_ref, pwb_ref, y_ref, ext_ref, cv_ref):
        i = pl.program_id(0)
        pa = pa_ref[...]
        a, ag, zc = pa[:, :CONV_W], pa[:, CONV_W:2 * CONV_W], pa[:, 2 * CONV_W:]
        ph = ph_ref[...]
        hglu = ph[:, :CONV_W] * _sig(ph[:, CONV_W:])
        ext_ref[0:HALO, :] = jnp.where(i > 0, hglu, 0.0)
        ext_ref[HALO:HALO + ts, :] = a * _sig(ag)
        _conv_taps(ext_ref, cw_ref, cv_ref, cb_ref[...], ts)
        xh, _ = _ln_fwd(cv_ref[...], CONV_W)
        ln = xh * lg_ref[...] + lb_ref[...]
        pw = _mm(_silu(ln), pww_ref[...]) + pwb_ref[...]
        y_ref[...] = pw * _silu(zc)

    vec = _full_spec((1, CONV_W))
    return pl.pallas_call(
        body, name=name, grid=(s // ts,),
        in_specs=[_row_spec(ts, 3 * CONV_W, 0), _halo_spec(ts), _full_spec((HALO, CONV_W)), vec, vec, vec,
                  _full_spec((CONV_W, CONV_W)), vec],
        out_specs=_row_spec(ts, CONV_W),
        out_shape=jax.ShapeDtypeStruct((s, CONV_W), F32),
        scratch_shapes=[pltpu.VMEM((HALO + ts, CONV_W), F32), pltpu.VMEM((ts, CONV_W), F32)],
        compiler_params=_cp(("parallel",)),
    )(proj, proj, cw, cb, lg, lb, pww, pwb)


def _conv_bwd(proj, dy, cw, cb, lg, lb, pww, pwb, name):
    s = proj.shape[0]
    ts = _tile(s)
    nt = s // ts
    per = ts // HALO

    def body(pa_ref, ph_ref, dy_ref, cw_ref, cb_ref, lg_ref, lb_ref, pww_ref, pwb_ref,
             dp_ref, gcw_ref, gpw_ref, gv_ref, ext_ref, cv_ref, dext_ref, carry_ref, gacc_ref):
        i = pl.program_id(0)
        ti = nt - 1 - i

        @pl.when(i == 0)
        def _():
            carry_ref[...] = jnp.zeros_like(carry_ref)
            gacc_ref[...] = jnp.zeros_like(gacc_ref)
            gpw_ref[...] = jnp.zeros_like(gpw_ref)
            gv_ref[...] = jnp.zeros_like(gv_ref)

        pa = pa_ref[...]
        a, ag, zc = pa[:, :CONV_W], pa[:, CONV_W:2 * CONV_W], pa[:, 2 * CONV_W:]
        sag = _sig(ag)
        ph = ph_ref[...]
        hglu = ph[:, :CONV_W] * _sig(ph[:, CONV_W:])
        ext_ref[0:HALO, :] = jnp.where(ti > 0, hglu, 0.0)
        ext_ref[HALO:HALO + ts, :] = a * sag
        _conv_taps(ext_ref, cw_ref, cv_ref, cb_ref[...], ts)
        xh, rl = _ln_fwd(cv_ref[...], CONV_W)
        lg = lg_ref[...]
        ln = xh * lg + lb_ref[...]
        sw = _silu(ln)
        pww = pww_ref[...]
        pw = _mm(sw, pww) + pwb_ref[...]
        d_y = dy_ref[...]
        d_pw = d_y * _silu(zc)
        d_zc = d_y * pw * _dsilu(zc)
        gpw_ref[...] += _mm_tn(sw, d_pw)
        d_ln = _mm_nt(d_pw, pww) * _dsilu(ln)
        d_cv = _ln_bwd(d_ln * lg, xh, rl, CONV_W)
        gv_ref[0:1, :] += _csum(d_cv)
        gv_ref[1:2, :] += _csum(d_ln * xh)
        gv_ref[2:3, :] += _csum(d_ln)
        gv_ref[3:4, :] += _csum(d_pw)
        dext_ref[0:ts, :] = d_cv
        dext_ref[ts:ts + HALO, :] = carry_ref[...]
        carry_ref[...] = d_cv[0:HALO, :]
        base = HALO - (CONV_K - 1)
        for r0 in range(0, ts, SUB):
            dcv_r = dext_ref[r0:r0 + SUB, :]
            dg = jnp.zeros((SUB, CONV_W), F32)
            for k in range(CONV_K):
                prod = dcv_r * ext_ref[r0 + base + k:r0 + base + k + SUB, :]
                gacc_ref[8 * k:8 * k + 8, :] += jnp.sum(prod.reshape(SUB // 8, 8, CONV_W), axis=0)
                dg = dg + cw_ref[k:k + 1, :] * dext_ref[r0 + CONV_K - 1 - k:r0 + CONV_K - 1 - k + SUB, :]
            dp_ref[r0:r0 + SUB, 0:CONV_W] = dg * sag[r0:r0 + SUB, :]
            dp_ref[r0:r0 + SUB, CONV_W:2 * CONV_W] = dg * a[r0:r0 + SUB, :] * sag[r0:r0 + SUB, :] * (1.0 - sag[r0:r0 + SUB, :])
        dp_ref[:, 2 * CONV_W:] = d_zc

        @pl.when(i == nt - 1)
        def _():
            gcw_ref[...] = jnp.zeros_like(gcw_ref)
            for k in range(CONV_K):
                gcw_ref[k:k + 1, :] = _csum(gacc_ref[8 * k:8 * k + 8, :])

    vec = _full_spec((1, CONV_W))
    rev = lambda w, col=0: pl.BlockSpec((ts, w), lambda i, col=col: (nt - 1 - i, col))
    halo = pl.BlockSpec((HALO, 2 * CONV_W), lambda i: (jnp.maximum((nt - 1 - i) * per - 1, 0), 0))
    return pl.pallas_call(
        body, name=name, grid=(nt,),
        in_specs=[rev(3 * CONV_W), halo, rev(CONV_W), _full_spec((HALO, CONV_W)), vec, vec, vec,
                  _full_spec((CONV_W, CONV_W)), vec],
        out_specs=[rev(3 * CONV_W), _full_spec((HALO, CONV_W)), _full_spec((CONV_W, CONV_W)), _full_spec((8, CONV_W))],
        out_shape=[jax.ShapeDtypeStruct((s, 3 * CONV_W), F32), jax.ShapeDtypeStruct((HALO, CONV_W), F32),
                   jax.ShapeDtypeStruct((CONV_W, CONV_W), F32), jax.ShapeDtypeStruct((8, CONV_W), F32)],
        scratch_shapes=[pltpu.VMEM((HALO + ts, CONV_W), F32), pltpu.VMEM((ts, CONV_W), F32),
                        pltpu.VMEM((ts + HALO, CONV_W), F32), pltpu.VMEM((HALO, CONV_W), F32),
                        pltpu.VMEM((8 * HALO, CONV_W), F32)],
        compiler_params=_cp(("arbitrary",)),
    )(proj, proj, dy, cw, cb, lg, lb, pww, pwb)


def _sg_mix(wm_ref, vc, head):
    out = jnp.zeros((SG_CHUNK, SG_W), F32)
    vb = vc.astype(BF16)
    for g in range(SG_HEADS):
        out = jnp.where(head == g, jnp.dot(wm_ref[g], vb, preferred_element_type=F32), out)
    return out


def _sgu_fwd(proj, lg, lb, wm, sbx, name):
    s = proj.shape[0]
    ts = _tile(s)

    def body(ps_ref, lg_ref, lb_ref, wm_ref, sbx_ref, y_ref, mix_ref):
        ps = ps_ref[...]
        us, vs, zs = ps[:, :SG_W], ps[:, SG_W:2 * SG_W], ps[:, 2 * SG_W:]
        xh, _ = _ln_fwd(_gelu(vs), SG_W)
        vn = xh * lg_ref[...] + lb_ref[...]
        head = lax.broadcasted_iota(jnp.int32, (SG_CHUNK, SG_W), 1) // (SG_W // SG_HEADS)
        for c0 in range(0, ts, SG_CHUNK):
            mix_ref[c0:c0 + SG_CHUNK, :] = _sg_mix(wm_ref, vn[c0:c0 + SG_CHUNK, :], head) + sbx_ref[...]
        y_ref[...] = _gelu(us) * mix_ref[...] * _silu(zs)

    vec = _full_spec((1, SG_W))
    return pl.pallas_call(
        body, name=name, grid=(s // ts,),
        in_specs=[_row_spec(ts, 3 * SG_W, C_SG // (3 * SG_W)), vec, vec,
                  _full_spec((SG_HEADS, SG_CHUNK, SG_CHUNK)), _full_spec((SG_CHUNK, SG_W))],
        out_specs=_row_spec(ts, SG_W),
        out_shape=jax.ShapeDtypeStruct((s, SG_W), F32),
        scratch_shapes=[pltpu.VMEM((ts, SG_W), F32)],
        compiler_params=_cp(("parallel",)),
    )(proj, lg, lb, wm, sbx)


def _sgu_bwd(proj, dy, lg, lb, wm, wmt, sbx, name):
    s = proj.shape[0]
    ts = _tile(s)

    def body(ps_ref, dy_ref, lg_ref, lb_ref, wm_ref, wmt_ref, sbx_ref,
             dp_ref, gwm_ref, dms_ref, gv_ref, mix_ref, dvn_ref):
        i = pl.program_id(0)

        @pl.when(i == 0)
        def _():
            gwm_ref[...] = jnp.zeros_like(gwm_ref)
            dms_ref[...] = jnp.zeros_like(dms_ref)
            gv_ref[...] = jnp.zeros_like(gv_ref)

        ps = ps_ref[...]
        us, vs, zs = ps[:, :SG_W], ps[:, SG_W:2 * SG_W], ps[:, 2 * SG_W:]
        xh, rl = _ln_fwd(_gelu(vs), SG_W)
        lg = lg_ref[...]
        vn = xh * lg + lb_ref[...]
        head = lax.broadcasted_iota(jnp.int32, (SG_CHUNK, SG_W), 1) // (SG_W // SG_HEADS)
        for c0 in range(0, ts, SG_CHUNK):
            mix_ref[c0:c0 + SG_CHUNK, :] = _sg_mix(wm_ref, vn[c0:c0 + SG_CHUNK, :], head) + sbx_ref[...]
        mixed = mix_ref[...]
        u = _gelu(us)
        sz = _silu(zs)
        d_y = dy_ref[...]
        d_mixed = d_y * u * sz
        dp_ref[:, 0:SG_W] = d_y * mixed * sz * _dgelu(us)
        dp_ref[:, 2 * SG_W:] = d_y * u * mixed * _dsilu(zs)
        dms = jnp.zeros((SG_CHUNK, SG_W), F32)
        for c0 in range(0, ts, SG_CHUNK):
            dm = d_mixed[c0:c0 + SG_CHUNK, :]
            vc = vn[c0:c0 + SG_CHUNK, :]
            dms = dms + dm
            for g in range(SG_HEADS):
                gwm_ref[g] += _mm_nt(jnp.where(head == g, dm, 0.0), vc)
            dvn_ref[c0:c0 + SG_CHUNK, :] = _sg_mix(wmt_ref, dm, head)
        dms_ref[...] += dms
        d_vn = dvn_ref[...]
        gv_ref[0:1, :] += _csum(d_vn * xh)
        gv_ref[1:2, :] += _csum(d_vn)
        dp_ref[:, SG_W:2 * SG_W] = _ln_bwd(d_vn * lg, xh, rl, SG_W) * _dgelu(vs)

    vec = _full_spec((1, SG_W))
    wspec = _full_spec((SG_HEADS, SG_CHUNK, SG_CHUNK))
    return pl.pallas_call(
        body, name=name, grid=(s // ts,),
        in_specs=[_row_spec(ts, 3 * SG_W, C_SG // (3 * SG_W)), _row_spec(ts, SG_W), vec, vec, wspec, wspec,
                  _full_spec((SG_CHUNK, SG_W))],
        out_specs=[_row_spec(ts, 3 * SG_W), wspec, _full_spec((SG_CHUNK, SG_W)), _full_spec((8, SG_W))],
        out_shape=[jax.ShapeDtypeStruct((s, 3 * SG_W), F32), jax.ShapeDtypeStruct((SG_HEADS, SG_CHUNK, SG_CHUNK), F32),
                   jax.ShapeDtypeStruct((SG_CHUNK, SG_W), F32), jax.ShapeDtypeStruct((8, SG_W), F32)],
        scratch_shapes=[pltpu.VMEM((ts, SG_W), F32), pltpu.VMEM((ts, SG_W), F32)],
        compiler_params=_cp(("arbitrary",)),
    )(proj, dy, lg, lb, wm, wmt, sbx)


QW = HEADS * HEAD_PAD
KVW = QW + MLA_W
ATT_SCALE = QK ** -0.5


def _mla_specs(ts):
    return [_row_spec(ts, Q_LORA, C_CQ // Q_LORA), _row_spec(ts, KV_LORA, C_CKV // KV_LORA),
            _row_spec(ts, LANES, C_KR // LANES), _row_spec(ts, LANES), _row_spec(ts, LANES),
            _full_spec((1, Q_LORA)), _full_spec((Q_LORA, QW)), _full_spec((1, KV_LORA)), _full_spec((KV_LORA, KVW)),
            _full_spec((1, LANES)), _full_spec((1, LANES))]


def _mla_fwd(proj, rc, rs, qg, wuq, kvg, wukv, gq, gk, name):
    s = proj.shape[0]
    ts = _tile(s)

    def body(cq_ref, ckv_ref, kr_ref, rc_ref, rs_ref, qg_ref, wuq_ref, kvg_ref, wukv_ref, gq_ref, gk_ref,
             q_ref, k_ref, v_ref):
        lane = lax.broadcasted_iota(jnp.int32, (ts, LANES), 1)
        c, sn = rc_ref[...], rs_ref[...]
        cqn, _ = _rms_fwd(cq_ref[...], Q_LORA)
        q0 = _mm(cqn * qg_ref[...], wuq_ref[...])
        gq = gq_ref[...]
        for h in range(HEADS):
            xn, _ = _rms_fwd(q0[:, h * LANES:(h + 1) * LANES], QK)
            qn = xn * gq
            q_ref[:, h * LANES:(h + 1) * LANES] = ((qn * c + _partner(qn, lane) * sn) * ATT_SCALE).astype(BF16)
        ckvn, _ = _rms_fwd(ckv_ref[...], KV_LORA)
        kv = _mm(ckvn * kvg_ref[...], wukv_ref[...])
        kr = pltpu.roll(kr_ref[...], NOPE, 1)
        gk = gk_ref[...]
        for h in range(HEADS):
            xn, _ = _rms_fwd(kv[:, h * LANES:(h + 1) * LANES] + kr, QK)
            kn = xn * gk
            k_ref[:, h * LANES:(h + 1) * LANES] = (kn * c + _partner(kn, lane) * sn).astype(BF16)
        v_ref[...] = kv[:, QW:].astype(BF16)

    return pl.pallas_call(
        body, name=name, grid=(s // ts,),
        in_specs=_mla_specs(ts),
        out_specs=[_row_spec(ts, QW), _row_spec(ts, QW), _row_spec(ts, MLA_W)],
        out_shape=[jax.ShapeDtypeStruct((s, QW), BF16), jax.ShapeDtypeStruct((s, QW), BF16),
                   jax.ShapeDtypeStruct((s, MLA_W), BF16)],
        compiler_params=_cp(("parallel",)),
    )(proj, proj, proj, rc, rs, qg, wuq, kvg, wukv, gq, gk)


def _mla_bwd(proj, rc, rs, qg, wuq, kvg, wukv, gq, gk, dq, dk, dv, name):
    s = proj.shape[0]
    ts = _tile(s)

    def body(cq_ref, ckv_ref, kr_ref, rc_ref, rs_ref, qg_ref, wuq_ref, kvg_ref, wukv_ref, gq_ref, gk_ref,
             dq_ref, dk_ref, dv_ref, dcq_ref, dckv_ref, dkr_ref, gwuq_ref, gwukv_ref, gv_ref, d0_ref):
        i = pl.program_id(0)

        @pl.when(i == 0)
        def _():
            gwuq_ref[...] = jnp.zeros_like(gwuq_ref)
            gwukv_ref[...] = jnp.zeros_like(gwukv_ref)
            gv_ref[...] = jnp.zeros_like(gv_ref)

        lane = lax.broadcasted_iota(jnp.int32, (ts, LANES), 1)
        c, sn = rc_ref[...], rs_ref[...]
        cq = cq_ref[...]
        cqx, rq0 = _rms_fwd(cq, Q_LORA)
        qg = qg_ref[...]
        cqn = cqx * qg
        wuq = wuq_ref[...]
        q0 = _mm(cqn, wuq)
        gq = gq_ref[...]
        ggq = jnp.zeros((1, LANES), F32)
        for h in range(HEADS):
            xn, r = _rms_fwd(q0[:, h * LANES:(h + 1) * LANES], QK)
            d = dq_ref[:, h * LANES:(h + 1) * LANES] * ATT_SCALE
            d_qn = d * c - _partner(d, lane) * sn
            ggq = ggq + _csum(d_qn * xn)
            d0_ref[:, h * LANES:(h + 1) * LANES] = _rms_bwd(d_qn * gq, xn, r, QK)
        dq0 = d0_ref[:, 0:QW]
        gwuq_ref[...] += _mm_tn(cqn, dq0)
        d_cqn = _mm_nt(dq0, wuq)
        gv_ref[0:1, 0:Q_LORA] += _csum(d_cqn * cqx)
        gv_ref[2:3, 0:LANES] += ggq
        dcq_ref[...] = _rms_bwd(d_cqn * qg, cqx, rq0, Q_LORA)
        ckv = ckv_ref[...]
        ckx, rk0 = _rms_fwd(ckv, KV_LORA)
        kvg = kvg_ref[...]
        ckvn = ckx * kvg
        wukv = wukv_ref[...]
        kv = _mm(ckvn, wukv)
        kr = pltpu.roll(kr_ref[...], NOPE, 1)
        gk = gk_ref[...]
        ggk = jnp.zeros((1, LANES), F32)
        dkr = jnp.zeros((ts, LANES), F32)
        for h in range(HEADS):
            xn, r = _rms_fwd(kv[:, h * LANES:(h + 1) * LANES] + kr, QK)
            d = dk_ref[:, h * LANES:(h + 1) * LANES]
            d_kn = d * c - _partner(d, lane) * sn
            ggk = ggk + _csum(d_kn * xn)
            d_k0 = _rms_bwd(d_kn * gk, xn, r, QK)
            dkr = dkr + d_k0
            d0_ref[:, h * LANES:(h + 1) * LANES] = d_k0
        d0_ref[:, QW:KVW] = dv_ref[...]
        dkv = d0_ref[...]
        dkr_ref[...] = jnp.where(lane < ROPE, pltpu.roll(dkr, NOPE, 1), 0.0)
        gwukv_ref[...] += _mm_tn(ckvn, dkv)
        d_ckvn = _mm_nt(dkv, wukv)
        gv_ref[1:2, 0:KV_LORA] += _csum(d_ckvn * ckx)
        gv_ref[3:4, 0:LANES] += ggk
        dckv_ref[...] = _rms_bwd(d_ckvn * kvg, ckx, rk0, KV_LORA)

    return pl.pallas_call(
        body, name=name, grid=(s // ts,),
        in_specs=_mla_specs(ts) + [_row_spec(ts, QW), _row_spec(ts, QW), _row_spec(ts, MLA_W)],
        out_specs=[_row_spec(ts, Q_LORA), _row_spec(ts, KV_LORA), _row_spec(ts, LANES),
                   _full_spec((Q_LORA, QW)), _full_spec((KV_LORA, KVW)), _full_spec((8, QW))],
        out_shape=[jax.ShapeDtypeStruct((s, Q_LORA), F32), jax.ShapeDtypeStruct((s, KV_LORA), F32),
                   jax.ShapeDtypeStruct((s, LANES), F32), jax.ShapeDtypeStruct((Q_LORA, QW), F32),
                   jax.ShapeDtypeStruct((KV_LORA, KVW), F32), jax.ShapeDtypeStruct((8, QW), F32)],
        scratch_shapes=[pltpu.VMEM((ts, KVW), F32)],
        compiler_params=_cp(("arbitrary",)),
    )(proj, proj, proj, rc, rs, qg, wuq, kvg, wukv, gq, gk, dq, dk, dv)


PAIRS = HEADS // 2


def _attn_fwd(q, k, v, name):
    s = q.shape[0]
    tq = _tile(s)
    tk = tq

    def body(q_ref, k_ref, v_ref, o_ref, lse_ref):
        i = pl.program_id(1)
        row = lax.broadcasted_iota(jnp.int32, (tq, tk), 0)
        col = lax.broadcasted_iota(jnp.int32, (tq, tk), 1)
        first = lax.broadcasted_iota(jnp.int32, (tq, LANES), 1) < V_DIM
        qs = [q_ref[:, a * LANES:(a + 1) * LANES] for a in range(2)]

        def blk(j, carry, masked):
            st = pl.multiple_of(j * tk, tk)
            vj = v_ref[pl.ds(st, tk), :]
            new = []
            for a in range(2):
                m, l, acc = carry[a]
                sc = _mm_nt(qs[a], k_ref[pl.ds(st, tk), a * LANES:(a + 1) * LANES])
                if masked:
                    sc = jnp.where(col <= row, sc, NEG)
                m_new = jnp.maximum(m, jnp.max(sc, axis=-1, keepdims=True))
                alpha = jnp.exp(m - m_new)
                p = jnp.exp(sc - m_new)
                l = alpha * l + _rsum(p)
                acc = alpha * acc + jnp.dot(p.astype(BF16), vj, preferred_element_type=F32)
                new.append((m_new, l, acc))
            return tuple(new)

        one = (jnp.full((tq, 1), NEG, F32), jnp.zeros((tq, 1), F32), jnp.zeros((tq, LANES), F32))
        carry = lax.fori_loop(0, i, functools.partial(blk, masked=False), (one, one))
        (m0, l0, acc0), (m1, l1, acc1) = blk(i, carry, True)
        o_ref[...] = jnp.where(first, acc0 / l0, acc1 / l1)
        lse_ref[...] = jnp.where(first, m0 + jnp.log(l0), m1 + jnp.log(l1))

    return pl.pallas_call(
        body, name=name, grid=(PAIRS, s // tq),
        in_specs=[pl.BlockSpec((tq, 2 * LANES), lambda p, i: (i, p)),
                  pl.BlockSpec((s, 2 * LANES), lambda p, i: (0, p)),
                  pl.BlockSpec((s, LANES), lambda p, i: (0, p))],
        out_specs=[pl.BlockSpec((tq, LANES), lambda p, i: (i, p)), pl.BlockSpec((tq, LANES), lambda p, i: (i, p))],
        out_shape=[jax.ShapeDtypeStruct((s, MLA_W), F32), jax.ShapeDtypeStruct((s, MLA_W), F32)],
        compiler_params=_cp(("parallel", "parallel")),
    )(q, k, v)


def _attn_bwd(q, k, v, do, stats, name):
    s = q.shape[0]
    tq = _tile(s)
    tk = tq
    nq = s // tq

    def body(q_ref, k_ref, v_ref, do_ref, st_ref, dq_ref, dk_ref, dv_ref):
        j = pl.program_id(1)

        @pl.when(j == 0)
        def _():
            dq_ref[...] = jnp.zeros_like(dq_ref)

        dk_ref[...] = jnp.zeros_like(dk_ref)
        dv_ref[...] = jnp.zeros_like(dv_ref)
        row = lax.broadcasted_iota(jnp.int32, (tq, tk), 0)
        col = lax.broadcasted_iota(jnp.int32, (tq, tk), 1)
        lane = lax.broadcasted_iota(jnp.int32, (tq, LANES), 1)
        vj = v_ref[...]
        ks = [k_ref[:, a * LANES:(a + 1) * LANES] for a in range(2)]

        def blk(i, masked):
            st = pl.multiple_of(i * tq, tq)
            do2 = do_ref[pl.ds(st, tq), :]
            stt = st_ref[pl.ds(st, tq), :]
            dv = None
            for a in range(2):
                mine = (lane < V_DIM) if a == 0 else (lane >= V_DIM)
                qa = q_ref[pl.ds(st, tq), a * LANES:(a + 1) * LANES]
                doa = jnp.where(mine, do2, jnp.zeros((), BF16))
                lse = stt[:, a * V_DIM:a * V_DIM + 1]
                dl = stt[:, a * V_DIM + V_DIM // 2:a * V_DIM + V_DIM // 2 + 1]
                p = jnp.exp(_mm_nt(qa, ks[a]) - lse)
                if masked:
                    p = jnp.where(col <= row, p, 0.0)
                ds = (p * (_mm_nt(doa, vj) - dl)).astype(BF16)
                dva = _mm_tn(p, doa)
                dv = dva if dv is None else dv + dva
                dk_ref[:, a * LANES:(a + 1) * LANES] += _mm_tn(ds, qa)
                dq_ref[pl.ds(st, tq), a * LANES:(a + 1) * LANES] += jnp.dot(ds, ks[a], preferred_element_type=F32)
            dv_ref[...] += dv

        blk(j, True)

        def loop_body(i, carry):
            blk(i, False)
            return carry

        lax.fori_loop(j + 1, nq, loop_body, 0)

    return pl.pallas_call(
        body, name=name, grid=(PAIRS, s // tk),
        in_specs=[pl.BlockSpec((s, 2 * LANES), lambda p, j: (0, p)),
                  pl.BlockSpec((tk, 2 * LANES), lambda p, j: (j, p)),
                  pl.BlockSpec((tk, LANES), lambda p, j: (j, p)),
                  pl.BlockSpec((s, LANES), lambda p, j: (0, p)),
                  pl.BlockSpec((s, LANES), lambda p, j: (0, p))],
        out_specs=[pl.BlockSpec((s, 2 * LANES), lambda p, j: (0, p)),
                   pl.BlockSpec((tk, 2 * LANES), lambda p, j: (j, p)),
                   pl.BlockSpec((tk, LANES), lambda p, j: (j, p))],
        out_shape=[jax.ShapeDtypeStruct((s, QW), F32), jax.ShapeDtypeStruct((s, QW), F32),
                   jax.ShapeDtypeStruct((s, MLA_W), F32)],
        compiler_params=_cp(("parallel", "arbitrary")),
    )(q, k, v, do, stats)


BR = ((0, CONV_W), (CONV_W, CONV_W + MLA_W), (CONV_W + MLA_W, D_MODEL))


def _post_fwd(x, yc, o, proj, ys, bng, wout, name):
    s = x.shape[0]
    ts = _tile(s)

    def body(x_ref, yc_ref, o_ref, zm_ref, ys_ref, g_ref, w_ref, out_ref):
        ys3 = (yc_ref[...], o_ref[...] * _silu(zm_ref[...]), ys_ref[...])
        acc = x_ref[...]
        for (lo, hi), yb in zip(BR, ys3):
            yn, _ = _rms_fwd(yb, hi - lo)
            acc = acc + _mm(yn * g_ref[:, lo:hi], w_ref[lo:hi, :])
        out_ref[...] = acc

    return pl.pallas_call(
        body, name=name, grid=(s // ts,),
        in_specs=[_row_spec(ts, D_MODEL), _row_spec(ts, CONV_W), _row_spec(ts, MLA_W),
                  _row_spec(ts, MLA_W, C_ZM // MLA_W), _row_spec(ts, SG_W), _full_spec((1, D_MODEL)),
                  _full_spec((D_MODEL, D_MODEL))],
        out_specs=_row_spec(ts, D_MODEL),
        out_shape=jax.ShapeDtypeStruct((s, D_MODEL), F32),
        compiler_params=_cp(("parallel",)),
    )(x, yc, o, proj, ys, bng, wout)


def _post_bwd(d_out, yc, o, lse, proj, ys, bng, wout, name):
    s = d_out.shape[0]
    ts = _tile(s)

    def body(do_ref, yc_ref, o_ref, lse_ref, zm_ref, ys_ref, g_ref, w_ref,
             dyc_ref, dys_ref, dob_ref, dzm_ref, st_ref, gw_ref, gg_ref, yn_ref):
        i = pl.program_id(0)

        @pl.when(i == 0)
        def _():
            gw_ref[...] = jnp.zeros_like(gw_ref)
            gg_ref[...] = jnp.zeros_like(gg_ref)

        d_out_b = do_ref[...].astype(BF16)
        o = o_ref[...]
        zm = zm_ref[...]
        szm = _silu(zm)
        ys3 = (yc_ref[...], o * szm, ys_ref[...])
        d_ys = []
        for (lo, hi), yb in zip(BR, ys3):
            n = hi - lo
            yn, r = _rms_fwd(yb, n)
            g = g_ref[:, lo:hi]
            yn_ref[:, lo:hi] = (yn * g).astype(BF16)
            d_yn = _mm_nt(d_out_b, w_ref[lo:hi, :])
            gg_ref[:, lo:hi] += _csum(d_yn * yn)
            d_ys.append(_rms_bwd(d_yn * g, yn, r, n))
        gw_ref[...] += _mm_tn(yn_ref[...], d_out_b)
        dyc_ref[...] = d_ys[0]
        dys_ref[...] = d_ys[2]
        d_ym = d_ys[1]
        d_o = d_ym * szm
        dob_ref[...] = d_o.astype(BF16)
        dzm_ref[...] = d_ym * o * _dsilu(zm)
        prod = d_o * o
        head = lax.broadcasted_iota(jnp.int32, (ts, MLA_W), 1) // V_DIM
        delta = jnp.zeros((ts, MLA_W), F32)
        for h in range(HEADS):
            delta = jnp.where(head == h, _rsum(jnp.where(head == h, prod, 0.0)), delta)
        lane = lax.broadcasted_iota(jnp.int32, (ts, MLA_W), 1)
        st_ref[...] = jnp.where(lane % V_DIM < V_DIM // 2, lse_ref[...], delta)

    return pl.pallas_call(
        body, name=name, grid=(s // ts,),
        in_specs=[_row_spec(ts, D_MODEL), _row_spec(ts, CONV_W), _row_spec(ts, MLA_W), _row_spec(ts, MLA_W),
                  _row_spec(ts, MLA_W, C_ZM // MLA_W), _row_spec(ts, SG_W), _full_spec((1, D_MODEL)),
                  _full_spec((D_MODEL, D_MODEL))],
        out_specs=[_row_spec(ts, CONV_W), _row_spec(ts, SG_W), _row_spec(ts, MLA_W), _row_spec(ts, MLA_W),
                   _row_spec(ts, MLA_W), _full_spec((D_MODEL, D_MODEL)), _full_spec((1, D_MODEL))],
        out_shape=[jax.ShapeDtypeStruct((s, CONV_W), F32), jax.ShapeDtypeStruct((s, SG_W), F32),
                   jax.ShapeDtypeStruct((s, MLA_W), BF16), jax.ShapeDtypeStruct((s, MLA_W), F32),
                   jax.ShapeDtypeStruct((s, MLA_W), F32), jax.ShapeDtypeStruct((D_MODEL, D_MODEL), F32),
                   jax.ShapeDtypeStruct((1, D_MODEL), F32)],
        scratch_shapes=[pltpu.VMEM((ts, D_MODEL), BF16)],
        compiler_params=_cp(("arbitrary",)),
    )(d_out, yc, o, lse, proj, ys, bng, wout)


def _loss_head(y, target, name):
    s = y.shape[0]
    ts = _tile(s)
    nt = s // ts

    def body(y_ref, t_ref, dy_ref, l_ref, acc_ref):
        i = pl.program_id(0)

        @pl.when(i == 0)
        def _():
            acc_ref[...] = jnp.zeros_like(acc_ref)

        e = y_ref[...] - t_ref[...]
        dy_ref[...] = e * (1.0 / D_MODEL)
        sq = jnp.sum((e * e).reshape(ts // 8, 8, D_MODEL), axis=0)
        part = sq[:, 0:LANES]
        for c in range(LANES, D_MODEL, LANES):
            part = part + sq[:, c:c + LANES]
        acc_ref[...] += part

        @pl.when(i == nt - 1)
        def _():
            tot = jnp.sum(_rsum(acc_ref[...]), axis=0, keepdims=True) * (0.5 / D_MODEL)
            l_ref[...] = jnp.broadcast_to(tot, (8, LANES))

    return pl.pallas_call(
        body, name=name, grid=(nt,),
        in_specs=[_row_spec(ts, D_MODEL), _row_spec(ts, D_MODEL)],
        out_specs=[_row_spec(ts, D_MODEL), _full_spec((8, LANES))],
        out_shape=[jax.ShapeDtypeStruct((s, D_MODEL), F32), jax.ShapeDtypeStruct((8, LANES), F32)],
        scratch_shapes=[pltpu.VMEM((8, LANES), F32)],
        compiler_params=_cp(("arbitrary",)),
    )(y, target)


MESH = pl.DeviceIdType.MESH
ANY = pl.BlockSpec(memory_space=pl.ANY)


def _all_gather(xs, name):
    n = len(xs)
    per = N_DEV - 1

    def body(*refs):
        x_refs, out_refs = refs[:n], refs[n:2 * n]
        send_sems, recv_sems, local_sems = refs[2 * n:]
        x, y, c = lax.axis_index("x"), lax.axis_index("y"), lax.axis_index("c")
        me, sibling = (x, y, c), (x, y, 1 - c)
        chips = [(1 - x, y), (x, 1 - y), (1 - x, 1 - y)]

        def slot(t, px, py, pc):
            return out_refs[t].at[4 * px + 2 * py + pc]

        def copy(t, k, block, to, src=None):
            return pltpu.make_async_remote_copy(
                src_ref=slot(t, *block) if src is None else src, dst_ref=slot(t, *block),
                send_sem=send_sems.at[t * per + k], recv_sem=recv_sems.at[t * per + k], device_id=to,
                device_id_type=MESH)

        mine = [pltpu.make_async_copy(x_refs[t], slot(t, *me), local_sems.at[t]) for t in range(n)]
        for cp in mine:
            cp.start()
        first = [copy(t, 0, me, sibling, src=x_refs[t]) for t in range(n)]
        first += [copy(t, 1 + j, me, (*chip, c), src=x_refs[t]) for j, chip in enumerate(chips) for t in range(n)]
        for cp in first:
            cp.start()
        passed = []
        for j, chip in enumerate(chips):
            for t in range(n):
                copy(t, 1 + j, (*chip, c), me).wait_recv()
                passed.append(copy(t, 4 + j, (*chip, c), sibling))
                passed[-1].start()
        for t in range(n):
            copy(t, 0, sibling, me).wait_recv()
        for j, chip in enumerate(chips):
            for t in range(n):
                copy(t, 4 + j, (*chip, 1 - c), me).wait_recv()
        for cp in first + passed:
            cp.wait_send()
        for cp in mine:
            cp.wait()

    return pl.pallas_call(
        body, name=name,
        out_shape=[jax.ShapeDtypeStruct((N_DEV,) + a.shape, a.dtype) for a in xs],
        in_specs=[ANY] * n, out_specs=[ANY] * n,
        scratch_shapes=[pltpu.SemaphoreType.DMA((per * n,)), pltpu.SemaphoreType.DMA((per * n,)),
                        pltpu.SemaphoreType.DMA((n,))],
    )(*xs)


def _grad_exchange(gss, gr, name):
    n = len(gss)
    per = N_DEV - 1

    def body(*refs):
        gs_refs, gr_ref = refs[:n], refs[n]
        os_refs, or_ref = refs[n + 1:2 * n + 1], refs[2 * n + 1]
        send_sems, recv_sems, local_sems = refs[2 * n + 2:]
        x, y, c = lax.axis_index("x"), lax.axis_index("y"), lax.axis_index("c")
        me = 4 * x + 2 * y + c
        local = [pltpu.make_async_copy(gs_refs[t].at[me], os_refs[t].at[me], local_sems.at[t]) for t in range(n)]
        local.append(pltpu.make_async_copy(gr_ref, or_ref.at[me], local_sems.at[n]))
        for cp in local:
            cp.start()
        sends, recvs = [], []
        for k in range(1, N_DEV):
            px = 1 - x if k & 4 else x
            py = 1 - y if k & 2 else y
            pc = 1 - c if k & 1 else c
            peer = 4 * px + 2 * py + pc
            to = (px, py, pc)
            for t in range(n + 1):
                sems = dict(send_sem=send_sems.at[t * per + k - 1], recv_sem=recv_sems.at[t * per + k - 1],
                            device_id=to, device_id_type=MESH)
                src = gs_refs[t].at[peer] if t < n else gr_ref
                out = os_refs[t] if t < n else or_ref
                sends.append(pltpu.make_async_remote_copy(src_ref=src, dst_ref=out.at[me], **sems))
                recvs.append(pltpu.make_async_remote_copy(src_ref=src, dst_ref=out.at[peer], **sems))
        for cp in sends:
            cp.start()
        for cp in recvs:
            cp.wait_recv()
        for cp in sends:
            cp.wait_send()
        for cp in local:
            cp.wait()

    nsem = per * (n + 1)
    return pl.pallas_call(
        body, name=name,
        out_shape=[jax.ShapeDtypeStruct(g.shape, F32) for g in gss] + [jax.ShapeDtypeStruct((N_DEV,) + gr.shape, F32)],
        in_specs=[ANY] * (n + 1), out_specs=[ANY] * (n + 1),
        scratch_shapes=[pltpu.SemaphoreType.DMA((nsem,)), pltpu.SemaphoreType.DMA((nsem,)),
                        pltpu.SemaphoreType.DMA((n + 1,))],
    )(*gss, gr)


ADAM_ROWS = 128


def _adamw(parts, w, m, v, name):
    r, cols = w.shape
    tr = ADAM_ROWS if r % ADAM_ROWS == 0 else r

    def body(p_ref, w_ref, m_ref, v_ref, g_ref, d_ref, nm_ref, nv_ref):
        g = p_ref[0]
        for sidx in range(1, N_DEV):
            g = g + p_ref[sidx]
        mm = ADAM_B1 * m_ref[...] + (1.0 - ADAM_B1) * g
        vv = ADAM_B2 * v_ref[...] + (1.0 - ADAM_B2) * (g * g)
        m_hat = mm / (1.0 - ADAM_B1 ** ADAM_STEP)
        v_hat = vv / (1.0 - ADAM_B2 ** ADAM_STEP)
        g_ref[...] = g
        d_ref[...] = -ADAM_LR * (m_hat / (jnp.sqrt(v_hat) + ADAM_EPS) + ADAM_WD * w_ref[...])
        nm_ref[...] = mm
        nv_ref[...] = vv

    row = pl.BlockSpec((tr, cols), lambda i: (i, 0))
    return pl.pallas_call(
        body, name=name, grid=(r // tr,),
        in_specs=[pl.BlockSpec((N_DEV, tr, cols), lambda i: (0, i, 0)), row, row, row],
        out_specs=[row, row, row, row],
        out_shape=[jax.ShapeDtypeStruct((r, cols), F32)] * 4,
        compiler_params=_cp(("parallel",)),
    )(parts, w, m, v)


PACK_W = 8 * LANES


def _pack(flat_parts, rows):
    flat = jnp.concatenate([p.reshape(-1) for p in flat_parts])
    return jnp.pad(flat, (0, rows * PACK_W - flat.shape[0])).reshape(rows, PACK_W)


def _pack8(parts8, rows):
    flat = jnp.concatenate([p.reshape(N_DEV, -1) for p in parts8], axis=1)
    return jnp.pad(flat, ((0, 0), (0, rows * PACK_W - flat.shape[1]))).reshape(N_DEV, rows, PACK_W)


def _rows_for(n, mult):
    rows = -(-n // PACK_W)
    return -(-rows // mult) * mult


def _unshard(arr8, axis):
    full = jnp.moveaxis(arr8, 0, axis)
    shp = list(full.shape)
    shp[axis:axis + 2] = [shp[axis] * shp[axis + 1]]
    return full.reshape(shp)


def _split8(full, axis):
    shp = list(full.shape)
    shp[axis:axis + 1] = [N_DEV, shp[axis] // N_DEV]
    return jnp.moveaxis(full.reshape(shp), axis, 0)


def _unpack(flat2d, shapes, lead=()):
    flat = flat2d.reshape(lead + (-1,))
    out, off = [], 0
    for shp in shapes:
        n = math.prod(shp)
        out.append(flat[..., off:off + n].reshape(lead + tuple(shp)))
        off += n
    return out


def _to_layout(w):
    return jnp.concatenate([w[:, :1536], w[:, 1824:2336], w[:, 1536:1792], w[:, 2336:3104], w[:, 1792:1824],
                            jnp.zeros((w.shape[0], NP - IN_COLS), w.dtype)], axis=1)


def _from_layout(g):
    return jnp.concatenate([g[:, :1536], g[:, C_CKV:C_CKV + KV_LORA], g[:, C_KR:C_KR + ROPE],
                            g[:, C_ZM:C_ZM + MLA_W], g[:, C_SG:C_SG + 3 * SG_W]], axis=1)


def _pad_heads(w, real):
    lead = w.shape[:-1]
    w = w.reshape(lead + (HEADS, real))
    return jnp.pad(w, [(0, 0)] * len(lead) + [(0, 0), (0, HEAD_PAD - real)]).reshape(lead + (QW,))


def _rope_tables(s):
    half = ROPE // 2
    inv_freq = ROPE_THETA ** (-jnp.arange(half, dtype=F32) / half)
    ang = jnp.arange(s, dtype=F32)[:, None] * inv_freq[None, :]
    cos, sin = jnp.cos(ang), jnp.sin(ang)
    ones = jnp.ones((s, NOPE), F32)
    zeros = jnp.zeros((s, NOPE), F32)
    pad = jnp.zeros((s, HEAD_PAD - QK), F32)
    rc = jnp.concatenate([ones, cos, cos, pad + 1.0], axis=1)
    rs = jnp.concatenate([zeros, -sin, sin, pad], axis=1)
    return rc, rs


def kernel(x, norm_g, w_in, conv_w, conv_b, conv_ln_g, conv_ln_b, conv_pw_w, conv_pw_b, q_norm_g, w_uq, kv_norm_g, w_ukv, qk_q_g, qk_k_g, sg_ln_g, sg_ln_b, sg_w, sg_b, branch_norm_g, w_out, loss_target, m_norm_g, m_w_in, m_conv_w, m_conv_b, m_conv_ln_g, m_conv_ln_b, m_conv_pw_w, m_conv_pw_b, m_q_norm_g, m_w_uq, m_kv_norm_g, m_w_ukv, m_qk_q_g, m_qk_k_g, m_sg_ln_g, m_sg_ln_b, m_sg_w, m_sg_b, m_branch_norm_g, m_w_out, v_norm_g, v_w_in, v_conv_w, v_conv_b, v_conv_ln_g, v_conv_ln_b, v_conv_pw_w, v_conv_pw_b, v_q_norm_g, v_w_uq, v_kv_norm_g, v_w_ukv, v_qk_q_g, v_qk_k_g, v_sg_ln_g, v_sg_ln_b, v_sg_w, v_sg_b, v_branch_norm_g, v_w_out):
    given = dict(locals())
    wts = {n: given[n] for n in W_NAMES}
    mom_m = {n: given['m_' + n] for n in W_NAMES}
    mom_v = {n: given['v_' + n] for n in W_NAMES}
    s = x.shape[1]
    xs = x.reshape(s, D_MODEL)
    target = loss_target.reshape(s, D_MODEL)

    packed = [n for n in SHARDED if n != 'w_in']
    sh_shapes = [wts[n].shape for n in packed]
    rp_shapes = [wts[n].shape for n in REPL]
    rows_sh = _rows_for(sum(math.prod(p) for p in sh_shapes), ADAM_ROWS)
    rows_rp = _rows_for(sum(math.prod(p) for p in rp_shapes), 8)
    win_shape = wts['w_in'].shape
    win_2d = (win_shape[0] * win_shape[1], win_shape[2])

    gathered, gathered_win = _all_gather([_pack([wts[n] for n in packed], rows_sh), wts['w_in'].reshape(win_2d)],
                                         "weights_all_gather")
    full = {n: _unshard(a, SHARD_AXIS[n]) for n, a in zip(packed, _unpack(gathered, sh_shapes, (N_DEV,)))}
    full['w_in'] = _unshard(gathered_win.reshape((N_DEV,) + win_shape), SHARD_AXIS['w_in'])
    full.update({n: wts[n] for n in REPL})

    rc, rs = _rope_tables(s)
    tril = jnp.tril(jnp.ones((SG_CHUNK, SG_CHUNK), dtype=bool))

    def vec(a, width=None):
        a = a.reshape(1, -1)
        return a if width is None else jnp.pad(a, ((0, 0), (0, width - a.shape[1])))

    layers = []
    for l in range(DEPTH):
        p = {n: full[n][l] for n in W_NAMES}
        wukv = p['w_ukv'].reshape(KV_LORA, HEADS, NOPE + V_DIM)
        wm = jnp.where(tril[None], p['sg_w'], 0.0)
        layers.append(dict(
            ng=vec(p['norm_g']), win=_to_layout(p['w_in']).astype(BF16),
            cw=jnp.pad(p['conv_w'], ((0, HALO - CONV_K), (0, 0))), cb=vec(p['conv_b']), clg=vec(p['conv_ln_g']),
            clb=vec(p['conv_ln_b']), pww=p['conv_pw_w'].astype(BF16), pwb=vec(p['conv_pw_b']),
            qg=vec(p['q_norm_g']), wuq=_pad_heads(p['w_uq'], QK).astype(BF16), kvg=vec(p['kv_norm_g']),
            wukv=jnp.concatenate([_pad_heads(wukv[:, :, :NOPE].reshape(KV_LORA, HEADS * NOPE), NOPE),
                                  wukv[:, :, NOPE:].reshape(KV_LORA, MLA_W)], axis=1).astype(BF16),
            gq=vec(p['qk_q_g'], LANES), gk=vec(p['qk_k_g'], LANES),
            slg=vec(p['sg_ln_g']), slb=vec(p['sg_ln_b']), wm=wm.astype(BF16),
            wmt=jnp.swapaxes(wm, 1, 2).astype(BF16),
            sbx=jnp.repeat(p['sg_b'].T, SG_W // SG_HEADS, axis=1),
            bng=vec(p['branch_norm_g']), wout=p['w_out'].astype(BF16)))

    acts = []
    h_in = xs
    for l, p in enumerate(layers):
        proj = _proj_fwd(h_in, p['ng'], p['win'], f"proj_fwd_{l}")
        yc = _conv_fwd(proj, p['cw'], p['cb'], p['clg'], p['clb'], p['pww'], p['pwb'], f"conv_fwd_{l}")
        ys = _sgu_fwd(proj, p['slg'], p['slb'], p['wm'], p['sbx'], f"sgu_fwd_{l}")
        q, k, v = _mla_fwd(proj, rc, rs, p['qg'], p['wuq'], p['kvg'], p['wukv'], p['gq'], p['gk'], f"mla_fwd_{l}")
        o, lse = _attn_fwd(q, k, v, f"attn_fwd_{l}")
        h_out = _post_fwd(h_in, yc, o, proj, ys, p['bng'], p['wout'], f"post_fwd_{l}")
        acts.append(dict(x=h_in, proj=proj, yc=yc, ys=ys, q=q, k=k, v=v, o=o, lse=lse))
        h_in = h_out

    d_out, loss_blk = _loss_head(h_in, target, "loss_head")
    loss = lax.psum(loss_blk[0, 0], ("x", "y", "c"))

    grads = {n: [None] * DEPTH for n in W_NAMES}
    for l in reversed(range(DEPTH)):
        p, a = layers[l], acts[l]
        d_yc, d_ys, d_o, d_zm, stats, g_wout, g_bng = _post_bwd(
            d_out, a['yc'], a['o'], a['lse'], a['proj'], a['ys'], p['bng'], p['wout'], f"post_bwd_{l}")
        dq, dk, dv = _attn_bwd(a['q'], a['k'], a['v'], d_o, stats, f"attn_bwd_{l}")
        d_a, g_cw, g_pww, gv_c = _conv_bwd(a['proj'], d_yc, p['cw'], p['cb'], p['clg'], p['clb'], p['pww'], p['pwb'],
                                           f"conv_bwd_{l}")
        d_sg, g_wm, dms, gv_s = _sgu_bwd(a['proj'], d_ys, p['slg'], p['slb'], p['wm'], p['wmt'], p['sbx'],
                                         f"sgu_bwd_{l}")
        d_cq, d_ckv, d_kr, g_wuq, g_wukv, gv_m = _mla_bwd(
            a['proj'], rc, rs, p['qg'], p['wuq'], p['kvg'], p['wukv'], p['gq'], p['gk'], dq, dk, dv, f"mla_bwd_{l}")
        pieces = [(d_a, C_A), (d_cq, C_CQ), (d_zm, C_ZM), (d_ckv, C_CKV), (d_sg, C_SG), (d_kr, C_KR)]
        d_x, h_b, g_ng = _proj_bwd(a['x'], p['ng'], p['win'], d_out, pieces, f"proj_bwd_{l}")
        g_win = jnp.concatenate([_tn_acc(h_b, dp, f"win_grad_{l}_{off}") for dp, off in pieces], axis=1)
        d_out = d_x

        grads['norm_g'][l] = g_ng[0]
        grads['w_in'][l] = _from_layout(g_win)
        grads['conv_w'][l] = g_cw[:CONV_K]
        grads['conv_b'][l] = gv_c[0]
        grads['conv_ln_g'][l] = gv_c[1]
        grads['conv_ln_b'][l] = gv_c[2]
        grads['conv_pw_w'][l] = g_pww
        grads['conv_pw_b'][l] = gv_c[3]
        grads['q_norm_g'][l] = gv_m[0, :Q_LORA]
        grads['w_uq'][l] = g_wuq.reshape(Q_LORA, HEADS, HEAD_PAD)[:, :, :QK].reshape(Q_LORA, HEADS * QK)
        grads['kv_norm_g'][l] = gv_m[1, :KV_LORA]
        grads['w_ukv'][l] = jnp.concatenate(
            [g_wukv[:, :QW].reshape(KV_LORA, HEADS, HEAD_PAD)[:, :, :NOPE],
             g_wukv[:, QW:].reshape(KV_LORA, HEADS, V_DIM)], axis=2).reshape(KV_LORA, HEADS * (NOPE + V_DIM))
        grads['qk_q_g'][l] = gv_m[2, :QK]
        grads['qk_k_g'][l] = gv_m[3, :QK]
        grads['sg_ln_g'][l] = gv_s[0]
        grads['sg_ln_b'][l] = gv_s[1]
        grads['sg_w'][l] = jnp.where(tril[None], g_wm, 0.0)
        grads['sg_b'][l] = dms.reshape(SG_CHUNK, SG_HEADS, SG_W // SG_HEADS).sum(axis=2).T
        grads['branch_norm_g'][l] = g_bng[0]
        grads['w_out'][l] = g_wout
    grad_x = d_out.reshape(x.shape)
    g_full = {n: jnp.stack(grads[n]) for n in W_NAMES}

    gs = _pack8([_split8(g_full[n], SHARD_AXIS[n]) for n in packed], rows_sh)
    gs_win = _split8(g_full['w_in'], SHARD_AXIS['w_in']).reshape((N_DEV,) + win_2d)
    gr = _pack([g_full[n] for n in REPL], rows_rp)
    parts_sh, parts_win, parts_rp = _grad_exchange([gs, gs_win], gr, "grad_exchange")
    res_sh = _adamw(parts_sh, _pack([wts[n] for n in packed], rows_sh), _pack([mom_m[n] for n in packed], rows_sh),
                    _pack([mom_v[n] for n in packed], rows_sh), "adamw_sharded")
    res_win = _adamw(parts_win, wts['w_in'].reshape(win_2d), mom_m['w_in'].reshape(win_2d),
                     mom_v['w_in'].reshape(win_2d), "adamw_w_in")
    res_rp = _adamw(parts_rp, _pack([wts[n] for n in REPL], rows_rp), _pack([mom_m[n] for n in REPL], rows_rp),
                    _pack([mom_v[n] for n in REPL], rows_rp), "adamw_replicated")
    outs = []
    for kind in range(4):
        vals = dict(zip(packed, _unpack(res_sh[kind], sh_shapes)))
        vals['w_in'] = res_win[kind].reshape(win_shape)
        vals.update(zip(REPL, _unpack(res_rp[kind], rp_shapes)))
        outs.extend(vals[n] for n in W_NAMES)
    return (loss, grad_x, *outs)
```

```python
import functools
import math

import jax
import jax.numpy as jnp
from jax import lax
from jax.experimental import pallas as pl
from jax.experimental.pallas import tpu as pltpu

F32 = jnp.float32
BF16 = jnp.bfloat16

N_DEV = 8
DEPTH = 2
D_MODEL = 1024
CONV_W = 256
CONV_K = 31
HEADS = 8
NOPE = 64
ROPE = 32
QK = NOPE + ROPE
HEAD_PAD = 128
V_DIM = 64
MLA_W = HEADS * V_DIM
Q_LORA = 768
KV_LORA = 256
SG_W = 256
SG_HEADS = 4
SG_CHUNK = 128
ROPE_THETA = 10000.0
EPS = 1e-6
IN_COLS = 3104
NP = 3200
C_A, C_CQ, C_ZM, C_CKV, C_SG, C_KR = 0, 768, 1536, 2048, 2304, 3072
HALO = 32
SUB = 64
NEG = -1e30
LANES = 128
VMEM_LIMIT_V7X = 52 * 1024 * 1024

ADAM_LR = 0.001
ADAM_B1 = 0.9
ADAM_B2 = 0.999
ADAM_EPS = 1e-08
ADAM_WD = 0.01
ADAM_STEP = 10

W_NAMES = ['norm_g', 'w_in', 'conv_w', 'conv_b', 'conv_ln_g', 'conv_ln_b', 'conv_pw_w', 'conv_pw_b',
           'q_norm_g', 'w_uq', 'kv_norm_g', 'w_ukv', 'qk_q_g', 'qk_k_g', 'sg_ln_g', 'sg_ln_b', 'sg_w',
           'sg_b', 'branch_norm_g', 'w_out']
SHARD_AXIS = {'w_in': 2, 'conv_w': 2, 'conv_pw_w': 1, 'w_uq': 1, 'w_ukv': 2, 'w_out': 1}
SHARDED = [n for n in W_NAMES if n in SHARD_AXIS]
REPL = [n for n in W_NAMES if n not in SHARD_AXIS]


def _tile(s):
    for t in (512, 256, 128):
        if s % t == 0 and s // t >= 2:
            return t
    return s


def _cp(sem):
    return pltpu.CompilerParams(dimension_semantics=sem, vmem_limit_bytes=VMEM_LIMIT_V7X)


def _mm(a, b):
    return jnp.dot(a.astype(BF16), b.astype(BF16), preferred_element_type=F32)


def _mm_nt(a, b):
    return lax.dot_general(a.astype(BF16), b.astype(BF16), (((1,), (1,)), ((), ())),
                           preferred_element_type=F32)


def _mm_tn(a, b):
    return lax.dot_general(a.astype(BF16), b.astype(BF16), (((0,), (0,)), ((), ())),
                           preferred_element_type=F32)


_GC = math.sqrt(2.0 / math.pi)
_GA = 0.044715


def _sig(x):
    return 1.0 / (1.0 + jnp.exp(-x))


def _silu(x):
    return x * _sig(x)


def _dsilu(x):
    s = _sig(x)
    return s * (1.0 + x * (1.0 - s))


def _gelu(x):
    return 0.5 * x * (1.0 + jnp.tanh(_GC * (x + _GA * x * x * x)))


def _dgelu(x):
    t = jnp.tanh(_GC * (x + _GA * x * x * x))
    return 0.5 * (1.0 + t) + 0.5 * x * (1.0 - t * t) * _GC * (1.0 + 3.0 * _GA * x * x)


def _rsum(x):
    return jnp.sum(x, axis=-1, keepdims=True)


def _csum(x):
    return jnp.sum(x, axis=0, keepdims=True)


def _rms_fwd(x, n):
    r = lax.rsqrt(_rsum(x * x) * (1.0 / n) + EPS)
    return x * r, r


def _rms_bwd(dxh, xn, r, n):
    return r * (dxh - xn * (_rsum(dxh * xn) * (1.0 / n)))


def _ln_fwd(x, n):
    mu = _rsum(x) * (1.0 / n)
    xc = x - mu
    r = lax.rsqrt(_rsum(xc * xc) * (1.0 / n) + EPS)
    return xc * r, r


def _ln_bwd(dxh, xh, r, n):
    return r * (dxh - _rsum(dxh) * (1.0 / n) - xh * (_rsum(dxh * xh) * (1.0 / n)))


def _partner(x, lane):
    return jnp.where(lane < NOPE + ROPE // 2, pltpu.roll(x, LANES - ROPE // 2, 1), pltpu.roll(x, ROPE // 2, 1))


def _row_spec(ts, w, col=0):
    return pl.BlockSpec((ts, w), lambda i, col=col: (i, col))


def _full_spec(shape):
    nd = len(shape)
    return pl.BlockSpec(shape, lambda i, nd=nd: (0,) * nd)


PROJ_CHUNK = 640


def _proj_fwd(x, ng, win_p, name):
    s = x.shape[0]
    ts = _tile(s)

    def body(x_ref, g_ref, w_ref, o_ref):
        xv = x_ref[...]
        xn, _ = _rms_fwd(xv, D_MODEL)
        h = (xn * g_ref[...]).astype(BF16)
        for c in range(0, NP, PROJ_CHUNK):
            o_ref[:, c:c + PROJ_CHUNK] = jnp.dot(h, w_ref[:, c:c + PROJ_CHUNK], preferred_element_type=F32)

    return pl.pallas_call(
        body, name=name, grid=(s // ts,),
        in_specs=[_row_spec(ts, D_MODEL), _full_spec((1, D_MODEL)), _full_spec((D_MODEL, NP))],
        out_specs=_row_spec(ts, NP),
        out_shape=jax.ShapeDtypeStruct((s, NP), F32),
        compiler_params=_cp(("parallel",)),
    )(x, ng, win_p)


def _proj_bwd(x, ng, win_p, d_out, pieces, name):
    s = x.shape[0]
    ts = _tile(s)
    offs = [o for _, o in pieces]
    widths = [p.shape[1] for p, _ in pieces]

    def body(x_ref, g_ref, w_ref, do_ref, *rest):
        p_refs = rest[:len(pieces)]
        dx_ref, h_ref, gg_ref = rest[len(pieces):]
        i = pl.program_id(0)
        xv = x_ref[...]
        xn, r = _rms_fwd(xv, D_MODEL)
        g = g_ref[...]
        h_ref[...] = (xn * g).astype(BF16)
        dh = jnp.zeros((ts, D_MODEL), F32)
        for p_ref, off, w in zip(p_refs, offs, widths):
            dh = dh + _mm_nt(p_ref[...], w_ref[:, off:off + w])

        @pl.when(i == 0)
        def _():
            gg_ref[...] = jnp.zeros_like(gg_ref)

        gg_ref[...] += _csum(dh * xn)
        dx_ref[...] = _rms_bwd(dh * g, xn, r, D_MODEL) + do_ref[...]

    in_specs = [_row_spec(ts, D_MODEL), _full_spec((1, D_MODEL)), _full_spec((D_MODEL, NP)), _row_spec(ts, D_MODEL)]
    in_specs += [_row_spec(ts, w) for w in widths]
    return pl.pallas_call(
        body, name=name, grid=(s // ts,),
        in_specs=in_specs,
        out_specs=[_row_spec(ts, D_MODEL), _row_spec(ts, D_MODEL), _full_spec((1, D_MODEL))],
        out_shape=[jax.ShapeDtypeStruct((s, D_MODEL), F32), jax.ShapeDtypeStruct((s, D_MODEL), BF16),
                   jax.ShapeDtypeStruct((1, D_MODEL), F32)],
        compiler_params=_cp(("arbitrary",)),
    )(x, ng, win_p, d_out, *[p for p, _ in pieces])


def _tn_acc(a, b, name):
    s, m = a.shape
    n = b.shape[1]
    ts = _tile(s)

    def body(a_ref, b_ref, o_ref):
        @pl.when(pl.program_id(0) == 0)
        def _():
            o_ref[...] = jnp.zeros_like(o_ref)

        o_ref[...] += _mm_tn(a_ref[...], b_ref[...])

    return pl.pallas_call(
        body, name=name, grid=(s // ts,),
        in_specs=[_row_spec(ts, m), _row_spec(ts, n)],
        out_specs=_full_spec((m, n)),
        out_shape=jax.ShapeDtypeStruct((m, n), F32),
        compiler_params=_cp(("arbitrary",)),
    )(a, b)


def _halo_spec(ts):
    per = ts // HALO
    return pl.BlockSpec((HALO, 2 * CONV_W), lambda i: (jnp.maximum(i * per - 1, 0), 0))


def _conv_taps(ext_ref, cw_ref, cv_ref, cb, ts):
    base = HALO - (CONV_K - 1)
    for r0 in range(0, ts, SUB):
        acc = jnp.zeros((SUB, CONV_W), F32)
        for k in range(CONV_K):
            acc = acc + cw_ref[k:k + 1, :] * ext_ref[r0 + base + k:r0 + base + k + SUB, :]
        cv_ref[r0:r0 + SUB, :] = acc + cb


def _conv_fwd(proj, cw, cb, lg, lb, pww, pwb, name):
    s = proj.shape[0]
    ts = _tile(s)

    def body(pa_ref, ph_ref, cw_ref, cb_ref, lg_ref, lb_ref, pww_ref, pwb_ref, y_ref, ext_ref, cv_ref):
        i = pl.program_id(0)
        pa = pa_ref[...]
        a, ag, zc = pa[:, :CONV_W], pa[:, CONV_W:2 * CONV_W], pa[:, 2 * CONV_W:]
        ph = ph_ref[...]
        hglu = ph[:, :CONV_W] * _sig(ph[:, CONV_W:])
        ext_ref[0:HALO, :] = jnp.where(i > 0, hglu, 0.0)
        ext_ref[HALO:HALO + ts, :] = a * _sig(ag)
        _conv_taps(ext_ref, cw_ref, cv_ref, cb_ref[...], ts)
        xh, _ = _ln_fwd(cv_ref[...], CONV_W)
        ln = xh * lg_ref[...] + lb_ref[...]
        pw = _mm(_silu(ln), pww_ref[...]) + pwb_ref[...]
        y_ref[...] = pw * _silu(zc)

    vec = _full_spec((1, CONV_W))
    return pl.pallas_call(
        body, name=name, grid=(s // ts,),
        in_specs=[_row_spec(ts, 3 * CONV_W, 0), _halo_spec(ts), _full_spec((HALO, CONV_W)), vec, vec, vec,
                  _full_spec((CONV_W, CONV_W)), vec],
        out_specs=_row_spec(ts, CONV_W),
        out_shape=jax.ShapeDtypeStruct((s, CONV_W), F32),
        scratch_shapes=[pltpu.VMEM((HALO + ts, CONV_W), F32), pltpu.VMEM((ts, CONV_W), F32)],
        compiler_params=_cp(("parallel",)),
    )(proj, proj, cw, cb, lg, lb, pww, pwb)


def _conv_bwd(proj, dy, cw, cb, lg, lb, pww, pwb, name):
    s = proj.shape[0]
    ts = _tile(s)
    nt = s // ts
    per = ts // HALO

    def body(pa_ref, ph_ref, dy_ref, cw_ref, cb_ref, lg_ref, lb_ref, pww_ref, pwb_ref,
             dp_ref, gcw_ref, gpw_ref, gv_ref, ext_ref, cv_ref, dext_ref, carry_ref, gacc_ref):
        i = pl.program_id(0)
        ti = nt - 1 - i

        @pl.when(i == 0)
        def _():
            carry_ref[...] = jnp.zeros_like(carry_ref)
            gacc_ref[...] = jnp.zeros_like(gacc_ref)
            gpw_ref[...] = jnp.zeros_like(gpw_ref)
            gv_ref[...] = jnp.zeros_like(gv_ref)

        pa = pa_ref[...]
        a, ag, zc = pa[:, :CONV_W], pa[:, CONV_W:2 * CONV_W], pa[:, 2 * CONV_W:]
        sag = _sig(ag)
        ph = ph_ref[...]
        hglu = ph[:, :CONV_W] * _sig(ph[:, CONV_W:])
        ext_ref[0:HALO, :] = jnp.where(ti > 0, hglu, 0.0)
        ext_ref[HALO:HALO + ts, :] = a * sag
        _conv_taps(ext_ref, cw_ref, cv_ref, cb_ref[...], ts)
        xh, rl = _ln_fwd(cv_ref[...], CONV_W)
        lg = lg_ref[...]
        ln = xh * lg + lb_ref[...]
        sw = _silu(ln)
        pww = pww_ref[...]
        pw = _mm(sw, pww) + pwb_ref[...]
        d_y = dy_ref[...]
        d_pw = d_y * _silu(zc)
        d_zc = d_y * pw * _dsilu(zc)
        gpw_ref[...] += _mm_tn(sw, d_pw)
        d_ln = _mm_nt(d_pw, pww) * _dsilu(ln)
        d_cv = _ln_bwd(d_ln * lg, xh, rl, CONV_W)
        gv_ref[0:1, :] += _csum(d_cv)
        gv_ref[1:2, :] += _csum(d_ln * xh)
        gv_ref[2:3, :] += _csum(d_ln)
        gv_ref[3:4, :] += _csum(d_pw)
        dext_ref[0:ts, :] = d_cv
        dext_ref[ts:ts + HALO, :] = carry_ref[...]
        carry_ref[...] = d_cv[0:HALO, :]
        base = HALO - (CONV_K - 1)
        for r0 in range(0, ts, SUB):
            dcv_r = dext_ref[r0:r0 + SUB, :]
            dg = jnp.zeros((SUB, CONV_W), F32)
            for k in range(CONV_K):
                prod = dcv_r * ext_ref[r0 + base + k:r0 + base + k + SUB, :]
                gacc_ref[8 * k:8 * k + 8, :] += jnp.sum(prod.reshape(SUB // 8, 8, CONV_W), axis=0)
                dg = dg + cw_ref[k:k + 1, :] * dext_ref[r0 + CONV_K - 1 - k:r0 + CONV_K - 1 - k + SUB, :]
            dp_ref[r0:r0 + SUB, 0:CONV_W] = dg * sag[r0:r0 + SUB, :]
            dp_ref[r0:r0 + SUB, CONV_W:2 * CONV_W] = dg * a[r0:r0 + SUB, :] * sag[r0:r0 + SUB, :] * (1.0 - sag[r0:r0 + SUB, :])
        dp_ref[:, 2 * CONV_W:] = d_zc

        @pl.when(i == nt - 1)
        def _():
            gcw_ref[...] = jnp.zeros_like(gcw_ref)
            for k in range(CONV_K):
                gcw_ref[k:k + 1, :] = _csum(gacc_ref[8 * k:8 * k + 8, :])

    vec = _full_spec((1, CONV_W))
    rev = lambda w, col=0: pl.BlockSpec((ts, w), lambda i, col=col: (nt - 1 - i, col))
    halo = pl.BlockSpec((HALO, 2 * CONV_W), lambda i: (jnp.maximum((nt - 1 - i) * per - 1, 0), 0))
    return pl.pallas_call(
        body, name=name, grid=(nt,),
        in_specs=[rev(3 * CONV_W), halo, rev(CONV_W), _full_spec((HALO, CONV_W)), vec, vec, vec,
                  _full_spec((CONV_W, CONV_W)), vec],
        out_specs=[rev(3 * CONV_W), _full_spec((HALO, CONV_W)), _full_spec((CONV_W, CONV_W)), _full_spec((8, CONV_W))],
        out_shape=[jax.ShapeDtypeStruct((s, 3 * CONV_W), F32), jax.ShapeDtypeStruct((HALO, CONV_W), F32),
                   jax.ShapeDtypeStruct((CONV_W, CONV_W), F32), jax.ShapeDtypeStruct((8, CONV_W), F32)],
        scratch_shapes=[pltpu.VMEM((HALO + ts, CONV_W), F32), pltpu.VMEM((ts, CONV_W), F32),
                        pltpu.VMEM((ts + HALO, CONV_W), F32), pltpu.VMEM((HALO, CONV_W), F32),
                        pltpu.VMEM((8 * HALO, CONV_W), F32)],
        compiler_params=_cp(("arbitrary",)),
    )(proj, proj, dy, cw, cb, lg, lb, pww, pwb)


def _sg_mix(wm_ref, vc, head):
    out = jnp.zeros((SG_CHUNK, SG_W), F32)
    vb = vc.astype(BF16)
    for g in range(SG_HEADS):
        out = jnp.where(head == g, jnp.dot(wm_ref[g], vb, preferred_element_type=F32), out)
    return out


def _sgu_fwd(proj, lg, lb, wm, sbx, name):
    s = proj.shape[0]
    ts = _tile(s)

    def body(ps_ref, lg_ref, lb_ref, wm_ref, sbx_ref, y_ref, mix_ref):
        ps = ps_ref[...]
        us, vs, zs = ps[:, :SG_W], ps[:, SG_W:2 * SG_W], ps[:, 2 * SG_W:]
        xh, _ = _ln_fwd(_gelu(vs), SG_W)
        vn = xh * lg_ref[...] + lb_ref[...]
        head = lax.broadcasted_iota(jnp.int32, (SG_CHUNK, SG_W), 1) // (SG_W // SG_HEADS)
        for c0 in range(0, ts, SG_CHUNK):
            mix_ref[c0:c0 + SG_CHUNK, :] = _sg_mix(wm_ref, vn[c0:c0 + SG_CHUNK, :], head) + sbx_ref[...]
        y_ref[...] = _gelu(us) * mix_ref[...] * _silu(zs)

    vec = _full_spec((1, SG_W))
    return pl.pallas_call(
        body, name=name, grid=(s // ts,),
        in_specs=[_row_spec(ts, 3 * SG_W, C_SG // (3 * SG_W)), vec, vec,
                  _full_spec((SG_HEADS, SG_CHUNK, SG_CHUNK)), _full_spec((SG_CHUNK, SG_W))],
        out_specs=_row_spec(ts, SG_W),
        out_shape=jax.ShapeDtypeStruct((s, SG_W), F32),
        scratch_shapes=[pltpu.VMEM((ts, SG_W), F32)],
        compiler_params=_cp(("parallel",)),
    )(proj, lg, lb, wm, sbx)


def _sgu_bwd(proj, dy, lg, lb, wm, wmt, sbx, name):
    s = proj.shape[0]
    ts = _tile(s)

    def body(ps_ref, dy_ref, lg_ref, lb_ref, wm_ref, wmt_ref, sbx_ref,
             dp_ref, gwm_ref, dms_ref, gv_ref, mix_ref, dvn_ref):
        i = pl.program_id(0)

        @pl.when(i == 0)
        def _():
            gwm_ref[...] = jnp.zeros_like(gwm_ref)
            dms_ref[...] = jnp.zeros_like(dms_ref)
            gv_ref[...] = jnp.zeros_like(gv_ref)

        ps = ps_ref[...]
        us, vs, zs = ps[:, :SG_W], ps[:, SG_W:2 * SG_W], ps[:, 2 * SG_W:]
        xh, rl = _ln_fwd(_gelu(vs), SG_W)
        lg = lg_ref[...]
        vn = xh * lg + lb_ref[...]
        head = lax.broadcasted_iota(jnp.int32, (SG_CHUNK, SG_W), 1) // (SG_W // SG_HEADS)
        for c0 in range(0, ts, SG_CHUNK):
            mix_ref[c0:c0 + SG_CHUNK, :] = _sg_mix(wm_ref, vn[c0:c0 + SG_CHUNK, :], head) + sbx_ref[...]
        mixed = mix_ref[...]
        u = _gelu(us)
        sz = _silu(zs)
        d_y = dy_ref[...]
        d_mixed = d_y * u * sz
        dp_ref[:, 0:SG_W] = d_y * mixed * sz * _dgelu(us)
        dp_ref[:, 2 * SG_W:] = d_y * u * mixed * _dsilu(zs)
        dms = jnp.zeros((SG_CHUNK, SG_W), F32)
        for c0 in range(0, ts, SG_CHUNK):
            dm = d_mixed[c0:c0 + SG_CHUNK, :]
            vc = vn[c0:c0 + SG_CHUNK, :]
            dms = dms + dm
            for g in range(SG_HEADS):
                gwm_ref[g] += _mm_nt(jnp.where(head == g, dm, 0.0), vc)
            dvn_ref[c0:c0 + SG_CHUNK, :] = _sg_mix(wmt_ref, dm, head)
        dms_ref[...] += dms
        d_vn = dvn_ref[...]
        gv_ref[0:1, :] += _csum(d_vn * xh)
        gv_ref[1:2, :] += _csum(d_vn)
        dp_ref[:, SG_W:2 * SG_W] = _ln_bwd(d_vn * lg, xh, rl, SG_W) * _dgelu(vs)

    vec = _full_spec((1, SG_W))
    wspec = _full_spec((SG_HEADS, SG_CHUNK, SG_CHUNK))
    return pl.pallas_call(
        body, name=name, grid=(s // ts,),
        in_specs=[_row_spec(ts, 3 * SG_W, C_SG // (3 * SG_W)), _row_spec(ts, SG_W), vec, vec, wspec, wspec,
                  _full_spec((SG_CHUNK, SG_W))],
        out_specs=[_row_spec(ts, 3 * SG_W), wspec, _full_spec((SG_CHUNK, SG_W)), _full_spec((8, SG_W))],
        out_shape=[jax.ShapeDtypeStruct((s, 3 * SG_W), F32), jax.ShapeDtypeStruct((SG_HEADS, SG_CHUNK, SG_CHUNK), F32),
                   jax.ShapeDtypeStruct((SG_CHUNK, SG_W), F32), jax.ShapeDtypeStruct((8, SG_W), F32)],
        scratch_shapes=[pltpu.VMEM((ts, SG_W), F32), pltpu.VMEM((ts, SG_W), F32)],
        compiler_params=_cp(("arbitrary",)),
    )(proj, dy, lg, lb, wm, wmt, sbx)


QW = HEADS * HEAD_PAD
KVW = QW + MLA_W
ATT_SCALE = QK ** -0.5


def _mla_specs(ts):
    return [_row_spec(ts, Q_LORA, C_CQ // Q_LORA), _row_spec(ts, KV_LORA, C_CKV // KV_LORA),
            _row_spec(ts, LANES, C_KR // LANES), _row_spec(ts, LANES), _row_spec(ts, LANES),
            _full_spec((1, Q_LORA)), _full_spec((Q_LORA, QW)), _full_spec((1, KV_LORA)), _full_spec((KV_LORA, KVW)),
            _full_spec((1, LANES)), _full_spec((1, LANES))]


def _mla_fwd(proj, rc, rs, qg, wuq, kvg, wukv, gq, gk, name):
    s = proj.shape[0]
    ts = _tile(s)

    def body(cq_ref, ckv_ref, kr_ref, rc_ref, rs_ref, qg_ref, wuq_ref, kvg_ref, wukv_ref, gq_ref, gk_ref,
             q_ref, k_ref, v_ref):
        lane = lax.broadcasted_iota(jnp.int32, (ts, LANES), 1)
        c, sn = rc_ref[...], rs_ref[...]
        cqn, _ = _rms_fwd(cq_ref[...], Q_LORA)
        q0 = _mm(cqn * qg_ref[...], wuq_ref[...])
        gq = gq_ref[...]
        for h in range(HEADS):
            xn, _ = _rms_fwd(q0[:, h * LANES:(h + 1) * LANES], QK)
            qn = xn * gq
            q_ref[:, h * LANES:(h + 1) * LANES] = ((qn * c + _partner(qn, lane) * sn) * ATT_SCALE).astype(BF16)
        ckvn, _ = _rms_fwd(ckv_ref[...], KV_LORA)
        kv = _mm(ckvn * kvg_ref[...], wukv_ref[...])
        kr = pltpu.roll(kr_ref[...], NOPE, 1)
        gk = gk_ref[...]
        for h in range(HEADS):
            xn, _ = _rms_fwd(kv[:, h * LANES:(h + 1) * LANES] + kr, QK)
            kn = xn * gk
            k_ref[:, h * LANES:(h + 1) * LANES] = (kn * c + _partner(kn, lane) * sn).astype(BF16)
        v_ref[...] = kv[:, QW:].astype(BF16)

    return pl.pallas_call(
        body, name=name, grid=(s // ts,),
        in_specs=_mla_specs(ts),
        out_specs=[_row_spec(ts, QW), _row_spec(ts, QW), _row_spec(ts, MLA_W)],
        out_shape=[jax.ShapeDtypeStruct((s, QW), BF16), jax.ShapeDtypeStruct((s, QW), BF16),
                   jax.ShapeDtypeStruct((s, MLA_W), BF16)],
        compiler_params=_cp(("parallel",)),
    )(proj, proj, proj, rc, rs, qg, wuq, kvg, wukv, gq, gk)


def _mla_bwd(proj, rc, rs, qg, wuq, kvg, wukv, gq, gk, dq, dk, dv, name):
    s = proj.shape[0]
    ts = _tile(s)

    def body(cq_ref, ckv_ref, kr_ref, rc_ref, rs_ref, qg_ref, wuq_ref, kvg_ref, wukv_ref, gq_ref, gk_ref,
             dq_ref, dk_ref, dv_ref, dcq_ref, dckv_ref, dkr_ref, gwuq_ref, gwukv_ref, gv_ref, d0_ref):
        i = pl.program_id(0)

        @pl.when(i == 0)
        def _():
            gwuq_ref[...] = jnp.zeros_like(gwuq_ref)
            gwukv_ref[...] = jnp.zeros_like(gwukv_ref)
            gv_ref[...] = jnp.zeros_like(gv_ref)

        lane = lax.broadcasted_iota(jnp.int32, (ts, LANES), 1)
        c, sn = rc_ref[...], rs_ref[...]
        cq = cq_ref[...]
        cqx, rq0 = _rms_fwd(cq, Q_LORA)
        qg = qg_ref[...]
        cqn = cqx * qg
        wuq = wuq_ref[...]
        q0 = _mm(cqn, wuq)
        gq = gq_ref[...]
        ggq = jnp.zeros((1, LANES), F32)
        for h in range(HEADS):
            xn, r = _rms_fwd(q0[:, h * LANES:(h + 1) * LANES], QK)
            d = dq_ref[:, h * LANES:(h + 1) * LANES] * ATT_SCALE
            d_qn = d * c - _partner(d, lane) * sn
            ggq = ggq + _csum(d_qn * xn)
            d0_ref[:, h * LANES:(h + 1) * LANES] = _rms_bwd(d_qn * gq, xn, r, QK)
        dq0 = d0_ref[:, 0:QW]
        gwuq_ref[...] += _mm_tn(cqn, dq0)
        d_cqn = _mm_nt(dq0, wuq)
        gv_ref[0:1, 0:Q_LORA] += _csum(d_cqn * cqx)
        gv_ref[2:3, 0:LANES] += ggq
        dcq_ref[...] = _rms_bwd(d_cqn * qg, cqx, rq0, Q_LORA)
        ckv = ckv_ref[...]
        ckx, rk0 = _rms_fwd(ckv, KV_LORA)
        kvg = kvg_ref[...]
        ckvn = ckx * kvg
        wukv = wukv_ref[...]
        kv = _mm(ckvn, wukv)
        kr = pltpu.roll(kr_ref[...], NOPE, 1)
        gk = gk_ref[...]
        ggk = jnp.zeros((1, LANES), F32)
        dkr = jnp.zeros((ts, LANES), F32)
        for h in range(HEADS):
            xn, r = _rms_fwd(kv[:, h * LANES:(h + 1) * LANES] + kr, QK)
            d = dk_ref[:, h * LANES:(h + 1) * LANES]
            d_kn = d * c - _partner(d, lane) * sn
            ggk = ggk + _csum(d_kn * xn)
            d_k0 = _rms_bwd(d_kn * gk, xn, r, QK)
            dkr = dkr + d_k0
            d0_ref[:, h * LANES:(h + 1) * LANES] = d_k0
        d0_ref[:, QW:KVW] = dv_ref[...]
        dkv = d0_ref[...]
        dkr_ref[...] = jnp.where(lane < ROPE, pltpu.roll(dkr, NOPE, 1), 0.0)
        gwukv_ref[...] += _mm_tn(ckvn, dkv)
        d_ckvn = _mm_nt(dkv, wukv)
        gv_ref[1:2, 0:KV_LORA] += _csum(d_ckvn * ckx)
        gv_ref[3:4, 0:LANES] += ggk
        dckv_ref[...] = _rms_bwd(d_ckvn * kvg, ckx, rk0, KV_LORA)

    return pl.pallas_call(
        body, name=name, grid=(s // ts,),
        in_specs=_mla_specs(ts) + [_row_spec(ts, QW), _row_spec(ts, QW), _row_spec(ts, MLA_W)],
        out_specs=[_row_spec(ts, Q_LORA), _row_spec(ts, KV_LORA), _row_spec(ts, LANES),
                   _full_spec((Q_LORA, QW)), _full_spec((KV_LORA, KVW)), _full_spec((8, QW))],
        out_shape=[jax.ShapeDtypeStruct((s, Q_LORA), F32), jax.ShapeDtypeStruct((s, KV_LORA), F32),
                   jax.ShapeDtypeStruct((s, LANES), F32), jax.ShapeDtypeStruct((Q_LORA, QW), F32),
                   jax.ShapeDtypeStruct((KV_LORA, KVW), F32), jax.ShapeDtypeStruct((8, QW), F32)],
        scratch_shapes=[pltpu.VMEM((ts, KVW), F32)],
        compiler_params=_cp(("arbitrary",)),
    )(proj, proj, proj, rc, rs, qg, wuq, kvg, wukv, gq, gk, dq, dk, dv)


PAIRS = HEADS // 2
ATT_STRIP = 32


def _attn_fwd(q, k, v, name):
    s = q.shape[0]
    tq = _tile(s)
    tk = tq

    def body(q_ref, k_ref, v_ref, o_ref, lse_ref, s0_ref, s1_ref, p0_ref, p1_ref, m_ref, l_ref, acc_ref):
        i = pl.program_id(1)
        s_refs, p_refs = (s0_ref, s1_ref), (p0_ref, p1_ref)
        row = lax.broadcasted_iota(jnp.int32, (ATT_STRIP, tk), 0)
        col = lax.broadcasted_iota(jnp.int32, (ATT_STRIP, tk), 1)
        first = lax.broadcasted_iota(jnp.int32, (tq, LANES), 1) < V_DIM
        qs = [q_ref[:, a * LANES:(a + 1) * LANES] for a in range(2)]
        m_ref[...] = jnp.full(m_ref.shape, NEG, F32)
        l_ref[...] = jnp.zeros(l_ref.shape, F32)
        acc_ref[...] = jnp.zeros(acc_ref.shape, F32)

        def blk(j, masked):
            st = pl.multiple_of(j * tk, tk)
            vj = v_ref[pl.ds(st, tk), :]
            for a in range(2):
                s_refs[a][...] = _mm_nt(qs[a], k_ref[pl.ds(st, tk), a * LANES:(a + 1) * LANES])
            for a in range(2):
                for r in range(0, tq, ATT_STRIP):
                    sc = s_refs[a][r:r + ATT_STRIP, :]
                    if masked:
                        sc = jnp.where(col <= row + r, sc, NEG)
                    m_old = m_ref[a, r:r + ATT_STRIP, :]
                    m_new = jnp.maximum(m_old, jnp.max(sc, axis=-1, keepdims=True))
                    alpha = jnp.exp(m_old - m_new)
                    p = jnp.exp(sc - jnp.tile(m_new, (1, tk // LANES)))
                    l_ref[a, r:r + ATT_STRIP, :] = alpha * l_ref[a, r:r + ATT_STRIP, :] + _rsum(p)
                    acc_ref[a, r:r + ATT_STRIP, :] = alpha * acc_ref[a, r:r + ATT_STRIP, :]
                    m_ref[a, r:r + ATT_STRIP, :] = m_new
                    p_refs[a][r:r + ATT_STRIP, :] = p.astype(BF16)
                acc_ref[a] += jnp.dot(p_refs[a][...], vj, preferred_element_type=F32)

        def loop_body(j, carry):
            blk(j, False)
            return carry

        lax.fori_loop(0, i, loop_body, 0)
        blk(i, True)
        o_ref[...] = jnp.where(first, acc_ref[0] / l_ref[0], acc_ref[1] / l_ref[1])
        lse_ref[...] = jnp.where(first, m_ref[0] + jnp.log(l_ref[0]), m_ref[1] + jnp.log(l_ref[1]))

    stat = pltpu.VMEM((2, tq, LANES), F32)
    return pl.pallas_call(
        body, name=name, grid=(PAIRS, s // tq),
        in_specs=[pl.BlockSpec((tq, 2 * LANES), lambda p, i: (i, p)),
                  pl.BlockSpec((s, 2 * LANES), lambda p, i: (0, p)),
                  pl.BlockSpec((s, LANES), lambda p, i: (0, p))],
        out_specs=[pl.BlockSpec((tq, LANES), lambda p, i: (i, p)), pl.BlockSpec((tq, LANES), lambda p, i: (i, p))],
        out_shape=[jax.ShapeDtypeStruct((s, MLA_W), F32), jax.ShapeDtypeStruct((s, MLA_W), F32)],
        scratch_shapes=[pltpu.VMEM((tq, tk), F32), pltpu.VMEM((tq, tk), F32), pltpu.VMEM((tq, tk), BF16),
                        pltpu.VMEM((tq, tk), BF16), stat, stat, stat],
        compiler_params=_cp(("parallel", "parallel")),
    )(q, k, v)


def _attn_bwd(q, k, v, do, stats, name):
    s = q.shape[0]
    tq = _tile(s)
    tk = tq
    nq = s // tq

    def body(q_ref, k_ref, v_ref, do_ref, st_ref, dq_ref, dk_ref, dv_ref):
        j = pl.program_id(1)

        @pl.when(j == 0)
        def _():
            dq_ref[...] = jnp.zeros_like(dq_ref)

        dk_ref[...] = jnp.zeros_like(dk_ref)
        dv_ref[...] = jnp.zeros_like(dv_ref)
        row = lax.broadcasted_iota(jnp.int32, (tq, tk), 0)
        col = lax.broadcasted_iota(jnp.int32, (tq, tk), 1)
        lane = lax.broadcasted_iota(jnp.int32, (tq, LANES), 1)
        vj = v_ref[...]
        ks = [k_ref[:, a * LANES:(a + 1) * LANES] for a in range(2)]

        def blk(i, masked):
            st = pl.multiple_of(i * tq, tq)
            do2 = do_ref[pl.ds(st, tq), :]
            stt = st_ref[pl.ds(st, tq), :]
            dv = None
            for a in range(2):
                mine = (lane < V_DIM) if a == 0 else (lane >= V_DIM)
                qa = q_ref[pl.ds(st, tq), a * LANES:(a + 1) * LANES]
                doa = jnp.where(mine, do2, jnp.zeros((), BF16))
                lse = stt[:, a * V_DIM:a * V_DIM + 1]
                dl = stt[:, a * V_DIM + V_DIM // 2:a * V_DIM + V_DIM // 2 + 1]
                p = jnp.exp(_mm_nt(qa, ks[a]) - lse)
                if masked:
                    p = jnp.where(col <= row, p, 0.0)
                ds = (p * (_mm_nt(doa, vj) - dl)).astype(BF16)
                dva = _mm_tn(p, doa)
                dv = dva if dv is None else dv + dva
                dk_ref[:, a * LANES:(a + 1) * LANES] += _mm_tn(ds, qa)
                dq_ref[pl.ds(st, tq), a * LANES:(a + 1) * LANES] += jnp.dot(ds, ks[a], preferred_element_type=F32)
            dv_ref[...] += dv

        blk(j, True)

        def loop_body(i, carry):
            blk(i, False)
            return carry

        lax.fori_loop(j + 1, nq, loop_body, 0)

    return pl.pallas_call(
        body, name=name, grid=(PAIRS, s // tk),
        in_specs=[pl.BlockSpec((s, 2 * LANES), lambda p, j: (0, p)),
                  pl.BlockSpec((tk, 2 * LANES), lambda p, j: (j, p)),
                  pl.BlockSpec((tk, LANES), lambda p, j: (j, p)),
                  pl.BlockSpec((s, LANES), lambda p, j: (0, p)),
                  pl.BlockSpec((s, LANES), lambda p, j: (0, p))],
        out_specs=[pl.BlockSpec((s, 2 * LANES), lambda p, j: (0, p)),
                   pl.BlockSpec((tk, 2 * LANES), lambda p, j: (j, p)),
                   pl.BlockSpec((tk, LANES), lambda p, j: (j, p))],
        out_shape=[jax.ShapeDtypeStruct((s, QW), F32), jax.ShapeDtypeStruct((s, QW), F32),
                   jax.ShapeDtypeStruct((s, MLA_W), F32)],
        compiler_params=_cp(("parallel", "arbitrary")),
    )(q, k, v, do, stats)


BR = ((0, CONV_W), (CONV_W, CONV_W + MLA_W), (CONV_W + MLA_W, D_MODEL))


def _post_fwd(x, yc, o, proj, ys, bng, wout, name):
    s = x.shape[0]
    ts = _tile(s)

    def body(x_ref, yc_ref, o_ref, zm_ref, ys_ref, g_ref, w_ref, out_ref):
        ys3 = (yc_ref[...], o_ref[...] * _silu(zm_ref[...]), ys_ref[...])
        acc = x_ref[...]
        for (lo, hi), yb in zip(BR, ys3):
            yn, _ = _rms_fwd(yb, hi - lo)
            acc = acc + _mm(yn * g_ref[:, lo:hi], w_ref[lo:hi, :])
        out_ref[...] = acc

    return pl.pallas_call(
        body, name=name, grid=(s // ts,),
        in_specs=[_row_spec(ts, D_MODEL), _row_spec(ts, CONV_W), _row_spec(ts, MLA_W),
                  _row_spec(ts, MLA_W, C_ZM // MLA_W), _row_spec(ts, SG_W), _full_spec((1, D_MODEL)),
                  _full_spec((D_MODEL, D_MODEL))],
        out_specs=_row_spec(ts, D_MODEL),
        out_shape=jax.ShapeDtypeStruct((s, D_MODEL), F32),
        compiler_params=_cp(("parallel",)),
    )(x, yc, o, proj, ys, bng, wout)


def _post_bwd(d_out, yc, o, lse, proj, ys, bng, wout, name):
    s = d_out.shape[0]
    ts = _tile(s)

    def body(do_ref, yc_ref, o_ref, lse_ref, zm_ref, ys_ref, g_ref, w_ref,
             dyc_ref, dys_ref, dob_ref, dzm_ref, st_ref, gw_ref, gg_ref, yn_ref):
        i = pl.program_id(0)

        @pl.when(i == 0)
        def _():
            gw_ref[...] = jnp.zeros_like(gw_ref)
            gg_ref[...] = jnp.zeros_like(gg_ref)

        d_out_b = do_ref[...].astype(BF16)
        o = o_ref[...]
        zm = zm_ref[...]
        szm = _silu(zm)
        ys3 = (yc_ref[...], o * szm, ys_ref[...])
        d_ys = []
        for (lo, hi), yb in zip(BR, ys3):
            n = hi - lo
            yn, r = _rms_fwd(yb, n)
            g = g_ref[:, lo:hi]
            yn_ref[:, lo:hi] = (yn * g).astype(BF16)
            d_yn = _mm_nt(d_out_b, w_ref[lo:hi, :])
            gg_ref[:, lo:hi] += _csum(d_yn * yn)
            d_ys.append(_rms_bwd(d_yn * g, yn, r, n))
        gw_ref[...] += _mm_tn(yn_ref[...], d_out_b)
        dyc_ref[...] = d_ys[0]
        dys_ref[...] = d_ys[2]
        d_ym = d_ys[1]
        d_o = d_ym * szm
        dob_ref[...] = d_o.astype(BF16)
        dzm_ref[...] = d_ym * o * _dsilu(zm)
        prod = d_o * o
        head = lax.broadcasted_iota(jnp.int32, (ts, MLA_W), 1) // V_DIM
        delta = jnp.zeros((ts, MLA_W), F32)
        for h in range(HEADS):
            delta = jnp.where(head == h, _rsum(jnp.where(head == h, prod, 0.0)), delta)
        lane = lax.broadcasted_iota(jnp.int32, (ts, MLA_W), 1)
        st_ref[...] = jnp.where(lane % V_DIM < V_DIM // 2, lse_ref[...], delta)

    return pl.pallas_call(
        body, name=name, grid=(s // ts,),
        in_specs=[_row_spec(ts, D_MODEL), _row_spec(ts, CONV_W), _row_spec(ts, MLA_W), _row_spec(ts, MLA_W),
                  _row_spec(ts, MLA_W, C_ZM // MLA_W), _row_spec(ts, SG_W), _full_spec((1, D_MODEL)),
                  _full_spec((D_MODEL, D_MODEL))],
        out_specs=[_row_spec(ts, CONV_W), _row_spec(ts, SG_W), _row_spec(ts, MLA_W), _row_spec(ts, MLA_W),
                   _row_spec(ts, MLA_W), _full_spec((D_MODEL, D_MODEL)), _full_spec((1, D_MODEL))],
        out_shape=[jax.ShapeDtypeStruct((s, CONV_W), F32), jax.ShapeDtypeStruct((s, SG_W), F32),
                   jax.ShapeDtypeStruct((s, MLA_W), BF16), jax.ShapeDtypeStruct((s, MLA_W), F32),
                   jax.ShapeDtypeStruct((s, MLA_W), F32), jax.ShapeDtypeStruct((D_MODEL, D_MODEL), F32),
                   jax.ShapeDtypeStruct((1, D_MODEL), F32)],
        scratch_shapes=[pltpu.VMEM((ts, D_MODEL), BF16)],
        compiler_params=_cp(("arbitrary",)),
    )(d_out, yc, o, lse, proj, ys, bng, wout)


def _loss_head(y, target, name):
    s = y.shape[0]
    ts = _tile(s)
    nt = s // ts

    def body(y_ref, t_ref, dy_ref, l_ref, acc_ref):
        i = pl.program_id(0)

        @pl.when(i == 0)
        def _():
            acc_ref[...] = jnp.zeros_like(acc_ref)

        e = y_ref[...] - t_ref[...]
        dy_ref[...] = e * (1.0 / D_MODEL)
        sq = jnp.sum((e * e).reshape(ts // 8, 8, D_MODEL), axis=0)
        part = sq[:, 0:LANES]
        for c in range(LANES, D_MODEL, LANES):
            part = part + sq[:, c:c + LANES]
        acc_ref[...] += part

        @pl.when(i == nt - 1)
        def _():
            tot = jnp.sum(_rsum(acc_ref[...]), axis=0, keepdims=True) * (0.5 / D_MODEL)
            l_ref[...] = jnp.broadcast_to(tot, (8, LANES))

    return pl.pallas_call(
        body, name=name, grid=(nt,),
        in_specs=[_row_spec(ts, D_MODEL), _row_spec(ts, D_MODEL)],
        out_specs=[_row_spec(ts, D_MODEL), _full_spec((8, LANES))],
        out_shape=[jax.ShapeDtypeStruct((s, D_MODEL), F32), jax.ShapeDtypeStruct((8, LANES), F32)],
        scratch_shapes=[pltpu.VMEM((8, LANES), F32)],
        compiler_params=_cp(("arbitrary",)),
    )(y, target)


MESH = pl.DeviceIdType.MESH
ANY = pl.BlockSpec(memory_space=pl.ANY)


def _all_gather(xs, name):
    n = len(xs)
    per = N_DEV - 1

    def body(*refs):
        x_refs, out_refs = refs[:n], refs[n:2 * n]
        send_sems, recv_sems, local_sems = refs[2 * n:]
        x, y, c = lax.axis_index("x"), lax.axis_index("y"), lax.axis_index("c")
        me, sibling = (x, y, c), (x, y, 1 - c)
        chips = [(1 - x, y), (x, 1 - y), (1 - x, 1 - y)]

        def slot(t, px, py, pc):
            return out_refs[t].at[4 * px + 2 * py + pc]

        def copy(t, k, block, to, src=None):
            return pltpu.make_async_remote_copy(
                src_ref=slot(t, *block) if src is None else src, dst_ref=slot(t, *block),
                send_sem=send_sems.at[t * per + k], recv_sem=recv_sems.at[t * per + k], device_id=to,
                device_id_type=MESH)

        mine = [pltpu.make_async_copy(x_refs[t], slot(t, *me), local_sems.at[t]) for t in range(n)]
        for cp in mine:
            cp.start()
        first = [copy(t, 0, me, sibling, src=x_refs[t]) for t in range(n)]
        first += [copy(t, 1 + j, me, (*chip, c), src=x_refs[t]) for j, chip in enumerate(chips) for t in range(n)]
        for cp in first:
            cp.start()
        passed = []
        for j, chip in enumerate(chips):
            for t in range(n):
                copy(t, 1 + j, (*chip, c), me).wait_recv()
                passed.append(copy(t, 4 + j, (*chip, c), sibling))
                passed[-1].start()
        for t in range(n):
            copy(t, 0, sibling, me).wait_recv()
        for j, chip in enumerate(chips):
            for t in range(n):
                copy(t, 4 + j, (*chip, 1 - c), me).wait_recv()
        for cp in first + passed:
            cp.wait_send()
        for cp in mine:
            cp.wait()

    return pl.pallas_call(
        body, name=name,
        out_shape=[jax.ShapeDtypeStruct((N_DEV,) + a.shape, a.dtype) for a in xs],
        in_specs=[ANY] * n, out_specs=[ANY] * n,
        scratch_shapes=[pltpu.SemaphoreType.DMA((per * n,)), pltpu.SemaphoreType.DMA((per * n,)),
                        pltpu.SemaphoreType.DMA((n,))],
    )(*xs)


def _grad_exchange(gss, gr, name):
    n = len(gss)
    per = N_DEV - 1

    def body(*refs):
        gs_refs, gr_ref = refs[:n], refs[n]
        os_refs, or_ref = refs[n + 1:2 * n + 1], refs[2 * n + 1]
        send_sems, recv_sems, local_sems = refs[2 * n + 2:]
        x, y, c = lax.axis_index("x"), lax.axis_index("y"), lax.axis_index("c")
        me = 4 * x + 2 * y + c
        local = [pltpu.make_async_copy(gs_refs[t].at[me], os_refs[t].at[me], local_sems.at[t]) for t in range(n)]
        local.append(pltpu.make_async_copy(gr_ref, or_ref.at[me], local_sems.at[n]))
        for cp in local:
            cp.start()
        sends, recvs = [], []
        for k in range(1, N_DEV):
            px = 1 - x if k & 4 else x
            py = 1 - y if k & 2 else y
            pc = 1 - c if k & 1 else c
            peer = 4 * px + 2 * py + pc
            to = (px, py, pc)
            for t in range(n + 1):
                sems = dict(send_sem=send_sems.at[t * per + k - 1], recv_sem=recv_sems.at[t * per + k - 1],
                            device_id=to, device_id_type=MESH)
                src = gs_refs[t].at[peer] if t < n else gr_ref
                out = os_refs[t] if t < n else or_ref
                sends.append(pltpu.make_async_remote_copy(src_ref=src, dst_ref=out.at[me], **sems))
                recvs.append(pltpu.make_async_remote_copy(src_ref=src, dst_ref=out.at[peer], **sems))
        for cp in sends:
            cp.start()
        for cp in recvs:
            cp.wait_recv()
        for cp in sends:
            cp.wait_send()
        for cp in local:
            cp.wait()

    nsem = per * (n + 1)
    return pl.pallas_call(
        body, name=name,
        out_shape=[jax.ShapeDtypeStruct(g.shape, g.dtype) for g in gss]
        + [jax.ShapeDtypeStruct((N_DEV,) + gr.shape, gr.dtype)],
        in_specs=[ANY] * (n + 1), out_specs=[ANY] * (n + 1),
        scratch_shapes=[pltpu.SemaphoreType.DMA((nsem,)), pltpu.SemaphoreType.DMA((nsem,)),
                        pltpu.SemaphoreType.DMA((n + 1,))],
    )(*gss, gr)


ADAM_ROWS = 128


def _adamw(parts, w, m, v, name):
    r, cols = w.shape
    tr = ADAM_ROWS if r % ADAM_ROWS == 0 else r

    def body(p_ref, w_ref, m_ref, v_ref, g_ref, d_ref, nm_ref, nv_ref):
        g = p_ref[0].astype(F32)
        for sidx in range(1, N_DEV):
            g = g + p_ref[sidx].astype(F32)
        mm = ADAM_B1 * m_ref[...] + (1.0 - ADAM_B1) * g
        vv = ADAM_B2 * v_ref[...] + (1.0 - ADAM_B2) * (g * g)
        m_hat = mm / (1.0 - ADAM_B1 ** ADAM_STEP)
        v_hat = vv / (1.0 - ADAM_B2 ** ADAM_STEP)
        g_ref[...] = g
        d_ref[...] = -ADAM_LR * (m_hat / (jnp.sqrt(v_hat) + ADAM_EPS) + ADAM_WD * w_ref[...])
        nm_ref[...] = mm
        nv_ref[...] = vv

    row = pl.BlockSpec((tr, cols), lambda i: (i, 0))
    return pl.pallas_call(
        body, name=name, grid=(r // tr,),
        in_specs=[pl.BlockSpec((N_DEV, tr, cols), lambda i: (0, i, 0)), row, row, row],
        out_specs=[row, row, row, row],
        out_shape=[jax.ShapeDtypeStruct((r, cols), F32)] * 4,
        compiler_params=_cp(("parallel",)),
    )(parts, w, m, v)


PACK_W = 8 * LANES
BF16_ROWS = 16


def _pack(flat_parts, rows):
    flat = jnp.concatenate([p.reshape(-1) for p in flat_parts])
    return jnp.pad(flat, (0, rows * PACK_W - flat.shape[0])).reshape(rows, PACK_W)


def _pack8(parts8, rows):
    flat = jnp.concatenate([p.reshape(N_DEV, -1) for p in parts8], axis=1)
    return jnp.pad(flat, ((0, 0), (0, rows * PACK_W - flat.shape[1]))).reshape(N_DEV, rows, PACK_W)


def _rows_for(n, mult):
    rows = -(-n // PACK_W)
    return -(-rows // mult) * mult


def _unshard(arr8, axis):
    full = jnp.moveaxis(arr8, 0, axis)
    shp = list(full.shape)
    shp[axis:axis + 2] = [shp[axis] * shp[axis + 1]]
    return full.reshape(shp)


def _split8(full, axis):
    shp = list(full.shape)
    shp[axis:axis + 1] = [N_DEV, shp[axis] // N_DEV]
    return jnp.moveaxis(full.reshape(shp), axis, 0)


def _unpack(flat2d, shapes, lead=()):
    flat = flat2d.reshape(lead + (-1,))
    out, off = [], 0
    for shp in shapes:
        n = math.prod(shp)
        out.append(flat[..., off:off + n].reshape(lead + tuple(shp)))
        off += n
    return out


def _to_layout(w):
    return jnp.concatenate([w[:, :1536], w[:, 1824:2336], w[:, 1536:1792], w[:, 2336:3104], w[:, 1792:1824],
                            jnp.zeros((w.shape[0], NP - IN_COLS), w.dtype)], axis=1)


def _from_layout(g):
    return jnp.concatenate([g[:, :1536], g[:, C_CKV:C_CKV + KV_LORA], g[:, C_KR:C_KR + ROPE],
                            g[:, C_ZM:C_ZM + MLA_W], g[:, C_SG:C_SG + 3 * SG_W]], axis=1)


def _pad_heads(w, real):
    lead = w.shape[:-1]
    w = w.reshape(lead + (HEADS, real))
    return jnp.pad(w, [(0, 0)] * len(lead) + [(0, 0), (0, HEAD_PAD - real)]).reshape(lead + (QW,))


def _rope_tables(s):
    half = ROPE // 2
    inv_freq = ROPE_THETA ** (-jnp.arange(half, dtype=F32) / half)
    ang = jnp.arange(s, dtype=F32)[:, None] * inv_freq[None, :]
    cos, sin = jnp.cos(ang), jnp.sin(ang)
    ones = jnp.ones((s, NOPE), F32)
    zeros = jnp.zeros((s, NOPE), F32)
    pad = jnp.zeros((s, HEAD_PAD - QK), F32)
    rc = jnp.concatenate([ones, cos, cos, pad + 1.0], axis=1)
    rs = jnp.concatenate([zeros, -sin, sin, pad], axis=1)
    return rc, rs


def kernel(x, norm_g, w_in, conv_w, conv_b, conv_ln_g, conv_ln_b, conv_pw_w, conv_pw_b, q_norm_g, w_uq, kv_norm_g, w_ukv, qk_q_g, qk_k_g, sg_ln_g, sg_ln_b, sg_w, sg_b, branch_norm_g, w_out, loss_target, m_norm_g, m_w_in, m_conv_w, m_conv_b, m_conv_ln_g, m_conv_ln_b, m_conv_pw_w, m_conv_pw_b, m_q_norm_g, m_w_uq, m_kv_norm_g, m_w_ukv, m_qk_q_g, m_qk_k_g, m_sg_ln_g, m_sg_ln_b, m_sg_w, m_sg_b, m_branch_norm_g, m_w_out, v_norm_g, v_w_in, v_conv_w, v_conv_b, v_conv_ln_g, v_conv_ln_b, v_conv_pw_w, v_conv_pw_b, v_q_norm_g, v_w_uq, v_kv_norm_g, v_w_ukv, v_qk_q_g, v_qk_k_g, v_sg_ln_g, v_sg_ln_b, v_sg_w, v_sg_b, v_branch_norm_g, v_w_out):
    given = dict(locals())
    wts = {n: given[n] for n in W_NAMES}
    mom_m = {n: given['m_' + n] for n in W_NAMES}
    mom_v = {n: given['v_' + n] for n in W_NAMES}
    s = x.shape[1]
    xs = x.reshape(s, D_MODEL)
    target = loss_target.reshape(s, D_MODEL)

    packed = [n for n in SHARDED if n != 'w_in']
    sh_shapes = [wts[n].shape for n in packed]
    rp_shapes = [wts[n].shape for n in REPL]
    rows_sh = _rows_for(sum(math.prod(p) for p in sh_shapes), ADAM_ROWS)
    rows_rp = _rows_for(sum(math.prod(p) for p in rp_shapes), BF16_ROWS)
    win_shape = wts['w_in'].shape
    win_2d = (win_shape[0] * win_shape[1], win_shape[2])

    mm_names = [n for n in packed if n != 'conv_w']
    mm_shapes = [wts[n].shape for n in mm_names]
    rows_mm = _rows_for(sum(math.prod(p) for p in mm_shapes), BF16_ROWS)
    cw_shape = wts['conv_w'].shape
    gathered, gathered_win, gathered_cw = _all_gather(
        [_pack([wts[n].astype(BF16) for n in mm_names], rows_mm), wts['w_in'].astype(BF16).reshape(win_2d),
         wts['conv_w'].reshape(cw_shape[0] * cw_shape[1], cw_shape[2])], "weights_all_gather")
    full = {n: _unshard(a, SHARD_AXIS[n]) for n, a in zip(mm_names, _unpack(gathered, mm_shapes, (N_DEV,)))}
    full['w_in'] = _unshard(gathered_win.reshape((N_DEV,) + win_shape), SHARD_AXIS['w_in'])
    full['conv_w'] = _unshard(gathered_cw.reshape((N_DEV,) + cw_shape), SHARD_AXIS['conv_w'])
    full.update({n: wts[n] for n in REPL})

    rc, rs = _rope_tables(s)
    tril = jnp.tril(jnp.ones((SG_CHUNK, SG_CHUNK), dtype=bool))

    def vec(a, width=None):
        a = a.reshape(1, -1)
        return a if width is None else jnp.pad(a, ((0, 0), (0, width - a.shape[1])))

    layers = []
    for l in range(DEPTH):
        p = {n: full[n][l] for n in W_NAMES}
        wukv = p['w_ukv'].reshape(KV_LORA, HEADS, NOPE + V_DIM)
        wm = jnp.where(tril[None], p['sg_w'], 0.0)
        layers.append(dict(
            ng=vec(p['norm_g']), win=_to_layout(p['w_in']).astype(BF16),
            cw=jnp.pad(p['conv_w'], ((0, HALO - CONV_K), (0, 0))), cb=vec(p['conv_b']), clg=vec(p['conv_ln_g']),
            clb=vec(p['conv_ln_b']), pww=p['conv_pw_w'].astype(BF16), pwb=vec(p['conv_pw_b']),
            qg=vec(p['q_norm_g']), wuq=_pad_heads(p['w_uq'], QK).astype(BF16), kvg=vec(p['kv_norm_g']),
            wukv=jnp.concatenate([_pad_heads(wukv[:, :, :NOPE].reshape(KV_LORA, HEADS * NOPE), NOPE),
                                  wukv[:, :, NOPE:].reshape(KV_LORA, MLA_W)], axis=1).astype(BF16),
            gq=vec(p['qk_q_g'], LANES), gk=vec(p['qk_k_g'], LANES),
            slg=vec(p['sg_ln_g']), slb=vec(p['sg_ln_b']), wm=wm.astype(BF16),
            wmt=jnp.swapaxes(wm, 1, 2).astype(BF16),
            sbx=jnp.repeat(p['sg_b'].T, SG_W // SG_HEADS, axis=1),
            bng=vec(p['branch_norm_g']), wout=p['w_out'].astype(BF16)))

    acts = []
    h_in = xs
    for l, p in enumerate(layers):
        proj = _proj_fwd(h_in, p['ng'], p['win'], f"proj_fwd_{l}")
        yc = _conv_fwd(proj, p['cw'], p['cb'], p['clg'], p['clb'], p['pww'], p['pwb'], f"conv_fwd_{l}")
        ys = _sgu_fwd(proj, p['slg'], p['slb'], p['wm'], p['sbx'], f"sgu_fwd_{l}")
        q, k, v = _mla_fwd(proj, rc, rs, p['qg'], p['wuq'], p['kvg'], p['wukv'], p['gq'], p['gk'], f"mla_fwd_{l}")
        o, lse = _attn_fwd(q, k, v, f"attn_fwd_{l}")
        h_out = _post_fwd(h_in, yc, o, proj, ys, p['bng'], p['wout'], f"post_fwd_{l}")
        acts.append(dict(x=h_in, proj=proj, yc=yc, ys=ys, q=q, k=k, v=v, o=o, lse=lse))
        h_in = h_out

    d_out, loss_blk = _loss_head(h_in, target, "loss_head")
    loss = lax.psum(loss_blk[0, 0], ("x", "y", "c"))

    grads = {n: [None] * DEPTH for n in W_NAMES}
    for l in reversed(range(DEPTH)):
        p, a = layers[l], acts[l]
        d_yc, d_ys, d_o, d_zm, stats, g_wout, g_bng = _post_bwd(
            d_out, a['yc'], a['o'], a['lse'], a['proj'], a['ys'], p['bng'], p['wout'], f"post_bwd_{l}")
        dq, dk, dv = _attn_bwd(a['q'], a['k'], a['v'], d_o, stats, f"attn_bwd_{l}")
        d_a, g_cw, g_pww, gv_c = _conv_bwd(a['proj'], d_yc, p['cw'], p['cb'], p['clg'], p['clb'], p['pww'], p['pwb'],
                                           f"conv_bwd_{l}")
        d_sg, g_wm, dms, gv_s = _sgu_bwd(a['proj'], d_ys, p['slg'], p['slb'], p['wm'], p['wmt'], p['sbx'],
                                         f"sgu_bwd_{l}")
        d_cq, d_ckv, d_kr, g_wuq, g_wukv, gv_m = _mla_bwd(
            a['proj'], rc, rs, p['qg'], p['wuq'], p['kvg'], p['wukv'], p['gq'], p['gk'], dq, dk, dv, f"mla_bwd_{l}")
        pieces = [(d_a, C_A), (d_cq, C_CQ), (d_zm, C_ZM), (d_ckv, C_CKV), (d_sg, C_SG), (d_kr, C_KR)]
        d_x, h_b, g_ng = _proj_bwd(a['x'], p['ng'], p['win'], d_out, pieces, f"proj_bwd_{l}")
        g_win = jnp.concatenate([_tn_acc(h_b, dp, f"win_grad_{l}_{off}") for dp, off in pieces], axis=1)
        d_out = d_x

        grads['norm_g'][l] = g_ng[0]
        grads['w_in'][l] = _from_layout(g_win)
        grads['conv_w'][l] = g_cw[:CONV_K]
        grads['conv_b'][l] = gv_c[0]
        grads['conv_ln_g'][l] = gv_c[1]
        grads['conv_ln_b'][l] = gv_c[2]
        grads['conv_pw_w'][l] = g_pww
        grads['conv_pw_b'][l] = gv_c[3]
        grads['q_norm_g'][l] = gv_m[0, :Q_LORA]
        grads['w_uq'][l] = g_wuq.reshape(Q_LORA, HEADS, HEAD_PAD)[:, :, :QK].reshape(Q_LORA, HEADS * QK)
        grads['kv_norm_g'][l] = gv_m[1, :KV_LORA]
        grads['w_ukv'][l] = jnp.concatenate(
            [g_wukv[:, :QW].reshape(KV_LORA, HEADS, HEAD_PAD)[:, :, :NOPE],
             g_wukv[:, QW:].reshape(KV_LORA, HEADS, V_DIM)], axis=2).reshape(KV_LORA, HEADS * (NOPE + V_DIM))
        grads['qk_q_g'][l] = gv_m[2, :QK]
        grads['qk_k_g'][l] = gv_m[3, :QK]
        grads['sg_ln_g'][l] = gv_s[0]
        grads['sg_ln_b'][l] = gv_s[1]
        grads['sg_w'][l] = jnp.where(tril[None], g_wm, 0.0)
        grads['sg_b'][l] = dms.reshape(SG_CHUNK, SG_HEADS, SG_W // SG_HEADS).sum(axis=2).T
        grads['branch_norm_g'][l] = g_bng[0]
        grads['w_out'][l] = g_wout
    grad_x = d_out.reshape(x.shape)
    g_full = {n: jnp.stack(grads[n]) for n in W_NAMES}

    gs = _pack8([_split8(g_full[n].astype(BF16), SHARD_AXIS[n]) for n in packed], rows_sh)
    gs_win = _split8(g_full['w_in'].astype(BF16), SHARD_AXIS['w_in']).reshape((N_DEV,) + win_2d)
    gr = _pack([g_full[n].astype(BF16) for n in REPL], rows_rp)
    parts_sh, parts_win, parts_rp = _grad_exchange([gs, gs_win], gr, "grad_exchange")
    res_sh = _adamw(parts_sh, _pack([wts[n] for n in packed], rows_sh), _pack([mom_m[n] for n in packed], rows_sh),
                    _pack([mom_v[n] for n in packed], rows_sh), "adamw_sharded")
    res_win = _adamw(parts_win, wts['w_in'].reshape(win_2d), mom_m['w_in'].reshape(win_2d),
                     mom_v['w_in'].reshape(win_2d), "adamw_w_in")
    res_rp = _adamw(parts_rp, _pack([wts[n] for n in REPL], rows_rp), _pack([mom_m[n] for n in REPL], rows_rp),
                    _pack([mom_v[n] for n in REPL], rows_rp), "adamw_replicated")
    outs = []
    for kind in range(4):
        vals = dict(zip(packed, _unpack(res_sh[kind], sh_shapes)))
        vals['w_in'] = res_win[kind].reshape(win_shape)
        vals.update(zip(REPL, _unpack(res_rp[kind], rp_shapes)))
        outs.extend(vals[n] for n in W_NAMES)
    return (loss, grad_x, *outs)
```

```python
import functools
import math

import jax
import jax.numpy as jnp
from jax import lax
from jax.experimental import pallas as pl
from jax.experimental.pallas import tpu as pltpu

F32 = jnp.float32
BF16 = jnp.bfloat16

N_DEV = 8
DEPTH = 2
D_MODEL = 1024
CONV_W = 256
CONV_K = 31
HEADS = 8
NOPE = 64
ROPE = 32
QK = NOPE + ROPE
HEAD_PAD = 128
V_DIM = 64
MLA_W = HEADS * V_DIM
Q_LORA = 768
KV_LORA = 256
SG_W = 256
SG_HEADS = 4
SG_CHUNK = 128
ROPE_THETA = 10000.0
EPS = 1e-6
IN_COLS = 3104
NP = 3200
C_A, C_CQ, C_ZM, C_CKV, C_SG, C_KR = 0, 768, 1536, 2048, 2304, 3072
HALO = 32
SUB = 64
NEG = -1e30
LANES = 128
VMEM_LIMIT_V7X = 52 * 1024 * 1024

ADAM_LR = 0.001
ADAM_B1 = 0.9
ADAM_B2 = 0.999
ADAM_EPS = 1e-08
ADAM_WD = 0.01
ADAM_STEP = 10

W_NAMES = ['norm_g', 'w_in', 'conv_w', 'conv_b', 'conv_ln_g', 'conv_ln_b', 'conv_pw_w', 'conv_pw_b',
           'q_norm_g', 'w_uq', 'kv_norm_g', 'w_ukv', 'qk_q_g', 'qk_k_g', 'sg_ln_g', 'sg_ln_b', 'sg_w',
           'sg_b', 'branch_norm_g', 'w_out']
SHARD_AXIS = {'w_in': 2, 'conv_w': 2, 'conv_pw_w': 1, 'w_uq': 1, 'w_ukv': 2, 'w_out': 1}
SHARDED = [n for n in W_NAMES if n in SHARD_AXIS]
REPL = [n for n in W_NAMES if n not in SHARD_AXIS]


def _tile(s):
    for t in (512, 256, 128):
        if s % t == 0 and s // t >= 2:
            return t
    return s


def _cp(sem):
    return pltpu.CompilerParams(dimension_semantics=sem, vmem_limit_bytes=VMEM_LIMIT_V7X)


def _mm(a, b):
    return jnp.dot(a.astype(BF16), b.astype(BF16), preferred_element_type=F32)


def _mm_nt(a, b):
    return lax.dot_general(a.astype(BF16), b.astype(BF16), (((1,), (1,)), ((), ())),
                           preferred_element_type=F32)


def _mm_tn(a, b):
    return lax.dot_general(a.astype(BF16), b.astype(BF16), (((0,), (0,)), ((), ())),
                           preferred_element_type=F32)


_GC = math.sqrt(2.0 / math.pi)
_GA = 0.044715


def _sig(x):
    return 1.0 / (1.0 + jnp.exp(-x))


def _silu(x):
    return x * _sig(x)


def _dsilu(x):
    s = _sig(x)
    return s * (1.0 + x * (1.0 - s))


def _gelu(x):
    return 0.5 * x * (1.0 + jnp.tanh(_GC * (x + _GA * x * x * x)))


def _dgelu(x):
    t = jnp.tanh(_GC * (x + _GA * x * x * x))
    return 0.5 * (1.0 + t) + 0.5 * x * (1.0 - t * t) * _GC * (1.0 + 3.0 * _GA * x * x)


def _rsum(x):
    return jnp.sum(x, axis=-1, keepdims=True)


def _csum(x):
    return jnp.sum(x, axis=0, keepdims=True)


def _rms_fwd(x, n):
    r = lax.rsqrt(_rsum(x * x) * (1.0 / n) + EPS)
    return x * r, r


def _rms_bwd(dxh, xn, r, n):
    return r * (dxh - xn * (_rsum(dxh * xn) * (1.0 / n)))


def _ln_fwd(x, n):
    mu = _rsum(x) * (1.0 / n)
    xc = x - mu
    r = lax.rsqrt(_rsum(xc * xc) * (1.0 / n) + EPS)
    return xc * r, r


def _ln_bwd(dxh, xh, r, n):
    return r * (dxh - _rsum(dxh) * (1.0 / n) - xh * (_rsum(dxh * xh) * (1.0 / n)))


def _partner(x, lane):
    return jnp.where(lane < NOPE + ROPE // 2, pltpu.roll(x, LANES - ROPE // 2, 1), pltpu.roll(x, ROPE // 2, 1))


def _row_spec(ts, w, col=0):
    return pl.BlockSpec((ts, w), lambda i, col=col: (i, col))


def _full_spec(shape):
    nd = len(shape)
    return pl.BlockSpec(shape, lambda i, nd=nd: (0,) * nd)


PROJ_CHUNK = 640


def _proj_fwd(x, ng, win_p, name):
    s = x.shape[0]
    ts = _tile(s)

    def body(x_ref, g_ref, w_ref, o_ref):
        xv = x_ref[...]
        xn, _ = _rms_fwd(xv, D_MODEL)
        h = (xn * g_ref[...]).astype(BF16)
        for c in range(0, NP, PROJ_CHUNK):
            o_ref[:, c:c + PROJ_CHUNK] = jnp.dot(h, w_ref[:, c:c + PROJ_CHUNK], preferred_element_type=F32)

    return pl.pallas_call(
        body, name=name, grid=(s // ts,),
        in_specs=[_row_spec(ts, D_MODEL), _full_spec((1, D_MODEL)), _full_spec((D_MODEL, NP))],
        out_specs=_row_spec(ts, NP),
        out_shape=jax.ShapeDtypeStruct((s, NP), F32),
        compiler_params=_cp(("parallel",)),
    )(x, ng, win_p)


def _proj_bwd(x, ng, win_p, d_out, pieces, name):
    s = x.shape[0]
    ts = _tile(s)
    offs = [o for _, o in pieces]
    widths = [p.shape[1] for p, _ in pieces]

    def body(x_ref, g_ref, w_ref, do_ref, *rest):
        p_refs = rest[:len(pieces)]
        dx_ref, h_ref, gg_ref = rest[len(pieces):]
        i = pl.program_id(0)
        xv = x_ref[...]
        xn, r = _rms_fwd(xv, D_MODEL)
        g = g_ref[...]
        h_ref[...] = (xn * g).T.astype(BF16)
        dh = jnp.zeros((ts, D_MODEL), F32)
        for p_ref, off, w in zip(p_refs, offs, widths):
            dh = dh + _mm_nt(p_ref[...], w_ref[:, off:off + w])

        @pl.when(i == 0)
        def _():
            gg_ref[...] = jnp.zeros_like(gg_ref)

        gg_ref[...] += _csum(dh * xn)
        dx_ref[...] = _rms_bwd(dh * g, xn, r, D_MODEL) + do_ref[...]

    in_specs = [_row_spec(ts, D_MODEL), _full_spec((1, D_MODEL)), _full_spec((D_MODEL, NP)), _row_spec(ts, D_MODEL)]
    in_specs += [_row_spec(ts, w) for w in widths]
    return pl.pallas_call(
        body, name=name, grid=(s // ts,),
        in_specs=in_specs,
        out_specs=[_row_spec(ts, D_MODEL), pl.BlockSpec((D_MODEL, ts), lambda i: (0, i)), _full_spec((1, D_MODEL))],
        out_shape=[jax.ShapeDtypeStruct((s, D_MODEL), F32), jax.ShapeDtypeStruct((D_MODEL, s), BF16),
                   jax.ShapeDtypeStruct((1, D_MODEL), F32)],
        compiler_params=_cp(("arbitrary",)),
    )(x, ng, win_p, d_out, *[p for p, _ in pieces])


WG_TILE = 256


def _win_grad(ht, pieces, name):
    s = ht.shape[1]
    ts = min(WG_TILE, s)
    offs = [o for _, o in pieces]
    widths = [p.shape[1] for p, _ in pieces]

    def body(ht_ref, *rest):
        p_refs, o_ref = rest[:-1], rest[-1]

        @pl.when(pl.program_id(0) == 0)
        def _():
            o_ref[...] = jnp.zeros_like(o_ref)

        hb = ht_ref[...]
        for p_ref, off, w in zip(p_refs, offs, widths):
            o_ref[:, off:off + w] += jnp.dot(hb, p_ref[...].astype(BF16), preferred_element_type=F32)

    return pl.pallas_call(
        body, name=name, grid=(s // ts,),
        in_specs=[pl.BlockSpec((D_MODEL, ts), lambda i: (0, i))] + [_row_spec(ts, w) for w in widths],
        out_specs=_full_spec((D_MODEL, NP)),
        out_shape=jax.ShapeDtypeStruct((D_MODEL, NP), F32),
        compiler_params=_cp(("arbitrary",)),
    )(ht, *[p for p, _ in pieces])


def _halo_spec(ts):
    per = ts // HALO
    return pl.BlockSpec((HALO, 2 * CONV_W), lambda i: (jnp.maximum(i * per - 1, 0), 0))


def _conv_taps(ext_ref, cw_ref, cv_ref, cb, ts):
    base = HALO - (CONV_K - 1)
    for r0 in range(0, ts, SUB):
        acc = jnp.zeros((SUB, CONV_W), F32)
        for k in range(CONV_K):
            acc = acc + cw_ref[k:k + 1, :] * ext_ref[r0 + base + k:r0 + base + k + SUB, :]
        cv_ref[r0:r0 + SUB, :] = acc + cb


def _conv_fwd(proj, cw, cb, lg, lb, pww, pwb, name):
    s = proj.shape[0]
    ts = _tile(s)

    def body(pa_ref, ph_ref, cw_ref, cb_ref, lg_ref, lb_ref, pww_ref, pwb_ref, y_ref, ext_ref, cv_ref):
        i = pl.program_id(0)
        pa = pa_ref[...]
        a, ag, zc = pa[:, :CONV_W], pa[:, CONV_W:2 * CONV_W], pa[:, 2 * CONV_W:]
        ph = ph_ref[...]
        hglu = ph[:, :CONV_W] * _sig(ph[:, CONV_W:])
        ext_ref[0:HALO, :] = jnp.where(i > 0, hglu, 0.0)
        ext_ref[HALO:HALO + ts, :] = a * _sig(ag)
        _conv_taps(ext_ref, cw_ref, cv_ref, cb_ref[...], ts)
        xh, _ = _ln_fwd(cv_ref[...], CONV_W)
        ln = xh * lg_ref[...] + lb_ref[...]
        pw = _mm(_silu(ln), pww_ref[...]) + pwb_ref[...]
        y_ref[...] = pw * _silu(zc)

    vec = _full_spec((1, CONV_W))
    return pl.pallas_call(
        body, name=name, grid=(s // ts,),
        in_specs=[_row_spec(ts, 3 * CONV_W, 0), _halo_spec(ts), _full_spec((HALO, CONV_W)), vec, vec, vec,
                  _full_spec((CONV_W, CONV_W)), vec],
        out_specs=_row_spec(ts, CONV_W),
        out_shape=jax.ShapeDtypeStruct((s, CONV_W), F32),
        scratch_shapes=[pltpu.VMEM((HALO + ts, CONV_W), F32), pltpu.VMEM((ts, CONV_W), F32)],
        compiler_params=_cp(("parallel",)),
    )(proj, proj, cw, cb, lg, lb, pww, pwb)


def _conv_bwd(proj, dy, cw, cb, lg, lb, pww, pwb, name):
    s = proj.shape[0]
    ts = _tile(s)
    nt = s // ts
    per = ts // HALO

    def body(pa_ref, ph_ref, dy_ref, cw_ref, cb_ref, lg_ref, lb_ref, pww_ref, pwb_ref,
             dp_ref, gcw_ref, gpw_ref, gv_ref, ext_ref, cv_ref, dext_ref, carry_ref, gacc_ref):
        i = pl.program_id(0)
        ti = nt - 1 - i

        @pl.when(i == 0)
        def _():
            carry_ref[...] = jnp.zeros_like(carry_ref)
            gacc_ref[...] = jnp.zeros_like(gacc_ref)
            gpw_ref[...] = jnp.zeros_like(gpw_ref)
            gv_ref[...] = jnp.zeros_like(gv_ref)

        pa = pa_ref[...]
        a, ag, zc = pa[:, :CONV_W], pa[:, CONV_W:2 * CONV_W], pa[:, 2 * CONV_W:]
        sag = _sig(ag)
        ph = ph_ref[...]
        hglu = ph[:, :CONV_W] * _sig(ph[:, CONV_W:])
        ext_ref[0:HALO, :] = jnp.where(ti > 0, hglu, 0.0)
        ext_ref[HALO:HALO + ts, :] = a * sag
        _conv_taps(ext_ref, cw_ref, cv_ref, cb_ref[...], ts)
        xh, rl = _ln_fwd(cv_ref[...], CONV_W)
        lg = lg_ref[...]
        ln = xh * lg + lb_ref[...]
        sw = _silu(ln)
        pww = pww_ref[...]
        pw = _mm(sw, pww) + pwb_ref[...]
        d_y = dy_ref[...]
        d_pw = d_y * _silu(zc)
        d_zc = d_y * pw * _dsilu(zc)
        gpw_ref[...] += _mm_tn(sw, d_pw)
        d_ln = _mm_nt(d_pw, pww) * _dsilu(ln)
        d_cv = _ln_bwd(d_ln * lg, xh, rl, CONV_W)
        gv_ref[0:1, :] += _csum(d_cv)
        gv_ref[1:2, :] += _csum(d_ln * xh)
        gv_ref[2:3, :] += _csum(d_ln)
        gv_ref[3:4, :] += _csum(d_pw)
        dext_ref[0:ts, :] = d_cv
        dext_ref[ts:ts + HALO, :] = carry_ref[...]
        carry_ref[...] = d_cv[0:HALO, :]
        base = HALO - (CONV_K - 1)
        for r0 in range(0, ts, SUB):
            dcv_r = dext_ref[r0:r0 + SUB, :]
            dg = jnp.zeros((SUB, CONV_W), F32)
            for k in range(CONV_K):
                prod = dcv_r * ext_ref[r0 + base + k:r0 + base + k + SUB, :]
                gacc_ref[8 * k:8 * k + 8, :] += jnp.sum(prod.reshape(SUB // 8, 8, CONV_W), axis=0)
                dg = dg + cw_ref[k:k + 1, :] * dext_ref[r0 + CONV_K - 1 - k:r0 + CONV_K - 1 - k + SUB, :]
            dp_ref[r0:r0 + SUB, 0:CONV_W] = dg * sag[r0:r0 + SUB, :]
            dp_ref[r0:r0 + SUB, CONV_W:2 * CONV_W] = dg * a[r0:r0 + SUB, :] * sag[r0:r0 + SUB, :] * (1.0 - sag[r0:r0 + SUB, :])
        dp_ref[:, 2 * CONV_W:] = d_zc

        @pl.when(i == nt - 1)
        def _():
            gcw_ref[...] = jnp.zeros_like(gcw_ref)
            for k in range(CONV_K):
                gcw_ref[k:k + 1, :] = _csum(gacc_ref[8 * k:8 * k + 8, :])

    vec = _full_spec((1, CONV_W))
    rev = lambda w, col=0: pl.BlockSpec((ts, w), lambda i, col=col: (nt - 1 - i, col))
    halo = pl.BlockSpec((HALO, 2 * CONV_W), lambda i: (jnp.maximum((nt - 1 - i) * per - 1, 0), 0))
    return pl.pallas_call(
        body, name=name, grid=(nt,),
        in_specs=[rev(3 * CONV_W), halo, rev(CONV_W), _full_spec((HALO, CONV_W)), vec, vec, vec,
                  _full_spec((CONV_W, CONV_W)), vec],
        out_specs=[rev(3 * CONV_W), _full_spec((HALO, CONV_W)), _full_spec((CONV_W, CONV_W)), _full_spec((8, CONV_W))],
        out_shape=[jax.ShapeDtypeStruct((s, 3 * CONV_W), F32), jax.ShapeDtypeStruct((HALO, CONV_W), F32),
                   jax.ShapeDtypeStruct((CONV_W, CONV_W), F32), jax.ShapeDtypeStruct((8, CONV_W), F32)],
        scratch_shapes=[pltpu.VMEM((HALO + ts, CONV_W), F32), pltpu.VMEM((ts, CONV_W), F32),
                        pltpu.VMEM((ts + HALO, CONV_W), F32), pltpu.VMEM((HALO, CONV_W), F32),
                        pltpu.VMEM((8 * HALO, CONV_W), F32)],
        compiler_params=_cp(("arbitrary",)),
    )(proj, proj, dy, cw, cb, lg, lb, pww, pwb)


def _sg_mix(wm_ref, vc, head):
    out = jnp.zeros((SG_CHUNK, SG_W), F32)
    vb = vc.astype(BF16)
    for g in range(SG_HEADS):
        out = jnp.where(head == g, jnp.dot(wm_ref[g], vb, preferred_element_type=F32), out)
    return out


def _sgu_fwd(proj, lg, lb, wm, sbx, name):
    s = proj.shape[0]
    ts = _tile(s)

    def body(ps_ref, lg_ref, lb_ref, wm_ref, sbx_ref, y_ref, mix_ref):
        ps = ps_ref[...]
        us, vs, zs = ps[:, :SG_W], ps[:, SG_W:2 * SG_W], ps[:, 2 * SG_W:]
        xh, _ = _ln_fwd(_gelu(vs), SG_W)
        vn = xh * lg_ref[...] + lb_ref[...]
        head = lax.broadcasted_iota(jnp.int32, (SG_CHUNK, SG_W), 1) // (SG_W // SG_HEADS)
        for c0 in range(0, ts, SG_CHUNK):
            mix_ref[c0:c0 + SG_CHUNK, :] = _sg_mix(wm_ref, vn[c0:c0 + SG_CHUNK, :], head) + sbx_ref[...]
        y_ref[...] = _gelu(us) * mix_ref[...] * _silu(zs)

    vec = _full_spec((1, SG_W))
    return pl.pallas_call(
        body, name=name, grid=(s // ts,),
        in_specs=[_row_spec(ts, 3 * SG_W, C_SG // (3 * SG_W)), vec, vec,
                  _full_spec((SG_HEADS, SG_CHUNK, SG_CHUNK)), _full_spec((SG_CHUNK, SG_W))],
        out_specs=_row_spec(ts, SG_W),
        out_shape=jax.ShapeDtypeStruct((s, SG_W), F32),
        scratch_shapes=[pltpu.VMEM((ts, SG_W), F32)],
        compiler_params=_cp(("parallel",)),
    )(proj, lg, lb, wm, sbx)


def _sgu_bwd(proj, dy, lg, lb, wm, wmt, sbx, name):
    s = proj.shape[0]
    ts = _tile(s)

    def body(ps_ref, dy_ref, lg_ref, lb_ref, wm_ref, wmt_ref, sbx_ref,
             dp_ref, gwm_ref, dms_ref, gv_ref, mix_ref, dvn_ref):
        i = pl.program_id(0)

        @pl.when(i == 0)
        def _():
            gwm_ref[...] = jnp.zeros_like(gwm_ref)
            dms_ref[...] = jnp.zeros_like(dms_ref)
            gv_ref[...] = jnp.zeros_like(gv_ref)

        ps = ps_ref[...]
        us, vs, zs = ps[:, :SG_W], ps[:, SG_W:2 * SG_W], ps[:, 2 * SG_W:]
        xh, rl = _ln_fwd(_gelu(vs), SG_W)
        lg = lg_ref[...]
        vn = xh * lg + lb_ref[...]
        head = lax.broadcasted_iota(jnp.int32, (SG_CHUNK, SG_W), 1) // (SG_W // SG_HEADS)
        for c0 in range(0, ts, SG_CHUNK):
            mix_ref[c0:c0 + SG_CHUNK, :] = _sg_mix(wm_ref, vn[c0:c0 + SG_CHUNK, :], head) + sbx_ref[...]
        mixed = mix_ref[...]
        u = _gelu(us)
        sz = _silu(zs)
        d_y = dy_ref[...]
        d_mixed = d_y * u * sz
        dp_ref[:, 0:SG_W] = d_y * mixed * sz * _dgelu(us)
        dp_ref[:, 2 * SG_W:] = d_y * u * mixed * _dsilu(zs)
        dms = jnp.zeros((SG_CHUNK, SG_W), F32)
        for c0 in range(0, ts, SG_CHUNK):
            dm = d_mixed[c0:c0 + SG_CHUNK, :]
            vc = vn[c0:c0 + SG_CHUNK, :]
            dms = dms + dm
            for g in range(SG_HEADS):
                gwm_ref[g] += _mm_nt(jnp.where(head == g, dm, 0.0), vc)
            dvn_ref[c0:c0 + SG_CHUNK, :] = _sg_mix(wmt_ref, dm, head)
        dms_ref[...] += dms
        d_vn = dvn_ref[...]
        gv_ref[0:1, :] += _csum(d_vn * xh)
        gv_ref[1:2, :] += _csum(d_vn)
        dp_ref[:, SG_W:2 * SG_W] = _ln_bwd(d_vn * lg, xh, rl, SG_W) * _dgelu(vs)

    vec = _full_spec((1, SG_W))
    wspec = _full_spec((SG_HEADS, SG_CHUNK, SG_CHUNK))
    return pl.pallas_call(
        body, name=name, grid=(s // ts,),
        in_specs=[_row_spec(ts, 3 * SG_W, C_SG // (3 * SG_W)), _row_spec(ts, SG_W), vec, vec, wspec, wspec,
                  _full_spec((SG_CHUNK, SG_W))],
        out_specs=[_row_spec(ts, 3 * SG_W), wspec, _full_spec((SG_CHUNK, SG_W)), _full_spec((8, SG_W))],
        out_shape=[jax.ShapeDtypeStruct((s, 3 * SG_W), F32), jax.ShapeDtypeStruct((SG_HEADS, SG_CHUNK, SG_CHUNK), F32),
                   jax.ShapeDtypeStruct((SG_CHUNK, SG_W), F32), jax.ShapeDtypeStruct((8, SG_W), F32)],
        scratch_shapes=[pltpu.VMEM((ts, SG_W), F32), pltpu.VMEM((ts, SG_W), F32)],
        compiler_params=_cp(("arbitrary",)),
    )(proj, dy, lg, lb, wm, wmt, sbx)


QW = HEADS * HEAD_PAD
KVW = QW + MLA_W
ATT_SCALE = QK ** -0.5


def _mla_specs(ts):
    return [_row_spec(ts, Q_LORA, C_CQ // Q_LORA), _row_spec(ts, KV_LORA, C_CKV // KV_LORA),
            _row_spec(ts, LANES, C_KR // LANES), _row_spec(ts, LANES), _row_spec(ts, LANES),
            _full_spec((1, Q_LORA)), _full_spec((Q_LORA, QW)), _full_spec((1, KV_LORA)), _full_spec((KV_LORA, KVW)),
            _full_spec((1, LANES)), _full_spec((1, LANES))]


def _mla_fwd(proj, rc, rs, qg, wuq, kvg, wukv, gq, gk, name):
    s = proj.shape[0]
    ts = _tile(s)

    def body(cq_ref, ckv_ref, kr_ref, rc_ref, rs_ref, qg_ref, wuq_ref, kvg_ref, wukv_ref, gq_ref, gk_ref,
             q_ref, k_ref, v_ref):
        lane = lax.broadcasted_iota(jnp.int32, (ts, LANES), 1)
        c, sn = rc_ref[...], rs_ref[...]
        cqn, _ = _rms_fwd(cq_ref[...], Q_LORA)
        q0 = _mm(cqn * qg_ref[...], wuq_ref[...])
        gq = gq_ref[...]
        for h in range(HEADS):
            xn, _ = _rms_fwd(q0[:, h * LANES:(h + 1) * LANES], QK)
            qn = xn * gq
            q_ref[:, h * LANES:(h + 1) * LANES] = ((qn * c + _partner(qn, lane) * sn) * ATT_SCALE).astype(BF16)
        ckvn, _ = _rms_fwd(ckv_ref[...], KV_LORA)
        kv = _mm(ckvn * kvg_ref[...], wukv_ref[...])
        kr = pltpu.roll(kr_ref[...], NOPE, 1)
        gk = gk_ref[...]
        for h in range(HEADS):
            xn, _ = _rms_fwd(kv[:, h * LANES:(h + 1) * LANES] + kr, QK)
            kn = xn * gk
            k_ref[:, h * LANES:(h + 1) * LANES] = (kn * c + _partner(kn, lane) * sn).astype(BF16)
        v_ref[...] = kv[:, QW:].astype(BF16)

    return pl.pallas_call(
        body, name=name, grid=(s // ts,),
        in_specs=_mla_specs(ts),
        out_specs=[_row_spec(ts, QW), _row_spec(ts, QW), _row_spec(ts, MLA_W)],
        out_shape=[jax.ShapeDtypeStruct((s, QW), BF16), jax.ShapeDtypeStruct((s, QW), BF16),
                   jax.ShapeDtypeStruct((s, MLA_W), BF16)],
        compiler_params=_cp(("parallel",)),
    )(proj, proj, proj, rc, rs, qg, wuq, kvg, wukv, gq, gk)


def _mla_bwd(proj, rc, rs, qg, wuq, kvg, wukv, gq, gk, dq, dk, dv, name):
    s = proj.shape[0]
    ts = _tile(s)

    def body(cq_ref, ckv_ref, kr_ref, rc_ref, rs_ref, qg_ref, wuq_ref, kvg_ref, wukv_ref, gq_ref, gk_ref,
             dq_ref, dk_ref, dv_ref, dcq_ref, dckv_ref, dkr_ref, gwuq_ref, gwukv_ref, gv_ref, d0_ref):
        i = pl.program_id(0)

        @pl.when(i == 0)
        def _():
            gwuq_ref[...] = jnp.zeros_like(gwuq_ref)
            gwukv_ref[...] = jnp.zeros_like(gwukv_ref)
            gv_ref[...] = jnp.zeros_like(gv_ref)

        lane = lax.broadcasted_iota(jnp.int32, (ts, LANES), 1)
        c, sn = rc_ref[...], rs_ref[...]
        cq = cq_ref[...]
        cqx, rq0 = _rms_fwd(cq, Q_LORA)
        qg = qg_ref[...]
        cqn = cqx * qg
        wuq = wuq_ref[...]
        q0 = _mm(cqn, wuq)
        gq = gq_ref[...]
        ggq = jnp.zeros((1, LANES), F32)
        for h in range(HEADS):
            xn, r = _rms_fwd(q0[:, h * LANES:(h + 1) * LANES], QK)
            d = dq_ref[:, h * LANES:(h + 1) * LANES] * ATT_SCALE
            d_qn = d * c - _partner(d, lane) * sn
            ggq = ggq + _csum(d_qn * xn)
            d0_ref[:, h * LANES:(h + 1) * LANES] = _rms_bwd(d_qn * gq, xn, r, QK)
        dq0 = d0_ref[:, 0:QW]
        gwuq_ref[...] += _mm_tn(cqn, dq0)
        d_cqn = _mm_nt(dq0, wuq)
        gv_ref[0:1, 0:Q_LORA] += _csum(d_cqn * cqx)
        gv_ref[2:3, 0:LANES] += ggq
        dcq_ref[...] = _rms_bwd(d_cqn * qg, cqx, rq0, Q_LORA)
        ckv = ckv_ref[...]
        ckx, rk0 = _rms_fwd(ckv, KV_LORA)
        kvg = kvg_ref[...]
        ckvn = ckx * kvg
        wukv = wukv_ref[...]
        kv = _mm(ckvn, wukv)
        kr = pltpu.roll(kr_ref[...], NOPE, 1)
        gk = gk_ref[...]
        ggk = jnp.zeros((1, LANES), F32)
        dkr = jnp.zeros((ts, LANES), F32)
        for h in range(HEADS):
            xn, r = _rms_fwd(kv[:, h * LANES:(h + 1) * LANES] + kr, QK)
            d = dk_ref[:, h * LANES:(h + 1) * LANES]
            d_kn = d * c - _partner(d, lane) * sn
            ggk = ggk + _csum(d_kn * xn)
            d_k0 = _rms_bwd(d_kn * gk, xn, r, QK)
            dkr = dkr + d_k0
            d0_ref[:, h * LANES:(h + 1) * LANES] = d_k0
        d0_ref[:, QW:KVW] = dv_ref[...]
        dkv = d0_ref[...]
        dkr_ref[...] = jnp.where(lane < ROPE, pltpu.roll(dkr, NOPE, 1), 0.0)
        gwukv_ref[...] += _mm_tn(ckvn, dkv)
        d_ckvn = _mm_nt(dkv, wukv)
        gv_ref[1:2, 0:KV_LORA] += _csum(d_ckvn * ckx)
        gv_ref[3:4, 0:LANES] += ggk
        dckv_ref[...] = _rms_bwd(d_ckvn * kvg, ckx, rk0, KV_LORA)

    return pl.pallas_call(
        body, name=name, grid=(s // ts,),
        in_specs=_mla_specs(ts) + [_row_spec(ts, QW), _row_spec(ts, QW), _row_spec(ts, MLA_W)],
        out_specs=[_row_spec(ts, Q_LORA), _row_spec(ts, KV_LORA), _row_spec(ts, LANES),
                   _full_spec((Q_LORA, QW)), _full_spec((KV_LORA, KVW)), _full_spec((8, QW))],
        out_shape=[jax.ShapeDtypeStruct((s, Q_LORA), F32), jax.ShapeDtypeStruct((s, KV_LORA), F32),
                   jax.ShapeDtypeStruct((s, LANES), F32), jax.ShapeDtypeStruct((Q_LORA, QW), F32),
                   jax.ShapeDtypeStruct((KV_LORA, KVW), F32), jax.ShapeDtypeStruct((8, QW), F32)],
        scratch_shapes=[pltpu.VMEM((ts, KVW), F32)],
        compiler_params=_cp(("arbitrary",)),
    )(proj, proj, proj, rc, rs, qg, wuq, kvg, wukv, gq, gk, dq, dk, dv)


PAIRS = HEADS // 2
ATT_STRIP = 32


def _attn_fwd(q, k, v, name):
    s = q.shape[0]
    tq = _tile(s)
    tk = tq

    def body(q_ref, k_ref, v_ref, o_ref, lse_ref, s0_ref, s1_ref, p0_ref, p1_ref, m_ref, l_ref, acc_ref):
        i = pl.program_id(1)
        s_refs, p_refs = (s0_ref, s1_ref), (p0_ref, p1_ref)
        row = lax.broadcasted_iota(jnp.int32, (ATT_STRIP, tk), 0)
        col = lax.broadcasted_iota(jnp.int32, (ATT_STRIP, tk), 1)
        first = lax.broadcasted_iota(jnp.int32, (tq, LANES), 1) < V_DIM
        qs = [q_ref[:, a * LANES:(a + 1) * LANES] for a in range(2)]
        m_ref[...] = jnp.full(m_ref.shape, NEG, F32)
        l_ref[...] = jnp.zeros(l_ref.shape, F32)
        acc_ref[...] = jnp.zeros(acc_ref.shape, F32)

        def blk(j, masked):
            st = pl.multiple_of(j * tk, tk)
            vj = v_ref[pl.ds(st, tk), :]
            for a in range(2):
                s_refs[a][...] = _mm_nt(qs[a], k_ref[pl.ds(st, tk), a * LANES:(a + 1) * LANES])
            for a in range(2):
                for r in range(0, tq, ATT_STRIP):
                    sc = s_refs[a][r:r + ATT_STRIP, :]
                    if masked:
                        sc = jnp.where(col <= row + r, sc, NEG)
                    m_old = m_ref[a, r:r + ATT_STRIP, :]
                    m_new = jnp.maximum(m_old, jnp.max(sc, axis=-1, keepdims=True))
                    alpha = jnp.exp(m_old - m_new)
                    p = jnp.exp(sc - jnp.tile(m_new, (1, tk // LANES)))
                    l_ref[a, r:r + ATT_STRIP, :] = alpha * l_ref[a, r:r + ATT_STRIP, :] + _rsum(p)
                    acc_ref[a, r:r + ATT_STRIP, :] = alpha * acc_ref[a, r:r + ATT_STRIP, :]
                    m_ref[a, r:r + ATT_STRIP, :] = m_new
                    p_refs[a][r:r + ATT_STRIP, :] = p.astype(BF16)
                acc_ref[a] += jnp.dot(p_refs[a][...], vj, preferred_element_type=F32)

        def two_blocks(t, carry):
            blk(2 * t, False)
            blk(2 * t + 1, False)
            return carry

        lax.fori_loop(0, i // 2, two_blocks, 0)

        @pl.when(i % 2 == 1)
        def _():
            blk(i - 1, False)

        blk(i, True)
        o_ref[...] = jnp.where(first, acc_ref[0] / l_ref[0], acc_ref[1] / l_ref[1])
        lse_ref[...] = jnp.where(first, m_ref[0] + jnp.log(l_ref[0]), m_ref[1] + jnp.log(l_ref[1]))

    stat = pltpu.VMEM((2, tq, LANES), F32)
    return pl.pallas_call(
        body, name=name, grid=(PAIRS, s // tq),
        in_specs=[pl.BlockSpec((tq, 2 * LANES), lambda p, i: (i, p)),
                  pl.BlockSpec((s, 2 * LANES), lambda p, i: (0, p)),
                  pl.BlockSpec((s, LANES), lambda p, i: (0, p))],
        out_specs=[pl.BlockSpec((tq, LANES), lambda p, i: (i, p)), pl.BlockSpec((tq, LANES), lambda p, i: (i, p))],
        out_shape=[jax.ShapeDtypeStruct((s, MLA_W), F32), jax.ShapeDtypeStruct((s, MLA_W), F32)],
        scratch_shapes=[pltpu.VMEM((tq, tk), F32), pltpu.VMEM((tq, tk), F32), pltpu.VMEM((tq, tk), BF16),
                        pltpu.VMEM((tq, tk), BF16), stat, stat, stat],
        compiler_params=_cp(("parallel", "parallel")),
    )(q, k, v)


def _attn_bwd(q, k, v, do, stats, name):
    s = q.shape[0]
    tq = _tile(s)
    tk = tq
    nq = s // tq

    def body(q_ref, k_ref, v_ref, do_ref, st_ref, dq_ref, dk_ref, dv_ref):
        j = pl.program_id(1)

        @pl.when(j == 0)
        def _():
            dq_ref[...] = jnp.zeros_like(dq_ref)

        dk_ref[...] = jnp.zeros_like(dk_ref)
        dv_ref[...] = jnp.zeros_like(dv_ref)
        row = lax.broadcasted_iota(jnp.int32, (tq, tk), 0)
        col = lax.broadcasted_iota(jnp.int32, (tq, tk), 1)
        lane = lax.broadcasted_iota(jnp.int32, (tq, LANES), 1)
        vj = v_ref[...]
        ks = [k_ref[:, a * LANES:(a + 1) * LANES] for a in range(2)]

        def blk(i, masked):
            st = i * tq if isinstance(i, int) else pl.multiple_of(i * tq, tq)
            do2 = do_ref[pl.ds(st, tq), :]
            stt = st_ref[pl.ds(st, tq), :]
            dv = None
            for a in range(2):
                mine = (lane < V_DIM) if a == 0 else (lane >= V_DIM)
                qa = q_ref[pl.ds(st, tq), a * LANES:(a + 1) * LANES]
                doa = jnp.where(mine, do2, jnp.zeros((), BF16))
                lse = stt[:, a * V_DIM:a * V_DIM + 1]
                dl = stt[:, a * V_DIM + V_DIM // 2:a * V_DIM + V_DIM // 2 + 1]
                p = jnp.exp(_mm_nt(qa, ks[a]) - lse)
                if masked:
                    p = jnp.where(col <= row, p, 0.0)
                ds = (p * (_mm_nt(doa, vj) - dl)).astype(BF16)
                dva = _mm_tn(p, doa)
                dv = dva if dv is None else dv + dva
                dk_ref[:, a * LANES:(a + 1) * LANES] += _mm_tn(ds, qa)
                dq_ref[pl.ds(st, tq), a * LANES:(a + 1) * LANES] += jnp.dot(ds, ks[a], preferred_element_type=F32)
            dv_ref[...] += dv

        blk(j, True)
        rest = nq - 1 - j

        def two_blocks(t, carry):
            blk(j + 1 + 2 * t, False)
            blk(j + 2 + 2 * t, False)
            return carry

        lax.fori_loop(0, rest // 2, two_blocks, 0)

        @pl.when(rest % 2 == 1)
        def _():
            blk(nq - 1, False)

    return pl.pallas_call(
        body, name=name, grid=(PAIRS, s // tk),
        in_specs=[pl.BlockSpec((s, 2 * LANES), lambda p, j: (0, p)),
                  pl.BlockSpec((tk, 2 * LANES), lambda p, j: (j, p)),
                  pl.BlockSpec((tk, LANES), lambda p, j: (j, p)),
                  pl.BlockSpec((s, LANES), lambda p, j: (0, p)),
                  pl.BlockSpec((s, LANES), lambda p, j: (0, p))],
        out_specs=[pl.BlockSpec((s, 2 * LANES), lambda p, j: (0, p)),
                   pl.BlockSpec((tk, 2 * LANES), lambda p, j: (j, p)),
                   pl.BlockSpec((tk, LANES), lambda p, j: (j, p))],
        out_shape=[jax.ShapeDtypeStruct((s, QW), F32), jax.ShapeDtypeStruct((s, QW), F32),
                   jax.ShapeDtypeStruct((s, MLA_W), F32)],
        compiler_params=_cp(("parallel", "arbitrary")),
    )(q, k, v, do, stats)


BR = ((0, CONV_W), (CONV_W, CONV_W + MLA_W), (CONV_W + MLA_W, D_MODEL))


def _post_fwd(x, yc, o, proj, ys, bng, wout, name):
    s = x.shape[0]
    ts = _tile(s)

    def body(x_ref, yc_ref, o_ref, zm_ref, ys_ref, g_ref, w_ref, out_ref):
        ys3 = (yc_ref[...], o_ref[...] * _silu(zm_ref[...]), ys_ref[...])
        acc = x_ref[...]
        for (lo, hi), yb in zip(BR, ys3):
            yn, _ = _rms_fwd(yb, hi - lo)
            acc = acc + _mm(yn * g_ref[:, lo:hi], w_ref[lo:hi, :])
        out_ref[...] = acc

    return pl.pallas_call(
        body, name=name, grid=(s // ts,),
        in_specs=[_row_spec(ts, D_MODEL), _row_spec(ts, CONV_W), _row_spec(ts, MLA_W),
                  _row_spec(ts, MLA_W, C_ZM // MLA_W), _row_spec(ts, SG_W), _full_spec((1, D_MODEL)),
                  _full_spec((D_MODEL, D_MODEL))],
        out_specs=_row_spec(ts, D_MODEL),
        out_shape=jax.ShapeDtypeStruct((s, D_MODEL), F32),
        compiler_params=_cp(("parallel",)),
    )(x, yc, o, proj, ys, bng, wout)


def _post_bwd(d_out, yc, o, lse, proj, ys, bng, wout, name):
    s = d_out.shape[0]
    ts = _tile(s)

    def body(do_ref, yc_ref, o_ref, lse_ref, zm_ref, ys_ref, g_ref, w_ref,
             dyc_ref, dys_ref, dob_ref, dzm_ref, st_ref, gw_ref, gg_ref, yn_ref):
        i = pl.program_id(0)

        @pl.when(i == 0)
        def _():
            gw_ref[...] = jnp.zeros_like(gw_ref)
            gg_ref[...] = jnp.zeros_like(gg_ref)

        d_out_b = do_ref[...].astype(BF16)
        o = o_ref[...]
        zm = zm_ref[...]
        szm = _silu(zm)
        ys3 = (yc_ref[...], o * szm, ys_ref[...])
        d_ys = []
        for (lo, hi), yb in zip(BR, ys3):
            n = hi - lo
            yn, r = _rms_fwd(yb, n)
            g = g_ref[:, lo:hi]
            yn_ref[:, lo:hi] = (yn * g).astype(BF16)
            d_yn = _mm_nt(d_out_b, w_ref[lo:hi, :])
            gg_ref[:, lo:hi] += _csum(d_yn * yn)
            d_ys.append(_rms_bwd(d_yn * g, yn, r, n))
        gw_ref[...] += _mm_tn(yn_ref[...], d_out_b)
        dyc_ref[...] = d_ys[0]
        dys_ref[...] = d_ys[2]
        d_ym = d_ys[1]
        d_o = d_ym * szm
        dob_ref[...] = d_o.astype(BF16)
        dzm_ref[...] = d_ym * o * _dsilu(zm)
        prod = d_o * o
        head = lax.broadcasted_iota(jnp.int32, (ts, MLA_W), 1) // V_DIM
        delta = jnp.zeros((ts, MLA_W), F32)
        for h in range(HEADS):
            delta = jnp.where(head == h, _rsum(jnp.where(head == h, prod, 0.0)), delta)
        lane = lax.broadcasted_iota(jnp.int32, (ts, MLA_W), 1)
        st_ref[...] = jnp.where(lane % V_DIM < V_DIM // 2, lse_ref[...], delta)

    return pl.pallas_call(
        body, name=name, grid=(s // ts,),
        in_specs=[_row_spec(ts, D_MODEL), _row_spec(ts, CONV_W), _row_spec(ts, MLA_W), _row_spec(ts, MLA_W),
                  _row_spec(ts, MLA_W, C_ZM // MLA_W), _row_spec(ts, SG_W), _full_spec((1, D_MODEL)),
                  _full_spec((D_MODEL, D_MODEL))],
        out_specs=[_row_spec(ts, CONV_W), _row_spec(ts, SG_W), _row_spec(ts, MLA_W), _row_spec(ts, MLA_W),
                   _row_spec(ts, MLA_W), _full_spec((D_MODEL, D_MODEL)), _full_spec((1, D_MODEL))],
        out_shape=[jax.ShapeDtypeStruct((s, CONV_W), F32), jax.ShapeDtypeStruct((s, SG_W), F32),
                   jax.ShapeDtypeStruct((s, MLA_W), BF16), jax.ShapeDtypeStruct((s, MLA_W), F32),
                   jax.ShapeDtypeStruct((s, MLA_W), F32), jax.ShapeDtypeStruct((D_MODEL, D_MODEL), F32),
                   jax.ShapeDtypeStruct((1, D_MODEL), F32)],
        scratch_shapes=[pltpu.VMEM((ts, D_MODEL), BF16)],
        compiler_params=_cp(("arbitrary",)),
    )(d_out, yc, o, lse, proj, ys, bng, wout)


def _loss_head(y, target, name):
    s = y.shape[0]
    ts = _tile(s)
    nt = s // ts

    def body(y_ref, t_ref, dy_ref, l_ref, acc_ref):
        i = pl.program_id(0)

        @pl.when(i == 0)
        def _():
            acc_ref[...] = jnp.zeros_like(acc_ref)

        e = y_ref[...] - t_ref[...]
        dy_ref[...] = e * (1.0 / D_MODEL)
        sq = jnp.sum((e * e).reshape(ts // 8, 8, D_MODEL), axis=0)
        part = sq[:, 0:LANES]
        for c in range(LANES, D_MODEL, LANES):
            part = part + sq[:, c:c + LANES]
        acc_ref[...] += part

        @pl.when(i == nt - 1)
        def _():
            tot = jnp.sum(_rsum(acc_ref[...]), axis=0, keepdims=True) * (0.5 / D_MODEL)
            l_ref[...] = jnp.broadcast_to(tot, (8, LANES))

    return pl.pallas_call(
        body, name=name, grid=(nt,),
        in_specs=[_row_spec(ts, D_MODEL), _row_spec(ts, D_MODEL)],
        out_specs=[_row_spec(ts, D_MODEL), _full_spec((8, LANES))],
        out_shape=[jax.ShapeDtypeStruct((s, D_MODEL), F32), jax.ShapeDtypeStruct((8, LANES), F32)],
        scratch_shapes=[pltpu.VMEM((8, LANES), F32)],
        compiler_params=_cp(("arbitrary",)),
    )(y, target)


MESH = pl.DeviceIdType.MESH
ANY = pl.BlockSpec(memory_space=pl.ANY)


def _all_gather(xs, name):
    n = len(xs)
    per = N_DEV - 1

    def body(*refs):
        x_refs, out_refs = refs[:n], refs[n:2 * n]
        send_sems, recv_sems, local_sems = refs[2 * n:]
        x, y, c = lax.axis_index("x"), lax.axis_index("y"), lax.axis_index("c")
        me, sibling = (x, y, c), (x, y, 1 - c)
        chips = [(1 - x, y), (x, 1 - y), (1 - x, 1 - y)]

        def slot(t, px, py, pc):
            return out_refs[t].at[4 * px + 2 * py + pc]

        def copy(t, k, block, to, src=None):
            return pltpu.make_async_remote_copy(
                src_ref=slot(t, *block) if src is None else src, dst_ref=slot(t, *block),
                send_sem=send_sems.at[t * per + k], recv_sem=recv_sems.at[t * per + k], device_id=to,
                device_id_type=MESH)

        mine = [pltpu.make_async_copy(x_refs[t], slot(t, *me), local_sems.at[t]) for t in range(n)]
        for cp in mine:
            cp.start()
        first = [copy(t, 0, me, sibling, src=x_refs[t]) for t in range(n)]
        first += [copy(t, 1 + j, me, (*chip, c), src=x_refs[t]) for j, chip in enumerate(chips) for t in range(n)]
        for cp in first:
            cp.start()
        passed = []
        for j, chip in enumerate(chips):
            for t in range(n):
                copy(t, 1 + j, (*chip, c), me).wait_recv()
                passed.append(copy(t, 4 + j, (*chip, c), sibling))
                passed[-1].start()
        for t in range(n):
            copy(t, 0, sibling, me).wait_recv()
        for j, chip in enumerate(chips):
            for t in range(n):
                copy(t, 4 + j, (*chip, 1 - c), me).wait_recv()
        for cp in first + passed:
            cp.wait_send()
        for cp in mine:
            cp.wait()

    return pl.pallas_call(
        body, name=name,
        out_shape=[jax.ShapeDtypeStruct((N_DEV,) + a.shape, a.dtype) for a in xs],
        in_specs=[ANY] * n, out_specs=[ANY] * n,
        scratch_shapes=[pltpu.SemaphoreType.DMA((per * n,)), pltpu.SemaphoreType.DMA((per * n,)),
                        pltpu.SemaphoreType.DMA((n,))],
    )(*xs)


def _grad_exchange(gss, gr, name):
    n = len(gss)
    per = N_DEV - 1

    def body(*refs):
        gs_refs, gr_ref = refs[:n], refs[n]
        os_refs, or_ref = refs[n + 1:2 * n + 1], refs[2 * n + 1]
        send_sems, recv_sems, local_sems = refs[2 * n + 2:]
        x, y, c = lax.axis_index("x"), lax.axis_index("y"), lax.axis_index("c")
        me = 4 * x + 2 * y + c
        local = [pltpu.make_async_copy(gs_refs[t].at[me], os_refs[t].at[me], local_sems.at[t]) for t in range(n)]
        local.append(pltpu.make_async_copy(gr_ref, or_ref.at[me], local_sems.at[n]))
        for cp in local:
            cp.start()
        sends, recvs = [], []
        for k in range(1, N_DEV):
            px = 1 - x if k & 4 else x
            py = 1 - y if k & 2 else y
            pc = 1 - c if k & 1 else c
            peer = 4 * px + 2 * py + pc
            to = (px, py, pc)
            for t in range(n + 1):
                sems = dict(send_sem=send_sems.at[t * per + k - 1], recv_sem=recv_sems.at[t * per + k - 1],
                            device_id=to, device_id_type=MESH)
                src = gs_refs[t].at[peer] if t < n else gr_ref
                out = os_refs[t] if t < n else or_ref
                sends.append(pltpu.make_async_remote_copy(src_ref=src, dst_ref=out.at[me], **sems))
                recvs.append(pltpu.make_async_remote_copy(src_ref=src, dst_ref=out.at[peer], **sems))
        for cp in sends:
            cp.start()
        for cp in recvs:
            cp.wait_recv()
        for cp in sends:
            cp.wait_send()
        for cp in local:
            cp.wait()

    nsem = per * (n + 1)
    return pl.pallas_call(
        body, name=name,
        out_shape=[jax.ShapeDtypeStruct(g.shape, g.dtype) for g in gss]
        + [jax.ShapeDtypeStruct((N_DEV,) + gr.shape, gr.dtype)],
        in_specs=[ANY] * (n + 1), out_specs=[ANY] * (n + 1),
        scratch_shapes=[pltpu.SemaphoreType.DMA((nsem,)), pltpu.SemaphoreType.DMA((nsem,)),
                        pltpu.SemaphoreType.DMA((n + 1,))],
    )(*gss, gr)


ADAM_ROWS = 128


def _adamw(parts, w, m, v, name):
    r, cols = w.shape
    tr = ADAM_ROWS if r % ADAM_ROWS == 0 else r

    def body(p_ref, w_ref, m_ref, v_ref, g_ref, d_ref, nm_ref, nv_ref):
        g = p_ref[0].astype(F32)
        for sidx in range(1, N_DEV):
            g = g + p_ref[sidx].astype(F32)
        mm = ADAM_B1 * m_ref[...] + (1.0 - ADAM_B1) * g
        vv = ADAM_B2 * v_ref[...] + (1.0 - ADAM_B2) * (g * g)
        m_hat = mm / (1.0 - ADAM_B1 ** ADAM_STEP)
        v_hat = vv / (1.0 - ADAM_B2 ** ADAM_STEP)
        g_ref[...] = g
        d_ref[...] = -ADAM_LR * (m_hat / (jnp.sqrt(v_hat) + ADAM_EPS) + ADAM_WD * w_ref[...])
        nm_ref[...] = mm
        nv_ref[...] = vv

    row = pl.BlockSpec((tr, cols), lambda i: (i, 0))
    return pl.pallas_call(
        body, name=name, grid=(r // tr,),
        in_specs=[pl.BlockSpec((N_DEV, tr, cols), lambda i: (0, i, 0)), row, row, row],
        out_specs=[row, row, row, row],
        out_shape=[jax.ShapeDtypeStruct((r, cols), F32)] * 4,
        compiler_params=_cp(("parallel",)),
    )(parts, w, m, v)


PACK_W = 8 * LANES
BF16_ROWS = 16


def _pack(flat_parts, rows):
    flat = jnp.concatenate([p.reshape(-1) for p in flat_parts])
    return jnp.pad(flat, (0, rows * PACK_W - flat.shape[0])).reshape(rows, PACK_W)


def _rows_for(n, mult):
    rows = -(-n // PACK_W)
    return -(-rows // mult) * mult


def _unshard(arr8, axis):
    full = jnp.moveaxis(arr8, 0, axis)
    shp = list(full.shape)
    shp[axis:axis + 2] = [shp[axis] * shp[axis + 1]]
    return full.reshape(shp)


def _split8(full, axis):
    shp = list(full.shape)
    shp[axis:axis + 1] = [N_DEV, shp[axis] // N_DEV]
    return jnp.moveaxis(full.reshape(shp), axis, 0)


def _unpack(flat2d, shapes, lead=()):
    flat = flat2d.reshape(lead + (-1,))
    out, off = [], 0
    for shp in shapes:
        n = math.prod(shp)
        out.append(flat[..., off:off + n].reshape(lead + tuple(shp)))
        off += n
    return out


def _to_layout(w):
    return jnp.concatenate([w[:, :1536], w[:, 1824:2336], w[:, 1536:1792], w[:, 2336:3104], w[:, 1792:1824],
                            jnp.zeros((w.shape[0], NP - IN_COLS), w.dtype)], axis=1)


def _from_layout(g):
    return jnp.concatenate([g[:, :1536], g[:, C_CKV:C_CKV + KV_LORA], g[:, C_KR:C_KR + ROPE],
                            g[:, C_ZM:C_ZM + MLA_W], g[:, C_SG:C_SG + 3 * SG_W]], axis=1)


def _pad_heads(w, real):
    lead = w.shape[:-1]
    w = w.reshape(lead + (HEADS, real))
    return jnp.pad(w, [(0, 0)] * len(lead) + [(0, 0), (0, HEAD_PAD - real)]).reshape(lead + (QW,))


def _rope_tables(s):
    half = ROPE // 2
    inv_freq = ROPE_THETA ** (-jnp.arange(half, dtype=F32) / half)
    ang = jnp.arange(s, dtype=F32)[:, None] * inv_freq[None, :]
    cos, sin = jnp.cos(ang), jnp.sin(ang)
    ones = jnp.ones((s, NOPE), F32)
    zeros = jnp.zeros((s, NOPE), F32)
    pad = jnp.zeros((s, HEAD_PAD - QK), F32)
    rc = jnp.concatenate([ones, cos, cos, pad + 1.0], axis=1)
    rs = jnp.concatenate([zeros, -sin, sin, pad], axis=1)
    return rc, rs


def kernel(x, norm_g, w_in, conv_w, conv_b, conv_ln_g, conv_ln_b, conv_pw_w, conv_pw_b, q_norm_g, w_uq, kv_norm_g, w_ukv, qk_q_g, qk_k_g, sg_ln_g, sg_ln_b, sg_w, sg_b, branch_norm_g, w_out, loss_target, m_norm_g, m_w_in, m_conv_w, m_conv_b, m_conv_ln_g, m_conv_ln_b, m_conv_pw_w, m_conv_pw_b, m_q_norm_g, m_w_uq, m_kv_norm_g, m_w_ukv, m_qk_q_g, m_qk_k_g, m_sg_ln_g, m_sg_ln_b, m_sg_w, m_sg_b, m_branch_norm_g, m_w_out, v_norm_g, v_w_in, v_conv_w, v_conv_b, v_conv_ln_g, v_conv_ln_b, v_conv_pw_w, v_conv_pw_b, v_q_norm_g, v_w_uq, v_kv_norm_g, v_w_ukv, v_qk_q_g, v_qk_k_g, v_sg_ln_g, v_sg_ln_b, v_sg_w, v_sg_b, v_branch_norm_g, v_w_out):
    given = dict(locals())
    wts = {n: given[n] for n in W_NAMES}
    mom_m = {n: given['m_' + n] for n in W_NAMES}
    mom_v = {n: given['v_' + n] for n in W_NAMES}
    s = x.shape[1]
    xs = x.reshape(s, D_MODEL)
    target = loss_target.reshape(s, D_MODEL)

    rp_shapes = [wts[n].shape for n in REPL]
    rows_rp = _rows_for(sum(math.prod(p) for p in rp_shapes), BF16_ROWS)
    sh_shape = {n: wts[n].shape for n in SHARDED}
    sh_2d = {n: (sh_shape[n][0] * sh_shape[n][1], sh_shape[n][2]) for n in SHARDED}

    gathered = _all_gather([wts[n].astype(F32 if n == 'conv_w' else BF16).reshape(sh_2d[n]) for n in SHARDED],
                           "weights_all_gather")
    full = {n: _unshard(g.reshape((N_DEV,) + sh_shape[n]), SHARD_AXIS[n]) for n, g in zip(SHARDED, gathered)}
    full.update({n: wts[n] for n in REPL})

    rc, rs = _rope_tables(s)
    tril = jnp.tril(jnp.ones((SG_CHUNK, SG_CHUNK), dtype=bool))

    def vec(a, width=None):
        a = a.reshape(1, -1)
        return a if width is None else jnp.pad(a, ((0, 0), (0, width - a.shape[1])))

    layers = []
    for l in range(DEPTH):
        p = {n: full[n][l] for n in W_NAMES}
        wukv = p['w_ukv'].reshape(KV_LORA, HEADS, NOPE + V_DIM)
        wm = jnp.where(tril[None], p['sg_w'], 0.0)
        layers.append(dict(
            ng=vec(p['norm_g']), win=_to_layout(p['w_in']).astype(BF16),
            cw=jnp.pad(p['conv_w'], ((0, HALO - CONV_K), (0, 0))), cb=vec(p['conv_b']), clg=vec(p['conv_ln_g']),
            clb=vec(p['conv_ln_b']), pww=p['conv_pw_w'].astype(BF16), pwb=vec(p['conv_pw_b']),
            qg=vec(p['q_norm_g']), wuq=_pad_heads(p['w_uq'], QK).astype(BF16), kvg=vec(p['kv_norm_g']),
            wukv=jnp.concatenate([_pad_heads(wukv[:, :, :NOPE].reshape(KV_LORA, HEADS * NOPE), NOPE),
                                  wukv[:, :, NOPE:].reshape(KV_LORA, MLA_W)], axis=1).astype(BF16),
            gq=vec(p['qk_q_g'], LANES), gk=vec(p['qk_k_g'], LANES),
            slg=vec(p['sg_ln_g']), slb=vec(p['sg_ln_b']), wm=wm.astype(BF16),
            wmt=jnp.swapaxes(wm, 1, 2).astype(BF16),
            sbx=jnp.repeat(p['sg_b'].T, SG_W // SG_HEADS, axis=1),
            bng=vec(p['branch_norm_g']), wout=p['w_out'].astype(BF16)))

    acts = []
    h_in = xs
    for l, p in enumerate(layers):
        proj = _proj_fwd(h_in, p['ng'], p['win'], f"proj_fwd_{l}")
        yc = _conv_fwd(proj, p['cw'], p['cb'], p['clg'], p['clb'], p['pww'], p['pwb'], f"conv_fwd_{l}")
        ys = _sgu_fwd(proj, p['slg'], p['slb'], p['wm'], p['sbx'], f"sgu_fwd_{l}")
        q, k, v = _mla_fwd(proj, rc, rs, p['qg'], p['wuq'], p['kvg'], p['wukv'], p['gq'], p['gk'], f"mla_fwd_{l}")
        o, lse = _attn_fwd(q, k, v, f"attn_fwd_{l}")
        h_out = _post_fwd(h_in, yc, o, proj, ys, p['bng'], p['wout'], f"post_fwd_{l}")
        acts.append(dict(x=h_in, proj=proj, yc=yc, ys=ys, q=q, k=k, v=v, o=o, lse=lse))
        h_in = h_out

    d_out, loss_blk = _loss_head(h_in, target, "loss_head")
    loss = lax.psum(loss_blk[0, 0], ("x", "y", "c"))

    grads = {n: [None] * DEPTH for n in W_NAMES}
    for l in reversed(range(DEPTH)):
        p, a = layers[l], acts[l]
        d_yc, d_ys, d_o, d_zm, stats, g_wout, g_bng = _post_bwd(
            d_out, a['yc'], a['o'], a['lse'], a['proj'], a['ys'], p['bng'], p['wout'], f"post_bwd_{l}")
        dq, dk, dv = _attn_bwd(a['q'], a['k'], a['v'], d_o, stats, f"attn_bwd_{l}")
        d_a, g_cw, g_pww, gv_c = _conv_bwd(a['proj'], d_yc, p['cw'], p['cb'], p['clg'], p['clb'], p['pww'], p['pwb'],
                                           f"conv_bwd_{l}")
        d_sg, g_wm, dms, gv_s = _sgu_bwd(a['proj'], d_ys, p['slg'], p['slb'], p['wm'], p['wmt'], p['sbx'],
                                         f"sgu_bwd_{l}")
        d_cq, d_ckv, d_kr, g_wuq, g_wukv, gv_m = _mla_bwd(
            a['proj'], rc, rs, p['qg'], p['wuq'], p['kvg'], p['wukv'], p['gq'], p['gk'], dq, dk, dv, f"mla_bwd_{l}")
        pieces = [(d_a, C_A), (d_cq, C_CQ), (d_zm, C_ZM), (d_ckv, C_CKV), (d_sg, C_SG), (d_kr, C_KR)]
        d_x, h_t, g_ng = _proj_bwd(a['x'], p['ng'], p['win'], d_out, pieces, f"proj_bwd_{l}")
        g_win = _win_grad(h_t, pieces, f"win_grad_{l}")
        d_out = d_x

        grads['norm_g'][l] = g_ng[0]
        grads['w_in'][l] = _from_layout(g_win)
        grads['conv_w'][l] = g_cw[:CONV_K]
        grads['conv_b'][l] = gv_c[0]
        grads['conv_ln_g'][l] = gv_c[1]
        grads['conv_ln_b'][l] = gv_c[2]
        grads['conv_pw_w'][l] = g_pww
        grads['conv_pw_b'][l] = gv_c[3]
        grads['q_norm_g'][l] = gv_m[0, :Q_LORA]
        grads['w_uq'][l] = g_wuq.reshape(Q_LORA, HEADS, HEAD_PAD)[:, :, :QK].reshape(Q_LORA, HEADS * QK)
        grads['kv_norm_g'][l] = gv_m[1, :KV_LORA]
        grads['w_ukv'][l] = jnp.concatenate(
            [g_wukv[:, :QW].reshape(KV_LORA, HEADS, HEAD_PAD)[:, :, :NOPE],
             g_wukv[:, QW:].reshape(KV_LORA, HEADS, V_DIM)], axis=2).reshape(KV_LORA, HEADS * (NOPE + V_DIM))
        grads['qk_q_g'][l] = gv_m[2, :QK]
        grads['qk_k_g'][l] = gv_m[3, :QK]
        grads['sg_ln_g'][l] = gv_s[0]
        grads['sg_ln_b'][l] = gv_s[1]
        grads['sg_w'][l] = jnp.where(tril[None], g_wm, 0.0)
        grads['sg_b'][l] = dms.reshape(SG_CHUNK, SG_HEADS, SG_W // SG_HEADS).sum(axis=2).T
        grads['branch_norm_g'][l] = g_bng[0]
        grads['w_out'][l] = g_wout
    grad_x = d_out.reshape(x.shape)
    g_full = {n: jnp.stack(grads[n]) for n in W_NAMES}

    gss = [_split8(g_full[n].astype(BF16), SHARD_AXIS[n]).reshape((N_DEV,) + sh_2d[n]) for n in SHARDED]
    gr = _pack([g_full[n].astype(BF16) for n in REPL], rows_rp)
    *parts_sh, parts_rp = _grad_exchange(gss, gr, "grad_exchange")
    res_sh = {n: _adamw(parts, wts[n].reshape(sh_2d[n]), mom_m[n].reshape(sh_2d[n]), mom_v[n].reshape(sh_2d[n]),
                        f"adamw_{n}") for n, parts in zip(SHARDED, parts_sh)}
    res_rp = _adamw(parts_rp, _pack([wts[n] for n in REPL], rows_rp), _pack([mom_m[n] for n in REPL], rows_rp),
                    _pack([mom_v[n] for n in REPL], rows_rp), "adamw_replicated")
    outs = []
    for kind in range(4):
        vals = {n: res_sh[n][kind].reshape(sh_shape[n]) for n in SHARDED}
        vals.update(zip(REPL, _unpack(res_rp[kind], rp_shapes)))
        outs.extend(vals[n] for n in W_NAMES)
    return (loss, grad_x, *outs)
```

```python
import functools
import math

import jax
import jax.numpy as jnp
from jax import lax
from jax.experimental import pallas as pl
from jax.experimental.pallas import tpu as pltpu

F32 = jnp.float32
BF16 = jnp.bfloat16

N_DEV = 8
DEPTH = 2
D_MODEL = 1024
CONV_W = 256
CONV_K = 31
HEADS = 8
NOPE = 64
ROPE = 32
QK = NOPE + ROPE
HEAD_PAD = 128
V_DIM = 64
MLA_W = HEADS * V_DIM
Q_LORA = 768
KV_LORA = 256
SG_W = 256
SG_HEADS = 4
SG_CHUNK = 128
ROPE_THETA = 10000.0
EPS = 1e-6
IN_COLS = 3104
NP = 3200
C_A, C_CQ, C_ZM, C_CKV, C_SG, C_KR = 0, 768, 1536, 2048, 2304, 3072
HALO = 32
SUB = 64
NEG = -1e30
LANES = 128
VMEM_LIMIT_V7X = 52 * 1024 * 1024

ADAM_LR = 0.001
ADAM_B1 = 0.9
ADAM_B2 = 0.999
ADAM_EPS = 1e-08
ADAM_WD = 0.01
ADAM_STEP = 10

W_NAMES = ['norm_g', 'w_in', 'conv_w', 'conv_b', 'conv_ln_g', 'conv_ln_b', 'conv_pw_w', 'conv_pw_b',
           'q_norm_g', 'w_uq', 'kv_norm_g', 'w_ukv', 'qk_q_g', 'qk_k_g', 'sg_ln_g', 'sg_ln_b', 'sg_w',
           'sg_b', 'branch_norm_g', 'w_out']
SHARD_AXIS = {'w_in': 2, 'conv_w': 2, 'conv_pw_w': 1, 'w_uq': 1, 'w_ukv': 2, 'w_out': 1}
SHARDED = [n for n in W_NAMES if n in SHARD_AXIS]
REPL = [n for n in W_NAMES if n not in SHARD_AXIS]


def _tile(s):
    for t in (512, 256, 128):
        if s % t == 0 and s // t >= 2:
            return t
    return s


def _cp(sem):
    return pltpu.CompilerParams(dimension_semantics=sem, vmem_limit_bytes=VMEM_LIMIT_V7X)


def _mm(a, b):
    return jnp.dot(a.astype(BF16), b.astype(BF16), preferred_element_type=F32)


def _mm_nt(a, b):
    return lax.dot_general(a.astype(BF16), b.astype(BF16), (((1,), (1,)), ((), ())),
                           preferred_element_type=F32)


def _mm_tn(a, b):
    return lax.dot_general(a.astype(BF16), b.astype(BF16), (((0,), (0,)), ((), ())),
                           preferred_element_type=F32)


_GC = math.sqrt(2.0 / math.pi)
_GA = 0.044715


def _sig(x):
    return 1.0 / (1.0 + jnp.exp(-x))


def _silu(x):
    return x * _sig(x)


def _dsilu(x):
    s = _sig(x)
    return s * (1.0 + x * (1.0 - s))


def _gelu(x):
    return 0.5 * x * (1.0 + jnp.tanh(_GC * (x + _GA * x * x * x)))


def _dgelu(x):
    t = jnp.tanh(_GC * (x + _GA * x * x * x))
    return 0.5 * (1.0 + t) + 0.5 * x * (1.0 - t * t) * _GC * (1.0 + 3.0 * _GA * x * x)


def _rsum(x):
    return jnp.sum(x, axis=-1, keepdims=True)


def _csum(x):
    return jnp.sum(x, axis=0, keepdims=True)


def _rms_fwd(x, n):
    r = lax.rsqrt(_rsum(x * x) * (1.0 / n) + EPS)
    return x * r, r


def _rms_bwd(dxh, xn, r, n):
    return r * (dxh - xn * (_rsum(dxh * xn) * (1.0 / n)))


def _ln_fwd(x, n):
    mu = _rsum(x) * (1.0 / n)
    xc = x - mu
    r = lax.rsqrt(_rsum(xc * xc) * (1.0 / n) + EPS)
    return xc * r, r


def _ln_bwd(dxh, xh, r, n):
    return r * (dxh - _rsum(dxh) * (1.0 / n) - xh * (_rsum(dxh * xh) * (1.0 / n)))


def _partner(x, lane):
    return jnp.where(lane < NOPE + ROPE // 2, pltpu.roll(x, LANES - ROPE // 2, 1), pltpu.roll(x, ROPE // 2, 1))


def _row_spec(ts, w, col=0):
    return pl.BlockSpec((ts, w), lambda i, col=col: (i, col))


def _full_spec(shape):
    nd = len(shape)
    return pl.BlockSpec(shape, lambda i, nd=nd: (0,) * nd)


PROJ_CHUNK = 640


def _proj_fwd(x, ng, win_p, name):
    s = x.shape[0]
    ts = _tile(s)

    def body(x_ref, g_ref, w_ref, o_ref):
        xv = x_ref[...]
        xn, _ = _rms_fwd(xv, D_MODEL)
        h = (xn * g_ref[...]).astype(BF16)
        for c in range(0, NP, PROJ_CHUNK):
            o_ref[:, c:c + PROJ_CHUNK] = jnp.dot(h, w_ref[:, c:c + PROJ_CHUNK], preferred_element_type=F32)

    return pl.pallas_call(
        body, name=name, grid=(s // ts,),
        in_specs=[_row_spec(ts, D_MODEL), _full_spec((1, D_MODEL)), _full_spec((D_MODEL, NP))],
        out_specs=_row_spec(ts, NP),
        out_shape=jax.ShapeDtypeStruct((s, NP), F32),
        compiler_params=_cp(("parallel",)),
    )(x, ng, win_p)


def _proj_bwd(x, ng, win_p, d_out, pieces, name):
    s = x.shape[0]
    ts = _tile(s)
    offs = [o for _, o in pieces]
    widths = [p.shape[1] for p, _ in pieces]

    def body(x_ref, g_ref, w_ref, do_ref, *rest):
        p_refs = rest[:len(pieces)]
        dx_ref, h_ref, gg_ref = rest[len(pieces):]
        i = pl.program_id(0)
        xv = x_ref[...]
        xn, r = _rms_fwd(xv, D_MODEL)
        g = g_ref[...]
        h_ref[...] = (xn * g).T.astype(BF16)
        dh = jnp.zeros((ts, D_MODEL), F32)
        for p_ref, off, w in zip(p_refs, offs, widths):
            dh = dh + _mm_nt(p_ref[...], w_ref[:, off:off + w])

        @pl.when(i == 0)
        def _():
            gg_ref[...] = jnp.zeros_like(gg_ref)

        gg_ref[...] += _csum(dh * xn)
        dx_ref[...] = _rms_bwd(dh * g, xn, r, D_MODEL) + do_ref[...]

    in_specs = [_row_spec(ts, D_MODEL), _full_spec((1, D_MODEL)), _full_spec((D_MODEL, NP)), _row_spec(ts, D_MODEL)]
    in_specs += [_row_spec(ts, w) for w in widths]
    return pl.pallas_call(
        body, name=name, grid=(s // ts,),
        in_specs=in_specs,
        out_specs=[_row_spec(ts, D_MODEL), pl.BlockSpec((D_MODEL, ts), lambda i: (0, i)), _full_spec((1, D_MODEL))],
        out_shape=[jax.ShapeDtypeStruct((s, D_MODEL), F32), jax.ShapeDtypeStruct((D_MODEL, s), BF16),
                   jax.ShapeDtypeStruct((1, D_MODEL), F32)],
        compiler_params=_cp(("arbitrary",)),
    )(x, ng, win_p, d_out, *[p for p, _ in pieces])


WG_TILE = 256


def _win_grad(ht, pieces, name):
    s = ht.shape[1]
    ts = min(WG_TILE, s)
    offs = [o for _, o in pieces]
    widths = [p.shape[1] for p, _ in pieces]

    def body(ht_ref, *rest):
        p_refs, o_ref = rest[:-1], rest[-1]

        @pl.when(pl.program_id(0) == 0)
        def _():
            o_ref[...] = jnp.zeros_like(o_ref)

        hb = ht_ref[...]
        for p_ref, off, w in zip(p_refs, offs, widths):
            o_ref[:, off:off + w] += jnp.dot(hb, p_ref[...].astype(BF16), preferred_element_type=F32)

    return pl.pallas_call(
        body, name=name, grid=(s // ts,),
        in_specs=[pl.BlockSpec((D_MODEL, ts), lambda i: (0, i))] + [_row_spec(ts, w) for w in widths],
        out_specs=_full_spec((D_MODEL, NP)),
        out_shape=jax.ShapeDtypeStruct((D_MODEL, NP), F32),
        compiler_params=_cp(("arbitrary",)),
    )(ht, *[p for p, _ in pieces])


def _halo_spec(ts):
    per = ts // HALO
    return pl.BlockSpec((HALO, 2 * CONV_W), lambda i: (jnp.maximum(i * per - 1, 0), 0))


def _conv_taps(ext_ref, cw_ref, cv_ref, cb, ts):
    base = HALO - (CONV_K - 1)
    for r0 in range(0, ts, SUB):
        acc = jnp.zeros((SUB, CONV_W), F32)
        for k in range(CONV_K):
            acc = acc + cw_ref[k:k + 1, :] * ext_ref[r0 + base + k:r0 + base + k + SUB, :]
        cv_ref[r0:r0 + SUB, :] = acc + cb


def _conv_fwd(proj, cw, cb, lg, lb, pww, pwb, name):
    s = proj.shape[0]
    ts = _tile(s)

    def body(pa_ref, ph_ref, cw_ref, cb_ref, lg_ref, lb_ref, pww_ref, pwb_ref, y_ref, cv_ref, ext_ref):
        i = pl.program_id(0)
        pa = pa_ref[...]
        a, ag, zc = pa[:, :CONV_W], pa[:, CONV_W:2 * CONV_W], pa[:, 2 * CONV_W:]
        ph = ph_ref[...]
        hglu = ph[:, :CONV_W] * _sig(ph[:, CONV_W:])
        ext_ref[0:HALO, :] = jnp.where(i > 0, hglu, 0.0)
        ext_ref[HALO:HALO + ts, :] = a * _sig(ag)
        _conv_taps(ext_ref, cw_ref, cv_ref, cb_ref[...], ts)
        xh, _ = _ln_fwd(cv_ref[...], CONV_W)
        ln = xh * lg_ref[...] + lb_ref[...]
        pw = _mm(_silu(ln), pww_ref[...]) + pwb_ref[...]
        y_ref[...] = pw * _silu(zc)

    vec = _full_spec((1, CONV_W))
    return pl.pallas_call(
        body, name=name, grid=(s // ts,),
        in_specs=[_row_spec(ts, 3 * CONV_W, 0), _halo_spec(ts), _full_spec((HALO, CONV_W)), vec, vec, vec,
                  _full_spec((CONV_W, CONV_W)), vec],
        out_specs=[_row_spec(ts, CONV_W), _row_spec(ts, CONV_W)],
        out_shape=[jax.ShapeDtypeStruct((s, CONV_W), F32), jax.ShapeDtypeStruct((s, CONV_W), F32)],
        scratch_shapes=[pltpu.VMEM((HALO + ts, CONV_W), F32)],
        compiler_params=_cp(("parallel",)),
    )(proj, proj, cw, cb, lg, lb, pww, pwb)


def _conv_bwd(proj, cv, dy, cw, lg, lb, pww, pwb, name):
    s = proj.shape[0]
    ts = _tile(s)
    nt = s // ts
    per = ts // HALO

    def body(pa_ref, ph_ref, cv_ref, dy_ref, cw_ref, lg_ref, lb_ref, pww_ref, pwb_ref,
             dp_ref, gcw_ref, gpw_ref, gv_ref, ext_ref, dext_ref, carry_ref, gacc_ref):
        i = pl.program_id(0)
        ti = nt - 1 - i

        @pl.when(i == 0)
        def _():
            carry_ref[...] = jnp.zeros_like(carry_ref)
            gacc_ref[...] = jnp.zeros_like(gacc_ref)
            gpw_ref[...] = jnp.zeros_like(gpw_ref)
            gv_ref[...] = jnp.zeros_like(gv_ref)

        pa = pa_ref[...]
        a, ag, zc = pa[:, :CONV_W], pa[:, CONV_W:2 * CONV_W], pa[:, 2 * CONV_W:]
        sag = _sig(ag)
        ph = ph_ref[...]
        hglu = ph[:, :CONV_W] * _sig(ph[:, CONV_W:])
        ext_ref[0:HALO, :] = jnp.where(ti > 0, hglu, 0.0)
        ext_ref[HALO:HALO + ts, :] = a * sag
        xh, rl = _ln_fwd(cv_ref[...], CONV_W)
        lg = lg_ref[...]
        ln = xh * lg + lb_ref[...]
        sw = _silu(ln)
        pww = pww_ref[...]
        pw = _mm(sw, pww) + pwb_ref[...]
        d_y = dy_ref[...]
        d_pw = d_y * _silu(zc)
        d_zc = d_y * pw * _dsilu(zc)
        gpw_ref[...] += _mm_tn(sw, d_pw)
        d_ln = _mm_nt(d_pw, pww) * _dsilu(ln)
        d_cv = _ln_bwd(d_ln * lg, xh, rl, CONV_W)
        gv_ref[0:1, :] += _csum(d_cv)
        gv_ref[1:2, :] += _csum(d_ln * xh)
        gv_ref[2:3, :] += _csum(d_ln)
        gv_ref[3:4, :] += _csum(d_pw)
        dext_ref[0:ts, :] = d_cv
        dext_ref[ts:ts + HALO, :] = carry_ref[...]
        carry_ref[...] = d_cv[0:HALO, :]
        base = HALO - (CONV_K - 1)
        for r0 in range(0, ts, SUB):
            dcv_r = dext_ref[r0:r0 + SUB, :]
            dg = jnp.zeros((SUB, CONV_W), F32)
            for k in range(CONV_K):
                prod = dcv_r * ext_ref[r0 + base + k:r0 + base + k + SUB, :]
                gacc_ref[8 * k:8 * k + 8, :] += jnp.sum(prod.reshape(SUB // 8, 8, CONV_W), axis=0)
                dg = dg + cw_ref[k:k + 1, :] * dext_ref[r0 + CONV_K - 1 - k:r0 + CONV_K - 1 - k + SUB, :]
            dp_ref[r0:r0 + SUB, 0:CONV_W] = dg * sag[r0:r0 + SUB, :]
            dp_ref[r0:r0 + SUB, CONV_W:2 * CONV_W] = dg * a[r0:r0 + SUB, :] * sag[r0:r0 + SUB, :] * (1.0 - sag[r0:r0 + SUB, :])
        dp_ref[:, 2 * CONV_W:] = d_zc

        @pl.when(i == nt - 1)
        def _():
            gcw_ref[...] = jnp.zeros_like(gcw_ref)
            for k in range(CONV_K):
                gcw_ref[k:k + 1, :] = _csum(gacc_ref[8 * k:8 * k + 8, :])

    vec = _full_spec((1, CONV_W))
    rev = lambda w, col=0: pl.BlockSpec((ts, w), lambda i, col=col: (nt - 1 - i, col))
    halo = pl.BlockSpec((HALO, 2 * CONV_W), lambda i: (jnp.maximum((nt - 1 - i) * per - 1, 0), 0))
    return pl.pallas_call(
        body, name=name, grid=(nt,),
        in_specs=[rev(3 * CONV_W), halo, rev(CONV_W), rev(CONV_W), _full_spec((HALO, CONV_W)), vec, vec,
                  _full_spec((CONV_W, CONV_W)), vec],
        out_specs=[rev(3 * CONV_W), _full_spec((HALO, CONV_W)), _full_spec((CONV_W, CONV_W)), _full_spec((8, CONV_W))],
        out_shape=[jax.ShapeDtypeStruct((s, 3 * CONV_W), F32), jax.ShapeDtypeStruct((HALO, CONV_W), F32),
                   jax.ShapeDtypeStruct((CONV_W, CONV_W), F32), jax.ShapeDtypeStruct((8, CONV_W), F32)],
        scratch_shapes=[pltpu.VMEM((HALO + ts, CONV_W), F32), pltpu.VMEM((ts + HALO, CONV_W), F32),
                        pltpu.VMEM((HALO, CONV_W), F32), pltpu.VMEM((8 * HALO, CONV_W), F32)],
        compiler_params=_cp(("arbitrary",)),
    )(proj, proj, cv, dy, cw, lg, lb, pww, pwb)


def _sg_mix(wm_ref, vc, head):
    out = jnp.zeros((SG_CHUNK, SG_W), F32)
    vb = vc.astype(BF16)
    for g in range(SG_HEADS):
        out = jnp.where(head == g, jnp.dot(wm_ref[g], vb, preferred_element_type=F32), out)
    return out


def _sgu_fwd(proj, lg, lb, wm, sbx, name):
    s = proj.shape[0]
    ts = _tile(s)

    def body(ps_ref, lg_ref, lb_ref, wm_ref, sbx_ref, y_ref, mix_ref):
        ps = ps_ref[...]
        us, vs, zs = ps[:, :SG_W], ps[:, SG_W:2 * SG_W], ps[:, 2 * SG_W:]
        xh, _ = _ln_fwd(_gelu(vs), SG_W)
        vn = xh * lg_ref[...] + lb_ref[...]
        head = lax.broadcasted_iota(jnp.int32, (SG_CHUNK, SG_W), 1) // (SG_W // SG_HEADS)
        for c0 in range(0, ts, SG_CHUNK):
            mix_ref[c0:c0 + SG_CHUNK, :] = _sg_mix(wm_ref, vn[c0:c0 + SG_CHUNK, :], head) + sbx_ref[...]
        y_ref[...] = _gelu(us) * mix_ref[...] * _silu(zs)

    vec = _full_spec((1, SG_W))
    return pl.pallas_call(
        body, name=name, grid=(s // ts,),
        in_specs=[_row_spec(ts, 3 * SG_W, C_SG // (3 * SG_W)), vec, vec,
                  _full_spec((SG_HEADS, SG_CHUNK, SG_CHUNK)), _full_spec((SG_CHUNK, SG_W))],
        out_specs=_row_spec(ts, SG_W),
        out_shape=jax.ShapeDtypeStruct((s, SG_W), F32),
        scratch_shapes=[pltpu.VMEM((ts, SG_W), F32)],
        compiler_params=_cp(("parallel",)),
    )(proj, lg, lb, wm, sbx)


def _sgu_bwd(proj, dy, lg, lb, wm, wmt, sbx, name):
    s = proj.shape[0]
    ts = _tile(s)

    def body(ps_ref, dy_ref, lg_ref, lb_ref, wm_ref, wmt_ref, sbx_ref,
             dp_ref, gwm_ref, dms_ref, gv_ref, mix_ref, dvn_ref):
        i = pl.program_id(0)

        @pl.when(i == 0)
        def _():
            gwm_ref[...] = jnp.zeros_like(gwm_ref)
            dms_ref[...] = jnp.zeros_like(dms_ref)
            gv_ref[...] = jnp.zeros_like(gv_ref)

        ps = ps_ref[...]
        us, vs, zs = ps[:, :SG_W], ps[:, SG_W:2 * SG_W], ps[:, 2 * SG_W:]
        xh, rl = _ln_fwd(_gelu(vs), SG_W)
        lg = lg_ref[...]
        vn = xh * lg + lb_ref[...]
        head = lax.broadcasted_iota(jnp.int32, (SG_CHUNK, SG_W), 1) // (SG_W // SG_HEADS)
        for c0 in range(0, ts, SG_CHUNK):
            mix_ref[c0:c0 + SG_CHUNK, :] = _sg_mix(wm_ref, vn[c0:c0 + SG_CHUNK, :], head) + sbx_ref[...]
        mixed = mix_ref[...]
        u = _gelu(us)
        sz = _silu(zs)
        d_y = dy_ref[...]
        d_mixed = d_y * u * sz
        dp_ref[:, 0:SG_W] = d_y * mixed * sz * _dgelu(us)
        dp_ref[:, 2 * SG_W:] = d_y * u * mixed * _dsilu(zs)
        dms = jnp.zeros((SG_CHUNK, SG_W), F32)
        for c0 in range(0, ts, SG_CHUNK):
            dm = d_mixed[c0:c0 + SG_CHUNK, :]
            vc = vn[c0:c0 + SG_CHUNK, :]
            dms = dms + dm
            for g in range(SG_HEADS):
                gwm_ref[g] += _mm_nt(jnp.where(head == g, dm, 0.0), vc)
            dvn_ref[c0:c0 + SG_CHUNK, :] = _sg_mix(wmt_ref, dm, head)
        dms_ref[...] += dms
        d_vn = dvn_ref[...]
        gv_ref[0:1, :] += _csum(d_vn * xh)
        gv_ref[1:2, :] += _csum(d_vn)
        dp_ref[:, SG_W:2 * SG_W] = _ln_bwd(d_vn * lg, xh, rl, SG_W) * _dgelu(vs)

    vec = _full_spec((1, SG_W))
    wspec = _full_spec((SG_HEADS, SG_CHUNK, SG_CHUNK))
    return pl.pallas_call(
        body, name=name, grid=(s // ts,),
        in_specs=[_row_spec(ts, 3 * SG_W, C_SG // (3 * SG_W)), _row_spec(ts, SG_W), vec, vec, wspec, wspec,
                  _full_spec((SG_CHUNK, SG_W))],
        out_specs=[_row_spec(ts, 3 * SG_W), wspec, _full_spec((SG_CHUNK, SG_W)), _full_spec((8, SG_W))],
        out_shape=[jax.ShapeDtypeStruct((s, 3 * SG_W), F32), jax.ShapeDtypeStruct((SG_HEADS, SG_CHUNK, SG_CHUNK), F32),
                   jax.ShapeDtypeStruct((SG_CHUNK, SG_W), F32), jax.ShapeDtypeStruct((8, SG_W), F32)],
        scratch_shapes=[pltpu.VMEM((ts, SG_W), F32), pltpu.VMEM((ts, SG_W), F32)],
        compiler_params=_cp(("arbitrary",)),
    )(proj, dy, lg, lb, wm, wmt, sbx)


QW = HEADS * HEAD_PAD
KVW = QW + MLA_W
ATT_SCALE = QK ** -0.5


def _mla_specs(ts):
    return [_row_spec(ts, Q_LORA, C_CQ // Q_LORA), _row_spec(ts, KV_LORA, C_CKV // KV_LORA),
            _row_spec(ts, LANES, C_KR // LANES), _row_spec(ts, LANES), _row_spec(ts, LANES),
            _full_spec((1, Q_LORA)), _full_spec((Q_LORA, QW)), _full_spec((1, KV_LORA)), _full_spec((KV_LORA, KVW)),
            _full_spec((1, LANES)), _full_spec((1, LANES))]


def _mla_fwd(proj, rc, rs, qg, wuq, kvg, wukv, gq, gk, name):
    s = proj.shape[0]
    ts = _tile(s)

    def body(cq_ref, ckv_ref, kr_ref, rc_ref, rs_ref, qg_ref, wuq_ref, kvg_ref, wukv_ref, gq_ref, gk_ref,
             q_ref, k_ref, v_ref):
        lane = lax.broadcasted_iota(jnp.int32, (ts, LANES), 1)
        c, sn = rc_ref[...], rs_ref[...]
        cqn, _ = _rms_fwd(cq_ref[...], Q_LORA)
        q0 = _mm(cqn * qg_ref[...], wuq_ref[...])
        gq = gq_ref[...]
        for h in range(HEADS):
            xn, _ = _rms_fwd(q0[:, h * LANES:(h + 1) * LANES], QK)
            qn = xn * gq
            q_ref[:, h * LANES:(h + 1) * LANES] = ((qn * c + _partner(qn, lane) * sn) * ATT_SCALE).astype(BF16)
        ckvn, _ = _rms_fwd(ckv_ref[...], KV_LORA)
        kv = _mm(ckvn * kvg_ref[...], wukv_ref[...])
        kr = pltpu.roll(kr_ref[...], NOPE, 1)
        gk = gk_ref[...]
        for h in range(HEADS):
            xn, _ = _rms_fwd(kv[:, h * LANES:(h + 1) * LANES] + kr, QK)
            kn = xn * gk
            k_ref[:, h * LANES:(h + 1) * LANES] = (kn * c + _partner(kn, lane) * sn).astype(BF16)
        v_ref[...] = kv[:, QW:].astype(BF16)

    return pl.pallas_call(
        body, name=name, grid=(s // ts,),
        in_specs=_mla_specs(ts),
        out_specs=[_row_spec(ts, QW), _row_spec(ts, QW), _row_spec(ts, MLA_W)],
        out_shape=[jax.ShapeDtypeStruct((s, QW), BF16), jax.ShapeDtypeStruct((s, QW), BF16),
                   jax.ShapeDtypeStruct((s, MLA_W), BF16)],
        compiler_params=_cp(("parallel",)),
    )(proj, proj, proj, rc, rs, qg, wuq, kvg, wukv, gq, gk)


def _mla_bwd(proj, rc, rs, qg, wuq, kvg, wukv, gq, gk, dq, dk, dv, name):
    s = proj.shape[0]
    ts = _tile(s)

    def body(cq_ref, ckv_ref, kr_ref, rc_ref, rs_ref, qg_ref, wuq_ref, kvg_ref, wukv_ref, gq_ref, gk_ref,
             dq_ref, dk_ref, dv_ref, dcq_ref, dckv_ref, dkr_ref, gwuq_ref, gwukv_ref, gv_ref, d0_ref):
        i = pl.program_id(0)

        @pl.when(i == 0)
        def _():
            gwuq_ref[...] = jnp.zeros_like(gwuq_ref)
            gwukv_ref[...] = jnp.zeros_like(gwukv_ref)
            gv_ref[...] = jnp.zeros_like(gv_ref)

        lane = lax.broadcasted_iota(jnp.int32, (ts, LANES), 1)
        c, sn = rc_ref[...], rs_ref[...]
        cq = cq_ref[...]
        cqx, rq0 = _rms_fwd(cq, Q_LORA)
        qg = qg_ref[...]
        cqn = cqx * qg
        wuq = wuq_ref[...]
        q0 = _mm(cqn, wuq)
        gq = gq_ref[...]
        ggq = jnp.zeros((1, LANES), F32)
        for h in range(HEADS):
            xn, r = _rms_fwd(q0[:, h * LANES:(h + 1) * LANES], QK)
            d = dq_ref[:, h * LANES:(h + 1) * LANES] * ATT_SCALE
            d_qn = d * c - _partner(d, lane) * sn
            ggq = ggq + _csum(d_qn * xn)
            d0_ref[:, h * LANES:(h + 1) * LANES] = _rms_bwd(d_qn * gq, xn, r, QK)
        dq0 = d0_ref[:, 0:QW]
        gwuq_ref[...] += _mm_tn(cqn, dq0)
        d_cqn = _mm_nt(dq0, wuq)
        gv_ref[0:1, 0:Q_LORA] += _csum(d_cqn * cqx)
        gv_ref[2:3, 0:LANES] += ggq
        dcq_ref[...] = _rms_bwd(d_cqn * qg, cqx, rq0, Q_LORA)
        ckv = ckv_ref[...]
        ckx, rk0 = _rms_fwd(ckv, KV_LORA)
        kvg = kvg_ref[...]
        ckvn = ckx * kvg
        wukv = wukv_ref[...]
        kv = _mm(ckvn, wukv)
        kr = pltpu.roll(kr_ref[...], NOPE, 1)
        gk = gk_ref[...]
        ggk = jnp.zeros((1, LANES), F32)
        dkr = jnp.zeros((ts, LANES), F32)
        for h in range(HEADS):
            xn, r = _rms_fwd(kv[:, h * LANES:(h + 1) * LANES] + kr, QK)
            d = dk_ref[:, h * LANES:(h + 1) * LANES]
            d_kn = d * c - _partner(d, lane) * sn
            ggk = ggk + _csum(d_kn * xn)
            d_k0 = _rms_bwd(d_kn * gk, xn, r, QK)
            dkr = dkr + d_k0
            d0_ref[:, h * LANES:(h + 1) * LANES] = d_k0
        d0_ref[:, QW:KVW] = dv_ref[...]
        dkv = d0_ref[...]
        dkr_ref[...] = jnp.where(lane < ROPE, pltpu.roll(dkr, NOPE, 1), 0.0)
        gwukv_ref[...] += _mm_tn(ckvn, dkv)
        d_ckvn = _mm_nt(dkv, wukv)
        gv_ref[1:2, 0:KV_LORA] += _csum(d_ckvn * ckx)
        gv_ref[3:4, 0:LANES] += ggk
        dckv_ref[...] = _rms_bwd(d_ckvn * kvg, ckx, rk0, KV_LORA)

    return pl.pallas_call(
        body, name=name, grid=(s // ts,),
        in_specs=_mla_specs(ts) + [_row_spec(ts, QW), _row_spec(ts, QW), _row_spec(ts, MLA_W)],
        out_specs=[_row_spec(ts, Q_LORA), _row_spec(ts, KV_LORA), _row_spec(ts, LANES),
                   _full_spec((Q_LORA, QW)), _full_spec((KV_LORA, KVW)), _full_spec((8, QW))],
        out_shape=[jax.ShapeDtypeStruct((s, Q_LORA), F32), jax.ShapeDtypeStruct((s, KV_LORA), F32),
                   jax.ShapeDtypeStruct((s, LANES), F32), jax.ShapeDtypeStruct((Q_LORA, QW), F32),
                   jax.ShapeDtypeStruct((KV_LORA, KVW), F32), jax.ShapeDtypeStruct((8, QW), F32)],
        scratch_shapes=[pltpu.VMEM((ts, KVW), F32)],
        compiler_params=_cp(("arbitrary",)),
    )(proj, proj, proj, rc, rs, qg, wuq, kvg, wukv, gq, gk, dq, dk, dv)


PAIRS = HEADS // 2
ATT_STRIP = 32


def _attn_fwd(q, k, v, name):
    s = q.shape[0]
    tq = _tile(s)
    tk = tq

    def body(q_ref, k_ref, v_ref, o_ref, lse_ref, s0_ref, s1_ref, p0_ref, p1_ref, m_ref, l_ref, acc_ref):
        i = pl.program_id(1)
        s_refs, p_refs = (s0_ref, s1_ref), (p0_ref, p1_ref)
        row = lax.broadcasted_iota(jnp.int32, (ATT_STRIP, tk), 0)
        col = lax.broadcasted_iota(jnp.int32, (ATT_STRIP, tk), 1)
        first = lax.broadcasted_iota(jnp.int32, (tq, LANES), 1) < V_DIM
        m_ref[...] = jnp.full(m_ref.shape, NEG, F32)
        l_ref[...] = jnp.zeros(l_ref.shape, F32)
        acc_ref[...] = jnp.zeros(acc_ref.shape, F32)

        def blk(j, masked):
            st = pl.multiple_of(j * tk, tk)
            for a in range(2):
                s_refs[a][...] = _mm_nt(q_ref[:, a * LANES:(a + 1) * LANES],
                                        k_ref[pl.ds(st, tk), a * LANES:(a + 1) * LANES])
            for a in range(2):
                for r in range(0, tq, ATT_STRIP):
                    sc = s_refs[a][r:r + ATT_STRIP, :]
                    if masked:
                        sc = jnp.where(col <= row + r, sc, NEG)
                    m_old = m_ref[a, r:r + ATT_STRIP, :]
                    m_new = jnp.maximum(m_old, jnp.max(sc, axis=-1, keepdims=True))
                    alpha = jnp.exp(m_old - m_new)
                    p = jnp.exp(sc - jnp.tile(m_new, (1, tk // LANES)))
                    l_ref[a, r:r + ATT_STRIP, :] = alpha * l_ref[a, r:r + ATT_STRIP, :] + _rsum(p)
                    acc_ref[a, r:r + ATT_STRIP, :] = alpha * acc_ref[a, r:r + ATT_STRIP, :]
                    m_ref[a, r:r + ATT_STRIP, :] = m_new
                    p_refs[a][r:r + ATT_STRIP, :] = p.astype(BF16)
                acc_ref[a] += jnp.dot(p_refs[a][...], v_ref[pl.ds(st, tk), :], preferred_element_type=F32)

        def two_blocks(t, carry):
            blk(2 * t, False)
            blk(2 * t + 1, False)
            return carry

        lax.fori_loop(0, i // 2, two_blocks, 0)

        @pl.when(i % 2 == 1)
        def _():
            blk(i - 1, False)

        blk(i, True)
        o_ref[...] = jnp.where(first, acc_ref[0] / l_ref[0], acc_ref[1] / l_ref[1])
        lse_ref[...] = jnp.where(first, m_ref[0] + jnp.log(l_ref[0]), m_ref[1] + jnp.log(l_ref[1]))

    stat = pltpu.VMEM((2, tq, LANES), F32)
    return pl.pallas_call(
        body, name=name, grid=(PAIRS, s // tq),
        in_specs=[pl.BlockSpec((tq, 2 * LANES), lambda p, i: (i, p)),
                  pl.BlockSpec((s, 2 * LANES), lambda p, i: (0, p)),
                  pl.BlockSpec((s, LANES), lambda p, i: (0, p))],
        out_specs=[pl.BlockSpec((tq, LANES), lambda p, i: (i, p)), pl.BlockSpec((tq, LANES), lambda p, i: (i, p))],
        out_shape=[jax.ShapeDtypeStruct((s, MLA_W), F32), jax.ShapeDtypeStruct((s, MLA_W), F32)],
        scratch_shapes=[pltpu.VMEM((tq, tk), F32), pltpu.VMEM((tq, tk), F32), pltpu.VMEM((tq, tk), BF16),
                        pltpu.VMEM((tq, tk), BF16), stat, stat, stat],
        compiler_params=_cp(("parallel", "parallel")),
    )(q, k, v)


def _attn_bwd(q, k, v, do, stats, name):
    s = q.shape[0]
    tq = _tile(s)
    tk = tq
    nq = s // tq

    def body(q_ref, k_ref, v_ref, do_ref, st_ref, dq_ref, dk_ref, dv_ref):
        j = pl.program_id(1)

        @pl.when(j == 0)
        def _():
            dq_ref[...] = jnp.zeros_like(dq_ref)

        dk_ref[...] = jnp.zeros_like(dk_ref)
        dv_ref[...] = jnp.zeros_like(dv_ref)
        row = lax.broadcasted_iota(jnp.int32, (tq, tk), 0)
        col = lax.broadcasted_iota(jnp.int32, (tq, tk), 1)
        lane = lax.broadcasted_iota(jnp.int32, (tq, LANES), 1)

        def blk(i, masked):
            st = i * tq if isinstance(i, int) else pl.multiple_of(i * tq, tq)
            do2 = do_ref[pl.ds(st, tq), :]
            stt = st_ref[pl.ds(st, tq), :]
            dv = None
            for a in range(2):
                mine = (lane < V_DIM) if a == 0 else (lane >= V_DIM)
                qa = q_ref[pl.ds(st, tq), a * LANES:(a + 1) * LANES]
                doa = jnp.where(mine, do2, jnp.zeros((), BF16))
                lse = stt[:, a * V_DIM:a * V_DIM + 1]
                dl = stt[:, a * V_DIM + V_DIM // 2:a * V_DIM + V_DIM // 2 + 1]
                p = jnp.exp(_mm_nt(qa, k_ref[:, a * LANES:(a + 1) * LANES]) - lse)
                if masked:
                    p = jnp.where(col <= row, p, 0.0)
                ds = (p * (_mm_nt(doa, v_ref[...]) - dl)).astype(BF16)
                dva = _mm_tn(p, doa)
                dv = dva if dv is None else dv + dva
                dk_ref[:, a * LANES:(a + 1) * LANES] += _mm_tn(ds, qa)
                dq_ref[pl.ds(st, tq), a * LANES:(a + 1) * LANES] += jnp.dot(
                    ds, k_ref[:, a * LANES:(a + 1) * LANES], preferred_element_type=F32)
            dv_ref[...] += dv

        blk(j, True)
        rest = nq - 1 - j

        def two_blocks(t, carry):
            blk(j + 1 + 2 * t, False)
            blk(j + 2 + 2 * t, False)
            return carry

        lax.fori_loop(0, rest // 2, two_blocks, 0)

        @pl.when(rest % 2 == 1)
        def _():
            blk(nq - 1, False)

    return pl.pallas_call(
        body, name=name, grid=(PAIRS, s // tk),
        in_specs=[pl.BlockSpec((s, 2 * LANES), lambda p, j: (0, p)),
                  pl.BlockSpec((tk, 2 * LANES), lambda p, j: (j, p)),
                  pl.BlockSpec((tk, LANES), lambda p, j: (j, p)),
                  pl.BlockSpec((s, LANES), lambda p, j: (0, p)),
                  pl.BlockSpec((s, LANES), lambda p, j: (0, p))],
        out_specs=[pl.BlockSpec((s, 2 * LANES), lambda p, j: (0, p)),
                   pl.BlockSpec((tk, 2 * LANES), lambda p, j: (j, p)),
                   pl.BlockSpec((tk, LANES), lambda p, j: (j, p))],
        out_shape=[jax.ShapeDtypeStruct((s, QW), F32), jax.ShapeDtypeStruct((s, QW), F32),
                   jax.ShapeDtypeStruct((s, MLA_W), F32)],
        compiler_params=_cp(("parallel", "arbitrary")),
    )(q, k, v, do, stats)


BR = ((0, CONV_W), (CONV_W, CONV_W + MLA_W), (CONV_W + MLA_W, D_MODEL))


def _post_fwd(x, yc, o, proj, ys, bng, wout, name):
    s = x.shape[0]
    ts = _tile(s)

    def body(x_ref, yc_ref, o_ref, zm_ref, ys_ref, g_ref, w_ref, out_ref):
        ys3 = (yc_ref[...], o_ref[...] * _silu(zm_ref[...]), ys_ref[...])
        acc = x_ref[...]
        for (lo, hi), yb in zip(BR, ys3):
            yn, _ = _rms_fwd(yb, hi - lo)
            acc = acc + _mm(yn * g_ref[:, lo:hi], w_ref[lo:hi, :])
        out_ref[...] = acc

    return pl.pallas_call(
        body, name=name, grid=(s // ts,),
        in_specs=[_row_spec(ts, D_MODEL), _row_spec(ts, CONV_W), _row_spec(ts, MLA_W),
                  _row_spec(ts, MLA_W, C_ZM // MLA_W), _row_spec(ts, SG_W), _full_spec((1, D_MODEL)),
                  _full_spec((D_MODEL, D_MODEL))],
        out_specs=_row_spec(ts, D_MODEL),
        out_shape=jax.ShapeDtypeStruct((s, D_MODEL), F32),
        compiler_params=_cp(("parallel",)),
    )(x, yc, o, proj, ys, bng, wout)


def _post_bwd(d_out, yc, o, lse, proj, ys, bng, wout, name):
    s = d_out.shape[0]
    ts = _tile(s)

    def body(do_ref, yc_ref, o_ref, lse_ref, zm_ref, ys_ref, g_ref, w_ref,
             dyc_ref, dys_ref, dob_ref, dzm_ref, st_ref, gw_ref, gg_ref, yn_ref):
        i = pl.program_id(0)

        @pl.when(i == 0)
        def _():
            gw_ref[...] = jnp.zeros_like(gw_ref)
            gg_ref[...] = jnp.zeros_like(gg_ref)

        d_out_b = do_ref[...].astype(BF16)
        o = o_ref[...]
        zm = zm_ref[...]
        szm = _silu(zm)
        ys3 = (yc_ref[...], o * szm, ys_ref[...])
        d_ys = []
        for (lo, hi), yb in zip(BR, ys3):
            n = hi - lo
            yn, r = _rms_fwd(yb, n)
            g = g_ref[:, lo:hi]
            yn_ref[:, lo:hi] = (yn * g).astype(BF16)
            d_yn = _mm_nt(d_out_b, w_ref[lo:hi, :])
            gg_ref[:, lo:hi] += _csum(d_yn * yn)
            d_ys.append(_rms_bwd(d_yn * g, yn, r, n))
        gw_ref[...] += _mm_tn(yn_ref[...], d_out_b)
        dyc_ref[...] = d_ys[0]
        dys_ref[...] = d_ys[2]
        d_ym = d_ys[1]
        d_o = d_ym * szm
        dob_ref[...] = d_o.astype(BF16)
        dzm_ref[...] = d_ym * o * _dsilu(zm)
        prod = d_o * o
        head = lax.broadcasted_iota(jnp.int32, (ts, MLA_W), 1) // V_DIM
        delta = jnp.zeros((ts, MLA_W), F32)
        for h in range(HEADS):
            delta = jnp.where(head == h, _rsum(jnp.where(head == h, prod, 0.0)), delta)
        lane = lax.broadcasted_iota(jnp.int32, (ts, MLA_W), 1)
        st_ref[...] = jnp.where(lane % V_DIM < V_DIM // 2, lse_ref[...], delta)

    return pl.pallas_call(
        body, name=name, grid=(s // ts,),
        in_specs=[_row_spec(ts, D_MODEL), _row_spec(ts, CONV_W), _row_spec(ts, MLA_W), _row_spec(ts, MLA_W),
                  _row_spec(ts, MLA_W, C_ZM // MLA_W), _row_spec(ts, SG_W), _full_spec((1, D_MODEL)),
                  _full_spec((D_MODEL, D_MODEL))],
        out_specs=[_row_spec(ts, CONV_W), _row_spec(ts, SG_W), _row_spec(ts, MLA_W), _row_spec(ts, MLA_W),
                   _row_spec(ts, MLA_W), _full_spec((D_MODEL, D_MODEL)), _full_spec((1, D_MODEL))],
        out_shape=[jax.ShapeDtypeStruct((s, CONV_W), F32), jax.ShapeDtypeStruct((s, SG_W), F32),
                   jax.ShapeDtypeStruct((s, MLA_W), BF16), jax.ShapeDtypeStruct((s, MLA_W), F32),
                   jax.ShapeDtypeStruct((s, MLA_W), F32), jax.ShapeDtypeStruct((D_MODEL, D_MODEL), F32),
                   jax.ShapeDtypeStruct((1, D_MODEL), F32)],
        scratch_shapes=[pltpu.VMEM((ts, D_MODEL), BF16)],
        compiler_params=_cp(("arbitrary",)),
    )(d_out, yc, o, lse, proj, ys, bng, wout)


def _loss_head(y, target, name):
    s = y.shape[0]
    ts = _tile(s)
    nt = s // ts

    def body(y_ref, t_ref, dy_ref, l_ref, acc_ref):
        i = pl.program_id(0)

        @pl.when(i == 0)
        def _():
            acc_ref[...] = jnp.zeros_like(acc_ref)

        e = y_ref[...] - t_ref[...]
        dy_ref[...] = e * (1.0 / D_MODEL)
        sq = jnp.sum((e * e).reshape(ts // 8, 8, D_MODEL), axis=0)
        part = sq[:, 0:LANES]
        for c in range(LANES, D_MODEL, LANES):
            part = part + sq[:, c:c + LANES]
        acc_ref[...] += part

        @pl.when(i == nt - 1)
        def _():
            tot = jnp.sum(_rsum(acc_ref[...]), axis=0, keepdims=True) * (0.5 / D_MODEL)
            l_ref[...] = jnp.broadcast_to(tot, (8, LANES))

    return pl.pallas_call(
        body, name=name, grid=(nt,),
        in_specs=[_row_spec(ts, D_MODEL), _row_spec(ts, D_MODEL)],
        out_specs=[_row_spec(ts, D_MODEL), _full_spec((8, LANES))],
        out_shape=[jax.ShapeDtypeStruct((s, D_MODEL), F32), jax.ShapeDtypeStruct((8, LANES), F32)],
        scratch_shapes=[pltpu.VMEM((8, LANES), F32)],
        compiler_params=_cp(("arbitrary",)),
    )(y, target)


MESH = pl.DeviceIdType.MESH
ANY = pl.BlockSpec(memory_space=pl.ANY)


def _all_gather(xs, name):
    n = len(xs)
    per = N_DEV - 1

    def body(*refs):
        x_refs, out_refs = refs[:n], refs[n:2 * n]
        send_sems, recv_sems, local_sems = refs[2 * n:]
        x, y, c = lax.axis_index("x"), lax.axis_index("y"), lax.axis_index("c")
        me, sibling = (x, y, c), (x, y, 1 - c)
        chips = [(1 - x, y), (x, 1 - y), (1 - x, 1 - y)]

        def slot(t, px, py, pc):
            return out_refs[t].at[4 * px + 2 * py + pc]

        def copy(t, k, block, to, src=None):
            return pltpu.make_async_remote_copy(
                src_ref=slot(t, *block) if src is None else src, dst_ref=slot(t, *block),
                send_sem=send_sems.at[t * per + k], recv_sem=recv_sems.at[t * per + k], device_id=to,
                device_id_type=MESH)

        mine = [pltpu.make_async_copy(x_refs[t], slot(t, *me), local_sems.at[t]) for t in range(n)]
        for cp in mine:
            cp.start()
        first = [copy(t, 0, me, sibling, src=x_refs[t]) for t in range(n)]
        first += [copy(t, 1 + j, me, (*chip, c), src=x_refs[t]) for j, chip in enumerate(chips) for t in range(n)]
        for cp in first:
            cp.start()
        passed = []
        for j, chip in enumerate(chips):
            for t in range(n):
                copy(t, 1 + j, (*chip, c), me).wait_recv()
                passed.append(copy(t, 4 + j, (*chip, c), sibling))
                passed[-1].start()
        for t in range(n):
            copy(t, 0, sibling, me).wait_recv()
        for j, chip in enumerate(chips):
            for t in range(n):
                copy(t, 4 + j, (*chip, 1 - c), me).wait_recv()
        for cp in first + passed:
            cp.wait_send()
        for cp in mine:
            cp.wait()

    return pl.pallas_call(
        body, name=name,
        out_shape=[jax.ShapeDtypeStruct((N_DEV,) + a.shape, a.dtype) for a in xs],
        in_specs=[ANY] * n, out_specs=[ANY] * n,
        scratch_shapes=[pltpu.SemaphoreType.DMA((per * n,)), pltpu.SemaphoreType.DMA((per * n,)),
                        pltpu.SemaphoreType.DMA((n,))],
    )(*xs)


N_CHIP = N_DEV // 2


def _grad_to_sibling(gss, gr, name):
    n = len(gss)

    def body(*refs):
        gs_refs, gr_ref = refs[:n], refs[n]
        os_refs, or_ref = refs[n + 1:2 * n + 1], refs[2 * n + 1]
        send_sems, recv_sems = refs[2 * n + 2:]
        x, y, c = lax.axis_index("x"), lax.axis_index("y"), lax.axis_index("c")
        copies = []
        for t in range(n + 1):
            copies.append(pltpu.make_async_remote_copy(
                src_ref=gs_refs[t].at[1 - c] if t < n else gr_ref, dst_ref=os_refs[t] if t < n else or_ref,
                send_sem=send_sems.at[t], recv_sem=recv_sems.at[t], device_id=(x, y, 1 - c), device_id_type=MESH))
        for cp in copies:
            cp.start()
        for cp in copies:
            cp.wait_recv()
        for cp in copies:
            cp.wait_send()

    return pl.pallas_call(
        body, name=name,
        out_shape=[jax.ShapeDtypeStruct(g.shape[1:], g.dtype) for g in gss] + [jax.ShapeDtypeStruct(gr.shape, gr.dtype)],
        in_specs=[ANY] * (n + 1), out_specs=[ANY] * (n + 1),
        scratch_shapes=[pltpu.SemaphoreType.DMA((n + 1,)), pltpu.SemaphoreType.DMA((n + 1,))],
    )(*gss, gr)


def _chip_sum(mine, theirs, gr, gr_theirs, name):
    n = len(mine)

    def body(*refs):
        a_refs, b_refs = refs[:n + 1], refs[n + 1:2 * n + 2]
        o_refs = refs[2 * n + 2:]
        for a_ref, b_ref, o_ref in zip(a_refs, b_refs, o_refs):
            o_ref[...] = (a_ref[...].astype(F32) + b_ref[...].astype(F32)).astype(o_ref.dtype)

    def spec(a):
        if a.ndim == 3:
            return pl.BlockSpec((1,) + a.shape[1:], lambda i: (i, 0, 0))
        return pl.BlockSpec(a.shape, lambda i: (0, 0))

    ins = list(mine) + [gr] + list(theirs) + [gr_theirs]
    return pl.pallas_call(
        body, name=name, grid=(N_CHIP,),
        in_specs=[spec(a) for a in ins], out_specs=[spec(a) for a in ins[:n + 1]],
        out_shape=[jax.ShapeDtypeStruct(a.shape, a.dtype) for a in ins[:n + 1]],
        compiler_params=_cp(("arbitrary",)),
    )(*ins)


def _grad_to_chips(gss, gr, name):
    n = len(gss)
    per = N_CHIP - 1

    def body(*refs):
        gs_refs, gr_ref = refs[:n], refs[n]
        os_refs, or_ref = refs[n + 1:2 * n + 1], refs[2 * n + 1]
        send_sems, recv_sems, local_sems = refs[2 * n + 2:]
        x, y, c = lax.axis_index("x"), lax.axis_index("y"), lax.axis_index("c")
        me = 2 * x + y
        local = [pltpu.make_async_copy(gs_refs[t].at[me], os_refs[t].at[me], local_sems.at[t]) for t in range(n)]
        local.append(pltpu.make_async_copy(gr_ref, or_ref.at[me], local_sems.at[n]))
        for cp in local:
            cp.start()
        sends, recvs = [], []
        for k in range(1, N_CHIP):
            px = 1 - x if k & 2 else x
            py = 1 - y if k & 1 else y
            peer = 2 * px + py
            for t in range(n + 1):
                sems = dict(send_sem=send_sems.at[t * per + k - 1], recv_sem=recv_sems.at[t * per + k - 1],
                            device_id=(px, py, c), device_id_type=MESH)
                src = gs_refs[t].at[peer] if t < n else gr_ref
                out = os_refs[t] if t < n else or_ref
                sends.append(pltpu.make_async_remote_copy(src_ref=src, dst_ref=out.at[me], **sems))
                recvs.append(pltpu.make_async_remote_copy(src_ref=src, dst_ref=out.at[peer], **sems))
        for cp in sends:
            cp.start()
        for cp in recvs:
            cp.wait_recv()
        for cp in sends:
            cp.wait_send()
        for cp in local:
            cp.wait()

    nsem = per * (n + 1)
    return pl.pallas_call(
        body, name=name,
        out_shape=[jax.ShapeDtypeStruct(g.shape, g.dtype) for g in gss]
        + [jax.ShapeDtypeStruct((N_CHIP,) + gr.shape, gr.dtype)],
        in_specs=[ANY] * (n + 1), out_specs=[ANY] * (n + 1),
        scratch_shapes=[pltpu.SemaphoreType.DMA((nsem,)), pltpu.SemaphoreType.DMA((nsem,)),
                        pltpu.SemaphoreType.DMA((n + 1,))],
    )(*gss, gr)


ADAM_ROWS = 128


def _adamw(parts, w, m, v, name):
    r, cols = w.shape
    tr = ADAM_ROWS if r % ADAM_ROWS == 0 else r
    n_parts = parts.shape[0]

    def body(p_ref, w_ref, m_ref, v_ref, g_ref, d_ref, nm_ref, nv_ref):
        g = p_ref[0].astype(F32)
        for sidx in range(1, n_parts):
            g = g + p_ref[sidx].astype(F32)
        mm = ADAM_B1 * m_ref[...] + (1.0 - ADAM_B1) * g
        vv = ADAM_B2 * v_ref[...] + (1.0 - ADAM_B2) * (g * g)
        m_hat = mm / (1.0 - ADAM_B1 ** ADAM_STEP)
        v_hat = vv / (1.0 - ADAM_B2 ** ADAM_STEP)
        g_ref[...] = g
        d_ref[...] = -ADAM_LR * (m_hat / (jnp.sqrt(v_hat) + ADAM_EPS) + ADAM_WD * w_ref[...])
        nm_ref[...] = mm
        nv_ref[...] = vv

    row = pl.BlockSpec((tr, cols), lambda i: (i, 0))
    return pl.pallas_call(
        body, name=name, grid=(r // tr,),
        in_specs=[pl.BlockSpec((n_parts, tr, cols), lambda i: (0, i, 0)), row, row, row],
        out_specs=[row, row, row, row],
        out_shape=[jax.ShapeDtypeStruct((r, cols), F32)] * 4,
        compiler_params=_cp(("parallel",)),
    )(parts, w, m, v)


PACK_W = 8 * LANES
BF16_ROWS = 16


def _pack(flat_parts, rows):
    flat = jnp.concatenate([p.reshape(-1) for p in flat_parts])
    return jnp.pad(flat, (0, rows * PACK_W - flat.shape[0])).reshape(rows, PACK_W)


def _rows_for(n, mult):
    rows = -(-n // PACK_W)
    return -(-rows // mult) * mult


def _unshard(arr8, axis):
    full = jnp.moveaxis(arr8, 0, axis)
    shp = list(full.shape)
    shp[axis:axis + 2] = [shp[axis] * shp[axis + 1]]
    return full.reshape(shp)


def _split_c_chip(full, axis):
    shp = list(full.shape)
    shp[axis:axis + 1] = [N_CHIP, 2, shp[axis] // N_DEV]
    return jnp.moveaxis(full.reshape(shp), (axis + 1, axis), (0, 1))


def _unpack(flat2d, shapes, lead=()):
    flat = flat2d.reshape(lead + (-1,))
    out, off = [], 0
    for shp in shapes:
        n = math.prod(shp)
        out.append(flat[..., off:off + n].reshape(lead + tuple(shp)))
        off += n
    return out


def _to_layout(w):
    return jnp.concatenate([w[:, :1536], w[:, 1824:2336], w[:, 1536:1792], w[:, 2336:3104], w[:, 1792:1824],
                            jnp.zeros((w.shape[0], NP - IN_COLS), w.dtype)], axis=1)


def _from_layout(g):
    return jnp.concatenate([g[:, :1536], g[:, C_CKV:C_CKV + KV_LORA], g[:, C_KR:C_KR + ROPE],
                            g[:, C_ZM:C_ZM + MLA_W], g[:, C_SG:C_SG + 3 * SG_W]], axis=1)


def _pad_heads(w, real):
    lead = w.shape[:-1]
    w = w.reshape(lead + (HEADS, real))
    return jnp.pad(w, [(0, 0)] * len(lead) + [(0, 0), (0, HEAD_PAD - real)]).reshape(lead + (QW,))


def _rope_tables(s):
    half = ROPE // 2
    inv_freq = ROPE_THETA ** (-jnp.arange(half, dtype=F32) / half)
    ang = jnp.arange(s, dtype=F32)[:, None] * inv_freq[None, :]
    cos, sin = jnp.cos(ang), jnp.sin(ang)
    ones = jnp.ones((s, NOPE), F32)
    zeros = jnp.zeros((s, NOPE), F32)
    pad = jnp.zeros((s, HEAD_PAD - QK), F32)
    rc = jnp.concatenate([ones, cos, cos, pad + 1.0], axis=1)
    rs = jnp.concatenate([zeros, -sin, sin, pad], axis=1)
    return rc, rs


def kernel(x, norm_g, w_in, conv_w, conv_b, conv_ln_g, conv_ln_b, conv_pw_w, conv_pw_b, q_norm_g, w_uq, kv_norm_g, w_ukv, qk_q_g, qk_k_g, sg_ln_g, sg_ln_b, sg_w, sg_b, branch_norm_g, w_out, loss_target, m_norm_g, m_w_in, m_conv_w, m_conv_b, m_conv_ln_g, m_conv_ln_b, m_conv_pw_w, m_conv_pw_b, m_q_norm_g, m_w_uq, m_kv_norm_g, m_w_ukv, m_qk_q_g, m_qk_k_g, m_sg_ln_g, m_sg_ln_b, m_sg_w, m_sg_b, m_branch_norm_g, m_w_out, v_norm_g, v_w_in, v_conv_w, v_conv_b, v_conv_ln_g, v_conv_ln_b, v_conv_pw_w, v_conv_pw_b, v_q_norm_g, v_w_uq, v_kv_norm_g, v_w_ukv, v_qk_q_g, v_qk_k_g, v_sg_ln_g, v_sg_ln_b, v_sg_w, v_sg_b, v_branch_norm_g, v_w_out):
    given = dict(locals())
    wts = {n: given[n] for n in W_NAMES}
    mom_m = {n: given['m_' + n] for n in W_NAMES}
    mom_v = {n: given['v_' + n] for n in W_NAMES}
    s = x.shape[1]
    xs = x.reshape(s, D_MODEL)
    target = loss_target.reshape(s, D_MODEL)

    rp_shapes = [wts[n].shape for n in REPL]
    rows_rp = _rows_for(sum(math.prod(p) for p in rp_shapes), BF16_ROWS)
    sh_shape = {n: wts[n].shape for n in SHARDED}
    sh_2d = {n: (sh_shape[n][0] * sh_shape[n][1], sh_shape[n][2]) for n in SHARDED}

    gathered = _all_gather([wts[n].astype(F32 if n == 'conv_w' else BF16).reshape(sh_2d[n]) for n in SHARDED],
                           "weights_all_gather")
    full = {n: _unshard(g.reshape((N_DEV,) + sh_shape[n]), SHARD_AXIS[n]) for n, g in zip(SHARDED, gathered)}
    full.update({n: wts[n] for n in REPL})

    rc, rs = _rope_tables(s)
    tril = jnp.tril(jnp.ones((SG_CHUNK, SG_CHUNK), dtype=bool))

    def vec(a, width=None):
        a = a.reshape(1, -1)
        return a if width is None else jnp.pad(a, ((0, 0), (0, width - a.shape[1])))

    layers = []
    for l in range(DEPTH):
        p = {n: full[n][l] for n in W_NAMES}
        wukv = p['w_ukv'].reshape(KV_LORA, HEADS, NOPE + V_DIM)
        wm = jnp.where(tril[None], p['sg_w'], 0.0)
        layers.append(dict(
            ng=vec(p['norm_g']), win=_to_layout(p['w_in']).astype(BF16),
            cw=jnp.pad(p['conv_w'], ((0, HALO - CONV_K), (0, 0))), cb=vec(p['conv_b']), clg=vec(p['conv_ln_g']),
            clb=vec(p['conv_ln_b']), pww=p['conv_pw_w'].astype(BF16), pwb=vec(p['conv_pw_b']),
            qg=vec(p['q_norm_g']), wuq=_pad_heads(p['w_uq'], QK).astype(BF16), kvg=vec(p['kv_norm_g']),
            wukv=jnp.concatenate([_pad_heads(wukv[:, :, :NOPE].reshape(KV_LORA, HEADS * NOPE), NOPE),
                                  wukv[:, :, NOPE:].reshape(KV_LORA, MLA_W)], axis=1).astype(BF16),
            gq=vec(p['qk_q_g'], LANES), gk=vec(p['qk_k_g'], LANES),
            slg=vec(p['sg_ln_g']), slb=vec(p['sg_ln_b']), wm=wm.astype(BF16),
            wmt=jnp.swapaxes(wm, 1, 2).astype(BF16),
            sbx=jnp.repeat(p['sg_b'].T, SG_W // SG_HEADS, axis=1),
            bng=vec(p['branch_norm_g']), wout=p['w_out'].astype(BF16)))

    acts = []
    h_in = xs
    for l, p in enumerate(layers):
        proj = _proj_fwd(h_in, p['ng'], p['win'], f"proj_fwd_{l}")
        yc, cv = _conv_fwd(proj, p['cw'], p['cb'], p['clg'], p['clb'], p['pww'], p['pwb'], f"conv_fwd_{l}")
        ys = _sgu_fwd(proj, p['slg'], p['slb'], p['wm'], p['sbx'], f"sgu_fwd_{l}")
        q, k, v = _mla_fwd(proj, rc, rs, p['qg'], p['wuq'], p['kvg'], p['wukv'], p['gq'], p['gk'], f"mla_fwd_{l}")
        o, lse = _attn_fwd(q, k, v, f"attn_fwd_{l}")
        h_out = _post_fwd(h_in, yc, o, proj, ys, p['bng'], p['wout'], f"post_fwd_{l}")
        acts.append(dict(x=h_in, proj=proj, yc=yc, cv=cv, ys=ys, q=q, k=k, v=v, o=o, lse=lse))
        h_in = h_out

    d_out, loss_blk = _loss_head(h_in, target, "loss_head")
    loss = lax.psum(loss_blk[0, 0], ("x", "y", "c"))

    grads = {n: [None] * DEPTH for n in W_NAMES}
    for l in reversed(range(DEPTH)):
        p, a = layers[l], acts[l]
        d_yc, d_ys, d_o, d_zm, stats, g_wout, g_bng = _post_bwd(
            d_out, a['yc'], a['o'], a['lse'], a['proj'], a['ys'], p['bng'], p['wout'], f"post_bwd_{l}")
        dq, dk, dv = _attn_bwd(a['q'], a['k'], a['v'], d_o, stats, f"attn_bwd_{l}")
        d_a, g_cw, g_pww, gv_c = _conv_bwd(a['proj'], a['cv'], d_yc, p['cw'], p['clg'], p['clb'], p['pww'], p['pwb'],
                                           f"conv_bwd_{l}")
        d_sg, g_wm, dms, gv_s = _sgu_bwd(a['proj'], d_ys, p['slg'], p['slb'], p['wm'], p['wmt'], p['sbx'],
                                         f"sgu_bwd_{l}")
        d_cq, d_ckv, d_kr, g_wuq, g_wukv, gv_m = _mla_bwd(
            a['proj'], rc, rs, p['qg'], p['wuq'], p['kvg'], p['wukv'], p['gq'], p['gk'], dq, dk, dv, f"mla_bwd_{l}")
        pieces = [(d_a, C_A), (d_cq, C_CQ), (d_zm, C_ZM), (d_ckv, C_CKV), (d_sg, C_SG), (d_kr, C_KR)]
        d_x, h_t, g_ng = _proj_bwd(a['x'], p['ng'], p['win'], d_out, pieces, f"proj_bwd_{l}")
        g_win = _win_grad(h_t, pieces, f"win_grad_{l}")
        d_out = d_x

        grads['norm_g'][l] = g_ng[0]
        grads['w_in'][l] = _from_layout(g_win)
        grads['conv_w'][l] = g_cw[:CONV_K]
        grads['conv_b'][l] = gv_c[0]
        grads['conv_ln_g'][l] = gv_c[1]
        grads['conv_ln_b'][l] = gv_c[2]
        grads['conv_pw_w'][l] = g_pww
        grads['conv_pw_b'][l] = gv_c[3]
        grads['q_norm_g'][l] = gv_m[0, :Q_LORA]
        grads['w_uq'][l] = g_wuq.reshape(Q_LORA, HEADS, HEAD_PAD)[:, :, :QK].reshape(Q_LORA, HEADS * QK)
        grads['kv_norm_g'][l] = gv_m[1, :KV_LORA]
        grads['w_ukv'][l] = jnp.concatenate(
            [g_wukv[:, :QW].reshape(KV_LORA, HEADS, HEAD_PAD)[:, :, :NOPE],
             g_wukv[:, QW:].reshape(KV_LORA, HEADS, V_DIM)], axis=2).reshape(KV_LORA, HEADS * (NOPE + V_DIM))
        grads['qk_q_g'][l] = gv_m[2, :QK]
        grads['qk_k_g'][l] = gv_m[3, :QK]
        grads['sg_ln_g'][l] = gv_s[0]
        grads['sg_ln_b'][l] = gv_s[1]
        grads['sg_w'][l] = jnp.where(tril[None], g_wm, 0.0)
        grads['sg_b'][l] = dms.reshape(SG_CHUNK, SG_HEADS, SG_W // SG_HEADS).sum(axis=2).T
        grads['branch_norm_g'][l] = g_bng[0]
        grads['w_out'][l] = g_wout
    grad_x = d_out.reshape(x.shape)
    g_full = {n: jnp.stack(grads[n]) for n in W_NAMES}

    gss = [_split_c_chip(g_full[n].astype(BF16), SHARD_AXIS[n]).reshape((2, N_CHIP) + sh_2d[n]) for n in SHARDED]
    gr = _pack([g_full[n].astype(BF16) for n in REPL], rows_rp)
    *theirs, gr_theirs = _grad_to_sibling(gss, gr, "grad_to_sibling")
    my_c = lax.axis_index("c")
    mine = [lax.dynamic_index_in_dim(g, my_c, 0, keepdims=False) for g in gss]
    *chip_sh, chip_rp = _chip_sum(mine, theirs, gr, gr_theirs, "chip_sum")
    *parts_sh, parts_rp = _grad_to_chips(chip_sh, chip_rp, "grad_to_chips")
    res_sh = {n: _adamw(parts, wts[n].reshape(sh_2d[n]), mom_m[n].reshape(sh_2d[n]), mom_v[n].reshape(sh_2d[n]),
                        f"adamw_{n}") for n, parts in zip(SHARDED, parts_sh)}
    res_rp = _adamw(parts_rp, _pack([wts[n] for n in REPL], rows_rp), _pack([mom_m[n] for n in REPL], rows_rp),
                    _pack([mom_v[n] for n in REPL], rows_rp), "adamw_replicated")
    outs = []
    for kind in range(4):
        vals = {n: res_sh[n][kind].reshape(sh_shape[n]) for n in SHARDED}
        vals.update(zip(REPL, _unpack(res_rp[kind], rp_shapes)))
        outs.extend(vals[n] for n in W_NAMES)
    return (loss, grad_x, *outs)
```

```python
import functools
import math

import jax
import jax.numpy as jnp
from jax import lax
from jax.experimental import pallas as pl
from jax.experimental.pallas import tpu as pltpu

F32 = jnp.float32
BF16 = jnp.bfloat16

N_DEV = 8
DEPTH = 2
D_MODEL = 1024
CONV_W = 256
CONV_K = 31
HEADS = 8
NOPE = 64
ROPE = 32
QK = NOPE + ROPE
HEAD_PAD = 128
V_DIM = 64
MLA_W = HEADS * V_DIM
Q_LORA = 768
KV_LORA = 256
SG_W = 256
SG_HEADS = 4
SG_CHUNK = 128
ROPE_THETA = 10000.0
EPS = 1e-6
IN_COLS = 3104
NP = 3200
C_A, C_CQ, C_ZM, C_CKV, C_SG, C_KR = 0, 768, 1536, 2048, 2304, 3072
HALO = 32
SUB = 64
NEG = -1e30
LANES = 128
VMEM_LIMIT_V7X = 52 * 1024 * 1024

ADAM_LR = 0.001
ADAM_B1 = 0.9
ADAM_B2 = 0.999
ADAM_EPS = 1e-08
ADAM_WD = 0.01
ADAM_STEP = 10

W_NAMES = ['norm_g', 'w_in', 'conv_w', 'conv_b', 'conv_ln_g', 'conv_ln_b', 'conv_pw_w', 'conv_pw_b',
           'q_norm_g', 'w_uq', 'kv_norm_g', 'w_ukv', 'qk_q_g', 'qk_k_g', 'sg_ln_g', 'sg_ln_b', 'sg_w',
           'sg_b', 'branch_norm_g', 'w_out']
SHARD_AXIS = {'w_in': 2, 'conv_w': 2, 'conv_pw_w': 1, 'w_uq': 1, 'w_ukv': 2, 'w_out': 1}
SHARDED = [n for n in W_NAMES if n in SHARD_AXIS]
REPL = [n for n in W_NAMES if n not in SHARD_AXIS]


def _tile(s):
    for t in (512, 256, 128):
        if s % t == 0 and s // t >= 2:
            return t
    return s


def _cp(sem):
    return pltpu.CompilerParams(dimension_semantics=sem, vmem_limit_bytes=VMEM_LIMIT_V7X)


def _mm(a, b):
    return jnp.dot(a.astype(BF16), b.astype(BF16), preferred_element_type=F32)


def _mm_nt(a, b):
    return lax.dot_general(a.astype(BF16), b.astype(BF16), (((1,), (1,)), ((), ())),
                           preferred_element_type=F32)


def _mm_tn(a, b):
    return lax.dot_general(a.astype(BF16), b.astype(BF16), (((0,), (0,)), ((), ())),
                           preferred_element_type=F32)


_GC = math.sqrt(2.0 / math.pi)
_GA = 0.044715


def _sig(x):
    return 1.0 / (1.0 + jnp.exp(-x))


def _silu(x):
    return x * _sig(x)


def _dsilu(x):
    s = _sig(x)
    return s * (1.0 + x * (1.0 - s))


def _gelu(x):
    return 0.5 * x * (1.0 + jnp.tanh(_GC * (x + _GA * x * x * x)))


def _dgelu(x):
    t = jnp.tanh(_GC * (x + _GA * x * x * x))
    return 0.5 * (1.0 + t) + 0.5 * x * (1.0 - t * t) * _GC * (1.0 + 3.0 * _GA * x * x)


def _rsum(x):
    return jnp.sum(x, axis=-1, keepdims=True)


def _csum(x):
    return jnp.sum(x, axis=0, keepdims=True)


def _rms_fwd(x, n):
    r = lax.rsqrt(_rsum(x * x) * (1.0 / n) + EPS)
    return x * r, r


def _rms_bwd(dxh, xn, r, n):
    return r * (dxh - xn * (_rsum(dxh * xn) * (1.0 / n)))


def _ln_fwd(x, n):
    mu = _rsum(x) * (1.0 / n)
    xc = x - mu
    r = lax.rsqrt(_rsum(xc * xc) * (1.0 / n) + EPS)
    return xc * r, r


def _ln_bwd(dxh, xh, r, n):
    return r * (dxh - _rsum(dxh) * (1.0 / n) - xh * (_rsum(dxh * xh) * (1.0 / n)))


def _partner(x, lane):
    return jnp.where(lane < NOPE + ROPE // 2, pltpu.roll(x, LANES - ROPE // 2, 1), pltpu.roll(x, ROPE // 2, 1))


def _row_spec(ts, w, col=0):
    return pl.BlockSpec((ts, w), lambda i, col=col: (i, col))


def _full_spec(shape):
    nd = len(shape)
    return pl.BlockSpec(shape, lambda i, nd=nd: (0,) * nd)


PROJ_CHUNK = 640


def _proj_fwd(x, ng, win_p, name):
    s = x.shape[0]
    ts = _tile(s)

    def body(x_ref, g_ref, w_ref, o_ref):
        xv = x_ref[...]
        xn, _ = _rms_fwd(xv, D_MODEL)
        h = (xn * g_ref[...]).astype(BF16)
        for c in range(0, NP, PROJ_CHUNK):
            o_ref[:, c:c + PROJ_CHUNK] = jnp.dot(h, w_ref[:, c:c + PROJ_CHUNK], preferred_element_type=F32)

    return pl.pallas_call(
        body, name=name, grid=(s // ts,),
        in_specs=[_row_spec(ts, D_MODEL), _full_spec((1, D_MODEL)), _full_spec((D_MODEL, NP))],
        out_specs=_row_spec(ts, NP),
        out_shape=jax.ShapeDtypeStruct((s, NP), F32),
        compiler_params=_cp(("parallel",)),
    )(x, ng, win_p)


def _proj_bwd(x, ng, win_p, d_out, pieces, name):
    s = x.shape[0]
    ts = _tile(s)
    offs = [o for _, o in pieces]
    widths = [p.shape[1] for p, _ in pieces]

    def body(x_ref, g_ref, w_ref, do_ref, *rest):
        p_refs = rest[:len(pieces)]
        dx_ref, h_ref, gg_ref = rest[len(pieces):]
        i = pl.program_id(0)
        xv = x_ref[...]
        xn, r = _rms_fwd(xv, D_MODEL)
        g = g_ref[...]
        h_ref[...] = (xn * g).T.astype(BF16)
        dh = jnp.zeros((ts, D_MODEL), F32)
        for p_ref, off, w in zip(p_refs, offs, widths):
            dh = dh + _mm_nt(p_ref[...], w_ref[:, off:off + w])

        @pl.when(i == 0)
        def _():
            gg_ref[...] = jnp.zeros_like(gg_ref)

        gg_ref[...] += _csum(dh * xn)
        dx_ref[...] = _rms_bwd(dh * g, xn, r, D_MODEL) + do_ref[...]

    in_specs = [_row_spec(ts, D_MODEL), _full_spec((1, D_MODEL)), _full_spec((D_MODEL, NP)), _row_spec(ts, D_MODEL)]
    in_specs += [_row_spec(ts, w) for w in widths]
    return pl.pallas_call(
        body, name=name, grid=(s // ts,),
        in_specs=in_specs,
        out_specs=[_row_spec(ts, D_MODEL), pl.BlockSpec((D_MODEL, ts), lambda i: (0, i)), _full_spec((1, D_MODEL))],
        out_shape=[jax.ShapeDtypeStruct((s, D_MODEL), F32), jax.ShapeDtypeStruct((D_MODEL, s), BF16),
                   jax.ShapeDtypeStruct((1, D_MODEL), F32)],
        compiler_params=_cp(("arbitrary",)),
    )(x, ng, win_p, d_out, *[p for p, _ in pieces])


WG_TILE = 256


def _win_grad(ht, pieces, name):
    s = ht.shape[1]
    ts = min(WG_TILE, s)
    offs = [o for _, o in pieces]
    widths = [p.shape[1] for p, _ in pieces]

    def body(ht_ref, *rest):
        p_refs, o_ref = rest[:-1], rest[-1]

        @pl.when(pl.program_id(0) == 0)
        def _():
            o_ref[...] = jnp.zeros_like(o_ref)

        hb = ht_ref[...]
        for p_ref, off, w in zip(p_refs, offs, widths):
            o_ref[:, off:off + w] += jnp.dot(hb, p_ref[...].astype(BF16), preferred_element_type=F32)

    return pl.pallas_call(
        body, name=name, grid=(s // ts,),
        in_specs=[pl.BlockSpec((D_MODEL, ts), lambda i: (0, i))] + [_row_spec(ts, w) for w in widths],
        out_specs=_full_spec((D_MODEL, NP)),
        out_shape=jax.ShapeDtypeStruct((D_MODEL, NP), F32),
        compiler_params=_cp(("arbitrary",)),
    )(ht, *[p for p, _ in pieces])


def _halo_spec(ts):
    per = ts // HALO
    return pl.BlockSpec((HALO, 2 * CONV_W), lambda i: (jnp.maximum(i * per - 1, 0), 0))


def _conv_taps(ext_ref, cw_ref, cv_ref, cb, ts):
    base = HALO - (CONV_K - 1)
    for r0 in range(0, ts, SUB):
        acc = jnp.zeros((SUB, CONV_W), F32)
        for k in range(CONV_K):
            acc = acc + cw_ref[k:k + 1, :] * ext_ref[r0 + base + k:r0 + base + k + SUB, :]
        cv_ref[r0:r0 + SUB, :] = acc + cb


def _conv_fwd(proj, cw, cb, lg, lb, pww, pwb, name):
    s = proj.shape[0]
    ts = _tile(s)

    def body(pa_ref, ph_ref, cw_ref, cb_ref, lg_ref, lb_ref, pww_ref, pwb_ref, y_ref, cv_ref, ext_ref):
        i = pl.program_id(0)
        pa = pa_ref[...]
        a, ag, zc = pa[:, :CONV_W], pa[:, CONV_W:2 * CONV_W], pa[:, 2 * CONV_W:]
        ph = ph_ref[...]
        hglu = ph[:, :CONV_W] * _sig(ph[:, CONV_W:])
        ext_ref[0:HALO, :] = jnp.where(i > 0, hglu, 0.0)
        ext_ref[HALO:HALO + ts, :] = a * _sig(ag)
        _conv_taps(ext_ref, cw_ref, cv_ref, cb_ref[...], ts)
        xh, _ = _ln_fwd(cv_ref[...], CONV_W)
        ln = xh * lg_ref[...] + lb_ref[...]
        pw = _mm(_silu(ln), pww_ref[...]) + pwb_ref[...]
        y_ref[...] = pw * _silu(zc)

    vec = _full_spec((1, CONV_W))
    return pl.pallas_call(
        body, name=name, grid=(s // ts,),
        in_specs=[_row_spec(ts, 3 * CONV_W, 0), _halo_spec(ts), _full_spec((HALO, CONV_W)), vec, vec, vec,
                  _full_spec((CONV_W, CONV_W)), vec],
        out_specs=[_row_spec(ts, CONV_W), _row_spec(ts, CONV_W)],
        out_shape=[jax.ShapeDtypeStruct((s, CONV_W), F32), jax.ShapeDtypeStruct((s, CONV_W), F32)],
        scratch_shapes=[pltpu.VMEM((HALO + ts, CONV_W), F32)],
        compiler_params=_cp(("parallel",)),
    )(proj, proj, cw, cb, lg, lb, pww, pwb)


def _conv_bwd(proj, cv, dy, cw, lg, lb, pww, pwb, name):
    s = proj.shape[0]
    ts = _tile(s)
    nt = s // ts
    per = ts // HALO

    def body(pa_ref, ph_ref, cv_ref, dy_ref, cw_ref, lg_ref, lb_ref, pww_ref, pwb_ref,
             dp_ref, gcw_ref, gpw_ref, gv_ref, ext_ref, dext_ref, carry_ref, gacc_ref):
        i = pl.program_id(0)
        ti = nt - 1 - i

        @pl.when(i == 0)
        def _():
            carry_ref[...] = jnp.zeros_like(carry_ref)
            gacc_ref[...] = jnp.zeros_like(gacc_ref)
            gpw_ref[...] = jnp.zeros_like(gpw_ref)
            gv_ref[...] = jnp.zeros_like(gv_ref)

        pa = pa_ref[...]
        a, ag, zc = pa[:, :CONV_W], pa[:, CONV_W:2 * CONV_W], pa[:, 2 * CONV_W:]
        sag = _sig(ag)
        ph = ph_ref[...]
        hglu = ph[:, :CONV_W] * _sig(ph[:, CONV_W:])
        ext_ref[0:HALO, :] = jnp.where(ti > 0, hglu, 0.0)
        ext_ref[HALO:HALO + ts, :] = a * sag
        xh, rl = _ln_fwd(cv_ref[...], CONV_W)
        lg = lg_ref[...]
        ln = xh * lg + lb_ref[...]
        sw = _silu(ln)
        pww = pww_ref[...]
        pw = _mm(sw, pww) + pwb_ref[...]
        d_y = dy_ref[...]
        d_pw = d_y * _silu(zc)
        d_zc = d_y * pw * _dsilu(zc)
        gpw_ref[...] += _mm_tn(sw, d_pw)
        d_ln = _mm_nt(d_pw, pww) * _dsilu(ln)
        d_cv = _ln_bwd(d_ln * lg, xh, rl, CONV_W)
        gv_ref[0:1, :] += _csum(d_cv)
        gv_ref[1:2, :] += _csum(d_ln * xh)
        gv_ref[2:3, :] += _csum(d_ln)
        gv_ref[3:4, :] += _csum(d_pw)
        dext_ref[0:ts, :] = d_cv
        dext_ref[ts:ts + HALO, :] = carry_ref[...]
        carry_ref[...] = d_cv[0:HALO, :]
        base = HALO - (CONV_K - 1)
        for r0 in range(0, ts, SUB):
            dcv_r = dext_ref[r0:r0 + SUB, :]
            dg = jnp.zeros((SUB, CONV_W), F32)
            for k in range(CONV_K):
                prod = dcv_r * ext_ref[r0 + base + k:r0 + base + k + SUB, :]
                gacc_ref[8 * k:8 * k + 8, :] += jnp.sum(prod.reshape(SUB // 8, 8, CONV_W), axis=0)
                dg = dg + cw_ref[k:k + 1, :] * dext_ref[r0 + CONV_K - 1 - k:r0 + CONV_K - 1 - k + SUB, :]
            sg_r, a_r = sag[r0:r0 + SUB, :], a[r0:r0 + SUB, :]
            dp_ref[r0:r0 + SUB, 0:CONV_W] = (dg * sg_r).astype(BF16)
            dp_ref[r0:r0 + SUB, CONV_W:2 * CONV_W] = (dg * a_r * sg_r * (1.0 - sg_r)).astype(BF16)
        dp_ref[:, 2 * CONV_W:] = d_zc.astype(BF16)

        @pl.when(i == nt - 1)
        def _():
            gcw_ref[...] = jnp.zeros_like(gcw_ref)
            for k in range(CONV_K):
                gcw_ref[k:k + 1, :] = _csum(gacc_ref[8 * k:8 * k + 8, :])

    vec = _full_spec((1, CONV_W))
    rev = lambda w, col=0: pl.BlockSpec((ts, w), lambda i, col=col: (nt - 1 - i, col))
    halo = pl.BlockSpec((HALO, 2 * CONV_W), lambda i: (jnp.maximum((nt - 1 - i) * per - 1, 0), 0))
    return pl.pallas_call(
        body, name=name, grid=(nt,),
        in_specs=[rev(3 * CONV_W), halo, rev(CONV_W), rev(CONV_W), _full_spec((HALO, CONV_W)), vec, vec,
                  _full_spec((CONV_W, CONV_W)), vec],
        out_specs=[rev(3 * CONV_W), _full_spec((HALO, CONV_W)), _full_spec((CONV_W, CONV_W)), _full_spec((8, CONV_W))],
        out_shape=[jax.ShapeDtypeStruct((s, 3 * CONV_W), BF16), jax.ShapeDtypeStruct((HALO, CONV_W), F32),
                   jax.ShapeDtypeStruct((CONV_W, CONV_W), F32), jax.ShapeDtypeStruct((8, CONV_W), F32)],
        scratch_shapes=[pltpu.VMEM((HALO + ts, CONV_W), F32), pltpu.VMEM((ts + HALO, CONV_W), F32),
                        pltpu.VMEM((HALO, CONV_W), F32), pltpu.VMEM((8 * HALO, CONV_W), F32)],
        compiler_params=_cp(("arbitrary",)),
    )(proj, proj, cv, dy, cw, lg, lb, pww, pwb)


def _sg_mix(wm_ref, vc, head):
    out = jnp.zeros((SG_CHUNK, SG_W), F32)
    vb = vc.astype(BF16)
    for g in range(SG_HEADS):
        out = jnp.where(head == g, jnp.dot(wm_ref[g], vb, preferred_element_type=F32), out)
    return out


def _sgu_fwd(proj, lg, lb, wm, sbx, name):
    s = proj.shape[0]
    ts = _tile(s)

    def body(ps_ref, lg_ref, lb_ref, wm_ref, sbx_ref, y_ref, mix_ref):
        ps = ps_ref[...]
        us, vs, zs = ps[:, :SG_W], ps[:, SG_W:2 * SG_W], ps[:, 2 * SG_W:]
        xh, _ = _ln_fwd(_gelu(vs), SG_W)
        vn = xh * lg_ref[...] + lb_ref[...]
        head = lax.broadcasted_iota(jnp.int32, (SG_CHUNK, SG_W), 1) // (SG_W // SG_HEADS)
        for c0 in range(0, ts, SG_CHUNK):
            mix_ref[c0:c0 + SG_CHUNK, :] = _sg_mix(wm_ref, vn[c0:c0 + SG_CHUNK, :], head) + sbx_ref[...]
        y_ref[...] = _gelu(us) * mix_ref[...] * _silu(zs)

    vec = _full_spec((1, SG_W))
    return pl.pallas_call(
        body, name=name, grid=(s // ts,),
        in_specs=[_row_spec(ts, 3 * SG_W, C_SG // (3 * SG_W)), vec, vec,
                  _full_spec((SG_HEADS, SG_CHUNK, SG_CHUNK)), _full_spec((SG_CHUNK, SG_W))],
        out_specs=_row_spec(ts, SG_W),
        out_shape=jax.ShapeDtypeStruct((s, SG_W), F32),
        scratch_shapes=[pltpu.VMEM((ts, SG_W), F32)],
        compiler_params=_cp(("parallel",)),
    )(proj, lg, lb, wm, sbx)


def _sgu_bwd(proj, dy, lg, lb, wm, wmt, sbx, name):
    s = proj.shape[0]
    ts = _tile(s)

    def body(ps_ref, dy_ref, lg_ref, lb_ref, wm_ref, wmt_ref, sbx_ref,
             dp_ref, gwm_ref, dms_ref, gv_ref, mix_ref, dvn_ref):
        i = pl.program_id(0)

        @pl.when(i == 0)
        def _():
            gwm_ref[...] = jnp.zeros_like(gwm_ref)
            dms_ref[...] = jnp.zeros_like(dms_ref)
            gv_ref[...] = jnp.zeros_like(gv_ref)

        ps = ps_ref[...]
        us, vs, zs = ps[:, :SG_W], ps[:, SG_W:2 * SG_W], ps[:, 2 * SG_W:]
        xh, rl = _ln_fwd(_gelu(vs), SG_W)
        lg = lg_ref[...]
        vn = xh * lg + lb_ref[...]
        head = lax.broadcasted_iota(jnp.int32, (SG_CHUNK, SG_W), 1) // (SG_W // SG_HEADS)
        for c0 in range(0, ts, SG_CHUNK):
            mix_ref[c0:c0 + SG_CHUNK, :] = _sg_mix(wm_ref, vn[c0:c0 + SG_CHUNK, :], head) + sbx_ref[...]
        mixed = mix_ref[...]
        u = _gelu(us)
        sz = _silu(zs)
        d_y = dy_ref[...]
        d_mixed = d_y * u * sz
        dp_ref[:, 0:SG_W] = (d_y * mixed * sz * _dgelu(us)).astype(BF16)
        dp_ref[:, 2 * SG_W:] = (d_y * u * mixed * _dsilu(zs)).astype(BF16)
        dms = jnp.zeros((SG_CHUNK, SG_W), F32)
        for c0 in range(0, ts, SG_CHUNK):
            dm = d_mixed[c0:c0 + SG_CHUNK, :]
            vc = vn[c0:c0 + SG_CHUNK, :]
            dms = dms + dm
            for g in range(SG_HEADS):
                gwm_ref[g] += _mm_nt(jnp.where(head == g, dm, 0.0), vc)
            dvn_ref[c0:c0 + SG_CHUNK, :] = _sg_mix(wmt_ref, dm, head)
        dms_ref[...] += dms
        d_vn = dvn_ref[...]
        gv_ref[0:1, :] += _csum(d_vn * xh)
        gv_ref[1:2, :] += _csum(d_vn)
        dp_ref[:, SG_W:2 * SG_W] = (_ln_bwd(d_vn * lg, xh, rl, SG_W) * _dgelu(vs)).astype(BF16)

    vec = _full_spec((1, SG_W))
    wspec = _full_spec((SG_HEADS, SG_CHUNK, SG_CHUNK))
    return pl.pallas_call(
        body, name=name, grid=(s // ts,),
        in_specs=[_row_spec(ts, 3 * SG_W, C_SG // (3 * SG_W)), _row_spec(ts, SG_W), vec, vec, wspec, wspec,
                  _full_spec((SG_CHUNK, SG_W))],
        out_specs=[_row_spec(ts, 3 * SG_W), wspec, _full_spec((SG_CHUNK, SG_W)), _full_spec((8, SG_W))],
        out_shape=[jax.ShapeDtypeStruct((s, 3 * SG_W), BF16), jax.ShapeDtypeStruct((SG_HEADS, SG_CHUNK, SG_CHUNK), F32),
                   jax.ShapeDtypeStruct((SG_CHUNK, SG_W), F32), jax.ShapeDtypeStruct((8, SG_W), F32)],
        scratch_shapes=[pltpu.VMEM((ts, SG_W), F32), pltpu.VMEM((ts, SG_W), F32)],
        compiler_params=_cp(("arbitrary",)),
    )(proj, dy, lg, lb, wm, wmt, sbx)


QW = HEADS * HEAD_PAD
KVW = QW + MLA_W
ATT_SCALE = QK ** -0.5


def _mla_specs(ts):
    return [_row_spec(ts, Q_LORA, C_CQ // Q_LORA), _row_spec(ts, KV_LORA, C_CKV // KV_LORA),
            _row_spec(ts, LANES, C_KR // LANES), _row_spec(ts, LANES), _row_spec(ts, LANES),
            _full_spec((1, Q_LORA)), _full_spec((Q_LORA, QW)), _full_spec((1, KV_LORA)), _full_spec((KV_LORA, KVW)),
            _full_spec((1, LANES)), _full_spec((1, LANES))]


def _mla_fwd(proj, rc, rs, qg, wuq, kvg, wukv, gq, gk, name):
    s = proj.shape[0]
    ts = _tile(s)

    def body(cq_ref, ckv_ref, kr_ref, rc_ref, rs_ref, qg_ref, wuq_ref, kvg_ref, wukv_ref, gq_ref, gk_ref,
             q_ref, k_ref, v_ref):
        lane = lax.broadcasted_iota(jnp.int32, (ts, LANES), 1)
        c, sn = rc_ref[...], rs_ref[...]
        cqn, _ = _rms_fwd(cq_ref[...], Q_LORA)
        q0 = _mm(cqn * qg_ref[...], wuq_ref[...])
        gq = gq_ref[...]
        for h in range(HEADS):
            xn, _ = _rms_fwd(q0[:, h * LANES:(h + 1) * LANES], QK)
            qn = xn * gq
            q_ref[:, h * LANES:(h + 1) * LANES] = ((qn * c + _partner(qn, lane) * sn) * ATT_SCALE).astype(BF16)
        ckvn, _ = _rms_fwd(ckv_ref[...], KV_LORA)
        kv = _mm(ckvn * kvg_ref[...], wukv_ref[...])
        kr = pltpu.roll(kr_ref[...], NOPE, 1)
        gk = gk_ref[...]
        for h in range(HEADS):
            xn, _ = _rms_fwd(kv[:, h * LANES:(h + 1) * LANES] + kr, QK)
            kn = xn * gk
            k_ref[:, h * LANES:(h + 1) * LANES] = (kn * c + _partner(kn, lane) * sn).astype(BF16)
        v_ref[...] = kv[:, QW:].astype(BF16)

    return pl.pallas_call(
        body, name=name, grid=(s // ts,),
        in_specs=_mla_specs(ts),
        out_specs=[_row_spec(ts, QW), _row_spec(ts, QW), _row_spec(ts, MLA_W)],
        out_shape=[jax.ShapeDtypeStruct((s, QW), BF16), jax.ShapeDtypeStruct((s, QW), BF16),
                   jax.ShapeDtypeStruct((s, MLA_W), BF16)],
        compiler_params=_cp(("parallel",)),
    )(proj, proj, proj, rc, rs, qg, wuq, kvg, wukv, gq, gk)


def _mla_bwd(proj, rc, rs, qg, wuq, kvg, wukv, gq, gk, dq, dk, dv, name):
    s = proj.shape[0]
    ts = _tile(s)

    def body(cq_ref, ckv_ref, kr_ref, rc_ref, rs_ref, qg_ref, wuq_ref, kvg_ref, wukv_ref, gq_ref, gk_ref,
             dq_ref, dk_ref, dv_ref, dcq_ref, dckv_ref, dkr_ref, gwuq_ref, gwukv_ref, gv_ref, d0_ref):
        i = pl.program_id(0)

        @pl.when(i == 0)
        def _():
            gwuq_ref[...] = jnp.zeros_like(gwuq_ref)
            gwukv_ref[...] = jnp.zeros_like(gwukv_ref)
            gv_ref[...] = jnp.zeros_like(gv_ref)

        lane = lax.broadcasted_iota(jnp.int32, (ts, LANES), 1)
        c, sn = rc_ref[...], rs_ref[...]
        cq = cq_ref[...]
        cqx, rq0 = _rms_fwd(cq, Q_LORA)
        qg = qg_ref[...]
        cqn = cqx * qg
        wuq = wuq_ref[...]
        q0 = _mm(cqn, wuq)
        gq = gq_ref[...]
        ggq = jnp.zeros((1, LANES), F32)
        for h in range(HEADS):
            xn, r = _rms_fwd(q0[:, h * LANES:(h + 1) * LANES], QK)
            d = dq_ref[:, h * LANES:(h + 1) * LANES] * ATT_SCALE
            d_qn = d * c - _partner(d, lane) * sn
            ggq = ggq + _csum(d_qn * xn)
            d0_ref[:, h * LANES:(h + 1) * LANES] = _rms_bwd(d_qn * gq, xn, r, QK)
        dq0 = d0_ref[:, 0:QW]
        gwuq_ref[...] += _mm_tn(cqn, dq0)
        d_cqn = _mm_nt(dq0, wuq)
        gv_ref[0:1, 0:Q_LORA] += _csum(d_cqn * cqx)
        gv_ref[2:3, 0:LANES] += ggq
        dcq_ref[...] = _rms_bwd(d_cqn * qg, cqx, rq0, Q_LORA).astype(BF16)
        ckv = ckv_ref[...]
        ckx, rk0 = _rms_fwd(ckv, KV_LORA)
        kvg = kvg_ref[...]
        ckvn = ckx * kvg
        wukv = wukv_ref[...]
        kv = _mm(ckvn, wukv)
        kr = pltpu.roll(kr_ref[...], NOPE, 1)
        gk = gk_ref[...]
        ggk = jnp.zeros((1, LANES), F32)
        dkr = jnp.zeros((ts, LANES), F32)
        for h in range(HEADS):
            xn, r = _rms_fwd(kv[:, h * LANES:(h + 1) * LANES] + kr, QK)
            d = dk_ref[:, h * LANES:(h + 1) * LANES]
            d_kn = d * c - _partner(d, lane) * sn
            ggk = ggk + _csum(d_kn * xn)
            d_k0 = _rms_bwd(d_kn * gk, xn, r, QK)
            dkr = dkr + d_k0
            d0_ref[:, h * LANES:(h + 1) * LANES] = d_k0
        d0_ref[:, QW:KVW] = dv_ref[...]
        dkv = d0_ref[...]
        dkr_ref[...] = jnp.where(lane < ROPE, pltpu.roll(dkr, NOPE, 1), 0.0).astype(BF16)
        gwukv_ref[...] += _mm_tn(ckvn, dkv)
        d_ckvn = _mm_nt(dkv, wukv)
        gv_ref[1:2, 0:KV_LORA] += _csum(d_ckvn * ckx)
        gv_ref[3:4, 0:LANES] += ggk
        dckv_ref[...] = _rms_bwd(d_ckvn * kvg, ckx, rk0, KV_LORA).astype(BF16)

    return pl.pallas_call(
        body, name=name, grid=(s // ts,),
        in_specs=_mla_specs(ts) + [_row_spec(ts, QW), _row_spec(ts, QW), _row_spec(ts, MLA_W)],
        out_specs=[_row_spec(ts, Q_LORA), _row_spec(ts, KV_LORA), _row_spec(ts, LANES),
                   _full_spec((Q_LORA, QW)), _full_spec((KV_LORA, KVW)), _full_spec((8, QW))],
        out_shape=[jax.ShapeDtypeStruct((s, Q_LORA), BF16), jax.ShapeDtypeStruct((s, KV_LORA), BF16),
                   jax.ShapeDtypeStruct((s, LANES), BF16), jax.ShapeDtypeStruct((Q_LORA, QW), F32),
                   jax.ShapeDtypeStruct((KV_LORA, KVW), F32), jax.ShapeDtypeStruct((8, QW), F32)],
        scratch_shapes=[pltpu.VMEM((ts, KVW), F32)],
        compiler_params=_cp(("arbitrary",)),
    )(proj, proj, proj, rc, rs, qg, wuq, kvg, wukv, gq, gk, dq, dk, dv)


PAIRS = HEADS // 2
ATT_STRIP = 32


def _attn_fwd(q, k, v, name):
    s = q.shape[0]
    tq = _tile(s)
    tk = tq

    def body(q_ref, k_ref, v_ref, o_ref, lse_ref, s0_ref, s1_ref, p0_ref, p1_ref, m_ref, l_ref, acc_ref):
        i = pl.program_id(1)
        s_refs, p_refs = (s0_ref, s1_ref), (p0_ref, p1_ref)
        row = lax.broadcasted_iota(jnp.int32, (ATT_STRIP, tk), 0)
        col = lax.broadcasted_iota(jnp.int32, (ATT_STRIP, tk), 1)
        first = lax.broadcasted_iota(jnp.int32, (tq, LANES), 1) < V_DIM
        m_ref[...] = jnp.full(m_ref.shape, NEG, F32)
        l_ref[...] = jnp.zeros(l_ref.shape, F32)
        acc_ref[...] = jnp.zeros(acc_ref.shape, F32)

        def blk(j, masked):
            st = pl.multiple_of(j * tk, tk)
            for a in range(2):
                s_refs[a][...] = _mm_nt(q_ref[:, a * LANES:(a + 1) * LANES],
                                        k_ref[pl.ds(st, tk), a * LANES:(a + 1) * LANES])
            for a in range(2):
                for r in range(0, tq, ATT_STRIP):
                    sc = s_refs[a][r:r + ATT_STRIP, :]
                    if masked:
                        sc = jnp.where(col <= row + r, sc, NEG)
                    m_old = m_ref[a, r:r + ATT_STRIP, :]
                    m_new = jnp.maximum(m_old, jnp.max(sc, axis=-1, keepdims=True))
                    alpha = jnp.exp(m_old - m_new)
                    p = jnp.exp(sc - jnp.tile(m_new, (1, tk // LANES)))
                    l_ref[a, r:r + ATT_STRIP, :] = alpha * l_ref[a, r:r + ATT_STRIP, :] + _rsum(p)
                    acc_ref[a, r:r + ATT_STRIP, :] = alpha * acc_ref[a, r:r + ATT_STRIP, :]
                    m_ref[a, r:r + ATT_STRIP, :] = m_new
                    p_refs[a][r:r + ATT_STRIP, :] = p.astype(BF16)
                acc_ref[a] += jnp.dot(p_refs[a][...], v_ref[pl.ds(st, tk), :], preferred_element_type=F32)

        def two_blocks(t, carry):
            blk(2 * t, False)
            blk(2 * t + 1, False)
            return carry

        lax.fori_loop(0, i // 2, two_blocks, 0)

        @pl.when(i % 2 == 1)
        def _():
            blk(i - 1, False)

        blk(i, True)
        o_ref[...] = jnp.where(first, acc_ref[0] / l_ref[0], acc_ref[1] / l_ref[1])
        lse_ref[...] = jnp.where(first, m_ref[0] + jnp.log(l_ref[0]), m_ref[1] + jnp.log(l_ref[1]))

    stat = pltpu.VMEM((2, tq, LANES), F32)
    return pl.pallas_call(
        body, name=name, grid=(PAIRS, s // tq),
        in_specs=[pl.BlockSpec((tq, 2 * LANES), lambda p, i: (i, p)),
                  pl.BlockSpec((s, 2 * LANES), lambda p, i: (0, p)),
                  pl.BlockSpec((s, LANES), lambda p, i: (0, p))],
        out_specs=[pl.BlockSpec((tq, LANES), lambda p, i: (i, p)), pl.BlockSpec((tq, LANES), lambda p, i: (i, p))],
        out_shape=[jax.ShapeDtypeStruct((s, MLA_W), F32), jax.ShapeDtypeStruct((s, MLA_W), F32)],
        scratch_shapes=[pltpu.VMEM((tq, tk), F32), pltpu.VMEM((tq, tk), F32), pltpu.VMEM((tq, tk), BF16),
                        pltpu.VMEM((tq, tk), BF16), stat, stat, stat],
        compiler_params=_cp(("parallel", "parallel")),
    )(q, k, v)


def _attn_bwd(q, k, v, do, stats, name):
    s = q.shape[0]
    tq = _tile(s)
    tk = tq
    nq = s // tq

    def body(q_ref, k_ref, v_ref, do_ref, st_ref, dq_ref, dk_ref, dv_ref):
        j = pl.program_id(1)

        @pl.when(j == 0)
        def _():
            dq_ref[...] = jnp.zeros_like(dq_ref)

        dk_ref[...] = jnp.zeros_like(dk_ref)
        dv_ref[...] = jnp.zeros_like(dv_ref)
        row = lax.broadcasted_iota(jnp.int32, (tq, tk), 0)
        col = lax.broadcasted_iota(jnp.int32, (tq, tk), 1)
        lane = lax.broadcasted_iota(jnp.int32, (tq, LANES), 1)

        def blk(i, masked):
            st = i * tq if isinstance(i, int) else pl.multiple_of(i * tq, tq)
            do2 = do_ref[pl.ds(st, tq), :]
            stt = st_ref[pl.ds(st, tq), :]
            dv = None
            for a in range(2):
                mine = (lane < V_DIM) if a == 0 else (lane >= V_DIM)
                qa = q_ref[pl.ds(st, tq), a * LANES:(a + 1) * LANES]
                doa = jnp.where(mine, do2, jnp.zeros((), BF16))
                lse = stt[:, a * V_DIM:a * V_DIM + 1]
                dl = stt[:, a * V_DIM + V_DIM // 2:a * V_DIM + V_DIM // 2 + 1]
                p = jnp.exp(_mm_nt(qa, k_ref[:, a * LANES:(a + 1) * LANES]) - lse)
                if masked:
                    p = jnp.where(col <= row, p, 0.0)
                ds = (p * (_mm_nt(doa, v_ref[...]) - dl)).astype(BF16)
                dva = _mm_tn(p, doa)
                dv = dva if dv is None else dv + dva
                dk_ref[:, a * LANES:(a + 1) * LANES] += _mm_tn(ds, qa)
                dq_ref[pl.ds(st, tq), a * LANES:(a + 1) * LANES] += jnp.dot(
                    ds, k_ref[:, a * LANES:(a + 1) * LANES], preferred_element_type=F32)
            dv_ref[...] += dv

        blk(j, True)
        rest = nq - 1 - j

        def two_blocks(t, carry):
            blk(j + 1 + 2 * t, False)
            blk(j + 2 + 2 * t, False)
            return carry

        lax.fori_loop(0, rest // 2, two_blocks, 0)

        @pl.when(rest % 2 == 1)
        def _():
            blk(nq - 1, False)

    return pl.pallas_call(
        body, name=name, grid=(PAIRS, s // tk),
        in_specs=[pl.BlockSpec((s, 2 * LANES), lambda p, j: (0, p)),
                  pl.BlockSpec((tk, 2 * LANES), lambda p, j: (j, p)),
                  pl.BlockSpec((tk, LANES), lambda p, j: (j, p)),
                  pl.BlockSpec((s, LANES), lambda p, j: (0, p)),
                  pl.BlockSpec((s, LANES), lambda p, j: (0, p))],
        out_specs=[pl.BlockSpec((s, 2 * LANES), lambda p, j: (0, p)),
                   pl.BlockSpec((tk, 2 * LANES), lambda p, j: (j, p)),
                   pl.BlockSpec((tk, LANES), lambda p, j: (j, p))],
        out_shape=[jax.ShapeDtypeStruct((s, QW), F32), jax.ShapeDtypeStruct((s, QW), F32),
                   jax.ShapeDtypeStruct((s, MLA_W), F32)],
        compiler_params=_cp(("parallel", "arbitrary")),
    )(q, k, v, do, stats)


BR = ((0, CONV_W), (CONV_W, CONV_W + MLA_W), (CONV_W + MLA_W, D_MODEL))


def _post_fwd(x, yc, o, proj, ys, bng, wout, name):
    s = x.shape[0]
    ts = _tile(s)

    def body(x_ref, yc_ref, o_ref, zm_ref, ys_ref, g_ref, w_ref, out_ref):
        ys3 = (yc_ref[...], o_ref[...] * _silu(zm_ref[...]), ys_ref[...])
        acc = x_ref[...]
        for (lo, hi), yb in zip(BR, ys3):
            yn, _ = _rms_fwd(yb, hi - lo)
            acc = acc + _mm(yn * g_ref[:, lo:hi], w_ref[lo:hi, :])
        out_ref[...] = acc

    return pl.pallas_call(
        body, name=name, grid=(s // ts,),
        in_specs=[_row_spec(ts, D_MODEL), _row_spec(ts, CONV_W), _row_spec(ts, MLA_W),
                  _row_spec(ts, MLA_W, C_ZM // MLA_W), _row_spec(ts, SG_W), _full_spec((1, D_MODEL)),
                  _full_spec((D_MODEL, D_MODEL))],
        out_specs=_row_spec(ts, D_MODEL),
        out_shape=jax.ShapeDtypeStruct((s, D_MODEL), F32),
        compiler_params=_cp(("parallel",)),
    )(x, yc, o, proj, ys, bng, wout)


def _post_bwd(d_out, yc, o, lse, proj, ys, bng, wout, name):
    s = d_out.shape[0]
    ts = _tile(s)

    def body(do_ref, yc_ref, o_ref, lse_ref, zm_ref, ys_ref, g_ref, w_ref,
             dyc_ref, dys_ref, dob_ref, dzm_ref, st_ref, gw_ref, gg_ref, yn_ref):
        i = pl.program_id(0)

        @pl.when(i == 0)
        def _():
            gw_ref[...] = jnp.zeros_like(gw_ref)
            gg_ref[...] = jnp.zeros_like(gg_ref)

        d_out_b = do_ref[...].astype(BF16)
        o = o_ref[...]
        zm = zm_ref[...]
        szm = _silu(zm)
        ys3 = (yc_ref[...], o * szm, ys_ref[...])
        d_ys = []
        for (lo, hi), yb in zip(BR, ys3):
            n = hi - lo
            yn, r = _rms_fwd(yb, n)
            g = g_ref[:, lo:hi]
            yn_ref[:, lo:hi] = (yn * g).astype(BF16)
            d_yn = _mm_nt(d_out_b, w_ref[lo:hi, :])
            gg_ref[:, lo:hi] += _csum(d_yn * yn)
            d_ys.append(_rms_bwd(d_yn * g, yn, r, n))
        gw_ref[...] += _mm_tn(yn_ref[...], d_out_b)
        dyc_ref[...] = d_ys[0]
        dys_ref[...] = d_ys[2]
        d_ym = d_ys[1]
        d_o = d_ym * szm
        dob_ref[...] = d_o.astype(BF16)
        dzm_ref[...] = (d_ym * o * _dsilu(zm)).astype(BF16)
        prod = d_o * o
        head = lax.broadcasted_iota(jnp.int32, (ts, MLA_W), 1) // V_DIM
        delta = jnp.zeros((ts, MLA_W), F32)
        for h in range(HEADS):
            delta = jnp.where(head == h, _rsum(jnp.where(head == h, prod, 0.0)), delta)
        lane = lax.broadcasted_iota(jnp.int32, (ts, MLA_W), 1)
        st_ref[...] = jnp.where(lane % V_DIM < V_DIM // 2, lse_ref[...], delta)

    return pl.pallas_call(
        body, name=name, grid=(s // ts,),
        in_specs=[_row_spec(ts, D_MODEL), _row_spec(ts, CONV_W), _row_spec(ts, MLA_W), _row_spec(ts, MLA_W),
                  _row_spec(ts, MLA_W, C_ZM // MLA_W), _row_spec(ts, SG_W), _full_spec((1, D_MODEL)),
                  _full_spec((D_MODEL, D_MODEL))],
        out_specs=[_row_spec(ts, CONV_W), _row_spec(ts, SG_W), _row_spec(ts, MLA_W), _row_spec(ts, MLA_W),
                   _row_spec(ts, MLA_W), _full_spec((D_MODEL, D_MODEL)), _full_spec((1, D_MODEL))],
        out_shape=[jax.ShapeDtypeStruct((s, CONV_W), F32), jax.ShapeDtypeStruct((s, SG_W), F32),
                   jax.ShapeDtypeStruct((s, MLA_W), BF16), jax.ShapeDtypeStruct((s, MLA_W), BF16),
                   jax.ShapeDtypeStruct((s, MLA_W), F32), jax.ShapeDtypeStruct((D_MODEL, D_MODEL), F32),
                   jax.ShapeDtypeStruct((1, D_MODEL), F32)],
        scratch_shapes=[pltpu.VMEM((ts, D_MODEL), BF16)],
        compiler_params=_cp(("arbitrary",)),
    )(d_out, yc, o, lse, proj, ys, bng, wout)


def _loss_head(y, target, name):
    s = y.shape[0]
    ts = _tile(s)
    nt = s // ts

    def body(y_ref, t_ref, dy_ref, l_ref, acc_ref):
        i = pl.program_id(0)

        @pl.when(i == 0)
        def _():
            acc_ref[...] = jnp.zeros_like(acc_ref)

        e = y_ref[...] - t_ref[...]
        dy_ref[...] = e * (1.0 / D_MODEL)
        sq = jnp.sum((e * e).reshape(ts // 8, 8, D_MODEL), axis=0)
        part = sq[:, 0:LANES]
        for c in range(LANES, D_MODEL, LANES):
            part = part + sq[:, c:c + LANES]
        acc_ref[...] += part

        @pl.when(i == nt - 1)
        def _():
            tot = jnp.sum(_rsum(acc_ref[...]), axis=0, keepdims=True) * (0.5 / D_MODEL)
            l_ref[...] = jnp.broadcast_to(tot, (8, LANES))

    return pl.pallas_call(
        body, name=name, grid=(nt,),
        in_specs=[_row_spec(ts, D_MODEL), _row_spec(ts, D_MODEL)],
        out_specs=[_row_spec(ts, D_MODEL), _full_spec((8, LANES))],
        out_shape=[jax.ShapeDtypeStruct((s, D_MODEL), F32), jax.ShapeDtypeStruct((8, LANES), F32)],
        scratch_shapes=[pltpu.VMEM((8, LANES), F32)],
        compiler_params=_cp(("arbitrary",)),
    )(y, target)


MESH = pl.DeviceIdType.MESH
ANY = pl.BlockSpec(memory_space=pl.ANY)


def _all_gather(xs, name):
    n = len(xs)
    per = N_DEV - 1

    def body(*refs):
        x_refs, out_refs = refs[:n], refs[n:2 * n]
        send_sems, recv_sems, local_sems = refs[2 * n:]
        x, y, c = lax.axis_index("x"), lax.axis_index("y"), lax.axis_index("c")
        me, sibling = (x, y, c), (x, y, 1 - c)
        chips = [(1 - x, y), (x, 1 - y), (1 - x, 1 - y)]

        def slot(t, px, py, pc):
            return out_refs[t].at[4 * px + 2 * py + pc]

        def copy(t, k, block, to, src=None):
            return pltpu.make_async_remote_copy(
                src_ref=slot(t, *block) if src is None else src, dst_ref=slot(t, *block),
                send_sem=send_sems.at[t * per + k], recv_sem=recv_sems.at[t * per + k], device_id=to,
                device_id_type=MESH)

        mine = [pltpu.make_async_copy(x_refs[t], slot(t, *me), local_sems.at[t]) for t in range(n)]
        for cp in mine:
            cp.start()
        first = [copy(t, 0, me, sibling, src=x_refs[t]) for t in range(n)]
        first += [copy(t, 1 + j, me, (*chip, c), src=x_refs[t]) for j, chip in enumerate(chips) for t in range(n)]
        for cp in first:
            cp.start()
        passed = []
        for j, chip in enumerate(chips):
            for t in range(n):
                copy(t, 1 + j, (*chip, c), me).wait_recv()
                passed.append(copy(t, 4 + j, (*chip, c), sibling))
                passed[-1].start()
        for t in range(n):
            copy(t, 0, sibling, me).wait_recv()
        for j, chip in enumerate(chips):
            for t in range(n):
                copy(t, 4 + j, (*chip, 1 - c), me).wait_recv()
        for cp in first + passed:
            cp.wait_send()
        for cp in mine:
            cp.wait()

    return pl.pallas_call(
        body, name=name,
        out_shape=[jax.ShapeDtypeStruct((N_DEV,) + a.shape, a.dtype) for a in xs],
        in_specs=[ANY] * n, out_specs=[ANY] * n,
        scratch_shapes=[pltpu.SemaphoreType.DMA((per * n,)), pltpu.SemaphoreType.DMA((per * n,)),
                        pltpu.SemaphoreType.DMA((n,))],
    )(*xs)


N_CHIP = N_DEV // 2


def _grad_to_sibling(gss, gr, name):
    n = len(gss)

    def body(*refs):
        gs_refs, gr_ref = refs[:n], refs[n]
        os_refs, or_ref = refs[n + 1:2 * n + 1], refs[2 * n + 1]
        send_sems, recv_sems = refs[2 * n + 2:]
        x, y, c = lax.axis_index("x"), lax.axis_index("y"), lax.axis_index("c")
        copies = []
        for t in range(n + 1):
            copies.append(pltpu.make_async_remote_copy(
                src_ref=gs_refs[t].at[1 - c] if t < n else gr_ref, dst_ref=os_refs[t] if t < n else or_ref,
                send_sem=send_sems.at[t], recv_sem=recv_sems.at[t], device_id=(x, y, 1 - c), device_id_type=MESH))
        for cp in copies:
            cp.start()
        for cp in copies:
            cp.wait_recv()
        for cp in copies:
            cp.wait_send()

    return pl.pallas_call(
        body, name=name,
        out_shape=[jax.ShapeDtypeStruct(g.shape[1:], g.dtype) for g in gss] + [jax.ShapeDtypeStruct(gr.shape, gr.dtype)],
        in_specs=[ANY] * (n + 1), out_specs=[ANY] * (n + 1),
        scratch_shapes=[pltpu.SemaphoreType.DMA((n + 1,)), pltpu.SemaphoreType.DMA((n + 1,))],
    )(*gss, gr)


def _chip_sum(mine, theirs, gr, gr_theirs, name):
    n = len(mine)

    def body(*refs):
        a_refs, b_refs = refs[:n + 1], refs[n + 1:2 * n + 2]
        o_refs = refs[2 * n + 2:]
        for a_ref, b_ref, o_ref in zip(a_refs, b_refs, o_refs):
            o_ref[...] = (a_ref[...].astype(F32) + b_ref[...].astype(F32)).astype(o_ref.dtype)

    def spec(a):
        if a.ndim == 3:
            return pl.BlockSpec((1,) + a.shape[1:], lambda i: (i, 0, 0))
        return pl.BlockSpec(a.shape, lambda i: (0, 0))

    ins = list(mine) + [gr] + list(theirs) + [gr_theirs]
    return pl.pallas_call(
        body, name=name, grid=(N_CHIP,),
        in_specs=[spec(a) for a in ins], out_specs=[spec(a) for a in ins[:n + 1]],
        out_shape=[jax.ShapeDtypeStruct(a.shape, a.dtype) for a in ins[:n + 1]],
        compiler_params=_cp(("arbitrary",)),
    )(*ins)


def _grad_to_chips(gss, gr, name):
    n = len(gss)
    per = N_CHIP - 1

    def body(*refs):
        gs_refs, gr_ref = refs[:n], refs[n]
        os_refs, or_ref = refs[n + 1:2 * n + 1], refs[2 * n + 1]
        send_sems, recv_sems, local_sems = refs[2 * n + 2:]
        x, y, c = lax.axis_index("x"), lax.axis_index("y"), lax.axis_index("c")
        me = 2 * x + y
        local = [pltpu.make_async_copy(gs_refs[t].at[me], os_refs[t].at[me], local_sems.at[t]) for t in range(n)]
        local.append(pltpu.make_async_copy(gr_ref, or_ref.at[me], local_sems.at[n]))
        for cp in local:
            cp.start()
        sends, recvs = [], []
        for k in range(1, N_CHIP):
            px = 1 - x if k & 2 else x
            py = 1 - y if k & 1 else y
            peer = 2 * px + py
            for t in range(n + 1):
                sems = dict(send_sem=send_sems.at[t * per + k - 1], recv_sem=recv_sems.at[t * per + k - 1],
                            device_id=(px, py, c), device_id_type=MESH)
                src = gs_refs[t].at[peer] if t < n else gr_ref
                out = os_refs[t] if t < n else or_ref
                sends.append(pltpu.make_async_remote_copy(src_ref=src, dst_ref=out.at[me], **sems))
                recvs.append(pltpu.make_async_remote_copy(src_ref=src, dst_ref=out.at[peer], **sems))
        for cp in sends:
            cp.start()
        for cp in recvs:
            cp.wait_recv()
        for cp in sends:
            cp.wait_send()
        for cp in local:
            cp.wait()

    nsem = per * (n + 1)
    return pl.pallas_call(
        body, name=name,
        out_shape=[jax.ShapeDtypeStruct(g.shape, g.dtype) for g in gss]
        + [jax.ShapeDtypeStruct((N_CHIP,) + gr.shape, gr.dtype)],
        in_specs=[ANY] * (n + 1), out_specs=[ANY] * (n + 1),
        scratch_shapes=[pltpu.SemaphoreType.DMA((nsem,)), pltpu.SemaphoreType.DMA((nsem,)),
                        pltpu.SemaphoreType.DMA((n + 1,))],
    )(*gss, gr)


ADAM_ROWS = 128


def _adamw(parts, w, m, v, name):
    r, cols = w.shape
    tr = ADAM_ROWS if r % ADAM_ROWS == 0 else r
    n_parts = parts.shape[0]

    def body(p_ref, w_ref, m_ref, v_ref, g_ref, d_ref, nm_ref, nv_ref):
        g = p_ref[0].astype(F32)
        for sidx in range(1, n_parts):
            g = g + p_ref[sidx].astype(F32)
        mm = ADAM_B1 * m_ref[...] + (1.0 - ADAM_B1) * g
        vv = ADAM_B2 * v_ref[...] + (1.0 - ADAM_B2) * (g * g)
        m_hat = mm / (1.0 - ADAM_B1 ** ADAM_STEP)
        v_hat = vv / (1.0 - ADAM_B2 ** ADAM_STEP)
        g_ref[...] = g
        d_ref[...] = -ADAM_LR * (m_hat / (jnp.sqrt(v_hat) + ADAM_EPS) + ADAM_WD * w_ref[...])
        nm_ref[...] = mm
        nv_ref[...] = vv

    row = pl.BlockSpec((tr, cols), lambda i: (i, 0))
    return pl.pallas_call(
        body, name=name, grid=(r // tr,),
        in_specs=[pl.BlockSpec((n_parts, tr, cols), lambda i: (0, i, 0)), row, row, row],
        out_specs=[row, row, row, row],
        out_shape=[jax.ShapeDtypeStruct((r, cols), F32)] * 4,
        compiler_params=_cp(("parallel",)),
    )(parts, w, m, v)


PACK_W = 8 * LANES
BF16_ROWS = 16


def _pack(flat_parts, rows):
    flat = jnp.concatenate([p.reshape(-1) for p in flat_parts])
    return jnp.pad(flat, (0, rows * PACK_W - flat.shape[0])).reshape(rows, PACK_W)


def _rows_for(n, mult):
    rows = -(-n // PACK_W)
    return -(-rows // mult) * mult


def _unshard(arr8, axis):
    full = jnp.moveaxis(arr8, 0, axis)
    shp = list(full.shape)
    shp[axis:axis + 2] = [shp[axis] * shp[axis + 1]]
    return full.reshape(shp)


def _split_c_chip(full, axis):
    shp = list(full.shape)
    shp[axis:axis + 1] = [N_CHIP, 2, shp[axis] // N_DEV]
    return jnp.moveaxis(full.reshape(shp), (axis + 1, axis), (0, 1))


def _unpack(flat2d, shapes, lead=()):
    flat = flat2d.reshape(lead + (-1,))
    out, off = [], 0
    for shp in shapes:
        n = math.prod(shp)
        out.append(flat[..., off:off + n].reshape(lead + tuple(shp)))
        off += n
    return out


def _to_layout(w):
    return jnp.concatenate([w[:, :1536], w[:, 1824:2336], w[:, 1536:1792], w[:, 2336:3104], w[:, 1792:1824],
                            jnp.zeros((w.shape[0], NP - IN_COLS), w.dtype)], axis=1)


def _from_layout(g):
    return jnp.concatenate([g[:, :1536], g[:, C_CKV:C_CKV + KV_LORA], g[:, C_KR:C_KR + ROPE],
                            g[:, C_ZM:C_ZM + MLA_W], g[:, C_SG:C_SG + 3 * SG_W]], axis=1)


def _pad_heads(w, real):
    lead = w.shape[:-1]
    w = w.reshape(lead + (HEADS, real))
    return jnp.pad(w, [(0, 0)] * len(lead) + [(0, 0), (0, HEAD_PAD - real)]).reshape(lead + (QW,))


def _rope_tables(s):
    half = ROPE // 2
    inv_freq = ROPE_THETA ** (-jnp.arange(half, dtype=F32) / half)
    ang = jnp.arange(s, dtype=F32)[:, None] * inv_freq[None, :]
    cos, sin = jnp.cos(ang), jnp.sin(ang)
    ones = jnp.ones((s, NOPE), F32)
    zeros = jnp.zeros((s, NOPE), F32)
    pad = jnp.zeros((s, HEAD_PAD - QK), F32)
    rc = jnp.concatenate([ones, cos, cos, pad + 1.0], axis=1)
    rs = jnp.concatenate([zeros, -sin, sin, pad], axis=1)
    return rc, rs


def kernel(x, norm_g, w_in, conv_w, conv_b, conv_ln_g, conv_ln_b, conv_pw_w, conv_pw_b, q_norm_g, w_uq, kv_norm_g, w_ukv, qk_q_g, qk_k_g, sg_ln_g, sg_ln_b, sg_w, sg_b, branch_norm_g, w_out, loss_target, m_norm_g, m_w_in, m_conv_w, m_conv_b, m_conv_ln_g, m_conv_ln_b, m_conv_pw_w, m_conv_pw_b, m_q_norm_g, m_w_uq, m_kv_norm_g, m_w_ukv, m_qk_q_g, m_qk_k_g, m_sg_ln_g, m_sg_ln_b, m_sg_w, m_sg_b, m_branch_norm_g, m_w_out, v_norm_g, v_w_in, v_conv_w, v_conv_b, v_conv_ln_g, v_conv_ln_b, v_conv_pw_w, v_conv_pw_b, v_q_norm_g, v_w_uq, v_kv_norm_g, v_w_ukv, v_qk_q_g, v_qk_k_g, v_sg_ln_g, v_sg_ln_b, v_sg_w, v_sg_b, v_branch_norm_g, v_w_out):
    given = dict(locals())
    wts = {n: given[n] for n in W_NAMES}
    mom_m = {n: given['m_' + n] for n in W_NAMES}
    mom_v = {n: given['v_' + n] for n in W_NAMES}
    s = x.shape[1]
    xs = x.reshape(s, D_MODEL)
    target = loss_target.reshape(s, D_MODEL)

    rp_shapes = [wts[n].shape for n in REPL]
    rows_rp = _rows_for(sum(math.prod(p) for p in rp_shapes), BF16_ROWS)
    sh_shape = {n: wts[n].shape for n in SHARDED}
    sh_2d = {n: (sh_shape[n][0] * sh_shape[n][1], sh_shape[n][2]) for n in SHARDED}

    gathered = _all_gather([wts[n].astype(F32 if n == 'conv_w' else BF16).reshape(sh_2d[n]) for n in SHARDED],
                           "weights_all_gather")
    full = {n: _unshard(g.reshape((N_DEV,) + sh_shape[n]), SHARD_AXIS[n]) for n, g in zip(SHARDED, gathered)}
    full.update({n: wts[n] for n in REPL})

    rc, rs = _rope_tables(s)
    tril = jnp.tril(jnp.ones((SG_CHUNK, SG_CHUNK), dtype=bool))

    def vec(a, width=None):
        a = a.reshape(1, -1)
        return a if width is None else jnp.pad(a, ((0, 0), (0, width - a.shape[1])))

    layers = []
    for l in range(DEPTH):
        p = {n: full[n][l] for n in W_NAMES}
        wukv = p['w_ukv'].reshape(KV_LORA, HEADS, NOPE + V_DIM)
        wm = jnp.where(tril[None], p['sg_w'], 0.0)
        layers.append(dict(
            ng=vec(p['norm_g']), win=_to_layout(p['w_in']).astype(BF16),
            cw=jnp.pad(p['conv_w'], ((0, HALO - CONV_K), (0, 0))), cb=vec(p['conv_b']), clg=vec(p['conv_ln_g']),
            clb=vec(p['conv_ln_b']), pww=p['conv_pw_w'].astype(BF16), pwb=vec(p['conv_pw_b']),
            qg=vec(p['q_norm_g']), wuq=_pad_heads(p['w_uq'], QK).astype(BF16), kvg=vec(p['kv_norm_g']),
            wukv=jnp.concatenate([_pad_heads(wukv[:, :, :NOPE].reshape(KV_LORA, HEADS * NOPE), NOPE),
                                  wukv[:, :, NOPE:].reshape(KV_LORA, MLA_W)], axis=1).astype(BF16),
            gq=vec(p['qk_q_g'], LANES), gk=vec(p['qk_k_g'], LANES),
            slg=vec(p['sg_ln_g']), slb=vec(p['sg_ln_b']), wm=wm.astype(BF16),
            wmt=jnp.swapaxes(wm, 1, 2).astype(BF16),
            sbx=jnp.repeat(p['sg_b'].T, SG_W // SG_HEADS, axis=1),
            bng=vec(p['branch_norm_g']), wout=p['w_out'].astype(BF16)))

    acts = []
    h_in = xs
    for l, p in enumerate(layers):
        proj = _proj_fwd(h_in, p['ng'], p['win'], f"proj_fwd_{l}")
        yc, cv = _conv_fwd(proj, p['cw'], p['cb'], p['clg'], p['clb'], p['pww'], p['pwb'], f"conv_fwd_{l}")
        ys = _sgu_fwd(proj, p['slg'], p['slb'], p['wm'], p['sbx'], f"sgu_fwd_{l}")
        q, k, v = _mla_fwd(proj, rc, rs, p['qg'], p['wuq'], p['kvg'], p['wukv'], p['gq'], p['gk'], f"mla_fwd_{l}")
        o, lse = _attn_fwd(q, k, v, f"attn_fwd_{l}")
        h_out = _post_fwd(h_in, yc, o, proj, ys, p['bng'], p['wout'], f"post_fwd_{l}")
        acts.append(dict(x=h_in, proj=proj, yc=yc, cv=cv, ys=ys, q=q, k=k, v=v, o=o, lse=lse))
        h_in = h_out

    d_out, loss_blk = _loss_head(h_in, target, "loss_head")
    loss = lax.psum(loss_blk[0, 0], ("x", "y", "c"))

    grads = {n: [None] * DEPTH for n in W_NAMES}
    for l in reversed(range(DEPTH)):
        p, a = layers[l], acts[l]
        d_yc, d_ys, d_o, d_zm, stats, g_wout, g_bng = _post_bwd(
            d_out, a['yc'], a['o'], a['lse'], a['proj'], a['ys'], p['bng'], p['wout'], f"post_bwd_{l}")
        dq, dk, dv = _attn_bwd(a['q'], a['k'], a['v'], d_o, stats, f"attn_bwd_{l}")
        d_a, g_cw, g_pww, gv_c = _conv_bwd(a['proj'], a['cv'], d_yc, p['cw'], p['clg'], p['clb'], p['pww'], p['pwb'],
                                           f"conv_bwd_{l}")
        d_sg, g_wm, dms, gv_s = _sgu_bwd(a['proj'], d_ys, p['slg'], p['slb'], p['wm'], p['wmt'], p['sbx'],
                                         f"sgu_bwd_{l}")
        d_cq, d_ckv, d_kr, g_wuq, g_wukv, gv_m = _mla_bwd(
            a['proj'], rc, rs, p['qg'], p['wuq'], p['kvg'], p['wukv'], p['gq'], p['gk'], dq, dk, dv, f"mla_bwd_{l}")
        pieces = [(d_a, C_A), (d_cq, C_CQ), (d_zm, C_ZM), (d_ckv, C_CKV), (d_sg, C_SG), (d_kr, C_KR)]
        d_x, h_t, g_ng = _proj_bwd(a['x'], p['ng'], p['win'], d_out, pieces, f"proj_bwd_{l}")
        g_win = _win_grad(h_t, pieces, f"win_grad_{l}")
        d_out = d_x

        grads['norm_g'][l] = g_ng[0]
        grads['w_in'][l] = _from_layout(g_win)
        grads['conv_w'][l] = g_cw[:CONV_K]
        grads['conv_b'][l] = gv_c[0]
        grads['conv_ln_g'][l] = gv_c[1]
        grads['conv_ln_b'][l] = gv_c[2]
        grads['conv_pw_w'][l] = g_pww
        grads['conv_pw_b'][l] = gv_c[3]
        grads['q_norm_g'][l] = gv_m[0, :Q_LORA]
        grads['w_uq'][l] = g_wuq.reshape(Q_LORA, HEADS, HEAD_PAD)[:, :, :QK].reshape(Q_LORA, HEADS * QK)
        grads['kv_norm_g'][l] = gv_m[1, :KV_LORA]
        grads['w_ukv'][l] = jnp.concatenate(
            [g_wukv[:, :QW].reshape(KV_LORA, HEADS, HEAD_PAD)[:, :, :NOPE],
             g_wukv[:, QW:].reshape(KV_LORA, HEADS, V_DIM)], axis=2).reshape(KV_LORA, HEADS * (NOPE + V_DIM))
        grads['qk_q_g'][l] = gv_m[2, :QK]
        grads['qk_k_g'][l] = gv_m[3, :QK]
        grads['sg_ln_g'][l] = gv_s[0]
        grads['sg_ln_b'][l] = gv_s[1]
        grads['sg_w'][l] = jnp.where(tril[None], g_wm, 0.0)
        grads['sg_b'][l] = dms.reshape(SG_CHUNK, SG_HEADS, SG_W // SG_HEADS).sum(axis=2).T
        grads['branch_norm_g'][l] = g_bng[0]
        grads['w_out'][l] = g_wout
    grad_x = d_out.reshape(x.shape)
    g_full = {n: jnp.stack(grads[n]) for n in W_NAMES}

    gss = [_split_c_chip(g_full[n].astype(BF16), SHARD_AXIS[n]).reshape((2, N_CHIP) + sh_2d[n]) for n in SHARDED]
    gr = _pack([g_full[n].astype(BF16) for n in REPL], rows_rp)
    *theirs, gr_theirs = _grad_to_sibling(gss, gr, "grad_to_sibling")
    my_c = lax.axis_index("c")
    mine = [lax.dynamic_index_in_dim(g, my_c, 0, keepdims=False) for g in gss]
    *chip_sh, chip_rp = _chip_sum(mine, theirs, gr, gr_theirs, "chip_sum")
    *parts_sh, parts_rp = _grad_to_chips(chip_sh, chip_rp, "grad_to_chips")
    res_sh = {n: _adamw(parts, wts[n].reshape(sh_2d[n]), mom_m[n].reshape(sh_2d[n]), mom_v[n].reshape(sh_2d[n]),
                        f"adamw_{n}") for n, parts in zip(SHARDED, parts_sh)}
    res_rp = _adamw(parts_rp, _pack([wts[n] for n in REPL], rows_rp), _pack([mom_m[n] for n in REPL], rows_rp),
                    _pack([mom_v[n] for n in REPL], rows_rp), "adamw_replicated")
    outs = []
    for kind in range(4):
        vals = {n: res_sh[n][kind].reshape(sh_shape[n]) for n in SHARDED}
        vals.update(zip(REPL, _unpack(res_rp[kind], rp_shapes)))
        outs.extend(vals[n] for n in W_NAMES)
    return (loss, grad_x, *outs)
```

```python
import functools
import math

import jax
import jax.numpy as jnp
from jax import lax
from jax.experimental import pallas as pl
from jax.experimental.pallas import tpu as pltpu

F32 = jnp.float32
BF16 = jnp.bfloat16

N_DEV = 8
DEPTH = 2
D_MODEL = 1024
CONV_W = 256
CONV_K = 31
HEADS = 8
NOPE = 64
ROPE = 32
QK = NOPE + ROPE
HEAD_PAD = 128
V_DIM = 64
MLA_W = HEADS * V_DIM
Q_LORA = 768
KV_LORA = 256
SG_W = 256
SG_HEADS = 4
SG_CHUNK = 128
ROPE_THETA = 10000.0
EPS = 1e-6
IN_COLS = 3104
NP = 3200
C_A, C_CQ, C_ZM, C_CKV, C_SG, C_KR = 0, 768, 1536, 2048, 2304, 3072
HALO = 32
SUB = 64
NEG = -1e30
LANES = 128
VMEM_LIMIT_V7X = 52 * 1024 * 1024

ADAM_LR = 0.001
ADAM_B1 = 0.9
ADAM_B2 = 0.999
ADAM_EPS = 1e-08
ADAM_WD = 0.01
ADAM_STEP = 10

W_NAMES = ['norm_g', 'w_in', 'conv_w', 'conv_b', 'conv_ln_g', 'conv_ln_b', 'conv_pw_w', 'conv_pw_b',
           'q_norm_g', 'w_uq', 'kv_norm_g', 'w_ukv', 'qk_q_g', 'qk_k_g', 'sg_ln_g', 'sg_ln_b', 'sg_w',
           'sg_b', 'branch_norm_g', 'w_out']
SHARD_AXIS = {'w_in': 2, 'conv_w': 2, 'conv_pw_w': 1, 'w_uq': 1, 'w_ukv': 2, 'w_out': 1}
SHARDED = [n for n in W_NAMES if n in SHARD_AXIS]
REPL = [n for n in W_NAMES if n not in SHARD_AXIS]


def _tile(s):
    for t in (512, 256, 128):
        if s % t == 0 and s // t >= 2:
            return t
    return s


def _cp(sem):
    return pltpu.CompilerParams(dimension_semantics=sem, vmem_limit_bytes=VMEM_LIMIT_V7X)


def _mm(a, b):
    return jnp.dot(a.astype(BF16), b.astype(BF16), preferred_element_type=F32)


def _mm_nt(a, b):
    return lax.dot_general(a.astype(BF16), b.astype(BF16), (((1,), (1,)), ((), ())),
                           preferred_element_type=F32)


def _mm_tn(a, b):
    return lax.dot_general(a.astype(BF16), b.astype(BF16), (((0,), (0,)), ((), ())),
                           preferred_element_type=F32)


_GC = math.sqrt(2.0 / math.pi)
_GA = 0.044715


def _sig(x):
    return 1.0 / (1.0 + jnp.exp(-x))


def _silu(x):
    return x * _sig(x)


def _dsilu(x):
    s = _sig(x)
    return s * (1.0 + x * (1.0 - s))


def _gelu(x):
    return 0.5 * x * (1.0 + jnp.tanh(_GC * (x + _GA * x * x * x)))


def _dgelu(x):
    t = jnp.tanh(_GC * (x + _GA * x * x * x))
    return 0.5 * (1.0 + t) + 0.5 * x * (1.0 - t * t) * _GC * (1.0 + 3.0 * _GA * x * x)


def _rsum(x):
    return jnp.sum(x, axis=-1, keepdims=True)


def _csum(x):
    return jnp.sum(x, axis=0, keepdims=True)


def _rms_fwd(x, n):
    r = lax.rsqrt(_rsum(x * x) * (1.0 / n) + EPS)
    return x * r, r


def _rms_bwd(dxh, xn, r, n):
    return r * (dxh - xn * (_rsum(dxh * xn) * (1.0 / n)))


def _ln_fwd(x, n):
    mu = _rsum(x) * (1.0 / n)
    xc = x - mu
    r = lax.rsqrt(_rsum(xc * xc) * (1.0 / n) + EPS)
    return xc * r, r


def _ln_bwd(dxh, xh, r, n):
    return r * (dxh - _rsum(dxh) * (1.0 / n) - xh * (_rsum(dxh * xh) * (1.0 / n)))


def _partner(x, lane):
    return jnp.where(lane < NOPE + ROPE // 2, pltpu.roll(x, LANES - ROPE // 2, 1), pltpu.roll(x, ROPE // 2, 1))


def _row_spec(ts, w, col=0):
    return pl.BlockSpec((ts, w), lambda i, col=col: (i, col))


def _full_spec(shape):
    nd = len(shape)
    return pl.BlockSpec(shape, lambda i, nd=nd: (0,) * nd)


PROJ_CHUNK = 640


def _proj_fwd(x, ng, win_p, name, gather=()):
    s = x.shape[0]
    ts = _tile(s)
    nt = s // ts
    n = len(gather)

    def body(x_ref, g_ref, w_ref, *rest):
        o_ref = rest[n]
        if n:
            i = pl.program_id(0)
            start, pass_on, finish = _ag_steps(rest[:n], rest[n + 1:2 * n + 1], *rest[2 * n + 1:])
            pl.when(i == 0)(start)
            pl.when(i == nt // 2)(pass_on)
        xv = x_ref[...]
        xn, _ = _rms_fwd(xv, D_MODEL)
        h = (xn * g_ref[...]).astype(BF16)
        for c in range(0, NP, PROJ_CHUNK):
            o_ref[:, c:c + PROJ_CHUNK] = jnp.dot(h, w_ref[:, c:c + PROJ_CHUNK], preferred_element_type=F32)
        if n:
            pl.when(i == nt - 1)(finish)

    out = pl.pallas_call(
        body, name=name, grid=(nt,),
        in_specs=[_row_spec(ts, D_MODEL), _full_spec((1, D_MODEL)), _full_spec((D_MODEL, NP))] + [ANY] * n,
        out_specs=[_row_spec(ts, NP)] + [ANY] * n,
        out_shape=[jax.ShapeDtypeStruct((s, NP), F32)]
        + [jax.ShapeDtypeStruct((N_DEV,) + a.shape, a.dtype) for a in gather],
        scratch_shapes=_ag_scratch(n) if n else [],
        compiler_params=_cp(("arbitrary",) if n else ("parallel",)),
    )(x, ng, win_p, *gather)
    return out[0], out[1:]


def _proj_bwd(x, ng, win_p, d_out, pieces, name):
    s = x.shape[0]
    ts = _tile(s)
    offs = [o for _, o in pieces]
    widths = [p.shape[1] for p, _ in pieces]

    def body(x_ref, g_ref, w_ref, do_ref, *rest):
        p_refs = rest[:len(pieces)]
        dx_ref, h_ref, gg_ref = rest[len(pieces):]
        i = pl.program_id(0)
        xv = x_ref[...]
        xn, r = _rms_fwd(xv, D_MODEL)
        g = g_ref[...]
        h_ref[...] = (xn * g).T.astype(BF16)
        dh = jnp.zeros((ts, D_MODEL), F32)
        for p_ref, off, w in zip(p_refs, offs, widths):
            dh = dh + _mm_nt(p_ref[...], w_ref[:, off:off + w])

        @pl.when(i == 0)
        def _():
            gg_ref[...] = jnp.zeros_like(gg_ref)

        gg_ref[...] += _csum(dh * xn)
        dx_ref[...] = _rms_bwd(dh * g, xn, r, D_MODEL) + do_ref[...]

    in_specs = [_row_spec(ts, D_MODEL), _full_spec((1, D_MODEL)), _full_spec((D_MODEL, NP)), _row_spec(ts, D_MODEL)]
    in_specs += [_row_spec(ts, w) for w in widths]
    return pl.pallas_call(
        body, name=name, grid=(s // ts,),
        in_specs=in_specs,
        out_specs=[_row_spec(ts, D_MODEL), pl.BlockSpec((D_MODEL, ts), lambda i: (0, i)), _full_spec((1, D_MODEL))],
        out_shape=[jax.ShapeDtypeStruct((s, D_MODEL), F32), jax.ShapeDtypeStruct((D_MODEL, s), BF16),
                   jax.ShapeDtypeStruct((1, D_MODEL), F32)],
        compiler_params=_cp(("arbitrary",)),
    )(x, ng, win_p, d_out, *[p for p, _ in pieces])


WG_TILE = 256


def _win_grad(ht, pieces, name):
    s = ht.shape[1]
    ts = min(WG_TILE, s)
    offs = [o for _, o in pieces]
    widths = [p.shape[1] for p, _ in pieces]

    def body(ht_ref, *rest):
        p_refs, o_ref = rest[:-1], rest[-1]

        @pl.when(pl.program_id(0) == 0)
        def _():
            o_ref[...] = jnp.zeros_like(o_ref)

        hb = ht_ref[...]
        for p_ref, off, w in zip(p_refs, offs, widths):
            o_ref[:, off:off + w] += jnp.dot(hb, p_ref[...].astype(BF16), preferred_element_type=F32)

    return pl.pallas_call(
        body, name=name, grid=(s // ts,),
        in_specs=[pl.BlockSpec((D_MODEL, ts), lambda i: (0, i))] + [_row_spec(ts, w) for w in widths],
        out_specs=_full_spec((D_MODEL, NP)),
        out_shape=jax.ShapeDtypeStruct((D_MODEL, NP), F32),
        compiler_params=_cp(("arbitrary",)),
    )(ht, *[p for p, _ in pieces])


def _halo_spec(ts):
    per = ts // HALO
    return pl.BlockSpec((HALO, 2 * CONV_W), lambda i: (jnp.maximum(i * per - 1, 0), 0))


def _conv_taps(ext_ref, cw_ref, cv_ref, cb, ts):
    base = HALO - (CONV_K - 1)
    for r0 in range(0, ts, SUB):
        acc = jnp.zeros((SUB, CONV_W), F32)
        for k in range(CONV_K):
            acc = acc + cw_ref[k:k + 1, :] * ext_ref[r0 + base + k:r0 + base + k + SUB, :]
        cv_ref[r0:r0 + SUB, :] = acc + cb


def _conv_fwd(proj, cw, cb, lg, lb, pww, pwb, name):
    s = proj.shape[0]
    ts = _tile(s)

    def body(pa_ref, ph_ref, cw_ref, cb_ref, lg_ref, lb_ref, pww_ref, pwb_ref, y_ref, cv_ref, ext_ref):
        i = pl.program_id(0)
        pa = pa_ref[...]
        a, ag, zc = pa[:, :CONV_W], pa[:, CONV_W:2 * CONV_W], pa[:, 2 * CONV_W:]
        ph = ph_ref[...]
        hglu = ph[:, :CONV_W] * _sig(ph[:, CONV_W:])
        ext_ref[0:HALO, :] = jnp.where(i > 0, hglu, 0.0)
        ext_ref[HALO:HALO + ts, :] = a * _sig(ag)
        _conv_taps(ext_ref, cw_ref, cv_ref, cb_ref[...], ts)
        xh, _ = _ln_fwd(cv_ref[...], CONV_W)
        ln = xh * lg_ref[...] + lb_ref[...]
        pw = _mm(_silu(ln), pww_ref[...]) + pwb_ref[...]
        y_ref[...] = pw * _silu(zc)

    vec = _full_spec((1, CONV_W))
    return pl.pallas_call(
        body, name=name, grid=(s // ts,),
        in_specs=[_row_spec(ts, 3 * CONV_W, 0), _halo_spec(ts), _full_spec((HALO, CONV_W)), vec, vec, vec,
                  _full_spec((CONV_W, CONV_W)), vec],
        out_specs=[_row_spec(ts, CONV_W), _row_spec(ts, CONV_W)],
        out_shape=[jax.ShapeDtypeStruct((s, CONV_W), F32), jax.ShapeDtypeStruct((s, CONV_W), F32)],
        scratch_shapes=[pltpu.VMEM((HALO + ts, CONV_W), F32)],
        compiler_params=_cp(("parallel",)),
    )(proj, proj, cw, cb, lg, lb, pww, pwb)


def _conv_bwd(proj, cv, dy, cw, lg, lb, pww, pwb, name):
    s = proj.shape[0]
    ts = _tile(s)
    nt = s // ts
    per = ts // HALO

    def body(pa_ref, ph_ref, cv_ref, dy_ref, cw_ref, lg_ref, lb_ref, pww_ref, pwb_ref,
             dp_ref, gcw_ref, gpw_ref, gv_ref, ext_ref, dext_ref, carry_ref, gacc_ref):
        i = pl.program_id(0)
        ti = nt - 1 - i

        @pl.when(i == 0)
        def _():
            carry_ref[...] = jnp.zeros_like(carry_ref)
            gacc_ref[...] = jnp.zeros_like(gacc_ref)
            gpw_ref[...] = jnp.zeros_like(gpw_ref)
            gv_ref[...] = jnp.zeros_like(gv_ref)

        pa = pa_ref[...]
        a, ag, zc = pa[:, :CONV_W], pa[:, CONV_W:2 * CONV_W], pa[:, 2 * CONV_W:]
        sag = _sig(ag)
        ph = ph_ref[...]
        hglu = ph[:, :CONV_W] * _sig(ph[:, CONV_W:])
        ext_ref[0:HALO, :] = jnp.where(ti > 0, hglu, 0.0)
        ext_ref[HALO:HALO + ts, :] = a * sag
        xh, rl = _ln_fwd(cv_ref[...], CONV_W)
        lg = lg_ref[...]
        ln = xh * lg + lb_ref[...]
        sw = _silu(ln)
        pww = pww_ref[...]
        pw = _mm(sw, pww) + pwb_ref[...]
        d_y = dy_ref[...]
        d_pw = d_y * _silu(zc)
        d_zc = d_y * pw * _dsilu(zc)
        gpw_ref[...] += _mm_tn(sw, d_pw)
        d_ln = _mm_nt(d_pw, pww) * _dsilu(ln)
        d_cv = _ln_bwd(d_ln * lg, xh, rl, CONV_W)
        gv_ref[0:1, :] += _csum(d_cv)
        gv_ref[1:2, :] += _csum(d_ln * xh)
        gv_ref[2:3, :] += _csum(d_ln)
        gv_ref[3:4, :] += _csum(d_pw)
        dext_ref[0:ts, :] = d_cv
        dext_ref[ts:ts + HALO, :] = carry_ref[...]
        carry_ref[...] = d_cv[0:HALO, :]
        base = HALO - (CONV_K - 1)
        for r0 in range(0, ts, SUB):
            dcv_r = dext_ref[r0:r0 + SUB, :]
            dg = jnp.zeros((SUB, CONV_W), F32)
            for k in range(CONV_K):
                prod = dcv_r * ext_ref[r0 + base + k:r0 + base + k + SUB, :]
                gacc_ref[8 * k:8 * k + 8, :] += jnp.sum(prod.reshape(SUB // 8, 8, CONV_W), axis=0)
                dg = dg + cw_ref[k:k + 1, :] * dext_ref[r0 + CONV_K - 1 - k:r0 + CONV_K - 1 - k + SUB, :]
            sg_r, a_r = sag[r0:r0 + SUB, :], a[r0:r0 + SUB, :]
            dp_ref[r0:r0 + SUB, 0:CONV_W] = (dg * sg_r).astype(BF16)
            dp_ref[r0:r0 + SUB, CONV_W:2 * CONV_W] = (dg * a_r * sg_r * (1.0 - sg_r)).astype(BF16)
        dp_ref[:, 2 * CONV_W:] = d_zc.astype(BF16)

        @pl.when(i == nt - 1)
        def _():
            gcw_ref[...] = jnp.zeros_like(gcw_ref)
            for k in range(CONV_K):
                gcw_ref[k:k + 1, :] = _csum(gacc_ref[8 * k:8 * k + 8, :])

    vec = _full_spec((1, CONV_W))
    rev = lambda w, col=0: pl.BlockSpec((ts, w), lambda i, col=col: (nt - 1 - i, col))
    halo = pl.BlockSpec((HALO, 2 * CONV_W), lambda i: (jnp.maximum((nt - 1 - i) * per - 1, 0), 0))
    return pl.pallas_call(
        body, name=name, grid=(nt,),
        in_specs=[rev(3 * CONV_W), halo, rev(CONV_W), rev(CONV_W), _full_spec((HALO, CONV_W)), vec, vec,
                  _full_spec((CONV_W, CONV_W)), vec],
        out_specs=[rev(3 * CONV_W), _full_spec((HALO, CONV_W)), _full_spec((CONV_W, CONV_W)), _full_spec((8, CONV_W))],
        out_shape=[jax.ShapeDtypeStruct((s, 3 * CONV_W), BF16), jax.ShapeDtypeStruct((HALO, CONV_W), F32),
                   jax.ShapeDtypeStruct((CONV_W, CONV_W), F32), jax.ShapeDtypeStruct((8, CONV_W), F32)],
        scratch_shapes=[pltpu.VMEM((HALO + ts, CONV_W), F32), pltpu.VMEM((ts + HALO, CONV_W), F32),
                        pltpu.VMEM((HALO, CONV_W), F32), pltpu.VMEM((8 * HALO, CONV_W), F32)],
        compiler_params=_cp(("arbitrary",)),
    )(proj, proj, cv, dy, cw, lg, lb, pww, pwb)


def _sg_mix(wm_ref, vc, head):
    out = jnp.zeros((SG_CHUNK, SG_W), F32)
    vb = vc.astype(BF16)
    for g in range(SG_HEADS):
        out = jnp.where(head == g, jnp.dot(wm_ref[g], vb, preferred_element_type=F32), out)
    return out


def _sgu_fwd(proj, lg, lb, wm, sbx, name):
    s = proj.shape[0]
    ts = _tile(s)

    def body(ps_ref, lg_ref, lb_ref, wm_ref, sbx_ref, y_ref, mix_ref):
        ps = ps_ref[...]
        us, vs, zs = ps[:, :SG_W], ps[:, SG_W:2 * SG_W], ps[:, 2 * SG_W:]
        xh, _ = _ln_fwd(_gelu(vs), SG_W)
        vn = xh * lg_ref[...] + lb_ref[...]
        head = lax.broadcasted_iota(jnp.int32, (SG_CHUNK, SG_W), 1) // (SG_W // SG_HEADS)
        for c0 in range(0, ts, SG_CHUNK):
            mix_ref[c0:c0 + SG_CHUNK, :] = _sg_mix(wm_ref, vn[c0:c0 + SG_CHUNK, :], head) + sbx_ref[...]
        y_ref[...] = _gelu(us) * mix_ref[...] * _silu(zs)

    vec = _full_spec((1, SG_W))
    return pl.pallas_call(
        body, name=name, grid=(s // ts,),
        in_specs=[_row_spec(ts, 3 * SG_W, C_SG // (3 * SG_W)), vec, vec,
                  _full_spec((SG_HEADS, SG_CHUNK, SG_CHUNK)), _full_spec((SG_CHUNK, SG_W))],
        out_specs=_row_spec(ts, SG_W),
        out_shape=jax.ShapeDtypeStruct((s, SG_W), F32),
        scratch_shapes=[pltpu.VMEM((ts, SG_W), F32)],
        compiler_params=_cp(("parallel",)),
    )(proj, lg, lb, wm, sbx)


def _sgu_bwd(proj, dy, lg, lb, wm, wmt, sbx, name):
    s = proj.shape[0]
    ts = _tile(s)

    def body(ps_ref, dy_ref, lg_ref, lb_ref, wm_ref, wmt_ref, sbx_ref,
             dp_ref, gwm_ref, dms_ref, gv_ref, mix_ref, dvn_ref):
        i = pl.program_id(0)

        @pl.when(i == 0)
        def _():
            gwm_ref[...] = jnp.zeros_like(gwm_ref)
            dms_ref[...] = jnp.zeros_like(dms_ref)
            gv_ref[...] = jnp.zeros_like(gv_ref)

        ps = ps_ref[...]
        us, vs, zs = ps[:, :SG_W], ps[:, SG_W:2 * SG_W], ps[:, 2 * SG_W:]
        xh, rl = _ln_fwd(_gelu(vs), SG_W)
        lg = lg_ref[...]
        vn = xh * lg + lb_ref[...]
        head = lax.broadcasted_iota(jnp.int32, (SG_CHUNK, SG_W), 1) // (SG_W // SG_HEADS)
        for c0 in range(0, ts, SG_CHUNK):
            mix_ref[c0:c0 + SG_CHUNK, :] = _sg_mix(wm_ref, vn[c0:c0 + SG_CHUNK, :], head) + sbx_ref[...]
        mixed = mix_ref[...]
        u = _gelu(us)
        sz = _silu(zs)
        d_y = dy_ref[...]
        d_mixed = d_y * u * sz
        dp_ref[:, 0:SG_W] = (d_y * mixed * sz * _dgelu(us)).astype(BF16)
        dp_ref[:, 2 * SG_W:] = (d_y * u * mixed * _dsilu(zs)).astype(BF16)
        dms = jnp.zeros((SG_CHUNK, SG_W), F32)
        for c0 in range(0, ts, SG_CHUNK):
            dm = d_mixed[c0:c0 + SG_CHUNK, :]
            vc = vn[c0:c0 + SG_CHUNK, :]
            dms = dms + dm
            for g in range(SG_HEADS):
                gwm_ref[g] += _mm_nt(jnp.where(head == g, dm, 0.0), vc)
            dvn_ref[c0:c0 + SG_CHUNK, :] = _sg_mix(wmt_ref, dm, head)
        dms_ref[...] += dms
        d_vn = dvn_ref[...]
        gv_ref[0:1, :] += _csum(d_vn * xh)
        gv_ref[1:2, :] += _csum(d_vn)
        dp_ref[:, SG_W:2 * SG_W] = (_ln_bwd(d_vn * lg, xh, rl, SG_W) * _dgelu(vs)).astype(BF16)

    vec = _full_spec((1, SG_W))
    wspec = _full_spec((SG_HEADS, SG_CHUNK, SG_CHUNK))
    return pl.pallas_call(
        body, name=name, grid=(s // ts,),
        in_specs=[_row_spec(ts, 3 * SG_W, C_SG // (3 * SG_W)), _row_spec(ts, SG_W), vec, vec, wspec, wspec,
                  _full_spec((SG_CHUNK, SG_W))],
        out_specs=[_row_spec(ts, 3 * SG_W), wspec, _full_spec((SG_CHUNK, SG_W)), _full_spec((8, SG_W))],
        out_shape=[jax.ShapeDtypeStruct((s, 3 * SG_W), BF16), jax.ShapeDtypeStruct((SG_HEADS, SG_CHUNK, SG_CHUNK), F32),
                   jax.ShapeDtypeStruct((SG_CHUNK, SG_W), F32), jax.ShapeDtypeStruct((8, SG_W), F32)],
        scratch_shapes=[pltpu.VMEM((ts, SG_W), F32), pltpu.VMEM((ts, SG_W), F32)],
        compiler_params=_cp(("arbitrary",)),
    )(proj, dy, lg, lb, wm, wmt, sbx)


QW = HEADS * HEAD_PAD
KVW = QW + MLA_W
ATT_SCALE = QK ** -0.5


def _mla_specs(ts):
    return [_row_spec(ts, Q_LORA, C_CQ // Q_LORA), _row_spec(ts, KV_LORA, C_CKV // KV_LORA),
            _row_spec(ts, LANES, C_KR // LANES), _row_spec(ts, LANES), _row_spec(ts, LANES),
            _full_spec((1, Q_LORA)), _full_spec((Q_LORA, QW)), _full_spec((1, KV_LORA)), _full_spec((KV_LORA, KVW)),
            _full_spec((1, LANES)), _full_spec((1, LANES))]


def _mla_fwd(proj, rc, rs, qg, wuq, kvg, wukv, gq, gk, name):
    s = proj.shape[0]
    ts = _tile(s)

    def body(cq_ref, ckv_ref, kr_ref, rc_ref, rs_ref, qg_ref, wuq_ref, kvg_ref, wukv_ref, gq_ref, gk_ref,
             q_ref, k_ref, v_ref):
        lane = lax.broadcasted_iota(jnp.int32, (ts, LANES), 1)
        c, sn = rc_ref[...], rs_ref[...]
        cqn, _ = _rms_fwd(cq_ref[...], Q_LORA)
        q0 = _mm(cqn * qg_ref[...], wuq_ref[...])
        gq = gq_ref[...]
        for h in range(HEADS):
            xn, _ = _rms_fwd(q0[:, h * LANES:(h + 1) * LANES], QK)
            qn = xn * gq
            q_ref[:, h * LANES:(h + 1) * LANES] = ((qn * c + _partner(qn, lane) * sn) * ATT_SCALE).astype(BF16)
        ckvn, _ = _rms_fwd(ckv_ref[...], KV_LORA)
        kv = _mm(ckvn * kvg_ref[...], wukv_ref[...])
        kr = pltpu.roll(kr_ref[...], NOPE, 1)
        gk = gk_ref[...]
        for h in range(HEADS):
            xn, _ = _rms_fwd(kv[:, h * LANES:(h + 1) * LANES] + kr, QK)
            kn = xn * gk
            k_ref[:, h * LANES:(h + 1) * LANES] = (kn * c + _partner(kn, lane) * sn).astype(BF16)
        v_ref[...] = kv[:, QW:].astype(BF16)

    return pl.pallas_call(
        body, name=name, grid=(s // ts,),
        in_specs=_mla_specs(ts),
        out_specs=[_row_spec(ts, QW), _row_spec(ts, QW), _row_spec(ts, MLA_W)],
        out_shape=[jax.ShapeDtypeStruct((s, QW), BF16), jax.ShapeDtypeStruct((s, QW), BF16),
                   jax.ShapeDtypeStruct((s, MLA_W), BF16)],
        compiler_params=_cp(("parallel",)),
    )(proj, proj, proj, rc, rs, qg, wuq, kvg, wukv, gq, gk)


def _mla_bwd(proj, rc, rs, qg, wuq, kvg, wukv, gq, gk, dq, dk, dv, name):
    s = proj.shape[0]
    ts = _tile(s)

    def body(cq_ref, ckv_ref, kr_ref, rc_ref, rs_ref, qg_ref, wuq_ref, kvg_ref, wukv_ref, gq_ref, gk_ref,
             dq_ref, dk_ref, dv_ref, dcq_ref, dckv_ref, dkr_ref, gwuq_ref, gwukv_ref, gv_ref, d0_ref):
        i = pl.program_id(0)

        @pl.when(i == 0)
        def _():
            gwuq_ref[...] = jnp.zeros_like(gwuq_ref)
            gwukv_ref[...] = jnp.zeros_like(gwukv_ref)
            gv_ref[...] = jnp.zeros_like(gv_ref)

        lane = lax.broadcasted_iota(jnp.int32, (ts, LANES), 1)
        c, sn = rc_ref[...], rs_ref[...]
        cq = cq_ref[...]
        cqx, rq0 = _rms_fwd(cq, Q_LORA)
        qg = qg_ref[...]
        cqn = cqx * qg
        wuq = wuq_ref[...]
        q0 = _mm(cqn, wuq)
        gq = gq_ref[...]
        ggq = jnp.zeros((1, LANES), F32)
        for h in range(HEADS):
            xn, r = _rms_fwd(q0[:, h * LANES:(h + 1) * LANES], QK)
            d = dq_ref[:, h * LANES:(h + 1) * LANES] * ATT_SCALE
            d_qn = d * c - _partner(d, lane) * sn
            ggq = ggq + _csum(d_qn * xn)
            d0_ref[:, h * LANES:(h + 1) * LANES] = _rms_bwd(d_qn * gq, xn, r, QK)
        dq0 = d0_ref[:, 0:QW]
        gwuq_ref[...] += _mm_tn(cqn, dq0)
        d_cqn = _mm_nt(dq0, wuq)
        gv_ref[0:1, 0:Q_LORA] += _csum(d_cqn * cqx)
        gv_ref[2:3, 0:LANES] += ggq
        dcq_ref[...] = _rms_bwd(d_cqn * qg, cqx, rq0, Q_LORA).astype(BF16)
        ckv = ckv_ref[...]
        ckx, rk0 = _rms_fwd(ckv, KV_LORA)
        kvg = kvg_ref[...]
        ckvn = ckx * kvg
        wukv = wukv_ref[...]
        kv = _mm(ckvn, wukv)
        kr = pltpu.roll(kr_ref[...], NOPE, 1)
        gk = gk_ref[...]
        ggk = jnp.zeros((1, LANES), F32)
        dkr = jnp.zeros((ts, LANES), F32)
        for h in range(HEADS):
            xn, r = _rms_fwd(kv[:, h * LANES:(h + 1) * LANES] + kr, QK)
            d = dk_ref[:, h * LANES:(h + 1) * LANES]
            d_kn = d * c - _partner(d, lane) * sn
            ggk = ggk + _csum(d_kn * xn)
            d_k0 = _rms_bwd(d_kn * gk, xn, r, QK)
            dkr = dkr + d_k0
            d0_ref[:, h * LANES:(h + 1) * LANES] = d_k0
        d0_ref[:, QW:KVW] = dv_ref[...]
        dkv = d0_ref[...]
        dkr_ref[...] = jnp.where(lane < ROPE, pltpu.roll(dkr, NOPE, 1), 0.0).astype(BF16)
        gwukv_ref[...] += _mm_tn(ckvn, dkv)
        d_ckvn = _mm_nt(dkv, wukv)
        gv_ref[1:2, 0:KV_LORA] += _csum(d_ckvn * ckx)
        gv_ref[3:4, 0:LANES] += ggk
        dckv_ref[...] = _rms_bwd(d_ckvn * kvg, ckx, rk0, KV_LORA).astype(BF16)

    return pl.pallas_call(
        body, name=name, grid=(s // ts,),
        in_specs=_mla_specs(ts) + [_row_spec(ts, QW), _row_spec(ts, QW), _row_spec(ts, MLA_W)],
        out_specs=[_row_spec(ts, Q_LORA), _row_spec(ts, KV_LORA), _row_spec(ts, LANES),
                   _full_spec((Q_LORA, QW)), _full_spec((KV_LORA, KVW)), _full_spec((8, QW))],
        out_shape=[jax.ShapeDtypeStruct((s, Q_LORA), BF16), jax.ShapeDtypeStruct((s, KV_LORA), BF16),
                   jax.ShapeDtypeStruct((s, LANES), BF16), jax.ShapeDtypeStruct((Q_LORA, QW), F32),
                   jax.ShapeDtypeStruct((KV_LORA, KVW), F32), jax.ShapeDtypeStruct((8, QW), F32)],
        scratch_shapes=[pltpu.VMEM((ts, KVW), F32)],
        compiler_params=_cp(("arbitrary",)),
    )(proj, proj, proj, rc, rs, qg, wuq, kvg, wukv, gq, gk, dq, dk, dv)


PAIRS = HEADS // 2
ATT_STRIP = 32


def _attn_fwd(q, k, v, name):
    s = q.shape[0]
    tq = _tile(s)
    tk = tq

    def body(q_ref, k_ref, v_ref, o_ref, lse_ref, s0_ref, s1_ref, p0_ref, p1_ref, m_ref, l_ref, acc_ref):
        i = pl.program_id(1)
        s_refs, p_refs = (s0_ref, s1_ref), (p0_ref, p1_ref)
        row = lax.broadcasted_iota(jnp.int32, (ATT_STRIP, tk), 0)
        col = lax.broadcasted_iota(jnp.int32, (ATT_STRIP, tk), 1)
        first = lax.broadcasted_iota(jnp.int32, (tq, LANES), 1) < V_DIM
        m_ref[...] = jnp.full(m_ref.shape, NEG, F32)
        l_ref[...] = jnp.zeros(l_ref.shape, F32)
        acc_ref[...] = jnp.zeros(acc_ref.shape, F32)

        def blk(j, masked):
            st = pl.multiple_of(j * tk, tk)
            for a in range(2):
                s_refs[a][...] = _mm_nt(q_ref[:, a * LANES:(a + 1) * LANES],
                                        k_ref[pl.ds(st, tk), a * LANES:(a + 1) * LANES])
            for a in range(2):
                for r in range(0, tq, ATT_STRIP):
                    sc = s_refs[a][r:r + ATT_STRIP, :]
                    if masked:
                        sc = jnp.where(col <= row + r, sc, NEG)
                    m_old = m_ref[a, r:r + ATT_STRIP, :]
                    m_new = jnp.maximum(m_old, jnp.max(sc, axis=-1, keepdims=True))
                    alpha = jnp.exp(m_old - m_new)
                    p = jnp.exp(sc - jnp.tile(m_new, (1, tk // LANES)))
                    l_ref[a, r:r + ATT_STRIP, :] = alpha * l_ref[a, r:r + ATT_STRIP, :] + _rsum(p)
                    acc_ref[a, r:r + ATT_STRIP, :] = alpha * acc_ref[a, r:r + ATT_STRIP, :]
                    m_ref[a, r:r + ATT_STRIP, :] = m_new
                    p_refs[a][r:r + ATT_STRIP, :] = p.astype(BF16)
                acc_ref[a] += jnp.dot(p_refs[a][...], v_ref[pl.ds(st, tk), :], preferred_element_type=F32)

        def two_blocks(t, carry):
            blk(2 * t, False)
            blk(2 * t + 1, False)
            return carry

        lax.fori_loop(0, i // 2, two_blocks, 0)

        @pl.when(i % 2 == 1)
        def _():
            blk(i - 1, False)

        blk(i, True)
        o_ref[...] = jnp.where(first, acc_ref[0] / l_ref[0], acc_ref[1] / l_ref[1])
        lse_ref[...] = jnp.where(first, m_ref[0] + jnp.log(l_ref[0]), m_ref[1] + jnp.log(l_ref[1]))

    stat = pltpu.VMEM((2, tq, LANES), F32)
    return pl.pallas_call(
        body, name=name, grid=(PAIRS, s // tq),
        in_specs=[pl.BlockSpec((tq, 2 * LANES), lambda p, i: (i, p)),
                  pl.BlockSpec((s, 2 * LANES), lambda p, i: (0, p)),
                  pl.BlockSpec((s, LANES), lambda p, i: (0, p))],
        out_specs=[pl.BlockSpec((tq, LANES), lambda p, i: (i, p)), pl.BlockSpec((tq, LANES), lambda p, i: (i, p))],
        out_shape=[jax.ShapeDtypeStruct((s, MLA_W), F32), jax.ShapeDtypeStruct((s, MLA_W), F32)],
        scratch_shapes=[pltpu.VMEM((tq, tk), F32), pltpu.VMEM((tq, tk), F32), pltpu.VMEM((tq, tk), BF16),
                        pltpu.VMEM((tq, tk), BF16), stat, stat, stat],
        compiler_params=_cp(("parallel", "parallel")),
    )(q, k, v)


def _attn_bwd(q, k, v, do, stats, name):
    s = q.shape[0]
    tq = _tile(s)
    tk = tq
    nq = s // tq

    def body(q_ref, k_ref, v_ref, do_ref, st_ref, dq_ref, dk_ref, dv_ref):
        j = pl.program_id(1)

        @pl.when(j == 0)
        def _():
            dq_ref[...] = jnp.zeros_like(dq_ref)

        dk_ref[...] = jnp.zeros_like(dk_ref)
        dv_ref[...] = jnp.zeros_like(dv_ref)
        row = lax.broadcasted_iota(jnp.int32, (tq, tk), 0)
        col = lax.broadcasted_iota(jnp.int32, (tq, tk), 1)
        lane = lax.broadcasted_iota(jnp.int32, (tq, LANES), 1)

        def blk(i, masked):
            st = i * tq if isinstance(i, int) else pl.multiple_of(i * tq, tq)
            do2 = do_ref[pl.ds(st, tq), :]
            stt = st_ref[pl.ds(st, tq), :]
            dv = None
            for a in range(2):
                mine = (lane < V_DIM) if a == 0 else (lane >= V_DIM)
                qa = q_ref[pl.ds(st, tq), a * LANES:(a + 1) * LANES]
                doa = jnp.where(mine, do2, jnp.zeros((), BF16))
                lse = stt[:, a * V_DIM:a * V_DIM + 1]
                dl = stt[:, a * V_DIM + V_DIM // 2:a * V_DIM + V_DIM // 2 + 1]
                p = jnp.exp(_mm_nt(qa, k_ref[:, a * LANES:(a + 1) * LANES]) - lse)
                if masked:
                    p = jnp.where(col <= row, p, 0.0)
                ds = (p * (_mm_nt(doa, v_ref[...]) - dl)).astype(BF16)
                dva = _mm_tn(p, doa)
                dv = dva if dv is None else dv + dva
                dk_ref[:, a * LANES:(a + 1) * LANES] += _mm_tn(ds, qa)
                dq_ref[pl.ds(st, tq), a * LANES:(a + 1) * LANES] += jnp.dot(
                    ds, k_ref[:, a * LANES:(a + 1) * LANES], preferred_element_type=F32)
            dv_ref[...] += dv

        blk(j, True)
        rest = nq - 1 - j

        def two_blocks(t, carry):
            blk(j + 1 + 2 * t, False)
            blk(j + 2 + 2 * t, False)
            return carry

        lax.fori_loop(0, rest // 2, two_blocks, 0)

        @pl.when(rest % 2 == 1)
        def _():
            blk(nq - 1, False)

    return pl.pallas_call(
        body, name=name, grid=(PAIRS, s // tk),
        in_specs=[pl.BlockSpec((s, 2 * LANES), lambda p, j: (0, p)),
                  pl.BlockSpec((tk, 2 * LANES), lambda p, j: (j, p)),
                  pl.BlockSpec((tk, LANES), lambda p, j: (j, p)),
                  pl.BlockSpec((s, LANES), lambda p, j: (0, p)),
                  pl.BlockSpec((s, LANES), lambda p, j: (0, p))],
        out_specs=[pl.BlockSpec((s, 2 * LANES), lambda p, j: (0, p)),
                   pl.BlockSpec((tk, 2 * LANES), lambda p, j: (j, p)),
                   pl.BlockSpec((tk, LANES), lambda p, j: (j, p))],
        out_shape=[jax.ShapeDtypeStruct((s, QW), F32), jax.ShapeDtypeStruct((s, QW), F32),
                   jax.ShapeDtypeStruct((s, MLA_W), F32)],
        compiler_params=_cp(("parallel", "arbitrary")),
    )(q, k, v, do, stats)


BR = ((0, CONV_W), (CONV_W, CONV_W + MLA_W), (CONV_W + MLA_W, D_MODEL))


def _post_fwd(x, yc, o, proj, ys, bng, wout, name):
    s = x.shape[0]
    ts = _tile(s)

    def body(x_ref, yc_ref, o_ref, zm_ref, ys_ref, g_ref, w_ref, out_ref):
        ys3 = (yc_ref[...], o_ref[...] * _silu(zm_ref[...]), ys_ref[...])
        acc = x_ref[...]
        for (lo, hi), yb in zip(BR, ys3):
            yn, _ = _rms_fwd(yb, hi - lo)
            acc = acc + _mm(yn * g_ref[:, lo:hi], w_ref[lo:hi, :])
        out_ref[...] = acc

    return pl.pallas_call(
        body, name=name, grid=(s // ts,),
        in_specs=[_row_spec(ts, D_MODEL), _row_spec(ts, CONV_W), _row_spec(ts, MLA_W),
                  _row_spec(ts, MLA_W, C_ZM // MLA_W), _row_spec(ts, SG_W), _full_spec((1, D_MODEL)),
                  _full_spec((D_MODEL, D_MODEL))],
        out_specs=_row_spec(ts, D_MODEL),
        out_shape=jax.ShapeDtypeStruct((s, D_MODEL), F32),
        compiler_params=_cp(("parallel",)),
    )(x, yc, o, proj, ys, bng, wout)


def _post_bwd(d_out, yc, o, lse, proj, ys, bng, wout, name):
    s = d_out.shape[0]
    ts = _tile(s)

    def body(do_ref, yc_ref, o_ref, lse_ref, zm_ref, ys_ref, g_ref, w_ref,
             dyc_ref, dys_ref, dob_ref, dzm_ref, st_ref, gw_ref, gg_ref, yn_ref):
        i = pl.program_id(0)

        @pl.when(i == 0)
        def _():
            gw_ref[...] = jnp.zeros_like(gw_ref)
            gg_ref[...] = jnp.zeros_like(gg_ref)

        d_out_b = do_ref[...].astype(BF16)
        o = o_ref[...]
        zm = zm_ref[...]
        szm = _silu(zm)
        ys3 = (yc_ref[...], o * szm, ys_ref[...])
        d_ys = []
        for (lo, hi), yb in zip(BR, ys3):
            n = hi - lo
            yn, r = _rms_fwd(yb, n)
            g = g_ref[:, lo:hi]
            yn_ref[:, lo:hi] = (yn * g).astype(BF16)
            d_yn = _mm_nt(d_out_b, w_ref[lo:hi, :])
            gg_ref[:, lo:hi] += _csum(d_yn * yn)
            d_ys.append(_rms_bwd(d_yn * g, yn, r, n))
        gw_ref[...] += _mm_tn(yn_ref[...], d_out_b)
        dyc_ref[...] = d_ys[0]
        dys_ref[...] = d_ys[2]
        d_ym = d_ys[1]
        d_o = d_ym * szm
        dob_ref[...] = d_o.astype(BF16)
        dzm_ref[...] = (d_ym * o * _dsilu(zm)).astype(BF16)
        prod = d_o * o
        head = lax.broadcasted_iota(jnp.int32, (ts, MLA_W), 1) // V_DIM
        delta = jnp.zeros((ts, MLA_W), F32)
        for h in range(HEADS):
            delta = jnp.where(head == h, _rsum(jnp.where(head == h, prod, 0.0)), delta)
        lane = lax.broadcasted_iota(jnp.int32, (ts, MLA_W), 1)
        st_ref[...] = jnp.where(lane % V_DIM < V_DIM // 2, lse_ref[...], delta)

    return pl.pallas_call(
        body, name=name, grid=(s // ts,),
        in_specs=[_row_spec(ts, D_MODEL), _row_spec(ts, CONV_W), _row_spec(ts, MLA_W), _row_spec(ts, MLA_W),
                  _row_spec(ts, MLA_W, C_ZM // MLA_W), _row_spec(ts, SG_W), _full_spec((1, D_MODEL)),
                  _full_spec((D_MODEL, D_MODEL))],
        out_specs=[_row_spec(ts, CONV_W), _row_spec(ts, SG_W), _row_spec(ts, MLA_W), _row_spec(ts, MLA_W),
                   _row_spec(ts, MLA_W), _full_spec((D_MODEL, D_MODEL)), _full_spec((1, D_MODEL))],
        out_shape=[jax.ShapeDtypeStruct((s, CONV_W), F32), jax.ShapeDtypeStruct((s, SG_W), F32),
                   jax.ShapeDtypeStruct((s, MLA_W), BF16), jax.ShapeDtypeStruct((s, MLA_W), BF16),
                   jax.ShapeDtypeStruct((s, MLA_W), F32), jax.ShapeDtypeStruct((D_MODEL, D_MODEL), F32),
                   jax.ShapeDtypeStruct((1, D_MODEL), F32)],
        scratch_shapes=[pltpu.VMEM((ts, D_MODEL), BF16)],
        compiler_params=_cp(("arbitrary",)),
    )(d_out, yc, o, lse, proj, ys, bng, wout)


def _loss_head(y, target, name):
    s = y.shape[0]
    ts = _tile(s)
    nt = s // ts

    def body(y_ref, t_ref, dy_ref, l_ref, acc_ref):
        i = pl.program_id(0)

        @pl.when(i == 0)
        def _():
            acc_ref[...] = jnp.zeros_like(acc_ref)

        e = y_ref[...] - t_ref[...]
        dy_ref[...] = e * (1.0 / D_MODEL)
        sq = jnp.sum((e * e).reshape(ts // 8, 8, D_MODEL), axis=0)
        part = sq[:, 0:LANES]
        for c in range(LANES, D_MODEL, LANES):
            part = part + sq[:, c:c + LANES]
        acc_ref[...] += part

        @pl.when(i == nt - 1)
        def _():
            tot = jnp.sum(_rsum(acc_ref[...]), axis=0, keepdims=True) * (0.5 / D_MODEL)
            l_ref[...] = jnp.broadcast_to(tot, (8, LANES))

    return pl.pallas_call(
        body, name=name, grid=(nt,),
        in_specs=[_row_spec(ts, D_MODEL), _row_spec(ts, D_MODEL)],
        out_specs=[_row_spec(ts, D_MODEL), _full_spec((8, LANES))],
        out_shape=[jax.ShapeDtypeStruct((s, D_MODEL), F32), jax.ShapeDtypeStruct((8, LANES), F32)],
        scratch_shapes=[pltpu.VMEM((8, LANES), F32)],
        compiler_params=_cp(("arbitrary",)),
    )(y, target)


MESH = pl.DeviceIdType.MESH
ANY = pl.BlockSpec(memory_space=pl.ANY)


AG_COPIES = N_DEV - 1


def _ag_steps(x_refs, out_refs, send_sems, recv_sems, local_sems):
    n = len(x_refs)
    x, y, c = lax.axis_index("x"), lax.axis_index("y"), lax.axis_index("c")
    me, sibling = (x, y, c), (x, y, 1 - c)
    chips = [(1 - x, y), (x, 1 - y), (1 - x, 1 - y)]

    def slot(t, px, py, pc):
        return out_refs[t].at[4 * px + 2 * py + pc]

    def copy(t, k, block, to, src=None):
        return pltpu.make_async_remote_copy(
            src_ref=slot(t, *block) if src is None else src, dst_ref=slot(t, *block),
            send_sem=send_sems.at[t * AG_COPIES + k], recv_sem=recv_sems.at[t * AG_COPIES + k], device_id=to,
            device_id_type=MESH)

    mine = [pltpu.make_async_copy(x_refs[t], slot(t, *me), local_sems.at[t]) for t in range(n)]
    first = [copy(t, 0, me, sibling, src=x_refs[t]) for t in range(n)]
    first += [copy(t, 1 + j, me, (*chip, c), src=x_refs[t]) for j, chip in enumerate(chips) for t in range(n)]
    passed = [copy(t, 4 + j, (*chip, c), sibling) for j, chip in enumerate(chips) for t in range(n)]

    def start():
        for cp in mine + first:
            cp.start()

    def pass_on():
        for j, chip in enumerate(chips):
            for t in range(n):
                copy(t, 1 + j, (*chip, c), me).wait_recv()
                passed[j * n + t].start()

    def finish():
        for t in range(n):
            copy(t, 0, sibling, me).wait_recv()
        for j, chip in enumerate(chips):
            for t in range(n):
                copy(t, 4 + j, (*chip, 1 - c), me).wait_recv()
        for cp in first + passed:
            cp.wait_send()
        for cp in mine:
            cp.wait()

    return start, pass_on, finish


def _ag_scratch(n):
    return [pltpu.SemaphoreType.DMA((AG_COPIES * n,)), pltpu.SemaphoreType.DMA((AG_COPIES * n,)),
            pltpu.SemaphoreType.DMA((n,))]


def _all_gather(xs, name):
    n = len(xs)

    def body(*refs):
        for step in _ag_steps(refs[:n], refs[n:2 * n], *refs[2 * n:]):
            step()

    return pl.pallas_call(
        body, name=name,
        out_shape=[jax.ShapeDtypeStruct((N_DEV,) + a.shape, a.dtype) for a in xs],
        in_specs=[ANY] * n, out_specs=[ANY] * n,
        scratch_shapes=_ag_scratch(n),
    )(*xs)


N_CHIP = N_DEV // 2


def _grad_to_sibling(gss, gr, name):
    n = len(gss)

    def body(*refs):
        gs_refs, gr_ref = refs[:n], refs[n]
        os_refs, or_ref = refs[n + 1:2 * n + 1], refs[2 * n + 1]
        send_sems, recv_sems = refs[2 * n + 2:]
        x, y, c = lax.axis_index("x"), lax.axis_index("y"), lax.axis_index("c")
        copies = []
        for t in range(n + 1):
            copies.append(pltpu.make_async_remote_copy(
                src_ref=gs_refs[t].at[1 - c] if t < n else gr_ref, dst_ref=os_refs[t] if t < n else or_ref,
                send_sem=send_sems.at[t], recv_sem=recv_sems.at[t], device_id=(x, y, 1 - c), device_id_type=MESH))
        for cp in copies:
            cp.start()
        for cp in copies:
            cp.wait_recv()
        for cp in copies:
            cp.wait_send()

    return pl.pallas_call(
        body, name=name,
        out_shape=[jax.ShapeDtypeStruct(g.shape[1:], g.dtype) for g in gss] + [jax.ShapeDtypeStruct(gr.shape, gr.dtype)],
        in_specs=[ANY] * (n + 1), out_specs=[ANY] * (n + 1),
        scratch_shapes=[pltpu.SemaphoreType.DMA((n + 1,)), pltpu.SemaphoreType.DMA((n + 1,))],
    )(*gss, gr)


def _chip_sum(mine, theirs, gr, gr_theirs, name):
    n = len(mine)

    def body(*refs):
        a_refs, b_refs = refs[:n + 1], refs[n + 1:2 * n + 2]
        o_refs = refs[2 * n + 2:]
        for a_ref, b_ref, o_ref in zip(a_refs, b_refs, o_refs):
            o_ref[...] = (a_ref[...].astype(F32) + b_ref[...].astype(F32)).astype(o_ref.dtype)

    def spec(a):
        if a.ndim == 3:
            return pl.BlockSpec((1,) + a.shape[1:], lambda i: (i, 0, 0))
        return pl.BlockSpec(a.shape, lambda i: (0, 0))

    ins = list(mine) + [gr] + list(theirs) + [gr_theirs]
    return pl.pallas_call(
        body, name=name, grid=(N_CHIP,),
        in_specs=[spec(a) for a in ins], out_specs=[spec(a) for a in ins[:n + 1]],
        out_shape=[jax.ShapeDtypeStruct(a.shape, a.dtype) for a in ins[:n + 1]],
        compiler_params=_cp(("arbitrary",)),
    )(*ins)


def _grad_to_chips(gss, gr, name):
    n = len(gss)
    per = N_CHIP - 1

    def body(*refs):
        gs_refs, gr_ref = refs[:n], refs[n]
        os_refs, or_ref = refs[n + 1:2 * n + 1], refs[2 * n + 1]
        send_sems, recv_sems, local_sems = refs[2 * n + 2:]
        x, y, c = lax.axis_index("x"), lax.axis_index("y"), lax.axis_index("c")
        me = 2 * x + y
        local = [pltpu.make_async_copy(gs_refs[t].at[me], os_refs[t].at[me], local_sems.at[t]) for t in range(n)]
        local.append(pltpu.make_async_copy(gr_ref, or_ref.at[me], local_sems.at[n]))
        for cp in local:
            cp.start()
        sends, recvs = [], []
        for k in range(1, N_CHIP):
            px = 1 - x if k & 2 else x
            py = 1 - y if k & 1 else y
            peer = 2 * px + py
            for t in range(n + 1):
                sems = dict(send_sem=send_sems.at[t * per + k - 1], recv_sem=recv_sems.at[t * per + k - 1],
                            device_id=(px, py, c), device_id_type=MESH)
                src = gs_refs[t].at[peer] if t < n else gr_ref
                out = os_refs[t] if t < n else or_ref
                sends.append(pltpu.make_async_remote_copy(src_ref=src, dst_ref=out.at[me], **sems))
                recvs.append(pltpu.make_async_remote_copy(src_ref=src, dst_ref=out.at[peer], **sems))
        for cp in sends:
            cp.start()
        for cp in recvs:
            cp.wait_recv()
        for cp in sends:
            cp.wait_send()
        for cp in local:
            cp.wait()

    nsem = per * (n + 1)
    return pl.pallas_call(
        body, name=name,
        out_shape=[jax.ShapeDtypeStruct(g.shape, g.dtype) for g in gss]
        + [jax.ShapeDtypeStruct((N_CHIP,) + gr.shape, gr.dtype)],
        in_specs=[ANY] * (n + 1), out_specs=[ANY] * (n + 1),
        scratch_shapes=[pltpu.SemaphoreType.DMA((nsem,)), pltpu.SemaphoreType.DMA((nsem,)),
                        pltpu.SemaphoreType.DMA((n + 1,))],
    )(*gss, gr)


ADAM_ROWS = 128


def _adamw(parts, w, m, v, name):
    r, cols = w.shape
    tr = ADAM_ROWS if r % ADAM_ROWS == 0 else r
    n_parts = parts.shape[0]

    def body(p_ref, w_ref, m_ref, v_ref, g_ref, d_ref, nm_ref, nv_ref):
        g = p_ref[0].astype(F32)
        for sidx in range(1, n_parts):
            g = g + p_ref[sidx].astype(F32)
        mm = ADAM_B1 * m_ref[...] + (1.0 - ADAM_B1) * g
        vv = ADAM_B2 * v_ref[...] + (1.0 - ADAM_B2) * (g * g)
        m_hat = mm / (1.0 - ADAM_B1 ** ADAM_STEP)
        v_hat = vv / (1.0 - ADAM_B2 ** ADAM_STEP)
        g_ref[...] = g
        d_ref[...] = -ADAM_LR * (m_hat / (jnp.sqrt(v_hat) + ADAM_EPS) + ADAM_WD * w_ref[...])
        nm_ref[...] = mm
        nv_ref[...] = vv

    row = pl.BlockSpec((tr, cols), lambda i: (i, 0))
    return pl.pallas_call(
        body, name=name, grid=(r // tr,),
        in_specs=[pl.BlockSpec((n_parts, tr, cols), lambda i: (0, i, 0)), row, row, row],
        out_specs=[row, row, row, row],
        out_shape=[jax.ShapeDtypeStruct((r, cols), F32)] * 4,
        compiler_params=_cp(("parallel",)),
    )(parts, w, m, v)


PACK_W = 8 * LANES
BF16_ROWS = 16


def _pack(flat_parts, rows):
    flat = jnp.concatenate([p.reshape(-1) for p in flat_parts])
    return jnp.pad(flat, (0, rows * PACK_W - flat.shape[0])).reshape(rows, PACK_W)


def _rows_for(n, mult):
    rows = -(-n // PACK_W)
    return -(-rows // mult) * mult


def _unshard(arr8, axis):
    full = jnp.moveaxis(arr8, 0, axis)
    shp = list(full.shape)
    shp[axis:axis + 2] = [shp[axis] * shp[axis + 1]]
    return full.reshape(shp)


def _split_c_chip(full, axis):
    shp = list(full.shape)
    shp[axis:axis + 1] = [N_CHIP, 2, shp[axis] // N_DEV]
    return jnp.moveaxis(full.reshape(shp), (axis + 1, axis), (0, 1))


def _unpack(flat2d, shapes, lead=()):
    flat = flat2d.reshape(lead + (-1,))
    out, off = [], 0
    for shp in shapes:
        n = math.prod(shp)
        out.append(flat[..., off:off + n].reshape(lead + tuple(shp)))
        off += n
    return out


def _to_layout(w):
    return jnp.concatenate([w[:, :1536], w[:, 1824:2336], w[:, 1536:1792], w[:, 2336:3104], w[:, 1792:1824],
                            jnp.zeros((w.shape[0], NP - IN_COLS), w.dtype)], axis=1)


def _from_layout(g):
    return jnp.concatenate([g[:, :1536], g[:, C_CKV:C_CKV + KV_LORA], g[:, C_KR:C_KR + ROPE],
                            g[:, C_ZM:C_ZM + MLA_W], g[:, C_SG:C_SG + 3 * SG_W]], axis=1)


def _pad_heads(w, real):
    lead = w.shape[:-1]
    w = w.reshape(lead + (HEADS, real))
    return jnp.pad(w, [(0, 0)] * len(lead) + [(0, 0), (0, HEAD_PAD - real)]).reshape(lead + (QW,))


def _rope_tables(s):
    half = ROPE // 2
    inv_freq = ROPE_THETA ** (-jnp.arange(half, dtype=F32) / half)
    ang = jnp.arange(s, dtype=F32)[:, None] * inv_freq[None, :]
    cos, sin = jnp.cos(ang), jnp.sin(ang)
    ones = jnp.ones((s, NOPE), F32)
    zeros = jnp.zeros((s, NOPE), F32)
    pad = jnp.zeros((s, HEAD_PAD - QK), F32)
    rc = jnp.concatenate([ones, cos, cos, pad + 1.0], axis=1)
    rs = jnp.concatenate([zeros, -sin, sin, pad], axis=1)
    return rc, rs


def kernel(x, norm_g, w_in, conv_w, conv_b, conv_ln_g, conv_ln_b, conv_pw_w, conv_pw_b, q_norm_g, w_uq, kv_norm_g, w_ukv, qk_q_g, qk_k_g, sg_ln_g, sg_ln_b, sg_w, sg_b, branch_norm_g, w_out, loss_target, m_norm_g, m_w_in, m_conv_w, m_conv_b, m_conv_ln_g, m_conv_ln_b, m_conv_pw_w, m_conv_pw_b, m_q_norm_g, m_w_uq, m_kv_norm_g, m_w_ukv, m_qk_q_g, m_qk_k_g, m_sg_ln_g, m_sg_ln_b, m_sg_w, m_sg_b, m_branch_norm_g, m_w_out, v_norm_g, v_w_in, v_conv_w, v_conv_b, v_conv_ln_g, v_conv_ln_b, v_conv_pw_w, v_conv_pw_b, v_q_norm_g, v_w_uq, v_kv_norm_g, v_w_ukv, v_qk_q_g, v_qk_k_g, v_sg_ln_g, v_sg_ln_b, v_sg_w, v_sg_b, v_branch_norm_g, v_w_out):
    given = dict(locals())
    wts = {n: given[n] for n in W_NAMES}
    mom_m = {n: given['m_' + n] for n in W_NAMES}
    mom_v = {n: given['v_' + n] for n in W_NAMES}
    s = x.shape[1]
    xs = x.reshape(s, D_MODEL)
    target = loss_target.reshape(s, D_MODEL)

    rp_shapes = [wts[n].shape for n in REPL]
    rows_rp = _rows_for(sum(math.prod(p) for p in rp_shapes), BF16_ROWS)
    sh_shape = {n: wts[n].shape for n in SHARDED}
    sh_2d = {n: (sh_shape[n][0] * sh_shape[n][1], sh_shape[n][2]) for n in SHARDED}

    def vec(a, width=None):
        a = a.reshape(1, -1)
        return a if width is None else jnp.pad(a, ((0, 0), (0, width - a.shape[1])))

    def win_full(g):
        return jnp.moveaxis(g, 0, 1).reshape(g.shape[1], N_DEV * g.shape[2])

    win_l = [wts['w_in'][l].astype(BF16) for l in range(DEPTH)]
    (g_win0,) = _all_gather([win_l[0]], "w_in0_all_gather")
    later = [n for n in SHARDED if n != 'w_in']
    proj0, gathered = _proj_fwd(
        xs, vec(wts['norm_g'][0]), _to_layout(win_full(g_win0)), "proj_fwd_0",
        gather=[win_l[1]] + [wts[n].astype(F32 if n == 'conv_w' else BF16).reshape(sh_2d[n]) for n in later])
    full = {n: _unshard(g.reshape((N_DEV,) + sh_shape[n]), SHARD_AXIS[n]) for n, g in zip(later, gathered[1:])}
    full['w_in'] = [win_full(g_win0), win_full(gathered[0])]
    full.update({n: wts[n] for n in REPL})

    rc, rs = _rope_tables(s)
    tril = jnp.tril(jnp.ones((SG_CHUNK, SG_CHUNK), dtype=bool))

    layers = []
    for l in range(DEPTH):
        p = {n: full[n][l] for n in W_NAMES}
        wukv = p['w_ukv'].reshape(KV_LORA, HEADS, NOPE + V_DIM)
        wm = jnp.where(tril[None], p['sg_w'], 0.0)
        layers.append(dict(
            ng=vec(p['norm_g']), win=_to_layout(p['w_in']).astype(BF16),
            cw=jnp.pad(p['conv_w'], ((0, HALO - CONV_K), (0, 0))), cb=vec(p['conv_b']), clg=vec(p['conv_ln_g']),
            clb=vec(p['conv_ln_b']), pww=p['conv_pw_w'].astype(BF16), pwb=vec(p['conv_pw_b']),
            qg=vec(p['q_norm_g']), wuq=_pad_heads(p['w_uq'], QK).astype(BF16), kvg=vec(p['kv_norm_g']),
            wukv=jnp.concatenate([_pad_heads(wukv[:, :, :NOPE].reshape(KV_LORA, HEADS * NOPE), NOPE),
                                  wukv[:, :, NOPE:].reshape(KV_LORA, MLA_W)], axis=1).astype(BF16),
            gq=vec(p['qk_q_g'], LANES), gk=vec(p['qk_k_g'], LANES),
            slg=vec(p['sg_ln_g']), slb=vec(p['sg_ln_b']), wm=wm.astype(BF16),
            wmt=jnp.swapaxes(wm, 1, 2).astype(BF16),
            sbx=jnp.repeat(p['sg_b'].T, SG_W // SG_HEADS, axis=1),
            bng=vec(p['branch_norm_g']), wout=p['w_out'].astype(BF16)))

    acts = []
    h_in = xs
    for l, p in enumerate(layers):
        proj = proj0 if l == 0 else _proj_fwd(h_in, p['ng'], p['win'], f"proj_fwd_{l}")[0]
        yc, cv = _conv_fwd(proj, p['cw'], p['cb'], p['clg'], p['clb'], p['pww'], p['pwb'], f"conv_fwd_{l}")
        ys = _sgu_fwd(proj, p['slg'], p['slb'], p['wm'], p['sbx'], f"sgu_fwd_{l}")
        q, k, v = _mla_fwd(proj, rc, rs, p['qg'], p['wuq'], p['kvg'], p['wukv'], p['gq'], p['gk'], f"mla_fwd_{l}")
        o, lse = _attn_fwd(q, k, v, f"attn_fwd_{l}")
        h_out = _post_fwd(h_in, yc, o, proj, ys, p['bng'], p['wout'], f"post_fwd_{l}")
        acts.append(dict(x=h_in, proj=proj, yc=yc, cv=cv, ys=ys, q=q, k=k, v=v, o=o, lse=lse))
        h_in = h_out

    d_out, loss_blk = _loss_head(h_in, target, "loss_head")
    loss = lax.psum(loss_blk[0, 0], ("x", "y", "c"))

    grads = {n: [None] * DEPTH for n in W_NAMES}
    for l in reversed(range(DEPTH)):
        p, a = layers[l], acts[l]
        d_yc, d_ys, d_o, d_zm, stats, g_wout, g_bng = _post_bwd(
            d_out, a['yc'], a['o'], a['lse'], a['proj'], a['ys'], p['bng'], p['wout'], f"post_bwd_{l}")
        dq, dk, dv = _attn_bwd(a['q'], a['k'], a['v'], d_o, stats, f"attn_bwd_{l}")
        d_a, g_cw, g_pww, gv_c = _conv_bwd(a['proj'], a['cv'], d_yc, p['cw'], p['clg'], p['clb'], p['pww'], p['pwb'],
                                           f"conv_bwd_{l}")
        d_sg, g_wm, dms, gv_s = _sgu_bwd(a['proj'], d_ys, p['slg'], p['slb'], p['wm'], p['wmt'], p['sbx'],
                                         f"sgu_bwd_{l}")
        d_cq, d_ckv, d_kr, g_wuq, g_wukv, gv_m = _mla_bwd(
            a['proj'], rc, rs, p['qg'], p['wuq'], p['kvg'], p['wukv'], p['gq'], p['gk'], dq, dk, dv, f"mla_bwd_{l}")
        pieces = [(d_a, C_A), (d_cq, C_CQ), (d_zm, C_ZM), (d_ckv, C_CKV), (d_sg, C_SG), (d_kr, C_KR)]
        d_x, h_t, g_ng = _proj_bwd(a['x'], p['ng'], p['win'], d_out, pieces, f"proj_bwd_{l}")
        g_win = _win_grad(h_t, pieces, f"win_grad_{l}")
        d_out = d_x

        grads['norm_g'][l] = g_ng[0]
        grads['w_in'][l] = _from_layout(g_win)
        grads['conv_w'][l] = g_cw[:CONV_K]
        grads['conv_b'][l] = gv_c[0]
        grads['conv_ln_g'][l] = gv_c[1]
        grads['conv_ln_b'][l] = gv_c[2]
        grads['conv_pw_w'][l] = g_pww
        grads['conv_pw_b'][l] = gv_c[3]
        grads['q_norm_g'][l] = gv_m[0, :Q_LORA]
        grads['w_uq'][l] = g_wuq.reshape(Q_LORA, HEADS, HEAD_PAD)[:, :, :QK].reshape(Q_LORA, HEADS * QK)
        grads['kv_norm_g'][l] = gv_m[1, :KV_LORA]
        grads['w_ukv'][l] = jnp.concatenate(
            [g_wukv[:, :QW].reshape(KV_LORA, HEADS, HEAD_PAD)[:, :, :NOPE],
             g_wukv[:, QW:].reshape(KV_LORA, HEADS, V_DIM)], axis=2).reshape(KV_LORA, HEADS * (NOPE + V_DIM))
        grads['qk_q_g'][l] = gv_m[2, :QK]
        grads['qk_k_g'][l] = gv_m[3, :QK]
        grads['sg_ln_g'][l] = gv_s[0]
        grads['sg_ln_b'][l] = gv_s[1]
        grads['sg_w'][l] = jnp.where(tril[None], g_wm, 0.0)
        grads['sg_b'][l] = dms.reshape(SG_CHUNK, SG_HEADS, SG_W // SG_HEADS).sum(axis=2).T
        grads['branch_norm_g'][l] = g_bng[0]
        grads['w_out'][l] = g_wout
    grad_x = d_out.reshape(x.shape)
    g_full = {n: jnp.stack(grads[n]) for n in W_NAMES}

    gss = [_split_c_chip(g_full[n].astype(BF16), SHARD_AXIS[n]).reshape((2, N_CHIP) + sh_2d[n]) for n in SHARDED]
    gr = _pack([g_full[n].astype(BF16) for n in REPL], rows_rp)
    *theirs, gr_theirs = _grad_to_sibling(gss, gr, "grad_to_sibling")
    my_c = lax.axis_index("c")
    mine = [lax.dynamic_index_in_dim(g, my_c, 0, keepdims=False) for g in gss]
    *chip_sh, chip_rp = _chip_sum(mine, theirs, gr, gr_theirs, "chip_sum")
    *parts_sh, parts_rp = _grad_to_chips(chip_sh, chip_rp, "grad_to_chips")
    res_sh = {n: _adamw(parts, wts[n].reshape(sh_2d[n]), mom_m[n].reshape(sh_2d[n]), mom_v[n].reshape(sh_2d[n]),
                        f"adamw_{n}") for n, parts in zip(SHARDED, parts_sh)}
    res_rp = _adamw(parts_rp, _pack([wts[n] for n in REPL], rows_rp), _pack([mom_m[n] for n in REPL], rows_rp),
                    _pack([mom_v[n] for n in REPL], rows_rp), "adamw_replicated")
    outs = []
    for kind in range(4):
        vals = {n: res_sh[n][kind].reshape(sh_shape[n]) for n in SHARDED}
        vals.update(zip(REPL, _unpack(res_rp[kind], rp_shapes)))
        outs.extend(vals[n] for n in W_NAMES)
    return (loss, grad_x, *outs)
```

```python
import functools
import math

import jax
import jax.numpy as jnp
from jax import lax
from jax.experimental import pallas as pl
from jax.experimental.pallas import tpu as pltpu

F32 = jnp.float32
BF16 = jnp.bfloat16

N_DEV = 8
DEPTH = 2
D_MODEL = 1024
CONV_W = 256
CONV_K = 31
HEADS = 8
NOPE = 64
ROPE = 32
QK = NOPE + ROPE
HEAD_PAD = 128
V_DIM = 64
MLA_W = HEADS * V_DIM
Q_LORA = 768
KV_LORA = 256
SG_W = 256
SG_HEADS = 4
SG_CHUNK = 128
ROPE_THETA = 10000.0
EPS = 1e-6
IN_COLS = 3104
NP = 3200
C_A, C_CQ, C_ZM, C_CKV, C_SG, C_KR = 0, 768, 1536, 2048, 2304, 3072
HALO = 32
SUB = 64
NEG = -1e30
LANES = 128
VMEM_LIMIT_V7X = 52 * 1024 * 1024

ADAM_LR = 0.001
ADAM_B1 = 0.9
ADAM_B2 = 0.999
ADAM_EPS = 1e-08
ADAM_WD = 0.01
ADAM_STEP = 10

W_NAMES = ['norm_g', 'w_in', 'conv_w', 'conv_b', 'conv_ln_g', 'conv_ln_b', 'conv_pw_w', 'conv_pw_b',
           'q_norm_g', 'w_uq', 'kv_norm_g', 'w_ukv', 'qk_q_g', 'qk_k_g', 'sg_ln_g', 'sg_ln_b', 'sg_w',
           'sg_b', 'branch_norm_g', 'w_out']
SHARD_AXIS = {'w_in': 2, 'conv_w': 2, 'conv_pw_w': 1, 'w_uq': 1, 'w_ukv': 2, 'w_out': 1}
SHARDED = [n for n in W_NAMES if n in SHARD_AXIS]
REPL = [n for n in W_NAMES if n not in SHARD_AXIS]


def _tile(s):
    for t in (512, 256, 128):
        if s % t == 0 and s // t >= 2:
            return t
    return s


def _cp(sem):
    return pltpu.CompilerParams(dimension_semantics=sem, vmem_limit_bytes=VMEM_LIMIT_V7X)


def _mm(a, b):
    return jnp.dot(a.astype(BF16), b.astype(BF16), preferred_element_type=F32)


def _mm_nt(a, b):
    return lax.dot_general(a.astype(BF16), b.astype(BF16), (((1,), (1,)), ((), ())),
                           preferred_element_type=F32)


def _mm_tn(a, b):
    return lax.dot_general(a.astype(BF16), b.astype(BF16), (((0,), (0,)), ((), ())),
                           preferred_element_type=F32)


_GC = math.sqrt(2.0 / math.pi)
_GA = 0.044715


def _sig(x):
    return 1.0 / (1.0 + jnp.exp(-x))


def _silu(x):
    return x * _sig(x)


def _dsilu(x):
    s = _sig(x)
    return s * (1.0 + x * (1.0 - s))


def _gelu(x):
    return 0.5 * x * (1.0 + jnp.tanh(_GC * (x + _GA * x * x * x)))


def _dgelu(x):
    t = jnp.tanh(_GC * (x + _GA * x * x * x))
    return 0.5 * (1.0 + t) + 0.5 * x * (1.0 - t * t) * _GC * (1.0 + 3.0 * _GA * x * x)


def _rsum(x):
    return jnp.sum(x, axis=-1, keepdims=True)


def _csum(x):
    return jnp.sum(x, axis=0, keepdims=True)


def _rms_fwd(x, n):
    r = lax.rsqrt(_rsum(x * x) * (1.0 / n) + EPS)
    return x * r, r


def _rms_bwd(dxh, xn, r, n):
    return r * (dxh - xn * (_rsum(dxh * xn) * (1.0 / n)))


def _ln_fwd(x, n):
    mu = _rsum(x) * (1.0 / n)
    xc = x - mu
    r = lax.rsqrt(_rsum(xc * xc) * (1.0 / n) + EPS)
    return xc * r, r


def _ln_bwd(dxh, xh, r, n):
    return r * (dxh - _rsum(dxh) * (1.0 / n) - xh * (_rsum(dxh * xh) * (1.0 / n)))


def _partner(x, lane):
    return jnp.where(lane < NOPE + ROPE // 2, pltpu.roll(x, LANES - ROPE // 2, 1), pltpu.roll(x, ROPE // 2, 1))


def _row_spec(ts, w, col=0):
    return pl.BlockSpec((ts, w), lambda i, col=col: (i, col))


def _full_spec(shape):
    nd = len(shape)
    return pl.BlockSpec(shape, lambda i, nd=nd: (0,) * nd)


PROJ_CHUNK = 640


def _proj_fwd(x, ng, win_p, name, gather=()):
    s = x.shape[0]
    ts = _tile(s)
    nt = s // ts
    n = len(gather)

    def body(x_ref, g_ref, w_ref, *rest):
        o_ref = rest[n]
        if n:
            i = pl.program_id(0)
            start, pass_on, finish = _ag_steps(rest[:n], rest[n + 1:2 * n + 1], *rest[2 * n + 1:])
            pl.when(i == 0)(start)
            pl.when(i == nt // 2)(pass_on)
        xv = x_ref[...]
        xn, _ = _rms_fwd(xv, D_MODEL)
        h = (xn * g_ref[...]).astype(BF16)
        for c in range(0, NP, PROJ_CHUNK):
            o_ref[:, c:c + PROJ_CHUNK] = jnp.dot(h, w_ref[:, c:c + PROJ_CHUNK], preferred_element_type=F32)
        if n:
            pl.when(i == nt - 1)(finish)

    out = pl.pallas_call(
        body, name=name, grid=(nt,),
        in_specs=[_row_spec(ts, D_MODEL), _full_spec((1, D_MODEL)), _full_spec((D_MODEL, NP))] + [ANY] * n,
        out_specs=[_row_spec(ts, NP)] + [ANY] * n,
        out_shape=[jax.ShapeDtypeStruct((s, NP), F32)]
        + [jax.ShapeDtypeStruct((N_DEV,) + a.shape, a.dtype) for a in gather],
        scratch_shapes=_ag_scratch(n) if n else [],
        compiler_params=_cp(("arbitrary",) if n else ("parallel",)),
    )(x, ng, win_p, *gather)
    return out[0], out[1:]


def _proj_bwd(x, ng, win_p, d_out, pieces, name):
    s = x.shape[0]
    ts = _tile(s)
    offs = [o for _, o in pieces]
    widths = [p.shape[1] for p, _ in pieces]

    def body(x_ref, g_ref, w_ref, do_ref, *rest):
        p_refs = rest[:len(pieces)]
        dx_ref, h_ref, gg_ref = rest[len(pieces):]
        i = pl.program_id(0)
        xv = x_ref[...]
        xn, r = _rms_fwd(xv, D_MODEL)
        g = g_ref[...]
        h_ref[...] = (xn * g).T.astype(BF16)
        dh = jnp.zeros((ts, D_MODEL), F32)
        for p_ref, off, w in zip(p_refs, offs, widths):
            dh = dh + _mm_nt(p_ref[...], w_ref[:, off:off + w])

        @pl.when(i == 0)
        def _():
            gg_ref[...] = jnp.zeros_like(gg_ref)

        gg_ref[...] += _csum(dh * xn)
        dx_ref[...] = _rms_bwd(dh * g, xn, r, D_MODEL) + do_ref[...]

    in_specs = [_row_spec(ts, D_MODEL), _full_spec((1, D_MODEL)), _full_spec((D_MODEL, NP)), _row_spec(ts, D_MODEL)]
    in_specs += [_row_spec(ts, w) for w in widths]
    return pl.pallas_call(
        body, name=name, grid=(s // ts,),
        in_specs=in_specs,
        out_specs=[_row_spec(ts, D_MODEL), pl.BlockSpec((D_MODEL, ts), lambda i: (0, i)), _full_spec((1, D_MODEL))],
        out_shape=[jax.ShapeDtypeStruct((s, D_MODEL), F32), jax.ShapeDtypeStruct((D_MODEL, s), BF16),
                   jax.ShapeDtypeStruct((1, D_MODEL), F32)],
        compiler_params=_cp(("arbitrary",)),
    )(x, ng, win_p, d_out, *[p for p, _ in pieces])


WG_TILE = 256


def _win_grad(ht, pieces, name):
    s = ht.shape[1]
    ts = min(WG_TILE, s)
    offs = [o for _, o in pieces]
    widths = [p.shape[1] for p, _ in pieces]

    def body(ht_ref, *rest):
        p_refs, o_ref = rest[:-1], rest[-1]

        @pl.when(pl.program_id(0) == 0)
        def _():
            o_ref[...] = jnp.zeros_like(o_ref)

        hb = ht_ref[...]
        for p_ref, off, w in zip(p_refs, offs, widths):
            o_ref[:, off:off + w] += jnp.dot(hb, p_ref[...].astype(BF16), preferred_element_type=F32)

    return pl.pallas_call(
        body, name=name, grid=(s // ts,),
        in_specs=[pl.BlockSpec((D_MODEL, ts), lambda i: (0, i))] + [_row_spec(ts, w) for w in widths],
        out_specs=_full_spec((D_MODEL, NP)),
        out_shape=jax.ShapeDtypeStruct((D_MODEL, NP), F32),
        compiler_params=_cp(("arbitrary",)),
    )(ht, *[p for p, _ in pieces])


def _halo_spec(ts):
    per = ts // HALO
    return pl.BlockSpec((HALO, 2 * CONV_W), lambda i: (jnp.maximum(i * per - 1, 0), 0))


def _conv_taps(ext_ref, cw_ref, cv_ref, cb, ts):
    base = HALO - (CONV_K - 1)
    for r0 in range(0, ts, SUB):
        acc = jnp.zeros((SUB, CONV_W), F32)
        for k in range(CONV_K):
            acc = acc + cw_ref[k:k + 1, :] * ext_ref[r0 + base + k:r0 + base + k + SUB, :]
        cv_ref[r0:r0 + SUB, :] = acc + cb


def _conv_fwd(proj, cw, cb, lg, lb, pww, pwb, name):
    s = proj.shape[0]
    ts = _tile(s)

    def body(pa_ref, ph_ref, cw_ref, cb_ref, lg_ref, lb_ref, pww_ref, pwb_ref, y_ref, cv_ref, ext_ref):
        i = pl.program_id(0)
        pa = pa_ref[...]
        a, ag, zc = pa[:, :CONV_W], pa[:, CONV_W:2 * CONV_W], pa[:, 2 * CONV_W:]
        ph = ph_ref[...]
        hglu = ph[:, :CONV_W] * _sig(ph[:, CONV_W:])
        ext_ref[0:HALO, :] = jnp.where(i > 0, hglu, 0.0)
        ext_ref[HALO:HALO + ts, :] = a * _sig(ag)
        _conv_taps(ext_ref, cw_ref, cv_ref, cb_ref[...], ts)
        xh, _ = _ln_fwd(cv_ref[...], CONV_W)
        ln = xh * lg_ref[...] + lb_ref[...]
        pw = _mm(_silu(ln), pww_ref[...]) + pwb_ref[...]
        y_ref[...] = pw * _silu(zc)

    vec = _full_spec((1, CONV_W))
    return pl.pallas_call(
        body, name=name, grid=(s // ts,),
        in_specs=[_row_spec(ts, 3 * CONV_W, 0), _halo_spec(ts), _full_spec((HALO, CONV_W)), vec, vec, vec,
                  _full_spec((CONV_W, CONV_W)), vec],
        out_specs=[_row_spec(ts, CONV_W), _row_spec(ts, CONV_W)],
        out_shape=[jax.ShapeDtypeStruct((s, CONV_W), F32), jax.ShapeDtypeStruct((s, CONV_W), F32)],
        scratch_shapes=[pltpu.VMEM((HALO + ts, CONV_W), F32)],
        compiler_params=_cp(("parallel",)),
    )(proj, proj, cw, cb, lg, lb, pww, pwb)


def _conv_bwd(proj, cv, dy, cw, lg, lb, pww, pwb, name):
    s = proj.shape[0]
    ts = _tile(s)
    nt = s // ts
    per = ts // HALO

    def body(pa_ref, ph_ref, cv_ref, dy_ref, cw_ref, lg_ref, lb_ref, pww_ref, pwb_ref,
             dp_ref, gcw_ref, gpw_ref, gv_ref, ext_ref, dext_ref, carry_ref, gacc_ref):
        i = pl.program_id(0)
        ti = nt - 1 - i

        @pl.when(i == 0)
        def _():
            carry_ref[...] = jnp.zeros_like(carry_ref)
            gacc_ref[...] = jnp.zeros_like(gacc_ref)
            gpw_ref[...] = jnp.zeros_like(gpw_ref)
            gv_ref[...] = jnp.zeros_like(gv_ref)

        pa = pa_ref[...]
        a, ag, zc = pa[:, :CONV_W], pa[:, CONV_W:2 * CONV_W], pa[:, 2 * CONV_W:]
        sag = _sig(ag)
        ph = ph_ref[...]
        hglu = ph[:, :CONV_W] * _sig(ph[:, CONV_W:])
        ext_ref[0:HALO, :] = jnp.where(ti > 0, hglu, 0.0)
        ext_ref[HALO:HALO + ts, :] = a * sag
        xh, rl = _ln_fwd(cv_ref[...], CONV_W)
        lg = lg_ref[...]
        ln = xh * lg + lb_ref[...]
        sw = _silu(ln)
        pww = pww_ref[...]
        pw = _mm(sw, pww) + pwb_ref[...]
        d_y = dy_ref[...]
        d_pw = d_y * _silu(zc)
        d_zc = d_y * pw * _dsilu(zc)
        gpw_ref[...] += _mm_tn(sw, d_pw)
        d_ln = _mm_nt(d_pw, pww) * _dsilu(ln)
        d_cv = _ln_bwd(d_ln * lg, xh, rl, CONV_W)
        gv_ref[0:1, :] += _csum(d_cv)
        gv_ref[1:2, :] += _csum(d_ln * xh)
        gv_ref[2:3, :] += _csum(d_ln)
        gv_ref[3:4, :] += _csum(d_pw)
        dext_ref[0:ts, :] = d_cv
        dext_ref[ts:ts + HALO, :] = carry_ref[...]
        carry_ref[...] = d_cv[0:HALO, :]
        base = HALO - (CONV_K - 1)
        for r0 in range(0, ts, SUB):
            dcv_r = dext_ref[r0:r0 + SUB, :]
            dg = jnp.zeros((SUB, CONV_W), F32)
            for k in range(CONV_K):
                prod = dcv_r * ext_ref[r0 + base + k:r0 + base + k + SUB, :]
                gacc_ref[8 * k:8 * k + 8, :] += jnp.sum(prod.reshape(SUB // 8, 8, CONV_W), axis=0)
                dg = dg + cw_ref[k:k + 1, :] * dext_ref[r0 + CONV_K - 1 - k:r0 + CONV_K - 1 - k + SUB, :]
            sg_r, a_r = sag[r0:r0 + SUB, :], a[r0:r0 + SUB, :]
            dp_ref[r0:r0 + SUB, 0:CONV_W] = (dg * sg_r).astype(BF16)
            dp_ref[r0:r0 + SUB, CONV_W:2 * CONV_W] = (dg * a_r * sg_r * (1.0 - sg_r)).astype(BF16)
        dp_ref[:, 2 * CONV_W:] = d_zc.astype(BF16)

        @pl.when(i == nt - 1)
        def _():
            gcw_ref[...] = jnp.zeros_like(gcw_ref)
            for k in range(CONV_K):
                gcw_ref[k:k + 1, :] = _csum(gacc_ref[8 * k:8 * k + 8, :])

    vec = _full_spec((1, CONV_W))
    rev = lambda w, col=0: pl.BlockSpec((ts, w), lambda i, col=col: (nt - 1 - i, col))
    halo = pl.BlockSpec((HALO, 2 * CONV_W), lambda i: (jnp.maximum((nt - 1 - i) * per - 1, 0), 0))
    return pl.pallas_call(
        body, name=name, grid=(nt,),
        in_specs=[rev(3 * CONV_W), halo, rev(CONV_W), rev(CONV_W), _full_spec((HALO, CONV_W)), vec, vec,
                  _full_spec((CONV_W, CONV_W)), vec],
        out_specs=[rev(3 * CONV_W), _full_spec((HALO, CONV_W)), _full_spec((CONV_W, CONV_W)), _full_spec((8, CONV_W))],
        out_shape=[jax.ShapeDtypeStruct((s, 3 * CONV_W), BF16), jax.ShapeDtypeStruct((HALO, CONV_W), F32),
                   jax.ShapeDtypeStruct((CONV_W, CONV_W), F32), jax.ShapeDtypeStruct((8, CONV_W), F32)],
        scratch_shapes=[pltpu.VMEM((HALO + ts, CONV_W), F32), pltpu.VMEM((ts + HALO, CONV_W), F32),
                        pltpu.VMEM((HALO, CONV_W), F32), pltpu.VMEM((8 * HALO, CONV_W), F32)],
        compiler_params=_cp(("arbitrary",)),
    )(proj, proj, cv, dy, cw, lg, lb, pww, pwb)


def _sg_mix(wm_ref, vc, head):
    out = jnp.zeros((SG_CHUNK, SG_W), F32)
    vb = vc.astype(BF16)
    for g in range(SG_HEADS):
        out = jnp.where(head == g, jnp.dot(wm_ref[g], vb, preferred_element_type=F32), out)
    return out


def _sgu_fwd(proj, lg, lb, wm, sbx, name):
    s = proj.shape[0]
    ts = _tile(s)

    def body(ps_ref, lg_ref, lb_ref, wm_ref, sbx_ref, y_ref, mix_ref):
        ps = ps_ref[...]
        us, vs, zs = ps[:, :SG_W], ps[:, SG_W:2 * SG_W], ps[:, 2 * SG_W:]
        xh, _ = _ln_fwd(_gelu(vs), SG_W)
        vn = xh * lg_ref[...] + lb_ref[...]
        head = lax.broadcasted_iota(jnp.int32, (SG_CHUNK, SG_W), 1) // (SG_W // SG_HEADS)
        for c0 in range(0, ts, SG_CHUNK):
            mix_ref[c0:c0 + SG_CHUNK, :] = _sg_mix(wm_ref, vn[c0:c0 + SG_CHUNK, :], head) + sbx_ref[...]
        y_ref[...] = _gelu(us) * mix_ref[...] * _silu(zs)

    vec = _full_spec((1, SG_W))
    return pl.pallas_call(
        body, name=name, grid=(s // ts,),
        in_specs=[_row_spec(ts, 3 * SG_W, C_SG // (3 * SG_W)), vec, vec,
                  _full_spec((SG_HEADS, SG_CHUNK, SG_CHUNK)), _full_spec((SG_CHUNK, SG_W))],
        out_specs=_row_spec(ts, SG_W),
        out_shape=jax.ShapeDtypeStruct((s, SG_W), F32),
        scratch_shapes=[pltpu.VMEM((ts, SG_W), F32)],
        compiler_params=_cp(("parallel",)),
    )(proj, lg, lb, wm, sbx)


def _sgu_bwd(proj, dy, lg, lb, wm, wmt, sbx, name):
    s = proj.shape[0]
    ts = _tile(s)

    def body(ps_ref, dy_ref, lg_ref, lb_ref, wm_ref, wmt_ref, sbx_ref,
             dp_ref, gwm_ref, dms_ref, gv_ref, mix_ref, dvn_ref):
        i = pl.program_id(0)

        @pl.when(i == 0)
        def _():
            gwm_ref[...] = jnp.zeros_like(gwm_ref)
            dms_ref[...] = jnp.zeros_like(dms_ref)
            gv_ref[...] = jnp.zeros_like(gv_ref)

        ps = ps_ref[...]
        us, vs, zs = ps[:, :SG_W], ps[:, SG_W:2 * SG_W], ps[:, 2 * SG_W:]
        xh, rl = _ln_fwd(_gelu(vs), SG_W)
        lg = lg_ref[...]
        vn = xh * lg + lb_ref[...]
        head = lax.broadcasted_iota(jnp.int32, (SG_CHUNK, SG_W), 1) // (SG_W // SG_HEADS)
        for c0 in range(0, ts, SG_CHUNK):
            mix_ref[c0:c0 + SG_CHUNK, :] = _sg_mix(wm_ref, vn[c0:c0 + SG_CHUNK, :], head) + sbx_ref[...]
        mixed = mix_ref[...]
        u = _gelu(us)
        sz = _silu(zs)
        d_y = dy_ref[...]
        d_mixed = d_y * u * sz
        dp_ref[:, 0:SG_W] = (d_y * mixed * sz * _dgelu(us)).astype(BF16)
        dp_ref[:, 2 * SG_W:] = (d_y * u * mixed * _dsilu(zs)).astype(BF16)
        dms = jnp.zeros((SG_CHUNK, SG_W), F32)
        for c0 in range(0, ts, SG_CHUNK):
            dm = d_mixed[c0:c0 + SG_CHUNK, :]
            vc = vn[c0:c0 + SG_CHUNK, :]
            dms = dms + dm
            for g in range(SG_HEADS):
                gwm_ref[g] += _mm_nt(jnp.where(head == g, dm, 0.0), vc)
            dvn_ref[c0:c0 + SG_CHUNK, :] = _sg_mix(wmt_ref, dm, head)
        dms_ref[...] += dms
        d_vn = dvn_ref[...]
        gv_ref[0:1, :] += _csum(d_vn * xh)
        gv_ref[1:2, :] += _csum(d_vn)
        dp_ref[:, SG_W:2 * SG_W] = (_ln_bwd(d_vn * lg, xh, rl, SG_W) * _dgelu(vs)).astype(BF16)

    vec = _full_spec((1, SG_W))
    wspec = _full_spec((SG_HEADS, SG_CHUNK, SG_CHUNK))
    return pl.pallas_call(
        body, name=name, grid=(s // ts,),
        in_specs=[_row_spec(ts, 3 * SG_W, C_SG // (3 * SG_W)), _row_spec(ts, SG_W), vec, vec, wspec, wspec,
                  _full_spec((SG_CHUNK, SG_W))],
        out_specs=[_row_spec(ts, 3 * SG_W), wspec, _full_spec((SG_CHUNK, SG_W)), _full_spec((8, SG_W))],
        out_shape=[jax.ShapeDtypeStruct((s, 3 * SG_W), BF16), jax.ShapeDtypeStruct((SG_HEADS, SG_CHUNK, SG_CHUNK), F32),
                   jax.ShapeDtypeStruct((SG_CHUNK, SG_W), F32), jax.ShapeDtypeStruct((8, SG_W), F32)],
        scratch_shapes=[pltpu.VMEM((ts, SG_W), F32), pltpu.VMEM((ts, SG_W), F32)],
        compiler_params=_cp(("arbitrary",)),
    )(proj, dy, lg, lb, wm, wmt, sbx)


QW = HEADS * HEAD_PAD
KVW = QW + MLA_W
ATT_SCALE = QK ** -0.5


def _mla_specs(ts):
    return [_row_spec(ts, Q_LORA, C_CQ // Q_LORA), _row_spec(ts, KV_LORA, C_CKV // KV_LORA),
            _row_spec(ts, LANES, C_KR // LANES), _row_spec(ts, LANES), _row_spec(ts, LANES),
            _full_spec((1, Q_LORA)), _full_spec((Q_LORA, QW)), _full_spec((1, KV_LORA)), _full_spec((KV_LORA, KVW)),
            _full_spec((1, LANES)), _full_spec((1, LANES))]


def _mla_fwd(proj, rc, rs, qg, wuq, kvg, wukv, gq, gk, name, gather=()):
    s = proj.shape[0]
    ts = _tile(s)
    nt = s // ts
    n = len(gather)

    def body(cq_ref, ckv_ref, kr_ref, rc_ref, rs_ref, qg_ref, wuq_ref, kvg_ref, wukv_ref, gq_ref, gk_ref, *rest):
        q_ref, k_ref, v_ref = rest[n:n + 3]
        if n:
            i = pl.program_id(0)
            start, pass_on, finish = _ag_steps(rest[:n], rest[n + 3:2 * n + 3], *rest[2 * n + 3:])
            pl.when(i == 0)(start)
            pl.when(i == nt // 2)(pass_on)
        lane = lax.broadcasted_iota(jnp.int32, (ts, LANES), 1)
        c, sn = rc_ref[...], rs_ref[...]
        cqn, _ = _rms_fwd(cq_ref[...], Q_LORA)
        q0 = _mm(cqn * qg_ref[...], wuq_ref[...])
        gq = gq_ref[...]
        for h in range(HEADS):
            xn, _ = _rms_fwd(q0[:, h * LANES:(h + 1) * LANES], QK)
            qn = xn * gq
            q_ref[:, h * LANES:(h + 1) * LANES] = ((qn * c + _partner(qn, lane) * sn) * ATT_SCALE).astype(BF16)
        ckvn, _ = _rms_fwd(ckv_ref[...], KV_LORA)
        kv = _mm(ckvn * kvg_ref[...], wukv_ref[...])
        kr = pltpu.roll(kr_ref[...], NOPE, 1)
        gk = gk_ref[...]
        for h in range(HEADS):
            xn, _ = _rms_fwd(kv[:, h * LANES:(h + 1) * LANES] + kr, QK)
            kn = xn * gk
            k_ref[:, h * LANES:(h + 1) * LANES] = (kn * c + _partner(kn, lane) * sn).astype(BF16)
        v_ref[...] = kv[:, QW:].astype(BF16)
        if n:
            pl.when(i == nt - 1)(finish)

    out = pl.pallas_call(
        body, name=name, grid=(nt,),
        in_specs=_mla_specs(ts) + [ANY] * n,
        out_specs=[_row_spec(ts, QW), _row_spec(ts, QW), _row_spec(ts, MLA_W)] + [ANY] * n,
        out_shape=[jax.ShapeDtypeStruct((s, QW), BF16), jax.ShapeDtypeStruct((s, QW), BF16),
                   jax.ShapeDtypeStruct((s, MLA_W), BF16)]
        + [jax.ShapeDtypeStruct((N_DEV,) + a.shape, a.dtype) for a in gather],
        scratch_shapes=_ag_scratch(n) if n else [],
        compiler_params=_cp(("arbitrary",) if n else ("parallel",)),
    )(proj, proj, proj, rc, rs, qg, wuq, kvg, wukv, gq, gk, *gather)
    return out[0], out[1], out[2], out[3:]


def _mla_bwd(proj, rc, rs, qg, wuq, kvg, wukv, gq, gk, dq, dk, dv, name):
    s = proj.shape[0]
    ts = _tile(s)

    def body(cq_ref, ckv_ref, kr_ref, rc_ref, rs_ref, qg_ref, wuq_ref, kvg_ref, wukv_ref, gq_ref, gk_ref,
             dq_ref, dk_ref, dv_ref, dcq_ref, dckv_ref, dkr_ref, gwuq_ref, gwukv_ref, gv_ref, d0_ref):
        i = pl.program_id(0)

        @pl.when(i == 0)
        def _():
            gwuq_ref[...] = jnp.zeros_like(gwuq_ref)
            gwukv_ref[...] = jnp.zeros_like(gwukv_ref)
            gv_ref[...] = jnp.zeros_like(gv_ref)

        lane = lax.broadcasted_iota(jnp.int32, (ts, LANES), 1)
        c, sn = rc_ref[...], rs_ref[...]
        cq = cq_ref[...]
        cqx, rq0 = _rms_fwd(cq, Q_LORA)
        qg = qg_ref[...]
        cqn = cqx * qg
        wuq = wuq_ref[...]
        q0 = _mm(cqn, wuq)
        gq = gq_ref[...]
        ggq = jnp.zeros((1, LANES), F32)
        for h in range(HEADS):
            xn, r = _rms_fwd(q0[:, h * LANES:(h + 1) * LANES], QK)
            d = dq_ref[:, h * LANES:(h + 1) * LANES] * ATT_SCALE
            d_qn = d * c - _partner(d, lane) * sn
            ggq = ggq + _csum(d_qn * xn)
            d0_ref[:, h * LANES:(h + 1) * LANES] = _rms_bwd(d_qn * gq, xn, r, QK)
        dq0 = d0_ref[:, 0:QW]
        gwuq_ref[...] += _mm_tn(cqn, dq0)
        d_cqn = _mm_nt(dq0, wuq)
        gv_ref[0:1, 0:Q_LORA] += _csum(d_cqn * cqx)
        gv_ref[2:3, 0:LANES] += ggq
        dcq_ref[...] = _rms_bwd(d_cqn * qg, cqx, rq0, Q_LORA).astype(BF16)
        ckv = ckv_ref[...]
        ckx, rk0 = _rms_fwd(ckv, KV_LORA)
        kvg = kvg_ref[...]
        ckvn = ckx * kvg
        wukv = wukv_ref[...]
        kv = _mm(ckvn, wukv)
        kr = pltpu.roll(kr_ref[...], NOPE, 1)
        gk = gk_ref[...]
        ggk = jnp.zeros((1, LANES), F32)
        dkr = jnp.zeros((ts, LANES), F32)
        for h in range(HEADS):
            xn, r = _rms_fwd(kv[:, h * LANES:(h + 1) * LANES] + kr, QK)
            d = dk_ref[:, h * LANES:(h + 1) * LANES]
            d_kn = d * c - _partner(d, lane) * sn
            ggk = ggk + _csum(d_kn * xn)
            d_k0 = _rms_bwd(d_kn * gk, xn, r, QK)
            dkr = dkr + d_k0
            d0_ref[:, h * LANES:(h + 1) * LANES] = d_k0
        d0_ref[:, QW:KVW] = dv_ref[...]
        dkv = d0_ref[...]
        dkr_ref[...] = jnp.where(lane < ROPE, pltpu.roll(dkr, NOPE, 1), 0.0).astype(BF16)
        gwukv_ref[...] += _mm_tn(ckvn, dkv)
        d_ckvn = _mm_nt(dkv, wukv)
        gv_ref[1:2, 0:KV_LORA] += _csum(d_ckvn * ckx)
        gv_ref[3:4, 0:LANES] += ggk
        dckv_ref[...] = _rms_bwd(d_ckvn * kvg, ckx, rk0, KV_LORA).astype(BF16)

    return pl.pallas_call(
        body, name=name, grid=(s // ts,),
        in_specs=_mla_specs(ts) + [_row_spec(ts, QW), _row_spec(ts, QW), _row_spec(ts, MLA_W)],
        out_specs=[_row_spec(ts, Q_LORA), _row_spec(ts, KV_LORA), _row_spec(ts, LANES),
                   _full_spec((Q_LORA, QW)), _full_spec((KV_LORA, KVW)), _full_spec((8, QW))],
        out_shape=[jax.ShapeDtypeStruct((s, Q_LORA), BF16), jax.ShapeDtypeStruct((s, KV_LORA), BF16),
                   jax.ShapeDtypeStruct((s, LANES), BF16), jax.ShapeDtypeStruct((Q_LORA, QW), F32),
                   jax.ShapeDtypeStruct((KV_LORA, KVW), F32), jax.ShapeDtypeStruct((8, QW), F32)],
        scratch_shapes=[pltpu.VMEM((ts, KVW), F32)],
        compiler_params=_cp(("arbitrary",)),
    )(proj, proj, proj, rc, rs, qg, wuq, kvg, wukv, gq, gk, dq, dk, dv)


PAIRS = HEADS // 2
ATT_STRIP = 32


def _attn_fwd(q, k, v, name):
    s = q.shape[0]
    tq = _tile(s)
    tk = tq

    def body(q_ref, k_ref, v_ref, o_ref, lse_ref, s0_ref, s1_ref, p0_ref, p1_ref, m_ref, l_ref, acc_ref):
        i = pl.program_id(1)
        s_refs, p_refs = (s0_ref, s1_ref), (p0_ref, p1_ref)
        row = lax.broadcasted_iota(jnp.int32, (ATT_STRIP, tk), 0)
        col = lax.broadcasted_iota(jnp.int32, (ATT_STRIP, tk), 1)
        first = lax.broadcasted_iota(jnp.int32, (tq, LANES), 1) < V_DIM
        m_ref[...] = jnp.full(m_ref.shape, NEG, F32)
        l_ref[...] = jnp.zeros(l_ref.shape, F32)
        acc_ref[...] = jnp.zeros(acc_ref.shape, F32)

        def blk(j, masked):
            st = pl.multiple_of(j * tk, tk)
            for a in range(2):
                s_refs[a][...] = _mm_nt(q_ref[:, a * LANES:(a + 1) * LANES],
                                        k_ref[pl.ds(st, tk), a * LANES:(a + 1) * LANES])
            for a in range(2):
                for r in range(0, tq, ATT_STRIP):
                    sc = s_refs[a][r:r + ATT_STRIP, :]
                    if masked:
                        sc = jnp.where(col <= row + r, sc, NEG)
                    m_old = m_ref[a, r:r + ATT_STRIP, :]
                    m_new = jnp.maximum(m_old, jnp.max(sc, axis=-1, keepdims=True))
                    alpha = jnp.exp(m_old - m_new)
                    p = jnp.exp(sc - jnp.tile(m_new, (1, tk // LANES)))
                    l_ref[a, r:r + ATT_STRIP, :] = alpha * l_ref[a, r:r + ATT_STRIP, :] + _rsum(p)
                    acc_ref[a, r:r + ATT_STRIP, :] = alpha * acc_ref[a, r:r + ATT_STRIP, :]
                    m_ref[a, r:r + ATT_STRIP, :] = m_new
                    p_refs[a][r:r + ATT_STRIP, :] = p.astype(BF16)
                acc_ref[a] += jnp.dot(p_refs[a][...], v_ref[pl.ds(st, tk), :], preferred_element_type=F32)

        def two_blocks(t, carry):
            blk(2 * t, False)
            blk(2 * t + 1, False)
            return carry

        lax.fori_loop(0, i // 2, two_blocks, 0)

        @pl.when(i % 2 == 1)
        def _():
            blk(i - 1, False)

        blk(i, True)
        o_ref[...] = jnp.where(first, acc_ref[0] / l_ref[0], acc_ref[1] / l_ref[1])
        lse_ref[...] = jnp.where(first, m_ref[0] + jnp.log(l_ref[0]), m_ref[1] + jnp.log(l_ref[1]))

    stat = pltpu.VMEM((2, tq, LANES), F32)
    return pl.pallas_call(
        body, name=name, grid=(PAIRS, s // tq),
        in_specs=[pl.BlockSpec((tq, 2 * LANES), lambda p, i: (i, p)),
                  pl.BlockSpec((s, 2 * LANES), lambda p, i: (0, p)),
                  pl.BlockSpec((s, LANES), lambda p, i: (0, p))],
        out_specs=[pl.BlockSpec((tq, LANES), lambda p, i: (i, p)), pl.BlockSpec((tq, LANES), lambda p, i: (i, p))],
        out_shape=[jax.ShapeDtypeStruct((s, MLA_W), F32), jax.ShapeDtypeStruct((s, MLA_W), F32)],
        scratch_shapes=[pltpu.VMEM((tq, tk), F32), pltpu.VMEM((tq, tk), F32), pltpu.VMEM((tq, tk), BF16),
                        pltpu.VMEM((tq, tk), BF16), stat, stat, stat],
        compiler_params=_cp(("parallel", "parallel")),
    )(q, k, v)


def _attn_bwd(q, k, v, do, stats, name):
    s = q.shape[0]
    tq = _tile(s)
    tk = tq
    nq = s // tq

    def body(q_ref, k_ref, v_ref, do_ref, st_ref, dq_ref, dk_ref, dv_ref):
        j = pl.program_id(1)

        @pl.when(j == 0)
        def _():
            dq_ref[...] = jnp.zeros_like(dq_ref)

        dk_ref[...] = jnp.zeros_like(dk_ref)
        dv_ref[...] = jnp.zeros_like(dv_ref)
        row = lax.broadcasted_iota(jnp.int32, (tq, tk), 0)
        col = lax.broadcasted_iota(jnp.int32, (tq, tk), 1)
        lane = lax.broadcasted_iota(jnp.int32, (tq, LANES), 1)

        def blk(i, masked):
            st = i * tq if isinstance(i, int) else pl.multiple_of(i * tq, tq)
            do2 = do_ref[pl.ds(st, tq), :]
            stt = st_ref[pl.ds(st, tq), :]
            dv = None
            for a in range(2):
                mine = (lane < V_DIM) if a == 0 else (lane >= V_DIM)
                qa = q_ref[pl.ds(st, tq), a * LANES:(a + 1) * LANES]
                doa = jnp.where(mine, do2, jnp.zeros((), BF16))
                lse = stt[:, a * V_DIM:a * V_DIM + 1]
                dl = stt[:, a * V_DIM + V_DIM // 2:a * V_DIM + V_DIM // 2 + 1]
                p = jnp.exp(_mm_nt(qa, k_ref[:, a * LANES:(a + 1) * LANES]) - lse)
                if masked:
                    p = jnp.where(col <= row, p, 0.0)
                ds = (p * (_mm_nt(doa, v_ref[...]) - dl)).astype(BF16)
                dva = _mm_tn(p, doa)
                dv = dva if dv is None else dv + dva
                dk_ref[:, a * LANES:(a + 1) * LANES] += _mm_tn(ds, qa)
                dq_ref[pl.ds(st, tq), a * LANES:(a + 1) * LANES] += jnp.dot(
                    ds, k_ref[:, a * LANES:(a + 1) * LANES], preferred_element_type=F32)
            dv_ref[...] += dv

        blk(j, True)
        rest = nq - 1 - j

        def two_blocks(t, carry):
            blk(j + 1 + 2 * t, False)
            blk(j + 2 + 2 * t, False)
            return carry

        lax.fori_loop(0, rest // 2, two_blocks, 0)

        @pl.when(rest % 2 == 1)
        def _():
            blk(nq - 1, False)

    return pl.pallas_call(
        body, name=name, grid=(PAIRS, s // tk),
        in_specs=[pl.BlockSpec((s, 2 * LANES), lambda p, j: (0, p)),
                  pl.BlockSpec((tk, 2 * LANES), lambda p, j: (j, p)),
                  pl.BlockSpec((tk, LANES), lambda p, j: (j, p)),
                  pl.BlockSpec((s, LANES), lambda p, j: (0, p)),
                  pl.BlockSpec((s, LANES), lambda p, j: (0, p))],
        out_specs=[pl.BlockSpec((s, 2 * LANES), lambda p, j: (0, p)),
                   pl.BlockSpec((tk, 2 * LANES), lambda p, j: (j, p)),
                   pl.BlockSpec((tk, LANES), lambda p, j: (j, p))],
        out_shape=[jax.ShapeDtypeStruct((s, QW), F32), jax.ShapeDtypeStruct((s, QW), F32),
                   jax.ShapeDtypeStruct((s, MLA_W), F32)],
        compiler_params=_cp(("parallel", "arbitrary")),
    )(q, k, v, do, stats)


BR = ((0, CONV_W), (CONV_W, CONV_W + MLA_W), (CONV_W + MLA_W, D_MODEL))


def _post_fwd(x, yc, o, proj, ys, bng, wout, name):
    s = x.shape[0]
    ts = _tile(s)

    def body(x_ref, yc_ref, o_ref, zm_ref, ys_ref, g_ref, w_ref, out_ref):
        ys3 = (yc_ref[...], o_ref[...] * _silu(zm_ref[...]), ys_ref[...])
        acc = x_ref[...]
        for (lo, hi), yb in zip(BR, ys3):
            yn, _ = _rms_fwd(yb, hi - lo)
            acc = acc + _mm(yn * g_ref[:, lo:hi], w_ref[lo:hi, :])
        out_ref[...] = acc

    return pl.pallas_call(
        body, name=name, grid=(s // ts,),
        in_specs=[_row_spec(ts, D_MODEL), _row_spec(ts, CONV_W), _row_spec(ts, MLA_W),
                  _row_spec(ts, MLA_W, C_ZM // MLA_W), _row_spec(ts, SG_W), _full_spec((1, D_MODEL)),
                  _full_spec((D_MODEL, D_MODEL))],
        out_specs=_row_spec(ts, D_MODEL),
        out_shape=jax.ShapeDtypeStruct((s, D_MODEL), F32),
        compiler_params=_cp(("parallel",)),
    )(x, yc, o, proj, ys, bng, wout)


def _post_bwd(d_out, yc, o, lse, proj, ys, bng, wout, name):
    s = d_out.shape[0]
    ts = _tile(s)

    def body(do_ref, yc_ref, o_ref, lse_ref, zm_ref, ys_ref, g_ref, w_ref,
             dyc_ref, dys_ref, dob_ref, dzm_ref, st_ref, gw_ref, gg_ref, yn_ref):
        i = pl.program_id(0)

        @pl.when(i == 0)
        def _():
            gw_ref[...] = jnp.zeros_like(gw_ref)
            gg_ref[...] = jnp.zeros_like(gg_ref)

        d_out_b = do_ref[...].astype(BF16)
        o = o_ref[...]
        zm = zm_ref[...]
        szm = _silu(zm)
        ys3 = (yc_ref[...], o * szm, ys_ref[...])
        d_ys = []
        for (lo, hi), yb in zip(BR, ys3):
            n = hi - lo
            yn, r = _rms_fwd(yb, n)
            g = g_ref[:, lo:hi]
            yn_ref[:, lo:hi] = (yn * g).astype(BF16)
            d_yn = _mm_nt(d_out_b, w_ref[lo:hi, :])
            gg_ref[:, lo:hi] += _csum(d_yn * yn)
            d_ys.append(_rms_bwd(d_yn * g, yn, r, n))
        gw_ref[...] += _mm_tn(yn_ref[...], d_out_b)
        dyc_ref[...] = d_ys[0]
        dys_ref[...] = d_ys[2]
        d_ym = d_ys[1]
        d_o = d_ym * szm
        dob_ref[...] = d_o.astype(BF16)
        dzm_ref[...] = (d_ym * o * _dsilu(zm)).astype(BF16)
        prod = d_o * o
        head = lax.broadcasted_iota(jnp.int32, (ts, MLA_W), 1) // V_DIM
        delta = jnp.zeros((ts, MLA_W), F32)
        for h in range(HEADS):
            delta = jnp.where(head == h, _rsum(jnp.where(head == h, prod, 0.0)), delta)
        lane = lax.broadcasted_iota(jnp.int32, (ts, MLA_W), 1)
        st_ref[...] = jnp.where(lane % V_DIM < V_DIM // 2, lse_ref[...], delta)

    return pl.pallas_call(
        body, name=name, grid=(s // ts,),
        in_specs=[_row_spec(ts, D_MODEL), _row_spec(ts, CONV_W), _row_spec(ts, MLA_W), _row_spec(ts, MLA_W),
                  _row_spec(ts, MLA_W, C_ZM // MLA_W), _row_spec(ts, SG_W), _full_spec((1, D_MODEL)),
                  _full_spec((D_MODEL, D_MODEL))],
        out_specs=[_row_spec(ts, CONV_W), _row_spec(ts, SG_W), _row_spec(ts, MLA_W), _row_spec(ts, MLA_W),
                   _row_spec(ts, MLA_W), _full_spec((D_MODEL, D_MODEL)), _full_spec((1, D_MODEL))],
        out_shape=[jax.ShapeDtypeStruct((s, CONV_W), F32), jax.ShapeDtypeStruct((s, SG_W), F32),
                   jax.ShapeDtypeStruct((s, MLA_W), BF16), jax.ShapeDtypeStruct((s, MLA_W), BF16),
                   jax.ShapeDtypeStruct((s, MLA_W), F32), jax.ShapeDtypeStruct((D_MODEL, D_MODEL), F32),
                   jax.ShapeDtypeStruct((1, D_MODEL), F32)],
        scratch_shapes=[pltpu.VMEM((ts, D_MODEL), BF16)],
        compiler_params=_cp(("arbitrary",)),
    )(d_out, yc, o, lse, proj, ys, bng, wout)


def _loss_head(y, target, name):
    s = y.shape[0]
    ts = _tile(s)
    nt = s // ts

    def body(y_ref, t_ref, dy_ref, l_ref, acc_ref):
        i = pl.program_id(0)

        @pl.when(i == 0)
        def _():
            acc_ref[...] = jnp.zeros_like(acc_ref)

        e = y_ref[...] - t_ref[...]
        dy_ref[...] = e * (1.0 / D_MODEL)
        sq = jnp.sum((e * e).reshape(ts // 8, 8, D_MODEL), axis=0)
        part = sq[:, 0:LANES]
        for c in range(LANES, D_MODEL, LANES):
            part = part + sq[:, c:c + LANES]
        acc_ref[...] += part

        @pl.when(i == nt - 1)
        def _():
            tot = jnp.sum(_rsum(acc_ref[...]), axis=0, keepdims=True) * (0.5 / D_MODEL)
            l_ref[...] = jnp.broadcast_to(tot, (8, LANES))

    return pl.pallas_call(
        body, name=name, grid=(nt,),
        in_specs=[_row_spec(ts, D_MODEL), _row_spec(ts, D_MODEL)],
        out_specs=[_row_spec(ts, D_MODEL), _full_spec((8, LANES))],
        out_shape=[jax.ShapeDtypeStruct((s, D_MODEL), F32), jax.ShapeDtypeStruct((8, LANES), F32)],
        scratch_shapes=[pltpu.VMEM((8, LANES), F32)],
        compiler_params=_cp(("arbitrary",)),
    )(y, target)


MESH = pl.DeviceIdType.MESH
ANY = pl.BlockSpec(memory_space=pl.ANY)


AG_COPIES = N_DEV - 1


def _ag_steps(x_refs, out_refs, send_sems, recv_sems, local_sems):
    n = len(x_refs)
    x, y, c = lax.axis_index("x"), lax.axis_index("y"), lax.axis_index("c")
    me, sibling = (x, y, c), (x, y, 1 - c)
    chips = [(1 - x, y), (x, 1 - y), (1 - x, 1 - y)]

    def slot(t, px, py, pc):
        return out_refs[t].at[4 * px + 2 * py + pc]

    def copy(t, k, block, to, src=None):
        return pltpu.make_async_remote_copy(
            src_ref=slot(t, *block) if src is None else src, dst_ref=slot(t, *block),
            send_sem=send_sems.at[t * AG_COPIES + k], recv_sem=recv_sems.at[t * AG_COPIES + k], device_id=to,
            device_id_type=MESH)

    mine = [pltpu.make_async_copy(x_refs[t], slot(t, *me), local_sems.at[t]) for t in range(n)]
    first = [copy(t, 0, me, sibling, src=x_refs[t]) for t in range(n)]
    first += [copy(t, 1 + j, me, (*chip, c), src=x_refs[t]) for j, chip in enumerate(chips) for t in range(n)]
    passed = [copy(t, 4 + j, (*chip, c), sibling) for j, chip in enumerate(chips) for t in range(n)]

    def start():
        for cp in mine + first:
            cp.start()

    def pass_on():
        for j, chip in enumerate(chips):
            for t in range(n):
                copy(t, 1 + j, (*chip, c), me).wait_recv()
                passed[j * n + t].start()

    def finish():
        for t in range(n):
            copy(t, 0, sibling, me).wait_recv()
        for j, chip in enumerate(chips):
            for t in range(n):
                copy(t, 4 + j, (*chip, 1 - c), me).wait_recv()
        for cp in first + passed:
            cp.wait_send()
        for cp in mine:
            cp.wait()

    return start, pass_on, finish


def _ag_scratch(n):
    return [pltpu.SemaphoreType.DMA((AG_COPIES * n,)), pltpu.SemaphoreType.DMA((AG_COPIES * n,)),
            pltpu.SemaphoreType.DMA((n,))]


def _all_gather(xs, name):
    n = len(xs)

    def body(*refs):
        for step in _ag_steps(refs[:n], refs[n:2 * n], *refs[2 * n:]):
            step()

    return pl.pallas_call(
        body, name=name,
        out_shape=[jax.ShapeDtypeStruct((N_DEV,) + a.shape, a.dtype) for a in xs],
        in_specs=[ANY] * n, out_specs=[ANY] * n,
        scratch_shapes=_ag_scratch(n),
    )(*xs)


N_CHIP = N_DEV // 2


def _grad_to_sibling(gss, gr, name):
    n = len(gss)

    def body(*refs):
        gs_refs, gr_ref = refs[:n], refs[n]
        os_refs, or_ref = refs[n + 1:2 * n + 1], refs[2 * n + 1]
        send_sems, recv_sems = refs[2 * n + 2:]
        x, y, c = lax.axis_index("x"), lax.axis_index("y"), lax.axis_index("c")
        copies = []
        for t in range(n + 1):
            copies.append(pltpu.make_async_remote_copy(
                src_ref=gs_refs[t].at[1 - c] if t < n else gr_ref, dst_ref=os_refs[t] if t < n else or_ref,
                send_sem=send_sems.at[t], recv_sem=recv_sems.at[t], device_id=(x, y, 1 - c), device_id_type=MESH))
        for cp in copies:
            cp.start()
        for cp in copies:
            cp.wait_recv()
        for cp in copies:
            cp.wait_send()

    return pl.pallas_call(
        body, name=name,
        out_shape=[jax.ShapeDtypeStruct(g.shape[1:], g.dtype) for g in gss] + [jax.ShapeDtypeStruct(gr.shape, gr.dtype)],
        in_specs=[ANY] * (n + 1), out_specs=[ANY] * (n + 1),
        scratch_shapes=[pltpu.SemaphoreType.DMA((n + 1,)), pltpu.SemaphoreType.DMA((n + 1,))],
    )(*gss, gr)


def _chip_sum(mine, theirs, gr, gr_theirs, name):
    n = len(mine)

    def body(*refs):
        a_refs, b_refs = refs[:n + 1], refs[n + 1:2 * n + 2]
        o_refs = refs[2 * n + 2:]
        for a_ref, b_ref, o_ref in zip(a_refs, b_refs, o_refs):
            o_ref[...] = (a_ref[...].astype(F32) + b_ref[...].astype(F32)).astype(o_ref.dtype)

    def spec(a):
        if a.ndim == 3:
            return pl.BlockSpec((1,) + a.shape[1:], lambda i: (i, 0, 0))
        return pl.BlockSpec(a.shape, lambda i: (0, 0))

    ins = list(mine) + [gr] + list(theirs) + [gr_theirs]
    return pl.pallas_call(
        body, name=name, grid=(N_CHIP,),
        in_specs=[spec(a) for a in ins], out_specs=[spec(a) for a in ins[:n + 1]],
        out_shape=[jax.ShapeDtypeStruct(a.shape, a.dtype) for a in ins[:n + 1]],
        compiler_params=_cp(("arbitrary",)),
    )(*ins)


def _grad_to_chips(gss, gr, name):
    n = len(gss)
    per = N_CHIP - 1

    def body(*refs):
        gs_refs, gr_ref = refs[:n], refs[n]
        os_refs, or_ref = refs[n + 1:2 * n + 1], refs[2 * n + 1]
        send_sems, recv_sems, local_sems = refs[2 * n + 2:]
        x, y, c = lax.axis_index("x"), lax.axis_index("y"), lax.axis_index("c")
        me = 2 * x + y
        local = [pltpu.make_async_copy(gs_refs[t].at[me], os_refs[t].at[me], local_sems.at[t]) for t in range(n)]
        local.append(pltpu.make_async_copy(gr_ref, or_ref.at[me], local_sems.at[n]))
        for cp in local:
            cp.start()
        sends, recvs = [], []
        for k in range(1, N_CHIP):
            px = 1 - x if k & 2 else x
            py = 1 - y if k & 1 else y
            peer = 2 * px + py
            for t in range(n + 1):
                sems = dict(send_sem=send_sems.at[t * per + k - 1], recv_sem=recv_sems.at[t * per + k - 1],
                            device_id=(px, py, c), device_id_type=MESH)
                src = gs_refs[t].at[peer] if t < n else gr_ref
                out = os_refs[t] if t < n else or_ref
                sends.append(pltpu.make_async_remote_copy(src_ref=src, dst_ref=out.at[me], **sems))
                recvs.append(pltpu.make_async_remote_copy(src_ref=src, dst_ref=out.at[peer], **sems))
        for cp in sends:
            cp.start()
        for cp in recvs:
            cp.wait_recv()
        for cp in sends:
            cp.wait_send()
        for cp in local:
            cp.wait()

    nsem = per * (n + 1)
    return pl.pallas_call(
        body, name=name,
        out_shape=[jax.ShapeDtypeStruct(g.shape, g.dtype) for g in gss]
        + [jax.ShapeDtypeStruct((N_CHIP,) + gr.shape, gr.dtype)],
        in_specs=[ANY] * (n + 1), out_specs=[ANY] * (n + 1),
        scratch_shapes=[pltpu.SemaphoreType.DMA((nsem,)), pltpu.SemaphoreType.DMA((nsem,)),
                        pltpu.SemaphoreType.DMA((n + 1,))],
    )(*gss, gr)


ADAM_ROWS = 128


def _adamw(parts, w, m, v, name):
    r, cols = w.shape
    tr = ADAM_ROWS if r % ADAM_ROWS == 0 else r
    n_parts = parts.shape[0]

    def body(p_ref, w_ref, m_ref, v_ref, g_ref, d_ref, nm_ref, nv_ref):
        g = p_ref[0].astype(F32)
        for sidx in range(1, n_parts):
            g = g + p_ref[sidx].astype(F32)
        mm = ADAM_B1 * m_ref[...] + (1.0 - ADAM_B1) * g
        vv = ADAM_B2 * v_ref[...] + (1.0 - ADAM_B2) * (g * g)
        m_hat = mm / (1.0 - ADAM_B1 ** ADAM_STEP)
        v_hat = vv / (1.0 - ADAM_B2 ** ADAM_STEP)
        g_ref[...] = g
        d_ref[...] = -ADAM_LR * (m_hat / (jnp.sqrt(v_hat) + ADAM_EPS) + ADAM_WD * w_ref[...])
        nm_ref[...] = mm
        nv_ref[...] = vv

    row = pl.BlockSpec((tr, cols), lambda i: (i, 0))
    return pl.pallas_call(
        body, name=name, grid=(r // tr,),
        in_specs=[pl.BlockSpec((n_parts, tr, cols), lambda i: (0, i, 0)), row, row, row],
        out_specs=[row, row, row, row],
        out_shape=[jax.ShapeDtypeStruct((r, cols), F32)] * 4,
        compiler_params=_cp(("parallel",)),
    )(parts, w, m, v)


PACK_W = 8 * LANES
BF16_ROWS = 16


def _pack(flat_parts, rows):
    flat = jnp.concatenate([p.reshape(-1) for p in flat_parts])
    return jnp.pad(flat, (0, rows * PACK_W - flat.shape[0])).reshape(rows, PACK_W)


def _rows_for(n, mult):
    rows = -(-n // PACK_W)
    return -(-rows // mult) * mult


def _unshard(arr8, axis):
    full = jnp.moveaxis(arr8, 0, axis)
    shp = list(full.shape)
    shp[axis:axis + 2] = [shp[axis] * shp[axis + 1]]
    return full.reshape(shp)


def _split_c_chip(full, axis):
    shp = list(full.shape)
    shp[axis:axis + 1] = [N_CHIP, 2, shp[axis] // N_DEV]
    return jnp.moveaxis(full.reshape(shp), (axis + 1, axis), (0, 1))


def _unpack(flat2d, shapes, lead=()):
    flat = flat2d.reshape(lead + (-1,))
    out, off = [], 0
    for shp in shapes:
        n = math.prod(shp)
        out.append(flat[..., off:off + n].reshape(lead + tuple(shp)))
        off += n
    return out


def _to_layout(w):
    return jnp.concatenate([w[:, :1536], w[:, 1824:2336], w[:, 1536:1792], w[:, 2336:3104], w[:, 1792:1824],
                            jnp.zeros((w.shape[0], NP - IN_COLS), w.dtype)], axis=1)


def _from_layout(g):
    return jnp.concatenate([g[:, :1536], g[:, C_CKV:C_CKV + KV_LORA], g[:, C_KR:C_KR + ROPE],
                            g[:, C_ZM:C_ZM + MLA_W], g[:, C_SG:C_SG + 3 * SG_W]], axis=1)


def _pad_heads(w, real):
    lead = w.shape[:-1]
    w = w.reshape(lead + (HEADS, real))
    return jnp.pad(w, [(0, 0)] * len(lead) + [(0, 0), (0, HEAD_PAD - real)]).reshape(lead + (QW,))


def _rope_tables(s):
    half = ROPE // 2
    inv_freq = ROPE_THETA ** (-jnp.arange(half, dtype=F32) / half)
    ang = jnp.arange(s, dtype=F32)[:, None] * inv_freq[None, :]
    cos, sin = jnp.cos(ang), jnp.sin(ang)
    ones = jnp.ones((s, NOPE), F32)
    zeros = jnp.zeros((s, NOPE), F32)
    pad = jnp.zeros((s, HEAD_PAD - QK), F32)
    rc = jnp.concatenate([ones, cos, cos, pad + 1.0], axis=1)
    rs = jnp.concatenate([zeros, -sin, sin, pad], axis=1)
    return rc, rs


def kernel(x, norm_g, w_in, conv_w, conv_b, conv_ln_g, conv_ln_b, conv_pw_w, conv_pw_b, q_norm_g, w_uq, kv_norm_g, w_ukv, qk_q_g, qk_k_g, sg_ln_g, sg_ln_b, sg_w, sg_b, branch_norm_g, w_out, loss_target, m_norm_g, m_w_in, m_conv_w, m_conv_b, m_conv_ln_g, m_conv_ln_b, m_conv_pw_w, m_conv_pw_b, m_q_norm_g, m_w_uq, m_kv_norm_g, m_w_ukv, m_qk_q_g, m_qk_k_g, m_sg_ln_g, m_sg_ln_b, m_sg_w, m_sg_b, m_branch_norm_g, m_w_out, v_norm_g, v_w_in, v_conv_w, v_conv_b, v_conv_ln_g, v_conv_ln_b, v_conv_pw_w, v_conv_pw_b, v_q_norm_g, v_w_uq, v_kv_norm_g, v_w_ukv, v_qk_q_g, v_qk_k_g, v_sg_ln_g, v_sg_ln_b, v_sg_w, v_sg_b, v_branch_norm_g, v_w_out):
    given = dict(locals())
    wts = {n: given[n] for n in W_NAMES}
    mom_m = {n: given['m_' + n] for n in W_NAMES}
    mom_v = {n: given['v_' + n] for n in W_NAMES}
    s = x.shape[1]
    xs = x.reshape(s, D_MODEL)
    target = loss_target.reshape(s, D_MODEL)

    rp_shapes = [wts[n].shape for n in REPL]
    rows_rp = _rows_for(sum(math.prod(p) for p in rp_shapes), BF16_ROWS)
    sh_shape = {n: wts[n].shape for n in SHARDED}
    sh_2d = {n: (sh_shape[n][0] * sh_shape[n][1], sh_shape[n][2]) for n in SHARDED}

    def vec(a, width=None):
        a = a.reshape(1, -1)
        return a if width is None else jnp.pad(a, ((0, 0), (0, width - a.shape[1])))

    def win_full(g):
        return jnp.moveaxis(g, 0, 1).reshape(g.shape[1], N_DEV * g.shape[2])

    later = [n for n in SHARDED if n != 'w_in']

    def shards(l):
        return [wts['w_in'][l].astype(BF16)] + [wts[n][l].astype(F32 if n == 'conv_w' else BF16) for n in later]

    def gathered_weights(g_win, g_later):
        full = {n: _unshard(g, SHARD_AXIS[n] - 1) for n, g in zip(later, g_later)}
        full['w_in'] = win_full(g_win)
        return full

    (g_win0,) = _all_gather(shards(0)[:1], "w_in0_all_gather")
    proj0, g_later0 = _proj_fwd(xs, vec(wts['norm_g'][0]), _to_layout(win_full(g_win0)), "proj_fwd_0",
                                gather=shards(0)[1:])

    rc, rs = _rope_tables(s)
    tril = jnp.tril(jnp.ones((SG_CHUNK, SG_CHUNK), dtype=bool))

    def layer_params(l, full):
        p = dict(full)
        p.update({n: wts[n][l] for n in REPL})
        wukv = p['w_ukv'].reshape(KV_LORA, HEADS, NOPE + V_DIM)
        wm = jnp.where(tril[None], p['sg_w'], 0.0)
        return (dict(
            ng=vec(p['norm_g']), win=_to_layout(p['w_in']).astype(BF16),
            cw=jnp.pad(p['conv_w'], ((0, HALO - CONV_K), (0, 0))), cb=vec(p['conv_b']), clg=vec(p['conv_ln_g']),
            clb=vec(p['conv_ln_b']), pww=p['conv_pw_w'].astype(BF16), pwb=vec(p['conv_pw_b']),
            qg=vec(p['q_norm_g']), wuq=_pad_heads(p['w_uq'], QK).astype(BF16), kvg=vec(p['kv_norm_g']),
            wukv=jnp.concatenate([_pad_heads(wukv[:, :, :NOPE].reshape(KV_LORA, HEADS * NOPE), NOPE),
                                  wukv[:, :, NOPE:].reshape(KV_LORA, MLA_W)], axis=1).astype(BF16),
            gq=vec(p['qk_q_g'], LANES), gk=vec(p['qk_k_g'], LANES),
            slg=vec(p['sg_ln_g']), slb=vec(p['sg_ln_b']), wm=wm.astype(BF16),
            wmt=jnp.swapaxes(wm, 1, 2).astype(BF16),
            sbx=jnp.repeat(p['sg_b'].T, SG_W // SG_HEADS, axis=1),
            bng=vec(p['branch_norm_g']), wout=p['w_out'].astype(BF16)))

    layers = [layer_params(0, gathered_weights(g_win0, g_later0))]

    acts = []
    h_in = xs
    for l in range(DEPTH):
        p = layers[l]
        proj = proj0 if l == 0 else _proj_fwd(h_in, p['ng'], p['win'], f"proj_fwd_{l}")[0]
        yc, cv = _conv_fwd(proj, p['cw'], p['cb'], p['clg'], p['clb'], p['pww'], p['pwb'], f"conv_fwd_{l}")
        ys = _sgu_fwd(proj, p['slg'], p['slb'], p['wm'], p['sbx'], f"sgu_fwd_{l}")
        q, k, v, g_next = _mla_fwd(proj, rc, rs, p['qg'], p['wuq'], p['kvg'], p['wukv'], p['gq'], p['gk'],
                                   f"mla_fwd_{l}", gather=shards(l + 1) if l + 1 < DEPTH else ())
        if l + 1 < DEPTH:
            layers.append(layer_params(l + 1, gathered_weights(g_next[0], g_next[1:])))
        o, lse = _attn_fwd(q, k, v, f"attn_fwd_{l}")
        h_out = _post_fwd(h_in, yc, o, proj, ys, p['bng'], p['wout'], f"post_fwd_{l}")
        acts.append(dict(x=h_in, proj=proj, yc=yc, cv=cv, ys=ys, q=q, k=k, v=v, o=o, lse=lse))
        h_in = h_out

    d_out, loss_blk = _loss_head(h_in, target, "loss_head")
    loss = lax.psum(loss_blk[0, 0], ("x", "y", "c"))

    grads = {n: [None] * DEPTH for n in W_NAMES}
    for l in reversed(range(DEPTH)):
        p, a = layers[l], acts[l]
        d_yc, d_ys, d_o, d_zm, stats, g_wout, g_bng = _post_bwd(
            d_out, a['yc'], a['o'], a['lse'], a['proj'], a['ys'], p['bng'], p['wout'], f"post_bwd_{l}")
        dq, dk, dv = _attn_bwd(a['q'], a['k'], a['v'], d_o, stats, f"attn_bwd_{l}")
        d_a, g_cw, g_pww, gv_c = _conv_bwd(a['proj'], a['cv'], d_yc, p['cw'], p['clg'], p['clb'], p['pww'], p['pwb'],
                                           f"conv_bwd_{l}")
        d_sg, g_wm, dms, gv_s = _sgu_bwd(a['proj'], d_ys, p['slg'], p['slb'], p['wm'], p['wmt'], p['sbx'],
                                         f"sgu_bwd_{l}")
        d_cq, d_ckv, d_kr, g_wuq, g_wukv, gv_m = _mla_bwd(
            a['proj'], rc, rs, p['qg'], p['wuq'], p['kvg'], p['wukv'], p['gq'], p['gk'], dq, dk, dv, f"mla_bwd_{l}")
        pieces = [(d_a, C_A), (d_cq, C_CQ), (d_zm, C_ZM), (d_ckv, C_CKV), (d_sg, C_SG), (d_kr, C_KR)]
        d_x, h_t, g_ng = _proj_bwd(a['x'], p['ng'], p['win'], d_out, pieces, f"proj_bwd_{l}")
        g_win = _win_grad(h_t, pieces, f"win_grad_{l}")
        d_out = d_x

        grads['norm_g'][l] = g_ng[0]
        grads['w_in'][l] = _from_layout(g_win)
        grads['conv_w'][l] = g_cw[:CONV_K]
        grads['conv_b'][l] = gv_c[0]
        grads['conv_ln_g'][l] = gv_c[1]
        grads['conv_ln_b'][l] = gv_c[2]
        grads['conv_pw_w'][l] = g_pww
        grads['conv_pw_b'][l] = gv_c[3]
        grads['q_norm_g'][l] = gv_m[0, :Q_LORA]
        grads['w_uq'][l] = g_wuq.reshape(Q_LORA, HEADS, HEAD_PAD)[:, :, :QK].reshape(Q_LORA, HEADS * QK)
        grads['kv_norm_g'][l] = gv_m[1, :KV_LORA]
        grads['w_ukv'][l] = jnp.concatenate(
            [g_wukv[:, :QW].reshape(KV_LORA, HEADS, HEAD_PAD)[:, :, :NOPE],
             g_wukv[:, QW:].reshape(KV_LORA, HEADS, V_DIM)], axis=2).reshape(KV_LORA, HEADS * (NOPE + V_DIM))
        grads['qk_q_g'][l] = gv_m[2, :QK]
        grads['qk_k_g'][l] = gv_m[3, :QK]
        grads['sg_ln_g'][l] = gv_s[0]
        grads['sg_ln_b'][l] = gv_s[1]
        grads['sg_w'][l] = jnp.where(tril[None], g_wm, 0.0)
        grads['sg_b'][l] = dms.reshape(SG_CHUNK, SG_HEADS, SG_W // SG_HEADS).sum(axis=2).T
        grads['branch_norm_g'][l] = g_bng[0]
        grads['w_out'][l] = g_wout
    grad_x = d_out.reshape(x.shape)
    g_full = {n: jnp.stack(grads[n]) for n in W_NAMES}

    gss = [_split_c_chip(g_full[n].astype(BF16), SHARD_AXIS[n]).reshape((2, N_CHIP) + sh_2d[n]) for n in SHARDED]
    gr = _pack([g_full[n].astype(BF16) for n in REPL], rows_rp)
    *theirs, gr_theirs = _grad_to_sibling(gss, gr, "grad_to_sibling")
    my_c = lax.axis_index("c")
    mine = [lax.dynamic_index_in_dim(g, my_c, 0, keepdims=False) for g in gss]
    *chip_sh, chip_rp = _chip_sum(mine, theirs, gr, gr_theirs, "chip_sum")
    *parts_sh, parts_rp = _grad_to_chips(chip_sh, chip_rp, "grad_to_chips")
    res_sh = {n: _adamw(parts, wts[n].reshape(sh_2d[n]), mom_m[n].reshape(sh_2d[n]), mom_v[n].reshape(sh_2d[n]),
                        f"adamw_{n}") for n, parts in zip(SHARDED, parts_sh)}
    res_rp = _adamw(parts_rp, _pack([wts[n] for n in REPL], rows_rp), _pack([mom_m[n] for n in REPL], rows_rp),
                    _pack([mom_v[n] for n in REPL], rows_rp), "adamw_replicated")
    outs = []
    for kind in range(4):
        vals = {n: res_sh[n][kind].reshape(sh_shape[n]) for n in SHARDED}
        vals.update(zip(REPL, _unpack(res_rp[kind], rp_shapes)))
        outs.extend(vals[n] for n in W_NAMES)
    return (loss, grad_x, *outs)
```

```python
import functools
import math

import jax
import jax.numpy as jnp
from jax import lax
from jax.experimental import pallas as pl
from jax.experimental.pallas import tpu as pltpu

F32 = jnp.float32
BF16 = jnp.bfloat16

N_DEV = 8
DEPTH = 2
D_MODEL = 1024
CONV_W = 256
CONV_K = 31
HEADS = 8
NOPE = 64
ROPE = 32
QK = NOPE + ROPE
HEAD_PAD = 128
V_DIM = 64
MLA_W = HEADS * V_DIM
Q_LORA = 768
KV_LORA = 256
SG_W = 256
SG_HEADS = 4
SG_CHUNK = 128
ROPE_THETA = 10000.0
EPS = 1e-6
IN_COLS = 3104
NP = 3200
C_A, C_CQ, C_ZM, C_CKV, C_SG, C_KR = 0, 768, 1536, 2048, 2304, 3072
HALO = 32
SUB = 64
NEG = -1e30
LANES = 128
VMEM_LIMIT_V7X = 52 * 1024 * 1024

ADAM_LR = 0.001
ADAM_B1 = 0.9
ADAM_B2 = 0.999
ADAM_EPS = 1e-08
ADAM_WD = 0.01
ADAM_STEP = 10

W_NAMES = ['norm_g', 'w_in', 'conv_w', 'conv_b', 'conv_ln_g', 'conv_ln_b', 'conv_pw_w', 'conv_pw_b',
           'q_norm_g', 'w_uq', 'kv_norm_g', 'w_ukv', 'qk_q_g', 'qk_k_g', 'sg_ln_g', 'sg_ln_b', 'sg_w',
           'sg_b', 'branch_norm_g', 'w_out']
SHARD_AXIS = {'w_in': 2, 'conv_w': 2, 'conv_pw_w': 1, 'w_uq': 1, 'w_ukv': 2, 'w_out': 1}
SHARDED = [n for n in W_NAMES if n in SHARD_AXIS]
REPL = [n for n in W_NAMES if n not in SHARD_AXIS]


def _tile(s):
    for t in (512, 256, 128):
        if s % t == 0 and s // t >= 2:
            return t
    return s


def _cp(sem):
    return pltpu.CompilerParams(dimension_semantics=sem, vmem_limit_bytes=VMEM_LIMIT_V7X)


def _mm(a, b):
    return jnp.dot(a.astype(BF16), b.astype(BF16), preferred_element_type=F32)


def _mm_nt(a, b):
    return lax.dot_general(a.astype(BF16), b.astype(BF16), (((1,), (1,)), ((), ())),
                           preferred_element_type=F32)


def _mm_tn(a, b):
    return lax.dot_general(a.astype(BF16), b.astype(BF16), (((0,), (0,)), ((), ())),
                           preferred_element_type=F32)


_GC = math.sqrt(2.0 / math.pi)
_GA = 0.044715


def _sig(x):
    return 1.0 / (1.0 + jnp.exp(-x))


def _silu(x):
    return x * _sig(x)


def _dsilu(x):
    s = _sig(x)
    return s * (1.0 + x * (1.0 - s))


def _gelu(x):
    return 0.5 * x * (1.0 + jnp.tanh(_GC * (x + _GA * x * x * x)))


def _dgelu(x):
    t = jnp.tanh(_GC * (x + _GA * x * x * x))
    return 0.5 * (1.0 + t) + 0.5 * x * (1.0 - t * t) * _GC * (1.0 + 3.0 * _GA * x * x)


def _rsum(x):
    return jnp.sum(x, axis=-1, keepdims=True)


def _csum(x):
    return jnp.sum(x, axis=0, keepdims=True)


def _rms_fwd(x, n):
    r = lax.rsqrt(_rsum(x * x) * (1.0 / n) + EPS)
    return x * r, r


def _rms_bwd(dxh, xn, r, n):
    return r * (dxh - xn * (_rsum(dxh * xn) * (1.0 / n)))


def _ln_fwd(x, n):
    mu = _rsum(x) * (1.0 / n)
    xc = x - mu
    r = lax.rsqrt(_rsum(xc * xc) * (1.0 / n) + EPS)
    return xc * r, r


def _ln_bwd(dxh, xh, r, n):
    return r * (dxh - _rsum(dxh) * (1.0 / n) - xh * (_rsum(dxh * xh) * (1.0 / n)))


def _partner(x, lane):
    return jnp.where(lane < NOPE + ROPE // 2, pltpu.roll(x, LANES - ROPE // 2, 1), pltpu.roll(x, ROPE // 2, 1))


def _row_spec(ts, w, col=0):
    return pl.BlockSpec((ts, w), lambda i, col=col: (i, col))


def _full_spec(shape):
    nd = len(shape)
    return pl.BlockSpec(shape, lambda i, nd=nd: (0,) * nd)


PROJ_CHUNK = 640


def _proj_fwd(x, ng, win_p, name, gather=()):
    s = x.shape[0]
    ts = _tile(s)
    nt = s // ts
    n = len(gather)

    def body(x_ref, g_ref, w_ref, *rest):
        o_ref = rest[n]
        if n:
            i = pl.program_id(0)
            start, pass_on, finish = _ag_steps(rest[:n], rest[n + 1:2 * n + 1], *rest[2 * n + 1:])
            pl.when(i == 0)(start)
            pl.when(i == nt // 2)(pass_on)
        xv = x_ref[...]
        xn, _ = _rms_fwd(xv, D_MODEL)
        h = (xn * g_ref[...]).astype(BF16)
        for c in range(0, NP, PROJ_CHUNK):
            o_ref[:, c:c + PROJ_CHUNK] = jnp.dot(h, w_ref[:, c:c + PROJ_CHUNK], preferred_element_type=F32)
        if n:
            pl.when(i == nt - 1)(finish)

    out = pl.pallas_call(
        body, name=name, grid=(nt,),
        in_specs=[_row_spec(ts, D_MODEL), _full_spec((1, D_MODEL)), _full_spec((D_MODEL, NP))] + [ANY] * n,
        out_specs=[_row_spec(ts, NP)] + [ANY] * n,
        out_shape=[jax.ShapeDtypeStruct((s, NP), F32)]
        + [jax.ShapeDtypeStruct((N_DEV,) + a.shape, a.dtype) for a in gather],
        scratch_shapes=_ag_scratch(n) if n else [],
        compiler_params=_cp(("arbitrary",) if n else ("parallel",)),
    )(x, ng, win_p, *gather)
    return out[0], out[1:]


def _proj_bwd(x, ng, win_p, d_out, pieces, name):
    s = x.shape[0]
    ts = _tile(s)
    offs = [o for _, o in pieces]
    widths = [p.shape[1] for p, _ in pieces]

    def body(x_ref, g_ref, w_ref, do_ref, *rest):
        p_refs = rest[:len(pieces)]
        dx_ref, h_ref, gg_ref = rest[len(pieces):]
        i = pl.program_id(0)
        xv = x_ref[...]
        xn, r = _rms_fwd(xv, D_MODEL)
        g = g_ref[...]
        h_ref[...] = (xn * g).T.astype(BF16)
        dh = jnp.zeros((ts, D_MODEL), F32)
        for p_ref, off, w in zip(p_refs, offs, widths):
            dh = dh + _mm_nt(p_ref[...], w_ref[:, off:off + w])

        @pl.when(i == 0)
        def _():
            gg_ref[...] = jnp.zeros_like(gg_ref)

        gg_ref[...] += _csum(dh * xn)
        dx_ref[...] = _rms_bwd(dh * g, xn, r, D_MODEL) + do_ref[...]

    in_specs = [_row_spec(ts, D_MODEL), _full_spec((1, D_MODEL)), _full_spec((D_MODEL, NP)), _row_spec(ts, D_MODEL)]
    in_specs += [_row_spec(ts, w) for w in widths]
    return pl.pallas_call(
        body, name=name, grid=(s // ts,),
        in_specs=in_specs,
        out_specs=[_row_spec(ts, D_MODEL), pl.BlockSpec((D_MODEL, ts), lambda i: (0, i)), _full_spec((1, D_MODEL))],
        out_shape=[jax.ShapeDtypeStruct((s, D_MODEL), F32), jax.ShapeDtypeStruct((D_MODEL, s), BF16),
                   jax.ShapeDtypeStruct((1, D_MODEL), F32)],
        compiler_params=_cp(("arbitrary",)),
    )(x, ng, win_p, d_out, *[p for p, _ in pieces])


WG_TILE = 256


def _win_grad(ht, pieces, name):
    s = ht.shape[1]
    ts = min(WG_TILE, s)
    offs = [o for _, o in pieces]
    widths = [p.shape[1] for p, _ in pieces]

    def body(ht_ref, *rest):
        p_refs, o_ref = rest[:-1], rest[-1]

        @pl.when(pl.program_id(0) == 0)
        def _():
            o_ref[...] = jnp.zeros_like(o_ref)

        hb = ht_ref[...]
        for p_ref, off, w in zip(p_refs, offs, widths):
            o_ref[:, off:off + w] += jnp.dot(hb, p_ref[...].astype(BF16), preferred_element_type=F32)

    return pl.pallas_call(
        body, name=name, grid=(s // ts,),
        in_specs=[pl.BlockSpec((D_MODEL, ts), lambda i: (0, i))] + [_row_spec(ts, w) for w in widths],
        out_specs=_full_spec((D_MODEL, NP)),
        out_shape=jax.ShapeDtypeStruct((D_MODEL, NP), F32),
        compiler_params=_cp(("arbitrary",)),
    )(ht, *[p for p, _ in pieces])


def _halo_spec(ts):
    per = ts // HALO
    return pl.BlockSpec((HALO, 2 * CONV_W), lambda i: (jnp.maximum(i * per - 1, 0), 0))


def _conv_taps(ext_ref, cw_ref, cv_ref, cb, ts):
    base = HALO - (CONV_K - 1)
    for r0 in range(0, ts, SUB):
        acc = jnp.zeros((SUB, CONV_W), F32)
        for k in range(CONV_K):
            acc = acc + cw_ref[k:k + 1, :] * ext_ref[r0 + base + k:r0 + base + k + SUB, :]
        cv_ref[r0:r0 + SUB, :] = acc + cb


def _conv_fwd(proj, cw, cb, lg, lb, pww, pwb, name):
    s = proj.shape[0]
    ts = _tile(s)

    def body(pa_ref, ph_ref, cw_ref, cb_ref, lg_ref, lb_ref, pww_ref, pwb_ref, y_ref, cv_ref, ext_ref):
        i = pl.program_id(0)
        pa = pa_ref[...]
        a, ag, zc = pa[:, :CONV_W], pa[:, CONV_W:2 * CONV_W], pa[:, 2 * CONV_W:]
        ph = ph_ref[...]
        hglu = ph[:, :CONV_W] * _sig(ph[:, CONV_W:])
        ext_ref[0:HALO, :] = jnp.where(i > 0, hglu, 0.0)
        ext_ref[HALO:HALO + ts, :] = a * _sig(ag)
        _conv_taps(ext_ref, cw_ref, cv_ref, cb_ref[...], ts)
        xh, _ = _ln_fwd(cv_ref[...], CONV_W)
        ln = xh * lg_ref[...] + lb_ref[...]
        pw = _mm(_silu(ln), pww_ref[...]) + pwb_ref[...]
        y_ref[...] = pw * _silu(zc)

    vec = _full_spec((1, CONV_W))
    return pl.pallas_call(
        body, name=name, grid=(s // ts,),
        in_specs=[_row_spec(ts, 3 * CONV_W, 0), _halo_spec(ts), _full_spec((HALO, CONV_W)), vec, vec, vec,
                  _full_spec((CONV_W, CONV_W)), vec],
        out_specs=[_row_spec(ts, CONV_W), _row_spec(ts, CONV_W)],
        out_shape=[jax.ShapeDtypeStruct((s, CONV_W), F32), jax.ShapeDtypeStruct((s, CONV_W), F32)],
        scratch_shapes=[pltpu.VMEM((HALO + ts, CONV_W), F32)],
        compiler_params=_cp(("parallel",)),
    )(proj, proj, cw, cb, lg, lb, pww, pwb)


def _conv_bwd(proj, cv, dy, cw, lg, lb, pww, pwb, name):
    s = proj.shape[0]
    ts = _tile(s)
    nt = s // ts
    per = ts // HALO

    def body(pa_ref, ph_ref, cv_ref, dy_ref, cw_ref, lg_ref, lb_ref, pww_ref, pwb_ref,
             dp_ref, gcw_ref, gpw_ref, gv_ref, ext_ref, dext_ref, carry_ref, gacc_ref):
        i = pl.program_id(0)
        ti = nt - 1 - i

        @pl.when(i == 0)
        def _():
            carry_ref[...] = jnp.zeros_like(carry_ref)
            gacc_ref[...] = jnp.zeros_like(gacc_ref)
            gpw_ref[...] = jnp.zeros_like(gpw_ref)
            gv_ref[...] = jnp.zeros_like(gv_ref)

        pa = pa_ref[...]
        a, ag, zc = pa[:, :CONV_W], pa[:, CONV_W:2 * CONV_W], pa[:, 2 * CONV_W:]
        sag = _sig(ag)
        ph = ph_ref[...]
        hglu = ph[:, :CONV_W] * _sig(ph[:, CONV_W:])
        ext_ref[0:HALO, :] = jnp.where(ti > 0, hglu, 0.0)
        ext_ref[HALO:HALO + ts, :] = a * sag
        xh, rl = _ln_fwd(cv_ref[...], CONV_W)
        lg = lg_ref[...]
        ln = xh * lg + lb_ref[...]
        sw = _silu(ln)
        pww = pww_ref[...]
        pw = _mm(sw, pww) + pwb_ref[...]
        d_y = dy_ref[...]
        d_pw = d_y * _silu(zc)
        d_zc = d_y * pw * _dsilu(zc)
        gpw_ref[...] += _mm_tn(sw, d_pw)
        d_ln = _mm_nt(d_pw, pww) * _dsilu(ln)
        d_cv = _ln_bwd(d_ln * lg, xh, rl, CONV_W)
        gv_ref[0:1, :] += _csum(d_cv)
        gv_ref[1:2, :] += _csum(d_ln * xh)
        gv_ref[2:3, :] += _csum(d_ln)
        gv_ref[3:4, :] += _csum(d_pw)
        dext_ref[0:ts, :] = d_cv
        dext_ref[ts:ts + HALO, :] = carry_ref[...]
        carry_ref[...] = d_cv[0:HALO, :]
        base = HALO - (CONV_K - 1)
        for r0 in range(0, ts, SUB):
            dcv_r = dext_ref[r0:r0 + SUB, :]
            dg = jnp.zeros((SUB, CONV_W), F32)
            for k in range(CONV_K):
                prod = dcv_r * ext_ref[r0 + base + k:r0 + base + k + SUB, :]
                gacc_ref[8 * k:8 * k + 8, :] += jnp.sum(prod.reshape(SUB // 8, 8, CONV_W), axis=0)
                dg = dg + cw_ref[k:k + 1, :] * dext_ref[r0 + CONV_K - 1 - k:r0 + CONV_K - 1 - k + SUB, :]
            sg_r, a_r = sag[r0:r0 + SUB, :], a[r0:r0 + SUB, :]
            dp_ref[r0:r0 + SUB, 0:CONV_W] = (dg * sg_r).astype(BF16)
            dp_ref[r0:r0 + SUB, CONV_W:2 * CONV_W] = (dg * a_r * sg_r * (1.0 - sg_r)).astype(BF16)
        dp_ref[:, 2 * CONV_W:] = d_zc.astype(BF16)

        @pl.when(i == nt - 1)
        def _():
            gcw_ref[...] = jnp.zeros_like(gcw_ref)
            for k in range(CONV_K):
                gcw_ref[k:k + 1, :] = _csum(gacc_ref[8 * k:8 * k + 8, :])

    vec = _full_spec((1, CONV_W))
    rev = lambda w, col=0: pl.BlockSpec((ts, w), lambda i, col=col: (nt - 1 - i, col))
    halo = pl.BlockSpec((HALO, 2 * CONV_W), lambda i: (jnp.maximum((nt - 1 - i) * per - 1, 0), 0))
    return pl.pallas_call(
        body, name=name, grid=(nt,),
        in_specs=[rev(3 * CONV_W), halo, rev(CONV_W), rev(CONV_W), _full_spec((HALO, CONV_W)), vec, vec,
                  _full_spec((CONV_W, CONV_W)), vec],
        out_specs=[rev(3 * CONV_W), _full_spec((HALO, CONV_W)), _full_spec((CONV_W, CONV_W)), _full_spec((8, CONV_W))],
        out_shape=[jax.ShapeDtypeStruct((s, 3 * CONV_W), BF16), jax.ShapeDtypeStruct((HALO, CONV_W), F32),
                   jax.ShapeDtypeStruct((CONV_W, CONV_W), F32), jax.ShapeDtypeStruct((8, CONV_W), F32)],
        scratch_shapes=[pltpu.VMEM((HALO + ts, CONV_W), F32), pltpu.VMEM((ts + HALO, CONV_W), F32),
                        pltpu.VMEM((HALO, CONV_W), F32), pltpu.VMEM((8 * HALO, CONV_W), F32)],
        compiler_params=_cp(("arbitrary",)),
    )(proj, proj, cv, dy, cw, lg, lb, pww, pwb)


def _sg_mix(wm_ref, vc, head):
    out = jnp.zeros((SG_CHUNK, SG_W), F32)
    vb = vc.astype(BF16)
    for g in range(SG_HEADS):
        out = jnp.where(head == g, jnp.dot(wm_ref[g], vb, preferred_element_type=F32), out)
    return out


def _sgu_fwd(proj, lg, lb, wm, sbx, name):
    s = proj.shape[0]
    ts = _tile(s)

    def body(ps_ref, lg_ref, lb_ref, wm_ref, sbx_ref, y_ref, mix_ref):
        ps = ps_ref[...]
        us, vs, zs = ps[:, :SG_W], ps[:, SG_W:2 * SG_W], ps[:, 2 * SG_W:]
        xh, _ = _ln_fwd(_gelu(vs), SG_W)
        vn = xh * lg_ref[...] + lb_ref[...]
        head = lax.broadcasted_iota(jnp.int32, (SG_CHUNK, SG_W), 1) // (SG_W // SG_HEADS)
        for c0 in range(0, ts, SG_CHUNK):
            mix_ref[c0:c0 + SG_CHUNK, :] = _sg_mix(wm_ref, vn[c0:c0 + SG_CHUNK, :], head) + sbx_ref[...]
        y_ref[...] = _gelu(us) * mix_ref[...] * _silu(zs)

    vec = _full_spec((1, SG_W))
    return pl.pallas_call(
        body, name=name, grid=(s // ts,),
        in_specs=[_row_spec(ts, 3 * SG_W, C_SG // (3 * SG_W)), vec, vec,
                  _full_spec((SG_HEADS, SG_CHUNK, SG_CHUNK)), _full_spec((SG_CHUNK, SG_W))],
        out_specs=_row_spec(ts, SG_W),
        out_shape=jax.ShapeDtypeStruct((s, SG_W), F32),
        scratch_shapes=[pltpu.VMEM((ts, SG_W), F32)],
        compiler_params=_cp(("parallel",)),
    )(proj, lg, lb, wm, sbx)


def _sgu_bwd(proj, dy, lg, lb, wm, wmt, sbx, name):
    s = proj.shape[0]
    ts = _tile(s)

    def body(ps_ref, dy_ref, lg_ref, lb_ref, wm_ref, wmt_ref, sbx_ref,
             dp_ref, gwm_ref, dms_ref, gv_ref, mix_ref, dvn_ref):
        i = pl.program_id(0)

        @pl.when(i == 0)
        def _():
            gwm_ref[...] = jnp.zeros_like(gwm_ref)
            dms_ref[...] = jnp.zeros_like(dms_ref)
            gv_ref[...] = jnp.zeros_like(gv_ref)

        ps = ps_ref[...]
        us, vs, zs = ps[:, :SG_W], ps[:, SG_W:2 * SG_W], ps[:, 2 * SG_W:]
        xh, rl = _ln_fwd(_gelu(vs), SG_W)
        lg = lg_ref[...]
        vn = xh * lg + lb_ref[...]
        head = lax.broadcasted_iota(jnp.int32, (SG_CHUNK, SG_W), 1) // (SG_W // SG_HEADS)
        for c0 in range(0, ts, SG_CHUNK):
            mix_ref[c0:c0 + SG_CHUNK, :] = _sg_mix(wm_ref, vn[c0:c0 + SG_CHUNK, :], head) + sbx_ref[...]
        mixed = mix_ref[...]
        u = _gelu(us)
        sz = _silu(zs)
        d_y = dy_ref[...]
        d_mixed = d_y * u * sz
        dp_ref[:, 0:SG_W] = (d_y * mixed * sz * _dgelu(us)).astype(BF16)
        dp_ref[:, 2 * SG_W:] = (d_y * u * mixed * _dsilu(zs)).astype(BF16)
        dms = jnp.zeros((SG_CHUNK, SG_W), F32)
        for c0 in range(0, ts, SG_CHUNK):
            dm = d_mixed[c0:c0 + SG_CHUNK, :]
            vc = vn[c0:c0 + SG_CHUNK, :]
            dms = dms + dm
            for g in range(SG_HEADS):
                gwm_ref[g] += _mm_nt(jnp.where(head == g, dm, 0.0), vc)
            dvn_ref[c0:c0 + SG_CHUNK, :] = _sg_mix(wmt_ref, dm, head)
        dms_ref[...] += dms
        d_vn = dvn_ref[...]
        gv_ref[0:1, :] += _csum(d_vn * xh)
        gv_ref[1:2, :] += _csum(d_vn)
        dp_ref[:, SG_W:2 * SG_W] = (_ln_bwd(d_vn * lg, xh, rl, SG_W) * _dgelu(vs)).astype(BF16)

    vec = _full_spec((1, SG_W))
    wspec = _full_spec((SG_HEADS, SG_CHUNK, SG_CHUNK))
    return pl.pallas_call(
        body, name=name, grid=(s // ts,),
        in_specs=[_row_spec(ts, 3 * SG_W, C_SG // (3 * SG_W)), _row_spec(ts, SG_W), vec, vec, wspec, wspec,
                  _full_spec((SG_CHUNK, SG_W))],
        out_specs=[_row_spec(ts, 3 * SG_W), wspec, _full_spec((SG_CHUNK, SG_W)), _full_spec((8, SG_W))],
        out_shape=[jax.ShapeDtypeStruct((s, 3 * SG_W), BF16), jax.ShapeDtypeStruct((SG_HEADS, SG_CHUNK, SG_CHUNK), F32),
                   jax.ShapeDtypeStruct((SG_CHUNK, SG_W), F32), jax.ShapeDtypeStruct((8, SG_W), F32)],
        scratch_shapes=[pltpu.VMEM((ts, SG_W), F32), pltpu.VMEM((ts, SG_W), F32)],
        compiler_params=_cp(("arbitrary",)),
    )(proj, dy, lg, lb, wm, wmt, sbx)


QW = HEADS * HEAD_PAD
KVW = QW + MLA_W
ATT_SCALE = QK ** -0.5


def _mla_specs(ts):
    return [_row_spec(ts, Q_LORA, C_CQ // Q_LORA), _row_spec(ts, KV_LORA, C_CKV // KV_LORA),
            _row_spec(ts, LANES, C_KR // LANES), _row_spec(ts, LANES), _row_spec(ts, LANES),
            _full_spec((1, Q_LORA)), _full_spec((Q_LORA, QW)), _full_spec((1, KV_LORA)), _full_spec((KV_LORA, KVW)),
            _full_spec((1, LANES)), _full_spec((1, LANES))]


def _mla_fwd(proj, rc, rs, qg, wuq, kvg, wukv, gq, gk, name, gather=()):
    s = proj.shape[0]
    ts = _tile(s)
    nt = s // ts
    n = len(gather)

    def body(cq_ref, ckv_ref, kr_ref, rc_ref, rs_ref, qg_ref, wuq_ref, kvg_ref, wukv_ref, gq_ref, gk_ref, *rest):
        q_ref, k_ref, v_ref = rest[n:n + 3]
        if n:
            i = pl.program_id(0)
            start, pass_on, finish = _ag_steps(rest[:n], rest[n + 3:2 * n + 3], *rest[2 * n + 3:])
            pl.when(i == 0)(start)
            pl.when(i == nt // 2)(pass_on)
        lane = lax.broadcasted_iota(jnp.int32, (ts, LANES), 1)
        c, sn = rc_ref[...], rs_ref[...]
        cqn, _ = _rms_fwd(cq_ref[...], Q_LORA)
        q0 = _mm(cqn * qg_ref[...], wuq_ref[...])
        gq = gq_ref[...]
        for h in range(HEADS):
            xn, _ = _rms_fwd(q0[:, h * LANES:(h + 1) * LANES], QK)
            qn = xn * gq
            q_ref[:, h * LANES:(h + 1) * LANES] = ((qn * c + _partner(qn, lane) * sn) * ATT_SCALE).astype(BF16)
        ckvn, _ = _rms_fwd(ckv_ref[...], KV_LORA)
        kv = _mm(ckvn * kvg_ref[...], wukv_ref[...])
        kr = pltpu.roll(kr_ref[...], NOPE, 1)
        gk = gk_ref[...]
        for h in range(HEADS):
            xn, _ = _rms_fwd(kv[:, h * LANES:(h + 1) * LANES] + kr, QK)
            kn = xn * gk
            k_ref[:, h * LANES:(h + 1) * LANES] = (kn * c + _partner(kn, lane) * sn).astype(BF16)
        v_ref[...] = kv[:, QW:].astype(BF16)
        if n:
            pl.when(i == nt - 1)(finish)

    out = pl.pallas_call(
        body, name=name, grid=(nt,),
        in_specs=_mla_specs(ts) + [ANY] * n,
        out_specs=[_row_spec(ts, QW), _row_spec(ts, QW), _row_spec(ts, MLA_W)] + [ANY] * n,
        out_shape=[jax.ShapeDtypeStruct((s, QW), BF16), jax.ShapeDtypeStruct((s, QW), BF16),
                   jax.ShapeDtypeStruct((s, MLA_W), BF16)]
        + [jax.ShapeDtypeStruct((N_DEV,) + a.shape, a.dtype) for a in gather],
        scratch_shapes=_ag_scratch(n) if n else [],
        compiler_params=_cp(("arbitrary",) if n else ("parallel",)),
    )(proj, proj, proj, rc, rs, qg, wuq, kvg, wukv, gq, gk, *gather)
    return out[0], out[1], out[2], out[3:]


def _mla_bwd(proj, rc, rs, qg, wuq, kvg, wukv, gq, gk, dq, dk, dv, name):
    s = proj.shape[0]
    ts = _tile(s)

    def body(cq_ref, ckv_ref, kr_ref, rc_ref, rs_ref, qg_ref, wuq_ref, kvg_ref, wukv_ref, gq_ref, gk_ref,
             dq_ref, dk_ref, dv_ref, dcq_ref, dckv_ref, dkr_ref, gwuq_ref, gwukv_ref, gv_ref, d0_ref):
        i = pl.program_id(0)

        @pl.when(i == 0)
        def _():
            gwuq_ref[...] = jnp.zeros_like(gwuq_ref)
            gwukv_ref[...] = jnp.zeros_like(gwukv_ref)
            gv_ref[...] = jnp.zeros_like(gv_ref)

        lane = lax.broadcasted_iota(jnp.int32, (ts, LANES), 1)
        c, sn = rc_ref[...], rs_ref[...]
        cq = cq_ref[...]
        cqx, rq0 = _rms_fwd(cq, Q_LORA)
        qg = qg_ref[...]
        cqn = cqx * qg
        wuq = wuq_ref[...]
        q0 = _mm(cqn, wuq)
        gq = gq_ref[...]
        ggq = jnp.zeros((1, LANES), F32)
        for h in range(HEADS):
            xn, r = _rms_fwd(q0[:, h * LANES:(h + 1) * LANES], QK)
            d = dq_ref[:, h * LANES:(h + 1) * LANES] * ATT_SCALE
            d_qn = d * c - _partner(d, lane) * sn
            ggq = ggq + _csum(d_qn * xn)
            d0_ref[:, h * LANES:(h + 1) * LANES] = _rms_bwd(d_qn * gq, xn, r, QK)
        dq0 = d0_ref[:, 0:QW]
        gwuq_ref[...] += _mm_tn(cqn, dq0)
        d_cqn = _mm_nt(dq0, wuq)
        gv_ref[0:1, 0:Q_LORA] += _csum(d_cqn * cqx)
        gv_ref[2:3, 0:LANES] += ggq
        dcq_ref[...] = _rms_bwd(d_cqn * qg, cqx, rq0, Q_LORA).astype(BF16)
        ckv = ckv_ref[...]
        ckx, rk0 = _rms_fwd(ckv, KV_LORA)
        kvg = kvg_ref[...]
        ckvn = ckx * kvg
        wukv = wukv_ref[...]
        kv = _mm(ckvn, wukv)
        kr = pltpu.roll(kr_ref[...], NOPE, 1)
        gk = gk_ref[...]
        ggk = jnp.zeros((1, LANES), F32)
        dkr = jnp.zeros((ts, LANES), F32)
        for h in range(HEADS):
            xn, r = _rms_fwd(kv[:, h * LANES:(h + 1) * LANES] + kr, QK)
            d = dk_ref[:, h * LANES:(h + 1) * LANES]
            d_kn = d * c - _partner(d, lane) * sn
            ggk = ggk + _csum(d_kn * xn)
            d_k0 = _rms_bwd(d_kn * gk, xn, r, QK)
            dkr = dkr + d_k0
            d0_ref[:, h * LANES:(h + 1) * LANES] = d_k0
        d0_ref[:, QW:KVW] = dv_ref[...]
        dkv = d0_ref[...]
        dkr_ref[...] = jnp.where(lane < ROPE, pltpu.roll(dkr, NOPE, 1), 0.0).astype(BF16)
        gwukv_ref[...] += _mm_tn(ckvn, dkv)
        d_ckvn = _mm_nt(dkv, wukv)
        gv_ref[1:2, 0:KV_LORA] += _csum(d_ckvn * ckx)
        gv_ref[3:4, 0:LANES] += ggk
        dckv_ref[...] = _rms_bwd(d_ckvn * kvg, ckx, rk0, KV_LORA).astype(BF16)

    return pl.pallas_call(
        body, name=name, grid=(s // ts,),
        in_specs=_mla_specs(ts) + [_row_spec(ts, QW), _row_spec(ts, QW), _row_spec(ts, MLA_W)],
        out_specs=[_row_spec(ts, Q_LORA), _row_spec(ts, KV_LORA), _row_spec(ts, LANES),
                   _full_spec((Q_LORA, QW)), _full_spec((KV_LORA, KVW)), _full_spec((8, QW))],
        out_shape=[jax.ShapeDtypeStruct((s, Q_LORA), BF16), jax.ShapeDtypeStruct((s, KV_LORA), BF16),
                   jax.ShapeDtypeStruct((s, LANES), BF16), jax.ShapeDtypeStruct((Q_LORA, QW), F32),
                   jax.ShapeDtypeStruct((KV_LORA, KVW), F32), jax.ShapeDtypeStruct((8, QW), F32)],
        scratch_shapes=[pltpu.VMEM((ts, KVW), F32)],
        compiler_params=_cp(("arbitrary",)),
    )(proj, proj, proj, rc, rs, qg, wuq, kvg, wukv, gq, gk, dq, dk, dv)


PAIRS = HEADS // 2
ATT_STRIP = 32


def _attn_fwd(q, k, v, name, gather=()):
    s = q.shape[0]
    tq = _tile(s)
    tk = tq
    nq = s // tq
    n = len(gather)

    def body(q_ref, k_ref, v_ref, *rest):
        o_ref, lse_ref = rest[n:n + 2]
        s0_ref, s1_ref, p0_ref, p1_ref, m_ref, l_ref, acc_ref = rest[2 * n + 2:2 * n + 9]
        i = pl.program_id(1)
        if n:
            pair = pl.program_id(0)
            start, pass_on, finish = _ag_steps(rest[:n], rest[n + 2:2 * n + 2], *rest[2 * n + 9:])
            pl.when((pair == 0) & (i == 0))(start)
            pl.when((pair == 0) & (i == nq - 1))(pass_on)
        s_refs, p_refs = (s0_ref, s1_ref), (p0_ref, p1_ref)
        row = lax.broadcasted_iota(jnp.int32, (ATT_STRIP, tk), 0)
        col = lax.broadcasted_iota(jnp.int32, (ATT_STRIP, tk), 1)
        first = lax.broadcasted_iota(jnp.int32, (tq, LANES), 1) < V_DIM
        m_ref[...] = jnp.full(m_ref.shape, NEG, F32)
        l_ref[...] = jnp.zeros(l_ref.shape, F32)
        acc_ref[...] = jnp.zeros(acc_ref.shape, F32)

        def blk(j, masked):
            st = pl.multiple_of(j * tk, tk)
            for a in range(2):
                s_refs[a][...] = _mm_nt(q_ref[:, a * LANES:(a + 1) * LANES],
                                        k_ref[pl.ds(st, tk), a * LANES:(a + 1) * LANES])
            for a in range(2):
                for r in range(0, tq, ATT_STRIP):
                    sc = s_refs[a][r:r + ATT_STRIP, :]
                    if masked:
                        sc = jnp.where(col <= row + r, sc, NEG)
                    m_old = m_ref[a, r:r + ATT_STRIP, :]
                    m_new = jnp.maximum(m_old, jnp.max(sc, axis=-1, keepdims=True))
                    alpha = jnp.exp(m_old - m_new)
                    p = jnp.exp(sc - jnp.tile(m_new, (1, tk // LANES)))
                    l_ref[a, r:r + ATT_STRIP, :] = alpha * l_ref[a, r:r + ATT_STRIP, :] + _rsum(p)
                    acc_ref[a, r:r + ATT_STRIP, :] = alpha * acc_ref[a, r:r + ATT_STRIP, :]
                    m_ref[a, r:r + ATT_STRIP, :] = m_new
                    p_refs[a][r:r + ATT_STRIP, :] = p.astype(BF16)
                acc_ref[a] += jnp.dot(p_refs[a][...], v_ref[pl.ds(st, tk), :], preferred_element_type=F32)

        def two_blocks(t, carry):
            blk(2 * t, False)
            blk(2 * t + 1, False)
            return carry

        lax.fori_loop(0, i // 2, two_blocks, 0)

        @pl.when(i % 2 == 1)
        def _():
            blk(i - 1, False)

        blk(i, True)
        o_ref[...] = jnp.where(first, acc_ref[0] / l_ref[0], acc_ref[1] / l_ref[1])
        lse_ref[...] = jnp.where(first, m_ref[0] + jnp.log(l_ref[0]), m_ref[1] + jnp.log(l_ref[1]))
        if n:
            pl.when((pair == PAIRS - 1) & (i == nq - 1))(finish)

    stat = pltpu.VMEM((2, tq, LANES), F32)
    out = pl.pallas_call(
        body, name=name, grid=(PAIRS, nq),
        in_specs=[pl.BlockSpec((tq, 2 * LANES), lambda p, i: (i, p)),
                  pl.BlockSpec((s, 2 * LANES), lambda p, i: (0, p)),
                  pl.BlockSpec((s, LANES), lambda p, i: (0, p))] + [ANY] * n,
        out_specs=[pl.BlockSpec((tq, LANES), lambda p, i: (i, p)), pl.BlockSpec((tq, LANES), lambda p, i: (i, p))]
        + [ANY] * n,
        out_shape=[jax.ShapeDtypeStruct((s, MLA_W), F32), jax.ShapeDtypeStruct((s, MLA_W), F32)]
        + [jax.ShapeDtypeStruct((N_DEV,) + a.shape, a.dtype) for a in gather],
        scratch_shapes=[pltpu.VMEM((tq, tk), F32), pltpu.VMEM((tq, tk), F32), pltpu.VMEM((tq, tk), BF16),
                        pltpu.VMEM((tq, tk), BF16), stat, stat, stat] + (_ag_scratch(n) if n else []),
        compiler_params=_cp(("arbitrary", "arbitrary") if n else ("parallel", "parallel")),
    )(q, k, v, *gather)
    return out[0], out[1], out[2:]


def _attn_bwd(q, k, v, do, stats, name):
    s = q.shape[0]
    tq = _tile(s)
    tk = tq
    nq = s // tq

    def body(q_ref, k_ref, v_ref, do_ref, st_ref, dq_ref, dk_ref, dv_ref):
        j = pl.program_id(1)

        @pl.when(j == 0)
        def _():
            dq_ref[...] = jnp.zeros_like(dq_ref)

        dk_ref[...] = jnp.zeros_like(dk_ref)
        dv_ref[...] = jnp.zeros_like(dv_ref)
        row = lax.broadcasted_iota(jnp.int32, (tq, tk), 0)
        col = lax.broadcasted_iota(jnp.int32, (tq, tk), 1)
        lane = lax.broadcasted_iota(jnp.int32, (tq, LANES), 1)

        def blk(i, masked):
            st = i * tq if isinstance(i, int) else pl.multiple_of(i * tq, tq)
            do2 = do_ref[pl.ds(st, tq), :]
            stt = st_ref[pl.ds(st, tq), :]
            dv = None
            for a in range(2):
                mine = (lane < V_DIM) if a == 0 else (lane >= V_DIM)
                qa = q_ref[pl.ds(st, tq), a * LANES:(a + 1) * LANES]
                doa = jnp.where(mine, do2, jnp.zeros((), BF16))
                lse = stt[:, a * V_DIM:a * V_DIM + 1]
                dl = stt[:, a * V_DIM + V_DIM // 2:a * V_DIM + V_DIM // 2 + 1]
                p = jnp.exp(_mm_nt(qa, k_ref[:, a * LANES:(a + 1) * LANES]) - lse)
                if masked:
                    p = jnp.where(col <= row, p, 0.0)
                ds = (p * (_mm_nt(doa, v_ref[...]) - dl)).astype(BF16)
                dva = _mm_tn(p, doa)
                dv = dva if dv is None else dv + dva
                dk_ref[:, a * LANES:(a + 1) * LANES] += _mm_tn(ds, qa)
                dq_ref[pl.ds(st, tq), a * LANES:(a + 1) * LANES] += jnp.dot(
                    ds, k_ref[:, a * LANES:(a + 1) * LANES], preferred_element_type=F32)
            dv_ref[...] += dv

        blk(j, True)
        rest = nq - 1 - j

        def two_blocks(t, carry):
            blk(j + 1 + 2 * t, False)
            blk(j + 2 + 2 * t, False)
            return carry

        lax.fori_loop(0, rest // 2, two_blocks, 0)

        @pl.when(rest % 2 == 1)
        def _():
            blk(nq - 1, False)

    return pl.pallas_call(
        body, name=name, grid=(PAIRS, s // tk),
        in_specs=[pl.BlockSpec((s, 2 * LANES), lambda p, j: (0, p)),
                  pl.BlockSpec((tk, 2 * LANES), lambda p, j: (j, p)),
                  pl.BlockSpec((tk, LANES), lambda p, j: (j, p)),
                  pl.BlockSpec((s, LANES), lambda p, j: (0, p)),
                  pl.BlockSpec((s, LANES), lambda p, j: (0, p))],
        out_specs=[pl.BlockSpec((s, 2 * LANES), lambda p, j: (0, p)),
                   pl.BlockSpec((tk, 2 * LANES), lambda p, j: (j, p)),
                   pl.BlockSpec((tk, LANES), lambda p, j: (j, p))],
        out_shape=[jax.ShapeDtypeStruct((s, QW), F32), jax.ShapeDtypeStruct((s, QW), F32),
                   jax.ShapeDtypeStruct((s, MLA_W), F32)],
        compiler_params=_cp(("parallel", "arbitrary")),
    )(q, k, v, do, stats)


BR = ((0, CONV_W), (CONV_W, CONV_W + MLA_W), (CONV_W + MLA_W, D_MODEL))


def _post_fwd(x, yc, o, proj, ys, bng, wout, name):
    s = x.shape[0]
    ts = _tile(s)

    def body(x_ref, yc_ref, o_ref, zm_ref, ys_ref, g_ref, w_ref, out_ref):
        ys3 = (yc_ref[...], o_ref[...] * _silu(zm_ref[...]), ys_ref[...])
        acc = x_ref[...]
        for (lo, hi), yb in zip(BR, ys3):
            yn, _ = _rms_fwd(yb, hi - lo)
            acc = acc + _mm(yn * g_ref[:, lo:hi], w_ref[lo:hi, :])
        out_ref[...] = acc

    return pl.pallas_call(
        body, name=name, grid=(s // ts,),
        in_specs=[_row_spec(ts, D_MODEL), _row_spec(ts, CONV_W), _row_spec(ts, MLA_W),
                  _row_spec(ts, MLA_W, C_ZM // MLA_W), _row_spec(ts, SG_W), _full_spec((1, D_MODEL)),
                  _full_spec((D_MODEL, D_MODEL))],
        out_specs=_row_spec(ts, D_MODEL),
        out_shape=jax.ShapeDtypeStruct((s, D_MODEL), F32),
        compiler_params=_cp(("parallel",)),
    )(x, yc, o, proj, ys, bng, wout)


def _post_bwd(d_out, yc, o, lse, proj, ys, bng, wout, name):
    s = d_out.shape[0]
    ts = _tile(s)

    def body(do_ref, yc_ref, o_ref, lse_ref, zm_ref, ys_ref, g_ref, w_ref,
             dyc_ref, dys_ref, dob_ref, dzm_ref, st_ref, gw_ref, gg_ref, yn_ref):
        i = pl.program_id(0)

        @pl.when(i == 0)
        def _():
            gw_ref[...] = jnp.zeros_like(gw_ref)
            gg_ref[...] = jnp.zeros_like(gg_ref)

        d_out_b = do_ref[...].astype(BF16)
        o = o_ref[...]
        zm = zm_ref[...]
        szm = _silu(zm)
        ys3 = (yc_ref[...], o * szm, ys_ref[...])
        d_ys = []
        for (lo, hi), yb in zip(BR, ys3):
            n = hi - lo
            yn, r = _rms_fwd(yb, n)
            g = g_ref[:, lo:hi]
            yn_ref[:, lo:hi] = (yn * g).astype(BF16)
            d_yn = _mm_nt(d_out_b, w_ref[lo:hi, :])
            gg_ref[:, lo:hi] += _csum(d_yn * yn)
            d_ys.append(_rms_bwd(d_yn * g, yn, r, n))
        gw_ref[...] += _mm_tn(yn_ref[...], d_out_b)
        dyc_ref[...] = d_ys[0]
        dys_ref[...] = d_ys[2]
        d_ym = d_ys[1]
        d_o = d_ym * szm
        dob_ref[...] = d_o.astype(BF16)
        dzm_ref[...] = (d_ym * o * _dsilu(zm)).astype(BF16)
        prod = d_o * o
        head = lax.broadcasted_iota(jnp.int32, (ts, MLA_W), 1) // V_DIM
        delta = jnp.zeros((ts, MLA_W), F32)
        for h in range(HEADS):
            delta = jnp.where(head == h, _rsum(jnp.where(head == h, prod, 0.0)), delta)
        lane = lax.broadcasted_iota(jnp.int32, (ts, MLA_W), 1)
        st_ref[...] = jnp.where(lane % V_DIM < V_DIM // 2, lse_ref[...], delta)

    return pl.pallas_call(
        body, name=name, grid=(s // ts,),
        in_specs=[_row_spec(ts, D_MODEL), _row_spec(ts, CONV_W), _row_spec(ts, MLA_W), _row_spec(ts, MLA_W),
                  _row_spec(ts, MLA_W, C_ZM // MLA_W), _row_spec(ts, SG_W), _full_spec((1, D_MODEL)),
                  _full_spec((D_MODEL, D_MODEL))],
        out_specs=[_row_spec(ts, CONV_W), _row_spec(ts, SG_W), _row_spec(ts, MLA_W), _row_spec(ts, MLA_W),
                   _row_spec(ts, MLA_W), _full_spec((D_MODEL, D_MODEL)), _full_spec((1, D_MODEL))],
        out_shape=[jax.ShapeDtypeStruct((s, CONV_W), F32), jax.ShapeDtypeStruct((s, SG_W), F32),
                   jax.ShapeDtypeStruct((s, MLA_W), BF16), jax.ShapeDtypeStruct((s, MLA_W), BF16),
                   jax.ShapeDtypeStruct((s, MLA_W), F32), jax.ShapeDtypeStruct((D_MODEL, D_MODEL), F32),
                   jax.ShapeDtypeStruct((1, D_MODEL), F32)],
        scratch_shapes=[pltpu.VMEM((ts, D_MODEL), BF16)],
        compiler_params=_cp(("arbitrary",)),
    )(d_out, yc, o, lse, proj, ys, bng, wout)


def _loss_head(y, target, name):
    s = y.shape[0]
    ts = _tile(s)
    nt = s // ts

    def body(y_ref, t_ref, dy_ref, l_ref, acc_ref):
        i = pl.program_id(0)

        @pl.when(i == 0)
        def _():
            acc_ref[...] = jnp.zeros_like(acc_ref)

        e = y_ref[...] - t_ref[...]
        dy_ref[...] = e * (1.0 / D_MODEL)
        sq = jnp.sum((e * e).reshape(ts // 8, 8, D_MODEL), axis=0)
        part = sq[:, 0:LANES]
        for c in range(LANES, D_MODEL, LANES):
            part = part + sq[:, c:c + LANES]
        acc_ref[...] += part

        @pl.when(i == nt - 1)
        def _():
            tot = jnp.sum(_rsum(acc_ref[...]), axis=0, keepdims=True) * (0.5 / D_MODEL)
            l_ref[...] = jnp.broadcast_to(tot, (8, LANES))

    return pl.pallas_call(
        body, name=name, grid=(nt,),
        in_specs=[_row_spec(ts, D_MODEL), _row_spec(ts, D_MODEL)],
        out_specs=[_row_spec(ts, D_MODEL), _full_spec((8, LANES))],
        out_shape=[jax.ShapeDtypeStruct((s, D_MODEL), F32), jax.ShapeDtypeStruct((8, LANES), F32)],
        scratch_shapes=[pltpu.VMEM((8, LANES), F32)],
        compiler_params=_cp(("arbitrary",)),
    )(y, target)


MESH = pl.DeviceIdType.MESH
ANY = pl.BlockSpec(memory_space=pl.ANY)


AG_COPIES = N_DEV - 1


def _ag_steps(x_refs, out_refs, send_sems, recv_sems, local_sems):
    n = len(x_refs)
    x, y, c = lax.axis_index("x"), lax.axis_index("y"), lax.axis_index("c")
    me, sibling = (x, y, c), (x, y, 1 - c)
    chips = [(1 - x, y), (x, 1 - y), (1 - x, 1 - y)]

    def slot(t, px, py, pc):
        return out_refs[t].at[4 * px + 2 * py + pc]

    def copy(t, k, block, to, src=None):
        return pltpu.make_async_remote_copy(
            src_ref=slot(t, *block) if src is None else src, dst_ref=slot(t, *block),
            send_sem=send_sems.at[t * AG_COPIES + k], recv_sem=recv_sems.at[t * AG_COPIES + k], device_id=to,
            device_id_type=MESH)

    mine = [pltpu.make_async_copy(x_refs[t], slot(t, *me), local_sems.at[t]) for t in range(n)]
    first = [copy(t, 0, me, sibling, src=x_refs[t]) for t in range(n)]
    first += [copy(t, 1 + j, me, (*chip, c), src=x_refs[t]) for j, chip in enumerate(chips) for t in range(n)]
    passed = [copy(t, 4 + j, (*chip, c), sibling) for j, chip in enumerate(chips) for t in range(n)]

    def start():
        for cp in mine + first:
            cp.start()

    def pass_on():
        for j, chip in enumerate(chips):
            for t in range(n):
                copy(t, 1 + j, (*chip, c), me).wait_recv()
                passed[j * n + t].start()

    def finish():
        for t in range(n):
            copy(t, 0, sibling, me).wait_recv()
        for j, chip in enumerate(chips):
            for t in range(n):
                copy(t, 4 + j, (*chip, 1 - c), me).wait_recv()
        for cp in first + passed:
            cp.wait_send()
        for cp in mine:
            cp.wait()

    return start, pass_on, finish


def _ag_scratch(n):
    return [pltpu.SemaphoreType.DMA((AG_COPIES * n,)), pltpu.SemaphoreType.DMA((AG_COPIES * n,)),
            pltpu.SemaphoreType.DMA((n,))]


def _all_gather(xs, name):
    n = len(xs)

    def body(*refs):
        for step in _ag_steps(refs[:n], refs[n:2 * n], *refs[2 * n:]):
            step()

    return pl.pallas_call(
        body, name=name,
        out_shape=[jax.ShapeDtypeStruct((N_DEV,) + a.shape, a.dtype) for a in xs],
        in_specs=[ANY] * n, out_specs=[ANY] * n,
        scratch_shapes=_ag_scratch(n),
    )(*xs)


N_CHIP = N_DEV // 2


def _grad_to_sibling(gss, gr, name):
    n = len(gss)

    def body(*refs):
        gs_refs, gr_ref = refs[:n], refs[n]
        os_refs, or_ref = refs[n + 1:2 * n + 1], refs[2 * n + 1]
        send_sems, recv_sems = refs[2 * n + 2:]
        x, y, c = lax.axis_index("x"), lax.axis_index("y"), lax.axis_index("c")
        copies = []
        for t in range(n + 1):
            copies.append(pltpu.make_async_remote_copy(
                src_ref=gs_refs[t].at[1 - c] if t < n else gr_ref, dst_ref=os_refs[t] if t < n else or_ref,
                send_sem=send_sems.at[t], recv_sem=recv_sems.at[t], device_id=(x, y, 1 - c), device_id_type=MESH))
        for cp in copies:
            cp.start()
        for cp in copies:
            cp.wait_recv()
        for cp in copies:
            cp.wait_send()

    return pl.pallas_call(
        body, name=name,
        out_shape=[jax.ShapeDtypeStruct(g.shape[1:], g.dtype) for g in gss] + [jax.ShapeDtypeStruct(gr.shape, gr.dtype)],
        in_specs=[ANY] * (n + 1), out_specs=[ANY] * (n + 1),
        scratch_shapes=[pltpu.SemaphoreType.DMA((n + 1,)), pltpu.SemaphoreType.DMA((n + 1,))],
    )(*gss, gr)


def _chip_sum(mine, theirs, gr, gr_theirs, name):
    n = len(mine)

    def body(*refs):
        a_refs, b_refs = refs[:n + 1], refs[n + 1:2 * n + 2]
        o_refs = refs[2 * n + 2:]
        for a_ref, b_ref, o_ref in zip(a_refs, b_refs, o_refs):
            o_ref[...] = (a_ref[...].astype(F32) + b_ref[...].astype(F32)).astype(o_ref.dtype)

    def spec(a):
        if a.ndim == 3:
            return pl.BlockSpec((1,) + a.shape[1:], lambda i: (i, 0, 0))
        return pl.BlockSpec(a.shape, lambda i: (0, 0))

    ins = list(mine) + [gr] + list(theirs) + [gr_theirs]
    return pl.pallas_call(
        body, name=name, grid=(N_CHIP,),
        in_specs=[spec(a) for a in ins], out_specs=[spec(a) for a in ins[:n + 1]],
        out_shape=[jax.ShapeDtypeStruct(a.shape, a.dtype) for a in ins[:n + 1]],
        compiler_params=_cp(("arbitrary",)),
    )(*ins)


def _grad_to_chips(gss, gr, name):
    n = len(gss)
    per = N_CHIP - 1

    def body(*refs):
        gs_refs, gr_ref = refs[:n], refs[n]
        os_refs, or_ref = refs[n + 1:2 * n + 1], refs[2 * n + 1]
        send_sems, recv_sems, local_sems = refs[2 * n + 2:]
        x, y, c = lax.axis_index("x"), lax.axis_index("y"), lax.axis_index("c")
        me = 2 * x + y
        local = [pltpu.make_async_copy(gs_refs[t].at[me], os_refs[t].at[me], local_sems.at[t]) for t in range(n)]
        local.append(pltpu.make_async_copy(gr_ref, or_ref.at[me], local_sems.at[n]))
        for cp in local:
            cp.start()
        sends, recvs = [], []
        for k in range(1, N_CHIP):
            px = 1 - x if k & 2 else x
            py = 1 - y if k & 1 else y
            peer = 2 * px + py
            for t in range(n + 1):
                sems = dict(send_sem=send_sems.at[t * per + k - 1], recv_sem=recv_sems.at[t * per + k - 1],
                            device_id=(px, py, c), device_id_type=MESH)
                src = gs_refs[t].at[peer] if t < n else gr_ref
                out = os_refs[t] if t < n else or_ref
                sends.append(pltpu.make_async_remote_copy(src_ref=src, dst_ref=out.at[me], **sems))
                recvs.append(pltpu.make_async_remote_copy(src_ref=src, dst_ref=out.at[peer], **sems))
        for cp in sends:
            cp.start()
        for cp in recvs:
            cp.wait_recv()
        for cp in sends:
            cp.wait_send()
        for cp in local:
            cp.wait()

    nsem = per * (n + 1)
    return pl.pallas_call(
        body, name=name,
        out_shape=[jax.ShapeDtypeStruct(g.shape, g.dtype) for g in gss]
        + [jax.ShapeDtypeStruct((N_CHIP,) + gr.shape, gr.dtype)],
        in_specs=[ANY] * (n + 1), out_specs=[ANY] * (n + 1),
        scratch_shapes=[pltpu.SemaphoreType.DMA((nsem,)), pltpu.SemaphoreType.DMA((nsem,)),
                        pltpu.SemaphoreType.DMA((n + 1,))],
    )(*gss, gr)


ADAM_ROWS = 128


def _adamw(parts, w, m, v, name):
    r, cols = w.shape
    tr = ADAM_ROWS if r % ADAM_ROWS == 0 else r
    n_parts = parts.shape[0]

    def body(p_ref, w_ref, m_ref, v_ref, g_ref, d_ref, nm_ref, nv_ref):
        g = p_ref[0].astype(F32)
        for sidx in range(1, n_parts):
            g = g + p_ref[sidx].astype(F32)
        mm = ADAM_B1 * m_ref[...] + (1.0 - ADAM_B1) * g
        vv = ADAM_B2 * v_ref[...] + (1.0 - ADAM_B2) * (g * g)
        m_hat = mm / (1.0 - ADAM_B1 ** ADAM_STEP)
        v_hat = vv / (1.0 - ADAM_B2 ** ADAM_STEP)
        g_ref[...] = g
        d_ref[...] = -ADAM_LR * (m_hat / (jnp.sqrt(v_hat) + ADAM_EPS) + ADAM_WD * w_ref[...])
        nm_ref[...] = mm
        nv_ref[...] = vv

    row = pl.BlockSpec((tr, cols), lambda i: (i, 0))
    return pl.pallas_call(
        body, name=name, grid=(r // tr,),
        in_specs=[pl.BlockSpec((n_parts, tr, cols), lambda i: (0, i, 0)), row, row, row],
        out_specs=[row, row, row, row],
        out_shape=[jax.ShapeDtypeStruct((r, cols), F32)] * 4,
        compiler_params=_cp(("parallel",)),
    )(parts, w, m, v)


PACK_W = 8 * LANES
BF16_ROWS = 16


def _pack(flat_parts, rows):
    flat = jnp.concatenate([p.reshape(-1) for p in flat_parts])
    return jnp.pad(flat, (0, rows * PACK_W - flat.shape[0])).reshape(rows, PACK_W)


def _rows_for(n, mult):
    rows = -(-n // PACK_W)
    return -(-rows // mult) * mult


def _unshard(arr8, axis):
    full = jnp.moveaxis(arr8, 0, axis)
    shp = list(full.shape)
    shp[axis:axis + 2] = [shp[axis] * shp[axis + 1]]
    return full.reshape(shp)


def _split_c_chip(full, axis):
    shp = list(full.shape)
    shp[axis:axis + 1] = [N_CHIP, 2, shp[axis] // N_DEV]
    return jnp.moveaxis(full.reshape(shp), (axis + 1, axis), (0, 1))


def _unpack(flat2d, shapes, lead=()):
    flat = flat2d.reshape(lead + (-1,))
    out, off = [], 0
    for shp in shapes:
        n = math.prod(shp)
        out.append(flat[..., off:off + n].reshape(lead + tuple(shp)))
        off += n
    return out


def _to_layout(w):
    return jnp.concatenate([w[:, :1536], w[:, 1824:2336], w[:, 1536:1792], w[:, 2336:3104], w[:, 1792:1824],
                            jnp.zeros((w.shape[0], NP - IN_COLS), w.dtype)], axis=1)


def _from_layout(g):
    return jnp.concatenate([g[:, :1536], g[:, C_CKV:C_CKV + KV_LORA], g[:, C_KR:C_KR + ROPE],
                            g[:, C_ZM:C_ZM + MLA_W], g[:, C_SG:C_SG + 3 * SG_W]], axis=1)


def _pad_heads(w, real):
    lead = w.shape[:-1]
    w = w.reshape(lead + (HEADS, real))
    return jnp.pad(w, [(0, 0)] * len(lead) + [(0, 0), (0, HEAD_PAD - real)]).reshape(lead + (QW,))


def _rope_tables(s):
    half = ROPE // 2
    inv_freq = ROPE_THETA ** (-jnp.arange(half, dtype=F32) / half)
    ang = jnp.arange(s, dtype=F32)[:, None] * inv_freq[None, :]
    cos, sin = jnp.cos(ang), jnp.sin(ang)
    ones = jnp.ones((s, NOPE), F32)
    zeros = jnp.zeros((s, NOPE), F32)
    pad = jnp.zeros((s, HEAD_PAD - QK), F32)
    rc = jnp.concatenate([ones, cos, cos, pad + 1.0], axis=1)
    rs = jnp.concatenate([zeros, -sin, sin, pad], axis=1)
    return rc, rs


def kernel(x, norm_g, w_in, conv_w, conv_b, conv_ln_g, conv_ln_b, conv_pw_w, conv_pw_b, q_norm_g, w_uq, kv_norm_g, w_ukv, qk_q_g, qk_k_g, sg_ln_g, sg_ln_b, sg_w, sg_b, branch_norm_g, w_out, loss_target, m_norm_g, m_w_in, m_conv_w, m_conv_b, m_conv_ln_g, m_conv_ln_b, m_conv_pw_w, m_conv_pw_b, m_q_norm_g, m_w_uq, m_kv_norm_g, m_w_ukv, m_qk_q_g, m_qk_k_g, m_sg_ln_g, m_sg_ln_b, m_sg_w, m_sg_b, m_branch_norm_g, m_w_out, v_norm_g, v_w_in, v_conv_w, v_conv_b, v_conv_ln_g, v_conv_ln_b, v_conv_pw_w, v_conv_pw_b, v_q_norm_g, v_w_uq, v_kv_norm_g, v_w_ukv, v_qk_q_g, v_qk_k_g, v_sg_ln_g, v_sg_ln_b, v_sg_w, v_sg_b, v_branch_norm_g, v_w_out):
    given = dict(locals())
    wts = {n: given[n] for n in W_NAMES}
    mom_m = {n: given['m_' + n] for n in W_NAMES}
    mom_v = {n: given['v_' + n] for n in W_NAMES}
    s = x.shape[1]
    xs = x.reshape(s, D_MODEL)
    target = loss_target.reshape(s, D_MODEL)

    rp_shapes = [wts[n].shape for n in REPL]
    rows_rp = _rows_for(sum(math.prod(p) for p in rp_shapes), BF16_ROWS)
    sh_shape = {n: wts[n].shape for n in SHARDED}
    sh_2d = {n: (sh_shape[n][0] * sh_shape[n][1], sh_shape[n][2]) for n in SHARDED}

    def vec(a, width=None):
        a = a.reshape(1, -1)
        return a if width is None else jnp.pad(a, ((0, 0), (0, width - a.shape[1])))

    def win_full(g):
        return jnp.moveaxis(g, 0, 1).reshape(g.shape[1], N_DEV * g.shape[2])

    later = [n for n in SHARDED if n != 'w_in']

    def shards(l):
        return [wts['w_in'][l].astype(BF16)] + [wts[n][l].astype(F32 if n == 'conv_w' else BF16) for n in later]

    def gathered_weights(g_win, g_later):
        full = {n: _unshard(g, SHARD_AXIS[n] - 1) for n, g in zip(later, g_later)}
        full['w_in'] = win_full(g_win)
        return full

    (g_win0,) = _all_gather(shards(0)[:1], "w_in0_all_gather")
    proj0, g_later0 = _proj_fwd(xs, vec(wts['norm_g'][0]), _to_layout(win_full(g_win0)), "proj_fwd_0",
                                gather=shards(0)[1:])

    rc, rs = _rope_tables(s)
    tril = jnp.tril(jnp.ones((SG_CHUNK, SG_CHUNK), dtype=bool))

    def layer_params(l, full):
        p = dict(full)
        p.update({n: wts[n][l] for n in REPL})
        wukv = p['w_ukv'].reshape(KV_LORA, HEADS, NOPE + V_DIM)
        wm = jnp.where(tril[None], p['sg_w'], 0.0)
        return (dict(
            ng=vec(p['norm_g']), win=_to_layout(p['w_in']).astype(BF16),
            cw=jnp.pad(p['conv_w'], ((0, HALO - CONV_K), (0, 0))), cb=vec(p['conv_b']), clg=vec(p['conv_ln_g']),
            clb=vec(p['conv_ln_b']), pww=p['conv_pw_w'].astype(BF16), pwb=vec(p['conv_pw_b']),
            qg=vec(p['q_norm_g']), wuq=_pad_heads(p['w_uq'], QK).astype(BF16), kvg=vec(p['kv_norm_g']),
            wukv=jnp.concatenate([_pad_heads(wukv[:, :, :NOPE].reshape(KV_LORA, HEADS * NOPE), NOPE),
                                  wukv[:, :, NOPE:].reshape(KV_LORA, MLA_W)], axis=1).astype(BF16),
            gq=vec(p['qk_q_g'], LANES), gk=vec(p['qk_k_g'], LANES),
            slg=vec(p['sg_ln_g']), slb=vec(p['sg_ln_b']), wm=wm.astype(BF16),
            wmt=jnp.swapaxes(wm, 1, 2).astype(BF16),
            sbx=jnp.repeat(p['sg_b'].T, SG_W // SG_HEADS, axis=1),
            bng=vec(p['branch_norm_g']), wout=p['w_out'].astype(BF16)))

    layers = [layer_params(0, gathered_weights(g_win0, g_later0))]

    acts = []
    h_in = xs
    for l in range(DEPTH):
        p = layers[l]
        proj = proj0 if l == 0 else _proj_fwd(h_in, p['ng'], p['win'], f"proj_fwd_{l}")[0]
        yc, cv = _conv_fwd(proj, p['cw'], p['cb'], p['clg'], p['clb'], p['pww'], p['pwb'], f"conv_fwd_{l}")
        ys = _sgu_fwd(proj, p['slg'], p['slb'], p['wm'], p['sbx'], f"sgu_fwd_{l}")
        q, k, v, _ = _mla_fwd(proj, rc, rs, p['qg'], p['wuq'], p['kvg'], p['wukv'], p['gq'], p['gk'], f"mla_fwd_{l}")
        o, lse, g_next = _attn_fwd(q, k, v, f"attn_fwd_{l}", gather=shards(l + 1) if l + 1 < DEPTH else ())
        if l + 1 < DEPTH:
            layers.append(layer_params(l + 1, gathered_weights(g_next[0], g_next[1:])))
        h_out = _post_fwd(h_in, yc, o, proj, ys, p['bng'], p['wout'], f"post_fwd_{l}")
        acts.append(dict(x=h_in, proj=proj, yc=yc, cv=cv, ys=ys, q=q, k=k, v=v, o=o, lse=lse))
        h_in = h_out

    d_out, loss_blk = _loss_head(h_in, target, "loss_head")
    loss = lax.psum(loss_blk[0, 0], ("x", "y", "c"))

    grads = {n: [None] * DEPTH for n in W_NAMES}
    for l in reversed(range(DEPTH)):
        p, a = layers[l], acts[l]
        d_yc, d_ys, d_o, d_zm, stats, g_wout, g_bng = _post_bwd(
            d_out, a['yc'], a['o'], a['lse'], a['proj'], a['ys'], p['bng'], p['wout'], f"post_bwd_{l}")
        dq, dk, dv = _attn_bwd(a['q'], a['k'], a['v'], d_o, stats, f"attn_bwd_{l}")
        d_a, g_cw, g_pww, gv_c = _conv_bwd(a['proj'], a['cv'], d_yc, p['cw'], p['clg'], p['clb'], p['pww'], p['pwb'],
                                           f"conv_bwd_{l}")
        d_sg, g_wm, dms, gv_s = _sgu_bwd(a['proj'], d_ys, p['slg'], p['slb'], p['wm'], p['wmt'], p['sbx'],
                                         f"sgu_bwd_{l}")
        d_cq, d_ckv, d_kr, g_wuq, g_wukv, gv_m = _mla_bwd(
            a['proj'], rc, rs, p['qg'], p['wuq'], p['kvg'], p['wukv'], p['gq'], p['gk'], dq, dk, dv, f"mla_bwd_{l}")
        pieces = [(d_a, C_A), (d_cq, C_CQ), (d_zm, C_ZM), (d_ckv, C_CKV), (d_sg, C_SG), (d_kr, C_KR)]
        d_x, h_t, g_ng = _proj_bwd(a['x'], p['ng'], p['win'], d_out, pieces, f"proj_bwd_{l}")
        g_win = _win_grad(h_t, pieces, f"win_grad_{l}")
        d_out = d_x

        grads['norm_g'][l] = g_ng[0]
        grads['w_in'][l] = _from_layout(g_win)
        grads['conv_w'][l] = g_cw[:CONV_K]
        grads['conv_b'][l] = gv_c[0]
        grads['conv_ln_g'][l] = gv_c[1]
        grads['conv_ln_b'][l] = gv_c[2]
        grads['conv_pw_w'][l] = g_pww
        grads['conv_pw_b'][l] = gv_c[3]
        grads['q_norm_g'][l] = gv_m[0, :Q_LORA]
        grads['w_uq'][l] = g_wuq.reshape(Q_LORA, HEADS, HEAD_PAD)[:, :, :QK].reshape(Q_LORA, HEADS * QK)
        grads['kv_norm_g'][l] = gv_m[1, :KV_LORA]
        grads['w_ukv'][l] = jnp.concatenate(
            [g_wukv[:, :QW].reshape(KV_LORA, HEADS, HEAD_PAD)[:, :, :NOPE],
             g_wukv[:, QW:].reshape(KV_LORA, HEADS, V_DIM)], axis=2).reshape(KV_LORA, HEADS * (NOPE + V_DIM))
        grads['qk_q_g'][l] = gv_m[2, :QK]
        grads['qk_k_g'][l] = gv_m[3, :QK]
        grads['sg_ln_g'][l] = gv_s[0]
        grads['sg_ln_b'][l] = gv_s[1]
        grads['sg_w'][l] = jnp.where(tril[None], g_wm, 0.0)
        grads['sg_b'][l] = dms.reshape(SG_CHUNK, SG_HEADS, SG_W // SG_HEADS).sum(axis=2).T
        grads['branch_norm_g'][l] = g_bng[0]
        grads['w_out'][l] = g_wout
    grad_x = d_out.reshape(x.shape)
    g_full = {n: jnp.stack(grads[n]) for n in W_NAMES}

    gss = [_split_c_chip(g_full[n].astype(BF16), SHARD_AXIS[n]).reshape((2, N_CHIP) + sh_2d[n]) for n in SHARDED]
    gr = _pack([g_full[n].astype(BF16) for n in REPL], rows_rp)
    *theirs, gr_theirs = _grad_to_sibling(gss, gr, "grad_to_sibling")
    my_c = lax.axis_index("c")
    mine = [lax.dynamic_index_in_dim(g, my_c, 0, keepdims=False) for g in gss]
    *chip_sh, chip_rp = _chip_sum(mine, theirs, gr, gr_theirs, "chip_sum")
    *parts_sh, parts_rp = _grad_to_chips(chip_sh, chip_rp, "grad_to_chips")
    res_sh = {n: _adamw(parts, wts[n].reshape(sh_2d[n]), mom_m[n].reshape(sh_2d[n]), mom_v[n].reshape(sh_2d[n]),
                        f"adamw_{n}") for n, parts in zip(SHARDED, parts_sh)}
    res_rp = _adamw(parts_rp, _pack([wts[n] for n in REPL], rows_rp), _pack([mom_m[n] for n in REPL], rows_rp),
                    _pack([mom_v[n] for n in REPL], rows_rp), "adamw_replicated")
    outs = []
    for kind in range(4):
        vals = {n: res_sh[n][kind].reshape(sh_shape[n]) for n in SHARDED}
        vals.update(zip(REPL, _unpack(res_rp[kind], rp_shapes)))
        outs.extend(vals[n] for n in W_NAMES)
    return (loss, grad_x, *outs)
```

```python
import functools
import math

import jax
import jax.numpy as jnp
from jax import lax
from jax.experimental import pallas as pl
from jax.experimental.pallas import tpu as pltpu

F32 = jnp.float32
BF16 = jnp.bfloat16

N_DEV = 8
DEPTH = 2
D_MODEL = 1024
CONV_W = 256
CONV_K = 31
HEADS = 8
NOPE = 64
ROPE = 32
QK = NOPE + ROPE
HEAD_PAD = 128
V_DIM = 64
MLA_W = HEADS * V_DIM
Q_LORA = 768
KV_LORA = 256
SG_W = 256
SG_HEADS = 4
SG_CHUNK = 128
ROPE_THETA = 10000.0
EPS = 1e-6
IN_COLS = 3104
NP = 3200
C_A, C_CQ, C_ZM, C_CKV, C_SG, C_KR = 0, 768, 1536, 2048, 2304, 3072
HALO = 32
SUB = 64
NEG = -1e30
LANES = 128
VMEM_LIMIT_V7X = 52 * 1024 * 1024

ADAM_LR = 0.001
ADAM_B1 = 0.9
ADAM_B2 = 0.999
ADAM_EPS = 1e-08
ADAM_WD = 0.01
ADAM_STEP = 10

W_NAMES = ['norm_g', 'w_in', 'conv_w', 'conv_b', 'conv_ln_g', 'conv_ln_b', 'conv_pw_w', 'conv_pw_b',
           'q_norm_g', 'w_uq', 'kv_norm_g', 'w_ukv', 'qk_q_g', 'qk_k_g', 'sg_ln_g', 'sg_ln_b', 'sg_w',
           'sg_b', 'branch_norm_g', 'w_out']
SHARD_AXIS = {'w_in': 2, 'conv_w': 2, 'conv_pw_w': 1, 'w_uq': 1, 'w_ukv': 2, 'w_out': 1}
SHARDED = [n for n in W_NAMES if n in SHARD_AXIS]
REPL = [n for n in W_NAMES if n not in SHARD_AXIS]


def _tile(s):
    for t in (512, 256, 128):
        if s % t == 0 and s // t >= 2:
            return t
    return s


def _cp(sem):
    return pltpu.CompilerParams(dimension_semantics=sem, vmem_limit_bytes=VMEM_LIMIT_V7X)


def _mm(a, b):
    return jnp.dot(a.astype(BF16), b.astype(BF16), preferred_element_type=F32)


def _mm_nt(a, b):
    return lax.dot_general(a.astype(BF16), b.astype(BF16), (((1,), (1,)), ((), ())),
                           preferred_element_type=F32)


def _mm_tn(a, b):
    return lax.dot_general(a.astype(BF16), b.astype(BF16), (((0,), (0,)), ((), ())),
                           preferred_element_type=F32)


_GC = math.sqrt(2.0 / math.pi)
_GA = 0.044715


def _sig(x):
    return 1.0 / (1.0 + jnp.exp(-x))


def _silu(x):
    return x * _sig(x)


def _dsilu(x):
    s = _sig(x)
    return s * (1.0 + x * (1.0 - s))


def _gelu(x):
    return 0.5 * x * (1.0 + jnp.tanh(_GC * (x + _GA * x * x * x)))


def _dgelu(x):
    t = jnp.tanh(_GC * (x + _GA * x * x * x))
    return 0.5 * (1.0 + t) + 0.5 * x * (1.0 - t * t) * _GC * (1.0 + 3.0 * _GA * x * x)


def _rsum(x):
    return jnp.sum(x, axis=-1, keepdims=True)


def _csum(x):
    return jnp.sum(x, axis=0, keepdims=True)


def _rms_fwd(x, n):
    r = lax.rsqrt(_rsum(x * x) * (1.0 / n) + EPS)
    return x * r, r


def _rms_bwd(dxh, xn, r, n):
    return r * (dxh - xn * (_rsum(dxh * xn) * (1.0 / n)))


def _ln_fwd(x, n):
    mu = _rsum(x) * (1.0 / n)
    xc = x - mu
    r = lax.rsqrt(_rsum(xc * xc) * (1.0 / n) + EPS)
    return xc * r, r


def _ln_bwd(dxh, xh, r, n):
    return r * (dxh - _rsum(dxh) * (1.0 / n) - xh * (_rsum(dxh * xh) * (1.0 / n)))


def _partner(x, lane):
    return jnp.where(lane < NOPE + ROPE // 2, pltpu.roll(x, LANES - ROPE // 2, 1), pltpu.roll(x, ROPE // 2, 1))


def _row_spec(ts, w, col=0):
    return pl.BlockSpec((ts, w), lambda i, col=col: (i, col))


def _full_spec(shape):
    nd = len(shape)
    return pl.BlockSpec(shape, lambda i, nd=nd: (0,) * nd)


PROJ_CHUNK = 640


def _proj_fwd(x, ng, win_p, name, gather=()):
    s = x.shape[0]
    ts = _tile(s)
    nt = s // ts
    n = len(gather)

    def body(x_ref, g_ref, w_ref, *rest):
        o_ref = rest[n]
        if n:
            i = pl.program_id(0)
            start, pass_on, finish = _ag_steps(rest[:n], rest[n + 1:2 * n + 1], *rest[2 * n + 1:])
            pl.when(i == 0)(start)
            pl.when(i == nt // 2)(pass_on)
        xv = x_ref[...]
        xn, _ = _rms_fwd(xv, D_MODEL)
        h = (xn * g_ref[...]).astype(BF16)
        for c in range(0, NP, PROJ_CHUNK):
            o_ref[:, c:c + PROJ_CHUNK] = jnp.dot(h, w_ref[:, c:c + PROJ_CHUNK], preferred_element_type=F32)
        if n:
            pl.when(i == nt - 1)(finish)

    out = pl.pallas_call(
        body, name=name, grid=(nt,),
        in_specs=[_row_spec(ts, D_MODEL), _full_spec((1, D_MODEL)), _full_spec((D_MODEL, NP))] + [ANY] * n,
        out_specs=[_row_spec(ts, NP)] + [ANY] * n,
        out_shape=[jax.ShapeDtypeStruct((s, NP), F32)]
        + [jax.ShapeDtypeStruct((N_DEV,) + a.shape, a.dtype) for a in gather],
        scratch_shapes=_ag_scratch(n) if n else [],
        compiler_params=_cp(("arbitrary",) if n else ("parallel",)),
    )(x, ng, win_p, *gather)
    return out[0], out[1:]


def _proj_bwd(x, ng, win_p, d_out, pieces, name):
    s = x.shape[0]
    ts = _tile(s)
    offs = [o for _, o in pieces]
    widths = [p.shape[1] for p, _ in pieces]

    def body(x_ref, g_ref, w_ref, do_ref, *rest):
        p_refs = rest[:len(pieces)]
        dx_ref, h_ref, gg_ref = rest[len(pieces):]
        i = pl.program_id(0)
        xv = x_ref[...]
        xn, r = _rms_fwd(xv, D_MODEL)
        g = g_ref[...]
        h_ref[...] = (xn * g).T.astype(BF16)
        dh = jnp.zeros((ts, D_MODEL), F32)
        for p_ref, off, w in zip(p_refs, offs, widths):
            dh = dh + _mm_nt(p_ref[...], w_ref[:, off:off + w])

        @pl.when(i == 0)
        def _():
            gg_ref[...] = jnp.zeros_like(gg_ref)

        gg_ref[...] += _csum(dh * xn)
        dx_ref[...] = _rms_bwd(dh * g, xn, r, D_MODEL) + do_ref[...]

    in_specs = [_row_spec(ts, D_MODEL), _full_spec((1, D_MODEL)), _full_spec((D_MODEL, NP)), _row_spec(ts, D_MODEL)]
    in_specs += [_row_spec(ts, w) for w in widths]
    return pl.pallas_call(
        body, name=name, grid=(s // ts,),
        in_specs=in_specs,
        out_specs=[_row_spec(ts, D_MODEL), pl.BlockSpec((D_MODEL, ts), lambda i: (0, i)), _full_spec((1, D_MODEL))],
        out_shape=[jax.ShapeDtypeStruct((s, D_MODEL), F32), jax.ShapeDtypeStruct((D_MODEL, s), BF16),
                   jax.ShapeDtypeStruct((1, D_MODEL), F32)],
        compiler_params=_cp(("arbitrary",)),
    )(x, ng, win_p, d_out, *[p for p, _ in pieces])


WG_TILE = 256


def _win_grad(ht, pieces, name):
    s = ht.shape[1]
    ts = min(WG_TILE, s)
    offs = [o for _, o in pieces]
    widths = [p.shape[1] for p, _ in pieces]

    def body(ht_ref, *rest):
        p_refs, o_ref = rest[:-1], rest[-1]

        @pl.when(pl.program_id(0) == 0)
        def _():
            o_ref[...] = jnp.zeros_like(o_ref)

        hb = ht_ref[...]
        for p_ref, off, w in zip(p_refs, offs, widths):
            o_ref[:, off:off + w] += jnp.dot(hb, p_ref[...].astype(BF16), preferred_element_type=F32)

    return pl.pallas_call(
        body, name=name, grid=(s // ts,),
        in_specs=[pl.BlockSpec((D_MODEL, ts), lambda i: (0, i))] + [_row_spec(ts, w) for w in widths],
        out_specs=_full_spec((D_MODEL, NP)),
        out_shape=jax.ShapeDtypeStruct((D_MODEL, NP), F32),
        compiler_params=_cp(("arbitrary",)),
    )(ht, *[p for p, _ in pieces])


def _halo_spec(ts):
    per = ts // HALO
    return pl.BlockSpec((HALO, 2 * CONV_W), lambda i: (jnp.maximum(i * per - 1, 0), 0))


def _conv_taps(ext_ref, cw_ref, cv_ref, cb, ts):
    base = HALO - (CONV_K - 1)
    for r0 in range(0, ts, SUB):
        acc = jnp.zeros((SUB, CONV_W), F32)
        for k in range(CONV_K):
            acc = acc + cw_ref[k:k + 1, :] * ext_ref[r0 + base + k:r0 + base + k + SUB, :]
        cv_ref[r0:r0 + SUB, :] = acc + cb


def _conv_fwd(proj, cw, cb, lg, lb, pww, pwb, name):
    s = proj.shape[0]
    ts = _tile(s)

    def body(pa_ref, ph_ref, cw_ref, cb_ref, lg_ref, lb_ref, pww_ref, pwb_ref, y_ref, cv_ref, ext_ref):
        i = pl.program_id(0)
        pa = pa_ref[...]
        a, ag, zc = pa[:, :CONV_W], pa[:, CONV_W:2 * CONV_W], pa[:, 2 * CONV_W:]
        ph = ph_ref[...]
        hglu = ph[:, :CONV_W] * _sig(ph[:, CONV_W:])
        ext_ref[0:HALO, :] = jnp.where(i > 0, hglu, 0.0)
        ext_ref[HALO:HALO + ts, :] = a * _sig(ag)
        _conv_taps(ext_ref, cw_ref, cv_ref, cb_ref[...], ts)
        xh, _ = _ln_fwd(cv_ref[...], CONV_W)
        ln = xh * lg_ref[...] + lb_ref[...]
        pw = _mm(_silu(ln), pww_ref[...]) + pwb_ref[...]
        y_ref[...] = pw * _silu(zc)

    vec = _full_spec((1, CONV_W))
    return pl.pallas_call(
        body, name=name, grid=(s // ts,),
        in_specs=[_row_spec(ts, 3 * CONV_W, 0), _halo_spec(ts), _full_spec((HALO, CONV_W)), vec, vec, vec,
                  _full_spec((CONV_W, CONV_W)), vec],
        out_specs=[_row_spec(ts, CONV_W), _row_spec(ts, CONV_W)],
        out_shape=[jax.ShapeDtypeStruct((s, CONV_W), F32), jax.ShapeDtypeStruct((s, CONV_W), F32)],
        scratch_shapes=[pltpu.VMEM((HALO + ts, CONV_W), F32)],
        compiler_params=_cp(("parallel",)),
    )(proj, proj, cw, cb, lg, lb, pww, pwb)


def _conv_bwd(proj, cv, dy, cw, lg, lb, pww, pwb, name):
    s = proj.shape[0]
    ts = _tile(s)
    nt = s // ts
    per = ts // HALO

    def body(pa_ref, ph_ref, cv_ref, dy_ref, cw_ref, lg_ref, lb_ref, pww_ref, pwb_ref,
             dp_ref, gcw_ref, gpw_ref, gv_ref, ext_ref, dext_ref, carry_ref, gacc_ref):
        i = pl.program_id(0)
        ti = nt - 1 - i

        @pl.when(i == 0)
        def _():
            carry_ref[...] = jnp.zeros_like(carry_ref)
            gacc_ref[...] = jnp.zeros_like(gacc_ref)
            gpw_ref[...] = jnp.zeros_like(gpw_ref)
            gv_ref[...] = jnp.zeros_like(gv_ref)

        pa = pa_ref[...]
        a, ag, zc = pa[:, :CONV_W], pa[:, CONV_W:2 * CONV_W], pa[:, 2 * CONV_W:]
        sag = _sig(ag)
        ph = ph_ref[...]
        hglu = ph[:, :CONV_W] * _sig(ph[:, CONV_W:])
        ext_ref[0:HALO, :] = jnp.where(ti > 0, hglu, 0.0)
        ext_ref[HALO:HALO + ts, :] = a * sag
        xh, rl = _ln_fwd(cv_ref[...], CONV_W)
        lg = lg_ref[...]
        ln = xh * lg + lb_ref[...]
        sw = _silu(ln)
        pww = pww_ref[...]
        pw = _mm(sw, pww) + pwb_ref[...]
        d_y = dy_ref[...]
        d_pw = d_y * _silu(zc)
        d_zc = d_y * pw * _dsilu(zc)
        gpw_ref[...] += _mm_tn(sw, d_pw)
        d_ln = _mm_nt(d_pw, pww) * _dsilu(ln)
        d_cv = _ln_bwd(d_ln * lg, xh, rl, CONV_W)
        gv_ref[0:1, :] += _csum(d_cv)
        gv_ref[1:2, :] += _csum(d_ln * xh)
        gv_ref[2:3, :] += _csum(d_ln)
        gv_ref[3:4, :] += _csum(d_pw)
        dext_ref[0:ts, :] = d_cv
        dext_ref[ts:ts + HALO, :] = carry_ref[...]
        carry_ref[...] = d_cv[0:HALO, :]
        base = HALO - (CONV_K - 1)
        for r0 in range(0, ts, SUB):
            dcv_r = dext_ref[r0:r0 + SUB, :]
            dg = jnp.zeros((SUB, CONV_W), F32)
            for k in range(CONV_K):
                prod = dcv_r * ext_ref[r0 + base + k:r0 + base + k + SUB, :]
                gacc_ref[8 * k:8 * k + 8, :] += jnp.sum(prod.reshape(SUB // 8, 8, CONV_W), axis=0)
                dg = dg + cw_ref[k:k + 1, :] * dext_ref[r0 + CONV_K - 1 - k:r0 + CONV_K - 1 - k + SUB, :]
            sg_r, a_r = sag[r0:r0 + SUB, :], a[r0:r0 + SUB, :]
            dp_ref[r0:r0 + SUB, 0:CONV_W] = (dg * sg_r).astype(BF16)
            dp_ref[r0:r0 + SUB, CONV_W:2 * CONV_W] = (dg * a_r * sg_r * (1.0 - sg_r)).astype(BF16)
        dp_ref[:, 2 * CONV_W:] = d_zc.astype(BF16)

        @pl.when(i == nt - 1)
        def _():
            gcw_ref[...] = jnp.zeros_like(gcw_ref)
            for k in range(CONV_K):
                gcw_ref[k:k + 1, :] = _csum(gacc_ref[8 * k:8 * k + 8, :])

    vec = _full_spec((1, CONV_W))
    rev = lambda w, col=0: pl.BlockSpec((ts, w), lambda i, col=col: (nt - 1 - i, col))
    halo = pl.BlockSpec((HALO, 2 * CONV_W), lambda i: (jnp.maximum((nt - 1 - i) * per - 1, 0), 0))
    return pl.pallas_call(
        body, name=name, grid=(nt,),
        in_specs=[rev(3 * CONV_W), halo, rev(CONV_W), rev(CONV_W), _full_spec((HALO, CONV_W)), vec, vec,
                  _full_spec((CONV_W, CONV_W)), vec],
        out_specs=[rev(3 * CONV_W), _full_spec((HALO, CONV_W)), _full_spec((CONV_W, CONV_W)), _full_spec((8, CONV_W))],
        out_shape=[jax.ShapeDtypeStruct((s, 3 * CONV_W), BF16), jax.ShapeDtypeStruct((HALO, CONV_W), F32),
                   jax.ShapeDtypeStruct((CONV_W, CONV_W), F32), jax.ShapeDtypeStruct((8, CONV_W), F32)],
        scratch_shapes=[pltpu.VMEM((HALO + ts, CONV_W), F32), pltpu.VMEM((ts + HALO, CONV_W), F32),
                        pltpu.VMEM((HALO, CONV_W), F32), pltpu.VMEM((8 * HALO, CONV_W), F32)],
        compiler_params=_cp(("arbitrary",)),
    )(proj, proj, cv, dy, cw, lg, lb, pww, pwb)


def _sg_mix(wm_ref, vc, head):
    out = jnp.zeros((SG_CHUNK, SG_W), F32)
    vb = vc.astype(BF16)
    for g in range(SG_HEADS):
        out = jnp.where(head == g, jnp.dot(wm_ref[g], vb, preferred_element_type=F32), out)
    return out


def _sgu_fwd(proj, lg, lb, wm, sbx, name):
    s = proj.shape[0]
    ts = _tile(s)

    def body(ps_ref, lg_ref, lb_ref, wm_ref, sbx_ref, y_ref, mix_ref):
        ps = ps_ref[...]
        us, vs, zs = ps[:, :SG_W], ps[:, SG_W:2 * SG_W], ps[:, 2 * SG_W:]
        xh, _ = _ln_fwd(_gelu(vs), SG_W)
        vn = xh * lg_ref[...] + lb_ref[...]
        head = lax.broadcasted_iota(jnp.int32, (SG_CHUNK, SG_W), 1) // (SG_W // SG_HEADS)
        for c0 in range(0, ts, SG_CHUNK):
            mix_ref[c0:c0 + SG_CHUNK, :] = _sg_mix(wm_ref, vn[c0:c0 + SG_CHUNK, :], head) + sbx_ref[...]
        y_ref[...] = _gelu(us) * mix_ref[...] * _silu(zs)

    vec = _full_spec((1, SG_W))
    return pl.pallas_call(
        body, name=name, grid=(s // ts,),
        in_specs=[_row_spec(ts, 3 * SG_W, C_SG // (3 * SG_W)), vec, vec,
                  _full_spec((SG_HEADS, SG_CHUNK, SG_CHUNK)), _full_spec((SG_CHUNK, SG_W))],
        out_specs=_row_spec(ts, SG_W),
        out_shape=jax.ShapeDtypeStruct((s, SG_W), F32),
        scratch_shapes=[pltpu.VMEM((ts, SG_W), F32)],
        compiler_params=_cp(("parallel",)),
    )(proj, lg, lb, wm, sbx)


def _sgu_bwd(proj, dy, lg, lb, wm, wmt, sbx, name):
    s = proj.shape[0]
    ts = _tile(s)

    def body(ps_ref, dy_ref, lg_ref, lb_ref, wm_ref, wmt_ref, sbx_ref,
             dp_ref, gwm_ref, dms_ref, gv_ref, mix_ref, dvn_ref):
        i = pl.program_id(0)

        @pl.when(i == 0)
        def _():
            gwm_ref[...] = jnp.zeros_like(gwm_ref)
            dms_ref[...] = jnp.zeros_like(dms_ref)
            gv_ref[...] = jnp.zeros_like(gv_ref)

        ps = ps_ref[...]
        us, vs, zs = ps[:, :SG_W], ps[:, SG_W:2 * SG_W], ps[:, 2 * SG_W:]
        xh, rl = _ln_fwd(_gelu(vs), SG_W)
        lg = lg_ref[...]
        vn = xh * lg + lb_ref[...]
        head = lax.broadcasted_iota(jnp.int32, (SG_CHUNK, SG_W), 1) // (SG_W // SG_HEADS)
        for c0 in range(0, ts, SG_CHUNK):
            mix_ref[c0:c0 + SG_CHUNK, :] = _sg_mix(wm_ref, vn[c0:c0 + SG_CHUNK, :], head) + sbx_ref[...]
        mixed = mix_ref[...]
        u = _gelu(us)
        sz = _silu(zs)
        d_y = dy_ref[...]
        d_mixed = d_y * u * sz
        dp_ref[:, 0:SG_W] = (d_y * mixed * sz * _dgelu(us)).astype(BF16)
        dp_ref[:, 2 * SG_W:] = (d_y * u * mixed * _dsilu(zs)).astype(BF16)
        dms = jnp.zeros((SG_CHUNK, SG_W), F32)
        for c0 in range(0, ts, SG_CHUNK):
            dm = d_mixed[c0:c0 + SG_CHUNK, :]
            vc = vn[c0:c0 + SG_CHUNK, :]
            dms = dms + dm
            for g in range(SG_HEADS):
                gwm_ref[g] += _mm_nt(jnp.where(head == g, dm, 0.0), vc)
            dvn_ref[c0:c0 + SG_CHUNK, :] = _sg_mix(wmt_ref, dm, head)
        dms_ref[...] += dms
        d_vn = dvn_ref[...]
        gv_ref[0:1, :] += _csum(d_vn * xh)
        gv_ref[1:2, :] += _csum(d_vn)
        dp_ref[:, SG_W:2 * SG_W] = (_ln_bwd(d_vn * lg, xh, rl, SG_W) * _dgelu(vs)).astype(BF16)

    vec = _full_spec((1, SG_W))
    wspec = _full_spec((SG_HEADS, SG_CHUNK, SG_CHUNK))
    return pl.pallas_call(
        body, name=name, grid=(s // ts,),
        in_specs=[_row_spec(ts, 3 * SG_W, C_SG // (3 * SG_W)), _row_spec(ts, SG_W), vec, vec, wspec, wspec,
                  _full_spec((SG_CHUNK, SG_W))],
        out_specs=[_row_spec(ts, 3 * SG_W), wspec, _full_spec((SG_CHUNK, SG_W)), _full_spec((8, SG_W))],
        out_shape=[jax.ShapeDtypeStruct((s, 3 * SG_W), BF16), jax.ShapeDtypeStruct((SG_HEADS, SG_CHUNK, SG_CHUNK), F32),
                   jax.ShapeDtypeStruct((SG_CHUNK, SG_W), F32), jax.ShapeDtypeStruct((8, SG_W), F32)],
        scratch_shapes=[pltpu.VMEM((ts, SG_W), F32), pltpu.VMEM((ts, SG_W), F32)],
        compiler_params=_cp(("arbitrary",)),
    )(proj, dy, lg, lb, wm, wmt, sbx)


QW = HEADS * HEAD_PAD
KVW = QW + MLA_W
ATT_SCALE = QK ** -0.5


def _mla_specs(ts):
    return [_row_spec(ts, Q_LORA, C_CQ // Q_LORA), _row_spec(ts, KV_LORA, C_CKV // KV_LORA),
            _row_spec(ts, LANES, C_KR // LANES), _row_spec(ts, LANES), _row_spec(ts, LANES),
            _full_spec((1, Q_LORA)), _full_spec((Q_LORA, QW)), _full_spec((1, KV_LORA)), _full_spec((KV_LORA, KVW)),
            _full_spec((1, LANES)), _full_spec((1, LANES))]


def _mla_fwd(proj, rc, rs, qg, wuq, kvg, wukv, gq, gk, name, gather=()):
    s = proj.shape[0]
    ts = _tile(s)
    nt = s // ts
    n = len(gather)

    def body(cq_ref, ckv_ref, kr_ref, rc_ref, rs_ref, qg_ref, wuq_ref, kvg_ref, wukv_ref, gq_ref, gk_ref, *rest):
        q_ref, k_ref, v_ref = rest[n:n + 3]
        if n:
            i = pl.program_id(0)
            start, pass_on, finish = _ag_steps(rest[:n], rest[n + 3:2 * n + 3], *rest[2 * n + 3:])
            pl.when(i == 0)(start)
            pl.when(i == nt // 2)(pass_on)
        lane = lax.broadcasted_iota(jnp.int32, (ts, LANES), 1)
        c, sn = rc_ref[...], rs_ref[...]
        cqn, _ = _rms_fwd(cq_ref[...], Q_LORA)
        q0 = _mm(cqn * qg_ref[...], wuq_ref[...])
        gq = gq_ref[...]
        for h in range(HEADS):
            xn, _ = _rms_fwd(q0[:, h * LANES:(h + 1) * LANES], QK)
            qn = xn * gq
            q_ref[:, h * LANES:(h + 1) * LANES] = ((qn * c + _partner(qn, lane) * sn) * ATT_SCALE).astype(BF16)
        ckvn, _ = _rms_fwd(ckv_ref[...], KV_LORA)
        kv = _mm(ckvn * kvg_ref[...], wukv_ref[...])
        kr = pltpu.roll(kr_ref[...], NOPE, 1)
        gk = gk_ref[...]
        for h in range(HEADS):
            xn, _ = _rms_fwd(kv[:, h * LANES:(h + 1) * LANES] + kr, QK)
            kn = xn * gk
            k_ref[:, h * LANES:(h + 1) * LANES] = (kn * c + _partner(kn, lane) * sn).astype(BF16)
        v_ref[...] = kv[:, QW:].astype(BF16)
        if n:
            pl.when(i == nt - 1)(finish)

    out = pl.pallas_call(
        body, name=name, grid=(nt,),
        in_specs=_mla_specs(ts) + [ANY] * n,
        out_specs=[_row_spec(ts, QW), _row_spec(ts, QW), _row_spec(ts, MLA_W)] + [ANY] * n,
        out_shape=[jax.ShapeDtypeStruct((s, QW), BF16), jax.ShapeDtypeStruct((s, QW), BF16),
                   jax.ShapeDtypeStruct((s, MLA_W), BF16)]
        + [jax.ShapeDtypeStruct((N_DEV,) + a.shape, a.dtype) for a in gather],
        scratch_shapes=_ag_scratch(n) if n else [],
        compiler_params=_cp(("arbitrary",) if n else ("parallel",)),
    )(proj, proj, proj, rc, rs, qg, wuq, kvg, wukv, gq, gk, *gather)
    return out[0], out[1], out[2], out[3:]


def _mla_bwd(proj, rc, rs, qg, wuq, kvg, wukv, gq, gk, dq, dk, dv, name):
    s = proj.shape[0]
    ts = _tile(s)

    def body(cq_ref, ckv_ref, kr_ref, rc_ref, rs_ref, qg_ref, wuq_ref, kvg_ref, wukv_ref, gq_ref, gk_ref,
             dq_ref, dk_ref, dv_ref, dcq_ref, dckv_ref, dkr_ref, gwuq_ref, gwukv_ref, gv_ref, d0_ref):
        i = pl.program_id(0)

        @pl.when(i == 0)
        def _():
            gwuq_ref[...] = jnp.zeros_like(gwuq_ref)
            gwukv_ref[...] = jnp.zeros_like(gwukv_ref)
            gv_ref[...] = jnp.zeros_like(gv_ref)

        lane = lax.broadcasted_iota(jnp.int32, (ts, LANES), 1)
        c, sn = rc_ref[...], rs_ref[...]
        cq = cq_ref[...]
        cqx, rq0 = _rms_fwd(cq, Q_LORA)
        qg = qg_ref[...]
        cqn = cqx * qg
        wuq = wuq_ref[...]
        q0 = _mm(cqn, wuq)
        gq = gq_ref[...]
        ggq = jnp.zeros((1, LANES), F32)
        for h in range(HEADS):
            xn, r = _rms_fwd(q0[:, h * LANES:(h + 1) * LANES], QK)
            d = dq_ref[:, h * LANES:(h + 1) * LANES] * ATT_SCALE
            d_qn = d * c - _partner(d, lane) * sn
            ggq = ggq + _csum(d_qn * xn)
            d0_ref[:, h * LANES:(h + 1) * LANES] = _rms_bwd(d_qn * gq, xn, r, QK)
        dq0 = d0_ref[:, 0:QW]
        gwuq_ref[...] += _mm_tn(cqn, dq0)
        d_cqn = _mm_nt(dq0, wuq)
        gv_ref[0:1, 0:Q_LORA] += _csum(d_cqn * cqx)
        gv_ref[2:3, 0:LANES] += ggq
        dcq_ref[...] = _rms_bwd(d_cqn * qg, cqx, rq0, Q_LORA).astype(BF16)
        ckv = ckv_ref[...]
        ckx, rk0 = _rms_fwd(ckv, KV_LORA)
        kvg = kvg_ref[...]
        ckvn = ckx * kvg
        wukv = wukv_ref[...]
        kv = _mm(ckvn, wukv)
        kr = pltpu.roll(kr_ref[...], NOPE, 1)
        gk = gk_ref[...]
        ggk = jnp.zeros((1, LANES), F32)
        dkr = jnp.zeros((ts, LANES), F32)
        for h in range(HEADS):
            xn, r = _rms_fwd(kv[:, h * LANES:(h + 1) * LANES] + kr, QK)
            d = dk_ref[:, h * LANES:(h + 1) * LANES]
            d_kn = d * c - _partner(d, lane) * sn
            ggk = ggk + _csum(d_kn * xn)
            d_k0 = _rms_bwd(d_kn * gk, xn, r, QK)
            dkr = dkr + d_k0
            d0_ref[:, h * LANES:(h + 1) * LANES] = d_k0
        d0_ref[:, QW:KVW] = dv_ref[...]
        dkv = d0_ref[...]
        dkr_ref[...] = jnp.where(lane < ROPE, pltpu.roll(dkr, NOPE, 1), 0.0).astype(BF16)
        gwukv_ref[...] += _mm_tn(ckvn, dkv)
        d_ckvn = _mm_nt(dkv, wukv)
        gv_ref[1:2, 0:KV_LORA] += _csum(d_ckvn * ckx)
        gv_ref[3:4, 0:LANES] += ggk
        dckv_ref[...] = _rms_bwd(d_ckvn * kvg, ckx, rk0, KV_LORA).astype(BF16)

    return pl.pallas_call(
        body, name=name, grid=(s // ts,),
        in_specs=_mla_specs(ts) + [_row_spec(ts, QW), _row_spec(ts, QW), _row_spec(ts, MLA_W)],
        out_specs=[_row_spec(ts, Q_LORA), _row_spec(ts, KV_LORA), _row_spec(ts, LANES),
                   _full_spec((Q_LORA, QW)), _full_spec((KV_LORA, KVW)), _full_spec((8, QW))],
        out_shape=[jax.ShapeDtypeStruct((s, Q_LORA), BF16), jax.ShapeDtypeStruct((s, KV_LORA), BF16),
                   jax.ShapeDtypeStruct((s, LANES), BF16), jax.ShapeDtypeStruct((Q_LORA, QW), F32),
                   jax.ShapeDtypeStruct((KV_LORA, KVW), F32), jax.ShapeDtypeStruct((8, QW), F32)],
        scratch_shapes=[pltpu.VMEM((ts, KVW), F32)],
        compiler_params=_cp(("arbitrary",)),
    )(proj, proj, proj, rc, rs, qg, wuq, kvg, wukv, gq, gk, dq, dk, dv)


PAIRS = HEADS // 2
ATT_STRIP = 32


def _attn_fwd(q, k, v, name, gather=()):
    s = q.shape[0]
    tq = _tile(s)
    tk = tq
    nq = s // tq
    n = len(gather)

    def body(q_ref, k_ref, v_ref, *rest):
        o_ref, lse_ref = rest[n:n + 2]
        s0_ref, s1_ref, p0_ref, p1_ref, m_ref, l_ref, acc_ref = rest[2 * n + 2:2 * n + 9]
        i = pl.program_id(1)
        if n:
            pair = pl.program_id(0)
            start, pass_on, finish = _ag_steps(rest[:n], rest[n + 2:2 * n + 2], *rest[2 * n + 9:])
            pl.when((pair == 0) & (i == 0))(start)
            pl.when((pair == 0) & (i == nq - 1))(pass_on)
        s_refs, p_refs = (s0_ref, s1_ref), (p0_ref, p1_ref)
        row = lax.broadcasted_iota(jnp.int32, (ATT_STRIP, tk), 0)
        col = lax.broadcasted_iota(jnp.int32, (ATT_STRIP, tk), 1)
        first = lax.broadcasted_iota(jnp.int32, (tq, LANES), 1) < V_DIM
        m_ref[...] = jnp.full(m_ref.shape, NEG, F32)
        l_ref[...] = jnp.zeros(l_ref.shape, F32)
        acc_ref[...] = jnp.zeros(acc_ref.shape, F32)

        def blk(j, masked):
            st = pl.multiple_of(j * tk, tk)
            for a in range(2):
                s_refs[a][...] = _mm_nt(q_ref[:, a * LANES:(a + 1) * LANES],
                                        k_ref[pl.ds(st, tk), a * LANES:(a + 1) * LANES])
            for a in range(2):
                for r in range(0, tq, ATT_STRIP):
                    sc = s_refs[a][r:r + ATT_STRIP, :]
                    if masked:
                        sc = jnp.where(col <= row + r, sc, NEG)
                    m_old = m_ref[a, r:r + ATT_STRIP, :]
                    m_new = jnp.maximum(m_old, jnp.max(sc, axis=-1, keepdims=True))
                    alpha = jnp.exp(m_old - m_new)
                    p = jnp.exp(sc - jnp.tile(m_new, (1, tk // LANES)))
                    l_ref[a, r:r + ATT_STRIP, :] = alpha * l_ref[a, r:r + ATT_STRIP, :] + _rsum(p)
                    acc_ref[a, r:r + ATT_STRIP, :] = alpha * acc_ref[a, r:r + ATT_STRIP, :]
                    m_ref[a, r:r + ATT_STRIP, :] = m_new
                    p_refs[a][r:r + ATT_STRIP, :] = p.astype(BF16)
                acc_ref[a] += jnp.dot(p_refs[a][...], v_ref[pl.ds(st, tk), :], preferred_element_type=F32)

        def two_blocks(t, carry):
            blk(2 * t, False)
            blk(2 * t + 1, False)
            return carry

        lax.fori_loop(0, i // 2, two_blocks, 0)

        @pl.when(i % 2 == 1)
        def _():
            blk(i - 1, False)

        blk(i, True)
        o_ref[...] = jnp.where(first, acc_ref[0] / l_ref[0], acc_ref[1] / l_ref[1])
        lse_ref[...] = jnp.where(first, m_ref[0] + jnp.log(l_ref[0]), m_ref[1] + jnp.log(l_ref[1]))
        if n:
            pl.when((pair == PAIRS - 1) & (i == nq - 1))(finish)

    stat = pltpu.VMEM((2, tq, LANES), F32)
    out = pl.pallas_call(
        body, name=name, grid=(PAIRS, nq),
        in_specs=[pl.BlockSpec((tq, 2 * LANES), lambda p, i: (i, p)),
                  pl.BlockSpec((s, 2 * LANES), lambda p, i: (0, p)),
                  pl.BlockSpec((s, LANES), lambda p, i: (0, p))] + [ANY] * n,
        out_specs=[pl.BlockSpec((tq, LANES), lambda p, i: (i, p)), pl.BlockSpec((tq, LANES), lambda p, i: (i, p))]
        + [ANY] * n,
        out_shape=[jax.ShapeDtypeStruct((s, MLA_W), F32), jax.ShapeDtypeStruct((s, MLA_W), F32)]
        + [jax.ShapeDtypeStruct((N_DEV,) + a.shape, a.dtype) for a in gather],
        scratch_shapes=[pltpu.VMEM((tq, tk), F32), pltpu.VMEM((tq, tk), F32), pltpu.VMEM((tq, tk), BF16),
                        pltpu.VMEM((tq, tk), BF16), stat, stat, stat] + (_ag_scratch(n) if n else []),
        compiler_params=_cp(("arbitrary", "arbitrary") if n else ("parallel", "parallel")),
    )(q, k, v, *gather)
    return out[0], out[1], out[2:]


def _attn_bwd(q, k, v, do, stats, name):
    s = q.shape[0]
    tq = _tile(s)
    tk = tq
    nq = s // tq

    def body(q_ref, k_ref, v_ref, do_ref, st_ref, dq_ref, dk_ref, dv_ref):
        j = pl.program_id(1)

        @pl.when(j == 0)
        def _():
            dq_ref[...] = jnp.zeros_like(dq_ref)

        dk_ref[...] = jnp.zeros_like(dk_ref)
        dv_ref[...] = jnp.zeros_like(dv_ref)
        row = lax.broadcasted_iota(jnp.int32, (tq, tk), 0)
        col = lax.broadcasted_iota(jnp.int32, (tq, tk), 1)
        lane = lax.broadcasted_iota(jnp.int32, (tq, LANES), 1)

        def blk(i, masked):
            st = i * tq if isinstance(i, int) else pl.multiple_of(i * tq, tq)
            do2 = do_ref[pl.ds(st, tq), :]
            stt = st_ref[pl.ds(st, tq), :]
            dv = None
            for a in range(2):
                mine = (lane < V_DIM) if a == 0 else (lane >= V_DIM)
                qa = q_ref[pl.ds(st, tq), a * LANES:(a + 1) * LANES]
                doa = jnp.where(mine, do2, jnp.zeros((), BF16))
                lse = stt[:, a * V_DIM:a * V_DIM + 1]
                dl = stt[:, a * V_DIM + V_DIM // 2:a * V_DIM + V_DIM // 2 + 1]
                p = jnp.exp(_mm_nt(qa, k_ref[:, a * LANES:(a + 1) * LANES]) - lse)
                if masked:
                    p = jnp.where(col <= row, p, 0.0)
                ds = (p * (_mm_nt(doa, v_ref[...]) - dl)).astype(BF16)
                dva = _mm_tn(p, doa)
                dv = dva if dv is None else dv + dva
                dk_ref[:, a * LANES:(a + 1) * LANES] += _mm_tn(ds, qa)
                dq_ref[pl.ds(st, tq), a * LANES:(a + 1) * LANES] += jnp.dot(
                    ds, k_ref[:, a * LANES:(a + 1) * LANES], preferred_element_type=F32)
            dv_ref[...] += dv

        blk(j, True)
        rest = nq - 1 - j

        def two_blocks(t, carry):
            blk(j + 1 + 2 * t, False)
            blk(j + 2 + 2 * t, False)
            return carry

        lax.fori_loop(0, rest // 2, two_blocks, 0)

        @pl.when(rest % 2 == 1)
        def _():
            blk(nq - 1, False)

    return pl.pallas_call(
        body, name=name, grid=(PAIRS, s // tk),
        in_specs=[pl.BlockSpec((s, 2 * LANES), lambda p, j: (0, p)),
                  pl.BlockSpec((tk, 2 * LANES), lambda p, j: (j, p)),
                  pl.BlockSpec((tk, LANES), lambda p, j: (j, p)),
                  pl.BlockSpec((s, LANES), lambda p, j: (0, p)),
                  pl.BlockSpec((s, LANES), lambda p, j: (0, p))],
        out_specs=[pl.BlockSpec((s, 2 * LANES), lambda p, j: (0, p)),
                   pl.BlockSpec((tk, 2 * LANES), lambda p, j: (j, p)),
                   pl.BlockSpec((tk, LANES), lambda p, j: (j, p))],
        out_shape=[jax.ShapeDtypeStruct((s, QW), F32), jax.ShapeDtypeStruct((s, QW), F32),
                   jax.ShapeDtypeStruct((s, MLA_W), F32)],
        compiler_params=_cp(("parallel", "arbitrary")),
    )(q, k, v, do, stats)


BR = ((0, CONV_W), (CONV_W, CONV_W + MLA_W), (CONV_W + MLA_W, D_MODEL))


def _post_fwd(x, yc, o, proj, ys, bng, wout, name):
    s = x.shape[0]
    ts = _tile(s)

    def body(x_ref, yc_ref, o_ref, zm_ref, ys_ref, g_ref, w_ref, out_ref):
        ys3 = (yc_ref[...], o_ref[...] * _silu(zm_ref[...]), ys_ref[...])
        acc = x_ref[...]
        for (lo, hi), yb in zip(BR, ys3):
            yn, _ = _rms_fwd(yb, hi - lo)
            acc = acc + _mm(yn * g_ref[:, lo:hi], w_ref[lo:hi, :])
        out_ref[...] = acc

    return pl.pallas_call(
        body, name=name, grid=(s // ts,),
        in_specs=[_row_spec(ts, D_MODEL), _row_spec(ts, CONV_W), _row_spec(ts, MLA_W),
                  _row_spec(ts, MLA_W, C_ZM // MLA_W), _row_spec(ts, SG_W), _full_spec((1, D_MODEL)),
                  _full_spec((D_MODEL, D_MODEL))],
        out_specs=_row_spec(ts, D_MODEL),
        out_shape=jax.ShapeDtypeStruct((s, D_MODEL), F32),
        compiler_params=_cp(("parallel",)),
    )(x, yc, o, proj, ys, bng, wout)


def _post_fwd_loss(x, yc, o, proj, ys, bng, wout, target, name):
    s = x.shape[0]
    ts = _tile(s)
    nt = s // ts

    def body(x_ref, yc_ref, o_ref, zm_ref, ys_ref, g_ref, w_ref, t_ref, dy_ref, l_ref, acc_ref):
        i = pl.program_id(0)

        @pl.when(i == 0)
        def _():
            acc_ref[...] = jnp.zeros_like(acc_ref)

        ys3 = (yc_ref[...], o_ref[...] * _silu(zm_ref[...]), ys_ref[...])
        y = x_ref[...]
        for (lo, hi), yb in zip(BR, ys3):
            yn, _ = _rms_fwd(yb, hi - lo)
            y = y + _mm(yn * g_ref[:, lo:hi], w_ref[lo:hi, :])
        e = y - t_ref[...]
        dy_ref[...] = e * (1.0 / D_MODEL)
        sq = jnp.sum((e * e).reshape(ts // 8, 8, D_MODEL), axis=0)
        part = sq[:, 0:LANES]
        for c in range(LANES, D_MODEL, LANES):
            part = part + sq[:, c:c + LANES]
        acc_ref[...] += part

        @pl.when(i == nt - 1)
        def _():
            tot = jnp.sum(_rsum(acc_ref[...]), axis=0, keepdims=True) * (0.5 / D_MODEL)
            l_ref[...] = jnp.broadcast_to(tot, (8, LANES))

    return pl.pallas_call(
        body, name=name, grid=(nt,),
        in_specs=[_row_spec(ts, D_MODEL), _row_spec(ts, CONV_W), _row_spec(ts, MLA_W),
                  _row_spec(ts, MLA_W, C_ZM // MLA_W), _row_spec(ts, SG_W), _full_spec((1, D_MODEL)),
                  _full_spec((D_MODEL, D_MODEL)), _row_spec(ts, D_MODEL)],
        out_specs=[_row_spec(ts, D_MODEL), _full_spec((8, LANES))],
        out_shape=[jax.ShapeDtypeStruct((s, D_MODEL), F32), jax.ShapeDtypeStruct((8, LANES), F32)],
        scratch_shapes=[pltpu.VMEM((8, LANES), F32)],
        compiler_params=_cp(("arbitrary",)),
    )(x, yc, o, proj, ys, bng, wout, target)


def _post_bwd(d_out, yc, o, lse, proj, ys, bng, wout, name):
    s = d_out.shape[0]
    ts = _tile(s)

    def body(do_ref, yc_ref, o_ref, lse_ref, zm_ref, ys_ref, g_ref, w_ref,
             dyc_ref, dys_ref, dob_ref, dzm_ref, st_ref, gw_ref, gg_ref, yn_ref):
        i = pl.program_id(0)

        @pl.when(i == 0)
        def _():
            gw_ref[...] = jnp.zeros_like(gw_ref)
            gg_ref[...] = jnp.zeros_like(gg_ref)

        d_out_b = do_ref[...].astype(BF16)
        o = o_ref[...]
        zm = zm_ref[...]
        szm = _silu(zm)
        ys3 = (yc_ref[...], o * szm, ys_ref[...])
        d_ys = []
        for (lo, hi), yb in zip(BR, ys3):
            n = hi - lo
            yn, r = _rms_fwd(yb, n)
            g = g_ref[:, lo:hi]
            yn_ref[:, lo:hi] = (yn * g).astype(BF16)
            d_yn = _mm_nt(d_out_b, w_ref[lo:hi, :])
            gg_ref[:, lo:hi] += _csum(d_yn * yn)
            d_ys.append(_rms_bwd(d_yn * g, yn, r, n))
        gw_ref[...] += _mm_tn(yn_ref[...], d_out_b)
        dyc_ref[...] = d_ys[0]
        dys_ref[...] = d_ys[2]
        d_ym = d_ys[1]
        d_o = d_ym * szm
        dob_ref[...] = d_o.astype(BF16)
        dzm_ref[...] = (d_ym * o * _dsilu(zm)).astype(BF16)
        prod = d_o * o
        head = lax.broadcasted_iota(jnp.int32, (ts, MLA_W), 1) // V_DIM
        delta = jnp.zeros((ts, MLA_W), F32)
        for h in range(HEADS):
            delta = jnp.where(head == h, _rsum(jnp.where(head == h, prod, 0.0)), delta)
        lane = lax.broadcasted_iota(jnp.int32, (ts, MLA_W), 1)
        st_ref[...] = jnp.where(lane % V_DIM < V_DIM // 2, lse_ref[...], delta)

    return pl.pallas_call(
        body, name=name, grid=(s // ts,),
        in_specs=[_row_spec(ts, D_MODEL), _row_spec(ts, CONV_W), _row_spec(ts, MLA_W), _row_spec(ts, MLA_W),
                  _row_spec(ts, MLA_W, C_ZM // MLA_W), _row_spec(ts, SG_W), _full_spec((1, D_MODEL)),
                  _full_spec((D_MODEL, D_MODEL))],
        out_specs=[_row_spec(ts, CONV_W), _row_spec(ts, SG_W), _row_spec(ts, MLA_W), _row_spec(ts, MLA_W),
                   _row_spec(ts, MLA_W), _full_spec((D_MODEL, D_MODEL)), _full_spec((1, D_MODEL))],
        out_shape=[jax.ShapeDtypeStruct((s, CONV_W), F32), jax.ShapeDtypeStruct((s, SG_W), F32),
                   jax.ShapeDtypeStruct((s, MLA_W), BF16), jax.ShapeDtypeStruct((s, MLA_W), BF16),
                   jax.ShapeDtypeStruct((s, MLA_W), F32), jax.ShapeDtypeStruct((D_MODEL, D_MODEL), F32),
                   jax.ShapeDtypeStruct((1, D_MODEL), F32)],
        scratch_shapes=[pltpu.VMEM((ts, D_MODEL), BF16)],
        compiler_params=_cp(("arbitrary",)),
    )(d_out, yc, o, lse, proj, ys, bng, wout)


def _loss_head(y, target, name):
    s = y.shape[0]
    ts = _tile(s)
    nt = s // ts

    def body(y_ref, t_ref, dy_ref, l_ref, acc_ref):
        i = pl.program_id(0)

        @pl.when(i == 0)
        def _():
            acc_ref[...] = jnp.zeros_like(acc_ref)

        e = y_ref[...] - t_ref[...]
        dy_ref[...] = e * (1.0 / D_MODEL)
        sq = jnp.sum((e * e).reshape(ts // 8, 8, D_MODEL), axis=0)
        part = sq[:, 0:LANES]
        for c in range(LANES, D_MODEL, LANES):
            part = part + sq[:, c:c + LANES]
        acc_ref[...] += part

        @pl.when(i == nt - 1)
        def _():
            tot = jnp.sum(_rsum(acc_ref[...]), axis=0, keepdims=True) * (0.5 / D_MODEL)
            l_ref[...] = jnp.broadcast_to(tot, (8, LANES))

    return pl.pallas_call(
        body, name=name, grid=(nt,),
        in_specs=[_row_spec(ts, D_MODEL), _row_spec(ts, D_MODEL)],
        out_specs=[_row_spec(ts, D_MODEL), _full_spec((8, LANES))],
        out_shape=[jax.ShapeDtypeStruct((s, D_MODEL), F32), jax.ShapeDtypeStruct((8, LANES), F32)],
        scratch_shapes=[pltpu.VMEM((8, LANES), F32)],
        compiler_params=_cp(("arbitrary",)),
    )(y, target)


MESH = pl.DeviceIdType.MESH
ANY = pl.BlockSpec(memory_space=pl.ANY)


AG_COPIES = N_DEV - 1


def _ag_steps(x_refs, out_refs, send_sems, recv_sems, local_sems):
    n = len(x_refs)
    x, y, c = lax.axis_index("x"), lax.axis_index("y"), lax.axis_index("c")
    me, sibling = (x, y, c), (x, y, 1 - c)
    chips = [(1 - x, y), (x, 1 - y), (1 - x, 1 - y)]

    def slot(t, px, py, pc):
        return out_refs[t].at[4 * px + 2 * py + pc]

    def copy(t, k, block, to, src=None):
        return pltpu.make_async_remote_copy(
            src_ref=slot(t, *block) if src is None else src, dst_ref=slot(t, *block),
            send_sem=send_sems.at[t * AG_COPIES + k], recv_sem=recv_sems.at[t * AG_COPIES + k], device_id=to,
            device_id_type=MESH)

    mine = [pltpu.make_async_copy(x_refs[t], slot(t, *me), local_sems.at[t]) for t in range(n)]
    first = [copy(t, 0, me, sibling, src=x_refs[t]) for t in range(n)]
    first += [copy(t, 1 + j, me, (*chip, c), src=x_refs[t]) for j, chip in enumerate(chips) for t in range(n)]
    passed = [copy(t, 4 + j, (*chip, c), sibling) for j, chip in enumerate(chips) for t in range(n)]

    def start():
        for cp in mine + first:
            cp.start()

    def pass_on():
        for j, chip in enumerate(chips):
            for t in range(n):
                copy(t, 1 + j, (*chip, c), me).wait_recv()
                passed[j * n + t].start()

    def finish():
        for t in range(n):
            copy(t, 0, sibling, me).wait_recv()
        for j, chip in enumerate(chips):
            for t in range(n):
                copy(t, 4 + j, (*chip, 1 - c), me).wait_recv()
        for cp in first + passed:
            cp.wait_send()
        for cp in mine:
            cp.wait()

    return start, pass_on, finish


def _ag_scratch(n):
    return [pltpu.SemaphoreType.DMA((AG_COPIES * n,)), pltpu.SemaphoreType.DMA((AG_COPIES * n,)),
            pltpu.SemaphoreType.DMA((n,))]


def _all_gather(xs, name):
    n = len(xs)

    def body(*refs):
        for step in _ag_steps(refs[:n], refs[n:2 * n], *refs[2 * n:]):
            step()

    return pl.pallas_call(
        body, name=name,
        out_shape=[jax.ShapeDtypeStruct((N_DEV,) + a.shape, a.dtype) for a in xs],
        in_specs=[ANY] * n, out_specs=[ANY] * n,
        scratch_shapes=_ag_scratch(n),
    )(*xs)


N_CHIP = N_DEV // 2


def _grad_to_sibling(gss, gr, name):
    n = len(gss)

    def body(*refs):
        gs_refs, gr_ref = refs[:n], refs[n]
        os_refs, or_ref = refs[n + 1:2 * n + 1], refs[2 * n + 1]
        send_sems, recv_sems = refs[2 * n + 2:]
        x, y, c = lax.axis_index("x"), lax.axis_index("y"), lax.axis_index("c")
        copies = []
        for t in range(n + 1):
            copies.append(pltpu.make_async_remote_copy(
                src_ref=gs_refs[t].at[1 - c] if t < n else gr_ref, dst_ref=os_refs[t] if t < n else or_ref,
                send_sem=send_sems.at[t], recv_sem=recv_sems.at[t], device_id=(x, y, 1 - c), device_id_type=MESH))
        for cp in copies:
            cp.start()
        for cp in copies:
            cp.wait_recv()
        for cp in copies:
            cp.wait_send()

    return pl.pallas_call(
        body, name=name,
        out_shape=[jax.ShapeDtypeStruct(g.shape[1:], g.dtype) for g in gss] + [jax.ShapeDtypeStruct(gr.shape, gr.dtype)],
        in_specs=[ANY] * (n + 1), out_specs=[ANY] * (n + 1),
        scratch_shapes=[pltpu.SemaphoreType.DMA((n + 1,)), pltpu.SemaphoreType.DMA((n + 1,))],
    )(*gss, gr)


def _chip_sum(mine, theirs, gr, gr_theirs, name):
    n = len(mine)

    def body(*refs):
        a_refs, b_refs = refs[:n + 1], refs[n + 1:2 * n + 2]
        o_refs = refs[2 * n + 2:]
        for a_ref, b_ref, o_ref in zip(a_refs, b_refs, o_refs):
            o_ref[...] = (a_ref[...].astype(F32) + b_ref[...].astype(F32)).astype(o_ref.dtype)

    def spec(a):
        if a.ndim == 3:
            return pl.BlockSpec((1,) + a.shape[1:], lambda i: (i, 0, 0))
        return pl.BlockSpec(a.shape, lambda i: (0, 0))

    ins = list(mine) + [gr] + list(theirs) + [gr_theirs]
    return pl.pallas_call(
        body, name=name, grid=(N_CHIP,),
        in_specs=[spec(a) for a in ins], out_specs=[spec(a) for a in ins[:n + 1]],
        out_shape=[jax.ShapeDtypeStruct(a.shape, a.dtype) for a in ins[:n + 1]],
        compiler_params=_cp(("arbitrary",)),
    )(*ins)


def _grad_to_chips(gss, gr, name):
    n = len(gss)
    per = N_CHIP - 1

    def body(*refs):
        gs_refs, gr_ref = refs[:n], refs[n]
        os_refs, or_ref = refs[n + 1:2 * n + 1], refs[2 * n + 1]
        send_sems, recv_sems, local_sems = refs[2 * n + 2:]
        x, y, c = lax.axis_index("x"), lax.axis_index("y"), lax.axis_index("c")
        me = 2 * x + y
        local = [pltpu.make_async_copy(gs_refs[t].at[me], os_refs[t].at[me], local_sems.at[t]) for t in range(n)]
        local.append(pltpu.make_async_copy(gr_ref, or_ref.at[me], local_sems.at[n]))
        for cp in local:
            cp.start()
        sends, recvs = [], []
        for k in range(1, N_CHIP):
            px = 1 - x if k & 2 else x
            py = 1 - y if k & 1 else y
            peer = 2 * px + py
            for t in range(n + 1):
                sems = dict(send_sem=send_sems.at[t * per + k - 1], recv_sem=recv_sems.at[t * per + k - 1],
                            device_id=(px, py, c), device_id_type=MESH)
                src = gs_refs[t].at[peer] if t < n else gr_ref
                out = os_refs[t] if t < n else or_ref
                sends.append(pltpu.make_async_remote_copy(src_ref=src, dst_ref=out.at[me], **sems))
                recvs.append(pltpu.make_async_remote_copy(src_ref=src, dst_ref=out.at[peer], **sems))
        for cp in sends:
            cp.start()
        for cp in recvs:
            cp.wait_recv()
        for cp in sends:
            cp.wait_send()
        for cp in local:
            cp.wait()

    nsem = per * (n + 1)
    return pl.pallas_call(
        body, name=name,
        out_shape=[jax.ShapeDtypeStruct(g.shape, g.dtype) for g in gss]
        + [jax.ShapeDtypeStruct((N_CHIP,) + gr.shape, gr.dtype)],
        in_specs=[ANY] * (n + 1), out_specs=[ANY] * (n + 1),
        scratch_shapes=[pltpu.SemaphoreType.DMA((nsem,)), pltpu.SemaphoreType.DMA((nsem,)),
                        pltpu.SemaphoreType.DMA((n + 1,))],
    )(*gss, gr)


ADAM_ROWS = 128


def _adamw(parts, w, m, v, name):
    r, cols = w.shape
    tr = ADAM_ROWS if r % ADAM_ROWS == 0 else r
    n_parts = parts.shape[0]

    def body(p_ref, w_ref, m_ref, v_ref, g_ref, d_ref, nm_ref, nv_ref):
        g = p_ref[0].astype(F32)
        for sidx in range(1, n_parts):
            g = g + p_ref[sidx].astype(F32)
        mm = ADAM_B1 * m_ref[...] + (1.0 - ADAM_B1) * g
        vv = ADAM_B2 * v_ref[...] + (1.0 - ADAM_B2) * (g * g)
        m_hat = mm / (1.0 - ADAM_B1 ** ADAM_STEP)
        v_hat = vv / (1.0 - ADAM_B2 ** ADAM_STEP)
        g_ref[...] = g
        d_ref[...] = -ADAM_LR * (m_hat / (jnp.sqrt(v_hat) + ADAM_EPS) + ADAM_WD * w_ref[...])
        nm_ref[...] = mm
        nv_ref[...] = vv

    row = pl.BlockSpec((tr, cols), lambda i: (i, 0))
    return pl.pallas_call(
        body, name=name, grid=(r // tr,),
        in_specs=[pl.BlockSpec((n_parts, tr, cols), lambda i: (0, i, 0)), row, row, row],
        out_specs=[row, row, row, row],
        out_shape=[jax.ShapeDtypeStruct((r, cols), F32)] * 4,
        compiler_params=_cp(("parallel",)),
    )(parts, w, m, v)


PACK_W = 8 * LANES
BF16_ROWS = 16


def _pack(flat_parts, rows):
    flat = jnp.concatenate([p.reshape(-1) for p in flat_parts])
    return jnp.pad(flat, (0, rows * PACK_W - flat.shape[0])).reshape(rows, PACK_W)


def _rows_for(n, mult):
    rows = -(-n // PACK_W)
    return -(-rows // mult) * mult


def _unshard(arr8, axis):
    full = jnp.moveaxis(arr8, 0, axis)
    shp = list(full.shape)
    shp[axis:axis + 2] = [shp[axis] * shp[axis + 1]]
    return full.reshape(shp)


def _split_c_chip(full, axis):
    shp = list(full.shape)
    shp[axis:axis + 1] = [N_CHIP, 2, shp[axis] // N_DEV]
    return jnp.moveaxis(full.reshape(shp), (axis + 1, axis), (0, 1))


def _unpack(flat2d, shapes, lead=()):
    flat = flat2d.reshape(lead + (-1,))
    out, off = [], 0
    for shp in shapes:
        n = math.prod(shp)
        out.append(flat[..., off:off + n].reshape(lead + tuple(shp)))
        off += n
    return out


def _to_layout(w):
    return jnp.concatenate([w[:, :1536], w[:, 1824:2336], w[:, 1536:1792], w[:, 2336:3104], w[:, 1792:1824],
                            jnp.zeros((w.shape[0], NP - IN_COLS), w.dtype)], axis=1)


def _from_layout(g):
    return jnp.concatenate([g[:, :1536], g[:, C_CKV:C_CKV + KV_LORA], g[:, C_KR:C_KR + ROPE],
                            g[:, C_ZM:C_ZM + MLA_W], g[:, C_SG:C_SG + 3 * SG_W]], axis=1)


def _pad_heads(w, real):
    lead = w.shape[:-1]
    w = w.reshape(lead + (HEADS, real))
    return jnp.pad(w, [(0, 0)] * len(lead) + [(0, 0), (0, HEAD_PAD - real)]).reshape(lead + (QW,))


def _rope_tables(s):
    half = ROPE // 2
    inv_freq = ROPE_THETA ** (-jnp.arange(half, dtype=F32) / half)
    ang = jnp.arange(s, dtype=F32)[:, None] * inv_freq[None, :]
    cos, sin = jnp.cos(ang), jnp.sin(ang)
    ones = jnp.ones((s, NOPE), F32)
    zeros = jnp.zeros((s, NOPE), F32)
    pad = jnp.zeros((s, HEAD_PAD - QK), F32)
    rc = jnp.concatenate([ones, cos, cos, pad + 1.0], axis=1)
    rs = jnp.concatenate([zeros, -sin, sin, pad], axis=1)
    return rc, rs


def kernel(x, norm_g, w_in, conv_w, conv_b, conv_ln_g, conv_ln_b, conv_pw_w, conv_pw_b, q_norm_g, w_uq, kv_norm_g, w_ukv, qk_q_g, qk_k_g, sg_ln_g, sg_ln_b, sg_w, sg_b, branch_norm_g, w_out, loss_target, m_norm_g, m_w_in, m_conv_w, m_conv_b, m_conv_ln_g, m_conv_ln_b, m_conv_pw_w, m_conv_pw_b, m_q_norm_g, m_w_uq, m_kv_norm_g, m_w_ukv, m_qk_q_g, m_qk_k_g, m_sg_ln_g, m_sg_ln_b, m_sg_w, m_sg_b, m_branch_norm_g, m_w_out, v_norm_g, v_w_in, v_conv_w, v_conv_b, v_conv_ln_g, v_conv_ln_b, v_conv_pw_w, v_conv_pw_b, v_q_norm_g, v_w_uq, v_kv_norm_g, v_w_ukv, v_qk_q_g, v_qk_k_g, v_sg_ln_g, v_sg_ln_b, v_sg_w, v_sg_b, v_branch_norm_g, v_w_out):
    given = dict(locals())
    wts = {n: given[n] for n in W_NAMES}
    mom_m = {n: given['m_' + n] for n in W_NAMES}
    mom_v = {n: given['v_' + n] for n in W_NAMES}
    s = x.shape[1]
    xs = x.reshape(s, D_MODEL)
    target = loss_target.reshape(s, D_MODEL)

    rp_shapes = [wts[n].shape for n in REPL]
    rows_rp = _rows_for(sum(math.prod(p) for p in rp_shapes), BF16_ROWS)
    sh_shape = {n: wts[n].shape for n in SHARDED}
    sh_2d = {n: (sh_shape[n][0] * sh_shape[n][1], sh_shape[n][2]) for n in SHARDED}

    def vec(a, width=None):
        a = a.reshape(1, -1)
        return a if width is None else jnp.pad(a, ((0, 0), (0, width - a.shape[1])))

    def win_full(g):
        return jnp.moveaxis(g, 0, 1).reshape(g.shape[1], N_DEV * g.shape[2])

    later = [n for n in SHARDED if n != 'w_in']

    def shards(l):
        return [wts['w_in'][l].astype(BF16)] + [wts[n][l].astype(F32 if n == 'conv_w' else BF16) for n in later]

    def gathered_weights(g_win, g_later):
        full = {n: _unshard(g, SHARD_AXIS[n] - 1) for n, g in zip(later, g_later)}
        full['w_in'] = win_full(g_win)
        return full

    (g_win0,) = _all_gather(shards(0)[:1], "w_in0_all_gather")
    proj0, g_later0 = _proj_fwd(xs, vec(wts['norm_g'][0]), _to_layout(win_full(g_win0)), "proj_fwd_0",
                                gather=shards(0)[1:])

    rc, rs = _rope_tables(s)
    tril = jnp.tril(jnp.ones((SG_CHUNK, SG_CHUNK), dtype=bool))

    def layer_params(l, full):
        p = dict(full)
        p.update({n: wts[n][l] for n in REPL})
        wukv = p['w_ukv'].reshape(KV_LORA, HEADS, NOPE + V_DIM)
        wm = jnp.where(tril[None], p['sg_w'], 0.0)
        return (dict(
            ng=vec(p['norm_g']), win=_to_layout(p['w_in']).astype(BF16),
            cw=jnp.pad(p['conv_w'], ((0, HALO - CONV_K), (0, 0))), cb=vec(p['conv_b']), clg=vec(p['conv_ln_g']),
            clb=vec(p['conv_ln_b']), pww=p['conv_pw_w'].astype(BF16), pwb=vec(p['conv_pw_b']),
            qg=vec(p['q_norm_g']), wuq=_pad_heads(p['w_uq'], QK).astype(BF16), kvg=vec(p['kv_norm_g']),
            wukv=jnp.concatenate([_pad_heads(wukv[:, :, :NOPE].reshape(KV_LORA, HEADS * NOPE), NOPE),
                                  wukv[:, :, NOPE:].reshape(KV_LORA, MLA_W)], axis=1).astype(BF16),
            gq=vec(p['qk_q_g'], LANES), gk=vec(p['qk_k_g'], LANES),
            slg=vec(p['sg_ln_g']), slb=vec(p['sg_ln_b']), wm=wm.astype(BF16),
            wmt=jnp.swapaxes(wm, 1, 2).astype(BF16),
            sbx=jnp.repeat(p['sg_b'].T, SG_W // SG_HEADS, axis=1),
            bng=vec(p['branch_norm_g']), wout=p['w_out'].astype(BF16)))

    layers = [layer_params(0, gathered_weights(g_win0, g_later0))]

    acts = []
    h_in = xs
    for l in range(DEPTH):
        p = layers[l]
        proj = proj0 if l == 0 else _proj_fwd(h_in, p['ng'], p['win'], f"proj_fwd_{l}")[0]
        yc, cv = _conv_fwd(proj, p['cw'], p['cb'], p['clg'], p['clb'], p['pww'], p['pwb'], f"conv_fwd_{l}")
        ys = _sgu_fwd(proj, p['slg'], p['slb'], p['wm'], p['sbx'], f"sgu_fwd_{l}")
        q, k, v, _ = _mla_fwd(proj, rc, rs, p['qg'], p['wuq'], p['kvg'], p['wukv'], p['gq'], p['gk'], f"mla_fwd_{l}")
        o, lse, g_next = _attn_fwd(q, k, v, f"attn_fwd_{l}", gather=shards(l + 1) if l + 1 < DEPTH else ())
        if l + 1 < DEPTH:
            layers.append(layer_params(l + 1, gathered_weights(g_next[0], g_next[1:])))
        acts.append(dict(x=h_in, proj=proj, yc=yc, cv=cv, ys=ys, q=q, k=k, v=v, o=o, lse=lse))
        if l + 1 < DEPTH:
            h_in = _post_fwd(h_in, yc, o, proj, ys, p['bng'], p['wout'], f"post_fwd_{l}")
        else:
            d_out, loss_blk = _post_fwd_loss(h_in, yc, o, proj, ys, p['bng'], p['wout'], target, f"post_fwd_loss_{l}")

    loss =lax.psum(loss_blk[0, 0], ("x", "y", "c"))

    grads = {n: [None] * DEPTH for n in W_NAMES}
    for l in reversed(range(DEPTH)):
        p, a = layers[l], acts[l]
        d_yc, d_ys, d_o, d_zm, stats, g_wout, g_bng = _post_bwd(
            d_out, a['yc'], a['o'], a['lse'], a['proj'], a['ys'], p['bng'], p['wout'], f"post_bwd_{l}")
        dq, dk, dv = _attn_bwd(a['q'], a['k'], a['v'], d_o, stats, f"attn_bwd_{l}")
        d_a, g_cw, g_pww, gv_c = _conv_bwd(a['proj'], a['cv'], d_yc, p['cw'], p['clg'], p['clb'], p['pww'], p['pwb'],
                                           f"conv_bwd_{l}")
        d_sg, g_wm, dms, gv_s = _sgu_bwd(a['proj'], d_ys, p['slg'], p['slb'], p['wm'], p['wmt'], p['sbx'],
                                         f"sgu_bwd_{l}")
        d_cq, d_ckv, d_kr, g_wuq, g_wukv, gv_m = _mla_bwd(
            a['proj'], rc, rs, p['qg'], p['wuq'], p['kvg'], p['wukv'], p['gq'], p['gk'], dq, dk, dv, f"mla_bwd_{l}")
        pieces = [(d_a, C_A), (d_cq, C_CQ), (d_zm, C_ZM), (d_ckv, C_CKV), (d_sg, C_SG), (d_kr, C_KR)]
        d_x, h_t, g_ng = _proj_bwd(a['x'], p['ng'], p['win'], d_out, pieces, f"proj_bwd_{l}")
        g_win = _win_grad(h_t, pieces, f"win_grad_{l}")
        d_out = d_x

        grads['norm_g'][l] = g_ng[0]
        grads['w_in'][l] = _from_layout(g_win)
        grads['conv_w'][l] = g_cw[:CONV_K]
        grads['conv_b'][l] = gv_c[0]
        grads['conv_ln_g'][l] = gv_c[1]
        grads['conv_ln_b'][l] = gv_c[2]
        grads['conv_pw_w'][l] = g_pww
        grads['conv_pw_b'][l] = gv_c[3]
        grads['q_norm_g'][l] = gv_m[0, :Q_LORA]
        grads['w_uq'][l] = g_wuq.reshape(Q_LORA, HEADS, HEAD_PAD)[:, :, :QK].reshape(Q_LORA, HEADS * QK)
        grads['kv_norm_g'][l] = gv_m[1, :KV_LORA]
        grads['w_ukv'][l] = jnp.concatenate(
            [g_wukv[:, :QW].reshape(KV_LORA, HEADS, HEAD_PAD)[:, :, :NOPE],
             g_wukv[:, QW:].reshape(KV_LORA, HEADS, V_DIM)], axis=2).reshape(KV_LORA, HEADS * (NOPE + V_DIM))
        grads['qk_q_g'][l] = gv_m[2, :QK]
        grads['qk_k_g'][l] = gv_m[3, :QK]
        grads['sg_ln_g'][l] = gv_s[0]
        grads['sg_ln_b'][l] = gv_s[1]
        grads['sg_w'][l] = jnp.where(tril[None], g_wm, 0.0)
        grads['sg_b'][l] = dms.reshape(SG_CHUNK, SG_HEADS, SG_W // SG_HEADS).sum(axis=2).T
        grads['branch_norm_g'][l] = g_bng[0]
        grads['w_out'][l] = g_wout
    grad_x = d_out.reshape(x.shape)
    g_full = {n: jnp.stack(grads[n]) for n in W_NAMES}

    gss = [_split_c_chip(g_full[n].astype(BF16), SHARD_AXIS[n]).reshape((2, N_CHIP) + sh_2d[n]) for n in SHARDED]
    gr = _pack([g_full[n].astype(BF16) for n in REPL], rows_rp)
    *theirs, gr_theirs = _grad_to_sibling(gss, gr, "grad_to_sibling")
    my_c = lax.axis_index("c")
    mine = [lax.dynamic_index_in_dim(g, my_c, 0, keepdims=False) for g in gss]
    *chip_sh, chip_rp = _chip_sum(mine, theirs, gr, gr_theirs, "chip_sum")
    *parts_sh, parts_rp = _grad_to_chips(chip_sh, chip_rp, "grad_to_chips")
    res_sh = {n: _adamw(parts, wts[n].reshape(sh_2d[n]), mom_m[n].reshape(sh_2d[n]), mom_v[n].reshape(sh_2d[n]),
                        f"adamw_{n}") for n, parts in zip(SHARDED, parts_sh)}
    res_rp = _adamw(parts_rp, _pack([wts[n] for n in REPL], rows_rp), _pack([mom_m[n] for n in REPL], rows_rp),
                    _pack([mom_v[n] for n in REPL], rows_rp), "adamw_replicated")
    outs = []
    for kind in range(4):
        vals = {n: res_sh[n][kind].reshape(sh_shape[n]) for n in SHARDED}
        vals.update(zip(REPL, _unpack(res_rp[kind], rp_shapes)))
        outs.extend(vals[n] for n in W_NAMES)
    return (loss, grad_x, *outs)
```
